```python
import math
import jax, jax.numpy as jnp
from jax import lax
import numpy as np

D_MODEL = 1024
BATCH = 8
SEQ = 8192
DEPTH = 4

N_A_LAYERS = DEPTH // 2
N_B_LAYERS = DEPTH - N_A_LAYERS
CONV_WIDTH = 31
N_HEADS = 16
HEAD_DIM = D_MODEL // N_HEADS
D_FF = 4 * D_MODEL
D_PLE = 256
Q_BLOCK = 128
EPS = 1e-6
NEG_BIG = -1e30

kernel_name = "yoco_conformer_fox_hybrid"


def rmsnorm(x, g):
    xf = x.astype(jnp.float32)
    y = xf * lax.rsqrt(jnp.mean(xf * xf, axis=-1, keepdims=True) + EPS)
    return (y * g.astype(jnp.float32)).astype(x.dtype)


def layernorm(x, g, b):
    xf = x.astype(jnp.float32)
    mu = jnp.mean(xf, axis=-1, keepdims=True)
    xc = xf - mu
    var = jnp.mean(xc * xc, axis=-1, keepdims=True)
    y = xc * lax.rsqrt(var + EPS)
    return (y * g.astype(jnp.float32) + b.astype(jnp.float32)).astype(x.dtype)


def conformer_conv(hn, w_pw1, b_pw1, w_dw, b_dw, ln_g, ln_b, w_pw2, b_pw2):
    u = hn @ w_pw1 + b_pw1
    a, g = jnp.split(u, 2, axis=-1)
    u = a * jax.nn.sigmoid(g)
    kern = w_dw[:, None, :].astype(u.dtype)
    u = lax.conv_general_dilated(
        u, kern, window_strides=(1,), padding=((CONV_WIDTH - 1, 0),),
        dimension_numbers=("NWC", "WIO", "NWC"),
        feature_group_count=D_MODEL) + b_dw
    u = layernorm(u, ln_g, ln_b)
    u = jax.nn.silu(u)
    return u @ w_pw2 + b_pw2


def shared_kv(h, kv_norm, w_kvf, b_f):
    B, S, _ = h.shape
    u = rmsnorm(h, kv_norm) @ w_kvf
    k = u[..., :D_MODEL].reshape(B, S, N_HEADS, HEAD_DIM)
    v = u[..., D_MODEL:2 * D_MODEL].reshape(B, S, N_HEADS, HEAD_DIM)
    f_logit = (u[..., 2 * D_MODEL:] + b_f).astype(jnp.float32)
    log_f = jax.nn.log_sigmoid(f_logit)
    c = jnp.cumsum(log_f, axis=1)
    return k, v, jnp.transpose(c, (0, 2, 1))


def fox_attention(hn, w_q, w_o, k, v, c_bhs):
    B, S, _ = hn.shape
    nb = S // Q_BLOCK
    q = (hn @ w_q).reshape(B, S, N_HEADS, HEAD_DIM) * (HEAD_DIM ** -0.5)
    qb = jnp.transpose(q.reshape(B, nb, Q_BLOCK, N_HEADS, HEAD_DIM), (1, 0, 2, 3, 4))
    cb = jnp.transpose(c_bhs.reshape(B, N_HEADS, nb, Q_BLOCK), (2, 0, 1, 3))
    k_pos = jnp.arange(S)

    def one_block(args):
        q_blk, c_blk, i = args
        s = jnp.einsum("bqhd,bkhd->bhqk", q_blk, k,
                       preferred_element_type=jnp.float32)
        bias = c_blk[:, :, :, None] - c_bhs[:, :, None, :]
        q_pos = i * Q_BLOCK + jnp.arange(Q_BLOCK)
        causal = k_pos[None, :] <= q_pos[:, None]
        s = jnp.where(causal, s + bias, NEG_BIG)
        pr = jax.nn.softmax(s, axis=-1)
        return jnp.einsum("bhqk,bkhd->bqhd", pr.astype(v.dtype), v)

    o = lax.map(one_block, (qb, cb, jnp.arange(nb)))
    o = jnp.transpose(o, (1, 0, 2, 3, 4)).reshape(B, S, D_MODEL)
    return o @ w_o


def _fwd_setup_inputs(seed: int = 0) -> dict:
    key = jax.random.key(seed)
    ks = jax.random.split(key, 32)
    f32 = jnp.float32
    nrm = lambda k, shape, scale: (jax.random.normal(k, shape, f32) * scale)
    gain = lambda k, shape: 1.0 + 0.05 * jax.random.normal(k, shape, f32)
    D = D_MODEL
    x = jax.random.normal(ks[0], (BATCH, SEQ, D), f32)
    p = jax.random.normal(ks[1], (DEPTH, BATCH, SEQ, D_PLE), f32)
    mix_norm = gain(ks[2], (DEPTH, D))
    conv_w_pw1 = nrm(ks[3], (N_A_LAYERS, D, 2 * D), D ** -0.5)
    conv_b_pw1 = nrm(ks[4], (N_A_LAYERS, 2 * D), 0.02)
    conv_w_dw = nrm(ks[5], (N_A_LAYERS, CONV_WIDTH, D), CONV_WIDTH ** -0.5)
    conv_b_dw = nrm(ks[6], (N_A_LAYERS, D), 0.02)
    conv_ln_g = gain(ks[7], (N_A_LAYERS, D))
    conv_ln_b = nrm(ks[8], (N_A_LAYERS, D), 0.02)
    conv_w_pw2 = nrm(ks[9], (N_A_LAYERS, D, D), 0.5 * D ** -0.5)
    conv_b_pw2 = nrm(ks[10], (N_A_LAYERS, D), 0.02)
    kv_norm = gain(ks[11], (D,))
    w_kvf = jnp.concatenate([
        nrm(ks[12], (D, 2 * D), D ** -0.5),
        nrm(ks[13], (D, N_HEADS), 0.1 * D ** -0.5),
    ], axis=1)
    b_f = jax.random.uniform(ks[14], (N_HEADS,), f32, 1.0, 6.0)
    attn_w_q = nrm(ks[15], (N_B_LAYERS, D, D), D ** -0.5)
    attn_w_o = nrm(ks[16], (N_B_LAYERS, D, D), 0.5 * D ** -0.5)
    ffn_norm = gain(ks[17], (DEPTH, D))
    ffn_w1 = nrm(ks[18], (DEPTH, D, D_FF), D ** -0.5)
    ffn_w2 = nrm(ks[19], (DEPTH, D_FF, D), 0.5 * D_FF ** -0.5)
    ple_norm = gain(ks[20], (DEPTH, D))
    ple_w_gate = nrm(ks[21], (DEPTH, D, D), D ** -0.5)
    ple_w_proj = nrm(ks[22], (DEPTH, D_PLE, D), 0.5 * D_PLE ** -0.5)
    final_norm = gain(ks[23], (D,))
    return {"x": x, "p": p, "mix_norm": mix_norm,
            "conv_w_pw1": conv_w_pw1, "conv_b_pw1": conv_b_pw1,
            "conv_w_dw": conv_w_dw, "conv_b_dw": conv_b_dw,
            "conv_ln_g": conv_ln_g, "conv_ln_b": conv_ln_b,
            "conv_w_pw2": conv_w_pw2, "conv_b_pw2": conv_b_pw2,
            "kv_norm": kv_norm, "w_kvf": w_kvf, "b_f": b_f,
            "attn_w_q": attn_w_q, "attn_w_o": attn_w_o,
            "ffn_norm": ffn_norm, "ffn_w1": ffn_w1, "ffn_w2": ffn_w2,
            "ple_norm": ple_norm, "ple_w_gate": ple_w_gate, "ple_w_proj": ple_w_proj,
            "final_norm": final_norm}


def _fwd_reference(x, p, mix_norm, conv_w_pw1, conv_b_pw1, conv_w_dw, conv_b_dw,
              conv_ln_g, conv_ln_b, conv_w_pw2, conv_b_pw2, kv_norm, w_kvf, b_f,
              attn_w_q, attn_w_o, ffn_norm, ffn_w1, ffn_w2, ple_norm, ple_w_gate,
              ple_w_proj, final_norm):
    h = x
    k = v = c_bhs = None
    for i in range(DEPTH):
        hn = rmsnorm(h, mix_norm[i])
        if i < N_A_LAYERS:
            h = h + conformer_conv(hn, conv_w_pw1[i], conv_b_pw1[i], conv_w_dw[i],
                                   conv_b_dw[i], conv_ln_g[i], conv_ln_b[i],
                                   conv_w_pw2[i], conv_b_pw2[i])
        else:
            j = i - N_A_LAYERS
            if j == 0:
                k, v, c_bhs = shared_kv(h, kv_norm, w_kvf, b_f)
            h = h + fox_attention(hn, attn_w_q[j], attn_w_o[j], k, v, c_bhs)
        hn = rmsnorm(h, ffn_norm[i])
        h = h + jnp.square(jax.nn.relu(hn @ ffn_w1[i])) @ ffn_w2[i]
        gate = jax.nn.sigmoid(rmsnorm(h, ple_norm[i]) @ ple_w_gate[i])
        h = h + gate * (p[i] @ ple_w_proj[i])
    return rmsnorm(h, final_norm)


import jax as _jax
import jax.numpy as _jnp

TWIN_FORMAT = 'train_step'
FWD_PARAMS = ['x', 'p', 'mix_norm', 'conv_w_pw1', 'conv_b_pw1', 'conv_w_dw', 'conv_b_dw', 'conv_ln_g', 'conv_ln_b', 'conv_w_pw2', 'conv_b_pw2', 'kv_norm', 'w_kvf', 'b_f', 'attn_w_q', 'attn_w_o', 'ffn_norm', 'ffn_w1', 'ffn_w2', 'ple_norm', 'ple_w_gate', 'ple_w_proj', 'final_norm']
TWIN_WEIGHTS = ['mix_norm', 'conv_w_pw1', 'conv_b_pw1', 'conv_w_dw', 'conv_b_dw', 'conv_ln_g', 'conv_ln_b', 'conv_w_pw2', 'conv_b_pw2', 'kv_norm', 'w_kvf', 'b_f', 'attn_w_q', 'attn_w_o', 'ffn_norm', 'ffn_w1', 'ffn_w2', 'ple_norm', 'ple_w_gate', 'ple_w_proj', 'final_norm']
TWIN_DIFF_INPUT = 'x'
TWIN_INPUTS = ['x', 'p', 'mix_norm', 'conv_w_pw1', 'conv_b_pw1', 'conv_w_dw', 'conv_b_dw', 'conv_ln_g', 'conv_ln_b', 'conv_w_pw2', 'conv_b_pw2', 'kv_norm', 'w_kvf', 'b_f', 'attn_w_q', 'attn_w_o', 'ffn_norm', 'ffn_w1', 'ffn_w2', 'ple_norm', 'ple_w_gate', 'ple_w_proj', 'final_norm', 'loss_target', 'm_mix_norm', 'm_conv_w_pw1', 'm_conv_b_pw1', 'm_conv_w_dw', 'm_conv_b_dw', 'm_conv_ln_g', 'm_conv_ln_b', 'm_conv_w_pw2', 'm_conv_b_pw2', 'm_kv_norm', 'm_w_kvf', 'm_b_f', 'm_attn_w_q', 'm_attn_w_o', 'm_ffn_norm', 'm_ffn_w1', 'm_ffn_w2', 'm_ple_norm', 'm_ple_w_gate', 'm_ple_w_proj', 'm_final_norm', 'v_mix_norm', 'v_conv_w_pw1', 'v_conv_b_pw1', 'v_conv_w_dw', 'v_conv_b_dw', 'v_conv_ln_g', 'v_conv_ln_b', 'v_conv_w_pw2', 'v_conv_b_pw2', 'v_kv_norm', 'v_w_kvf', 'v_b_f', 'v_attn_w_q', 'v_attn_w_o', 'v_ffn_norm', 'v_ffn_w1', 'v_ffn_w2', 'v_ple_norm', 'v_ple_w_gate', 'v_ple_w_proj', 'v_final_norm']
TWIN_OUTPUTS = ['loss', 'grad_x', 'grad_mix_norm', 'grad_conv_w_pw1', 'grad_conv_b_pw1', 'grad_conv_w_dw', 'grad_conv_b_dw', 'grad_conv_ln_g', 'grad_conv_ln_b', 'grad_conv_w_pw2', 'grad_conv_b_pw2', 'grad_kv_norm', 'grad_w_kvf', 'grad_b_f', 'grad_attn_w_q', 'grad_attn_w_o', 'grad_ffn_norm', 'grad_ffn_w1', 'grad_ffn_w2', 'grad_ple_norm', 'grad_ple_w_gate', 'grad_ple_w_proj', 'grad_final_norm', 'delta_mix_norm', 'delta_conv_w_pw1', 'delta_conv_b_pw1', 'delta_conv_w_dw', 'delta_conv_b_dw', 'delta_conv_ln_g', 'delta_conv_ln_b', 'delta_conv_w_pw2', 'delta_conv_b_pw2', 'delta_kv_norm', 'delta_w_kvf', 'delta_b_f', 'delta_attn_w_q', 'delta_attn_w_o', 'delta_ffn_norm', 'delta_ffn_w1', 'delta_ffn_w2', 'delta_ple_norm', 'delta_ple_w_gate', 'delta_ple_w_proj', 'delta_final_norm', 'new_m_mix_norm', 'new_m_conv_w_pw1', 'new_m_conv_b_pw1', 'new_m_conv_w_dw', 'new_m_conv_b_dw', 'new_m_conv_ln_g', 'new_m_conv_ln_b', 'new_m_conv_w_pw2', 'new_m_conv_b_pw2', 'new_m_kv_norm', 'new_m_w_kvf', 'new_m_b_f', 'new_m_attn_w_q', 'new_m_attn_w_o', 'new_m_ffn_norm', 'new_m_ffn_w1', 'new_m_ffn_w2', 'new_m_ple_norm', 'new_m_ple_w_gate', 'new_m_ple_w_proj', 'new_m_final_norm', 'new_v_mix_norm', 'new_v_conv_w_pw1', 'new_v_conv_b_pw1', 'new_v_conv_w_dw', 'new_v_conv_b_dw', 'new_v_conv_ln_g', 'new_v_conv_ln_b', 'new_v_conv_w_pw2', 'new_v_conv_b_pw2', 'new_v_kv_norm', 'new_v_w_kvf', 'new_v_b_f', 'new_v_attn_w_q', 'new_v_attn_w_o', 'new_v_ffn_norm', 'new_v_ffn_w1', 'new_v_ffn_w2', 'new_v_ple_norm', 'new_v_ple_w_gate', 'new_v_ple_w_proj', 'new_v_final_norm']
TWIN_LEAF_KINDS = {'loss': 'loss', 'grad_x': 'grad_x', 'grad_mix_norm': 'grad_w', 'grad_conv_w_pw1': 'grad_w', 'grad_conv_b_pw1': 'grad_w', 'grad_conv_w_dw': 'grad_w', 'grad_conv_b_dw': 'grad_w', 'grad_conv_ln_g': 'grad_w', 'grad_conv_ln_b': 'grad_w', 'grad_conv_w_pw2': 'grad_w', 'grad_conv_b_pw2': 'grad_w', 'grad_kv_norm': 'grad_w', 'grad_w_kvf': 'grad_w', 'grad_b_f': 'grad_w', 'grad_attn_w_q': 'grad_w', 'grad_attn_w_o': 'grad_w', 'grad_ffn_norm': 'grad_w', 'grad_ffn_w1': 'grad_w', 'grad_ffn_w2': 'grad_w', 'grad_ple_norm': 'grad_w', 'grad_ple_w_gate': 'grad_w', 'grad_ple_w_proj': 'grad_w', 'grad_final_norm': 'grad_w', 'delta_mix_norm': 'delta_w', 'delta_conv_w_pw1': 'delta_w', 'delta_conv_b_pw1': 'delta_w', 'delta_conv_w_dw': 'delta_w', 'delta_conv_b_dw': 'delta_w', 'delta_conv_ln_g': 'delta_w', 'delta_conv_ln_b': 'delta_w', 'delta_conv_w_pw2': 'delta_w', 'delta_conv_b_pw2': 'delta_w', 'delta_kv_norm': 'delta_w', 'delta_w_kvf': 'delta_w', 'delta_b_f': 'delta_w', 'delta_attn_w_q': 'delta_w', 'delta_attn_w_o': 'delta_w', 'delta_ffn_norm': 'delta_w', 'delta_ffn_w1': 'delta_w', 'delta_ffn_w2': 'delta_w', 'delta_ple_norm': 'delta_w', 'delta_ple_w_gate': 'delta_w', 'delta_ple_w_proj': 'delta_w', 'delta_final_norm': 'delta_w', 'new_m_mix_norm': 'new_m', 'new_m_conv_w_pw1': 'new_m', 'new_m_conv_b_pw1': 'new_m', 'new_m_conv_w_dw': 'new_m', 'new_m_conv_b_dw': 'new_m', 'new_m_conv_ln_g': 'new_m', 'new_m_conv_ln_b': 'new_m', 'new_m_conv_w_pw2': 'new_m', 'new_m_conv_b_pw2': 'new_m', 'new_m_kv_norm': 'new_m', 'new_m_w_kvf': 'new_m', 'new_m_b_f': 'new_m', 'new_m_attn_w_q': 'new_m', 'new_m_attn_w_o': 'new_m', 'new_m_ffn_norm': 'new_m', 'new_m_ffn_w1': 'new_m', 'new_m_ffn_w2': 'new_m', 'new_m_ple_norm': 'new_m', 'new_m_ple_w_gate': 'new_m', 'new_m_ple_w_proj': 'new_m', 'new_m_final_norm': 'new_m', 'new_v_mix_norm': 'new_v', 'new_v_conv_w_pw1': 'new_v', 'new_v_conv_b_pw1': 'new_v', 'new_v_conv_w_dw': 'new_v', 'new_v_conv_b_dw': 'new_v', 'new_v_conv_ln_g': 'new_v', 'new_v_conv_ln_b': 'new_v', 'new_v_conv_w_pw2': 'new_v', 'new_v_conv_b_pw2': 'new_v', 'new_v_kv_norm': 'new_v', 'new_v_w_kvf': 'new_v', 'new_v_b_f': 'new_v', 'new_v_attn_w_q': 'new_v', 'new_v_attn_w_o': 'new_v', 'new_v_ffn_norm': 'new_v', 'new_v_ffn_w1': 'new_v', 'new_v_ffn_w2': 'new_v', 'new_v_ple_norm': 'new_v', 'new_v_ple_w_gate': 'new_v', 'new_v_ple_w_proj': 'new_v', 'new_v_final_norm': 'new_v'}


def _forward(args):
    return _fwd_reference(*[args[k] for k in FWD_PARAMS])


def _output_shape():
    def fwd():
        inp = _fwd_setup_inputs(0)
        return _fwd_reference(*[inp[k] for k in FWD_PARAMS])
    out = _jax.eval_shape(fwd)
    return out.shape, out.dtype

N_MICROBATCH = 1
ADAM_LR = 0.001
ADAM_B1 = 0.9
ADAM_B2 = 0.999
ADAM_EPS = 1e-08
ADAM_WD = 0.01
ADAM_STEP = 10
PER_EXAMPLE_BATCH_AXIS = {'x': 0, 'p': 1, 'loss_target': 0}
SHARED_INPUTS = []
_WEIGHT_DTYPES = {'mix_norm': _jnp.float32, 'conv_w_pw1': _jnp.float32, 'conv_b_pw1': _jnp.float32, 'conv_w_dw': _jnp.float32, 'conv_b_dw': _jnp.float32, 'conv_ln_g': _jnp.float32, 'conv_ln_b': _jnp.float32, 'conv_w_pw2': _jnp.float32, 'conv_b_pw2': _jnp.float32, 'kv_norm': _jnp.float32, 'w_kvf': _jnp.float32, 'b_f': _jnp.float32, 'attn_w_q': _jnp.float32, 'attn_w_o': _jnp.float32, 'ffn_norm': _jnp.float32, 'ffn_w1': _jnp.float32, 'ffn_w2': _jnp.float32, 'ple_norm': _jnp.float32, 'ple_w_gate': _jnp.float32, 'ple_w_proj': _jnp.float32, 'final_norm': _jnp.float32}
MOMENT_SCALE = {'mix_norm': 6.023752e-02, 'conv_w_pw1': 6.024237e-02, 'conv_b_pw1': 2.322552e-01, 'conv_w_dw': 8.391178e-02, 'conv_b_dw': 5.303820e-01, 'conv_ln_g': 2.100691e-01, 'conv_ln_b': 3.221369e-01, 'conv_w_pw2': 2.545660e-01, 'conv_b_pw2': 1.263452e+00, 'kv_norm': 1.893688e-01, 'w_kvf': 1.270771e-01, 'b_f': 2.763891e-01, 'attn_w_q': 1.921398e-02, 'attn_w_o': 2.644349e-01, 'ffn_norm': 1.512467e-01, 'ffn_w1': 7.425077e-02, 'ffn_w2': 5.343670e-01, 'ple_norm': 2.082048e-02, 'ple_w_gate': 2.043519e-02, 'ple_w_proj': 9.497591e-02, 'final_norm': 6.487477e+01}


def _to_microbatches(a, axis):
    t = _jnp.moveaxis(a, axis, 0)
    t = t.reshape((N_MICROBATCH, t.shape[0] // N_MICROBATCH) + t.shape[1:])
    return _jnp.moveaxis(t, 1, axis + 1)


def setup_inputs(seed: int = 0) -> dict:
    inp = _fwd_setup_inputs(seed)
    key = _jax.random.fold_in(_jax.random.key(seed), 7919)
    shape, _ = _output_shape()
    out = dict(inp)
    out["loss_target"] = _jax.random.normal(_jax.random.fold_in(key, 0), shape, _jnp.float32)
    for i, name in enumerate(TWIN_WEIGHTS):
        w = inp[name].astype(_jnp.float32)
        if MOMENT_SCALE is None:
            s = _jnp.sqrt(_jnp.mean(_jnp.square(w)) + 1e-30)
        else:
            s = MOMENT_SCALE[name]
        km, kv = _jax.random.split(_jax.random.fold_in(key, i + 1))
        out[name] = w
        out["m_" + name] = s * _jax.random.normal(km, w.shape, _jnp.float32)
        out["v_" + name] = (s * s) * _jax.random.uniform(kv, w.shape, _jnp.float32, 0.5, 1.5)
    if N_MICROBATCH > 1:
        for name, axis in PER_EXAMPLE_BATCH_AXIS.items():
            out[name] = _to_microbatches(out[name], axis)
    return {'x': out['x'], 'p': out['p'], 'mix_norm': out['mix_norm'], 'conv_w_pw1': out['conv_w_pw1'], 'conv_b_pw1': out['conv_b_pw1'], 'conv_w_dw': out['conv_w_dw'], 'conv_b_dw': out['conv_b_dw'], 'conv_ln_g': out['conv_ln_g'], 'conv_ln_b': out['conv_ln_b'], 'conv_w_pw2': out['conv_w_pw2'], 'conv_b_pw2': out['conv_b_pw2'], 'kv_norm': out['kv_norm'], 'w_kvf': out['w_kvf'], 'b_f': out['b_f'], 'attn_w_q': out['attn_w_q'], 'attn_w_o': out['attn_w_o'], 'ffn_norm': out['ffn_norm'], 'ffn_w1': out['ffn_w1'], 'ffn_w2': out['ffn_w2'], 'ple_norm': out['ple_norm'], 'ple_w_gate': out['ple_w_gate'], 'ple_w_proj': out['ple_w_proj'], 'final_norm': out['final_norm'], 'loss_target': out['loss_target'], 'm_mix_norm': out['m_mix_norm'], 'm_conv_w_pw1': out['m_conv_w_pw1'], 'm_conv_b_pw1': out['m_conv_b_pw1'], 'm_conv_w_dw': out['m_conv_w_dw'], 'm_conv_b_dw': out['m_conv_b_dw'], 'm_conv_ln_g': out['m_conv_ln_g'], 'm_conv_ln_b': out['m_conv_ln_b'], 'm_conv_w_pw2': out['m_conv_w_pw2'], 'm_conv_b_pw2': out['m_conv_b_pw2'], 'm_kv_norm': out['m_kv_norm'], 'm_w_kvf': out['m_w_kvf'], 'm_b_f': out['m_b_f'], 'm_attn_w_q': out['m_attn_w_q'], 'm_attn_w_o': out['m_attn_w_o'], 'm_ffn_norm': out['m_ffn_norm'], 'm_ffn_w1': out['m_ffn_w1'], 'm_ffn_w2': out['m_ffn_w2'], 'm_ple_norm': out['m_ple_norm'], 'm_ple_w_gate': out['m_ple_w_gate'], 'm_ple_w_proj': out['m_ple_w_proj'], 'm_final_norm': out['m_final_norm'], 'v_mix_norm': out['v_mix_norm'], 'v_conv_w_pw1': out['v_conv_w_pw1'], 'v_conv_b_pw1': out['v_conv_b_pw1'], 'v_conv_w_dw': out['v_conv_w_dw'], 'v_conv_b_dw': out['v_conv_b_dw'], 'v_conv_ln_g': out['v_conv_ln_g'], 'v_conv_ln_b': out['v_conv_ln_b'], 'v_conv_w_pw2': out['v_conv_w_pw2'], 'v_conv_b_pw2': out['v_conv_b_pw2'], 'v_kv_norm': out['v_kv_norm'], 'v_w_kvf': out['v_w_kvf'], 'v_b_f': out['v_b_f'], 'v_attn_w_q': out['v_attn_w_q'], 'v_attn_w_o': out['v_attn_w_o'], 'v_ffn_norm': out['v_ffn_norm'], 'v_ffn_w1': out['v_ffn_w1'], 'v_ffn_w2': out['v_ffn_w2'], 'v_ple_norm': out['v_ple_norm'], 'v_ple_w_gate': out['v_ple_w_gate'], 'v_ple_w_proj': out['v_ple_w_proj'], 'v_final_norm': out['v_final_norm']}


def _loss(weights, diff, rest, loss_target):
    with _jax.named_scope("forward"):
        args = {**rest, TWIN_DIFF_INPUT: diff, **{k: w.astype(_WEIGHT_DTYPES[k]) for k, w in weights.items()}}
        y = _forward(args)
    with _jax.named_scope("loss_head"):
        err = _jnp.square(y.astype(_jnp.float32) - loss_target)
        return 0.5 * _jnp.sum(_jnp.mean(err, axis=-1)) if err.ndim else 0.5 * err


def _adamw(w, g, m, v):
    m = ADAM_B1 * m + (1.0 - ADAM_B1) * g
    v = ADAM_B2 * v + (1.0 - ADAM_B2) * _jnp.square(g)
    m_hat = m / (1.0 - ADAM_B1 ** ADAM_STEP)
    v_hat = v / (1.0 - ADAM_B2 ** ADAM_STEP)
    delta = -ADAM_LR * (m_hat / (_jnp.sqrt(v_hat) + ADAM_EPS) + ADAM_WD * w)
    return delta, m, v


def reference(x, p, mix_norm, conv_w_pw1, conv_b_pw1, conv_w_dw, conv_b_dw, conv_ln_g, conv_ln_b, conv_w_pw2, conv_b_pw2, kv_norm, w_kvf, b_f, attn_w_q, attn_w_o, ffn_norm, ffn_w1, ffn_w2, ple_norm, ple_w_gate, ple_w_proj, final_norm, loss_target, m_mix_norm, m_conv_w_pw1, m_conv_b_pw1, m_conv_w_dw, m_conv_b_dw, m_conv_ln_g, m_conv_ln_b, m_conv_w_pw2, m_conv_b_pw2, m_kv_norm, m_w_kvf, m_b_f, m_attn_w_q, m_attn_w_o, m_ffn_norm, m_ffn_w1, m_ffn_w2, m_ple_norm, m_ple_w_gate, m_ple_w_proj, m_final_norm, v_mix_norm, v_conv_w_pw1, v_conv_b_pw1, v_conv_w_dw, v_conv_b_dw, v_conv_ln_g, v_conv_ln_b, v_conv_w_pw2, v_conv_b_pw2, v_kv_norm, v_w_kvf, v_b_f, v_attn_w_q, v_attn_w_o, v_ffn_norm, v_ffn_w1, v_ffn_w2, v_ple_norm, v_ple_w_gate, v_ple_w_proj, v_final_norm):
    given = dict(x=x, p=p, mix_norm=mix_norm, conv_w_pw1=conv_w_pw1, conv_b_pw1=conv_b_pw1, conv_w_dw=conv_w_dw, conv_b_dw=conv_b_dw, conv_ln_g=conv_ln_g, conv_ln_b=conv_ln_b, conv_w_pw2=conv_w_pw2, conv_b_pw2=conv_b_pw2, kv_norm=kv_norm, w_kvf=w_kvf, b_f=b_f, attn_w_q=attn_w_q, attn_w_o=attn_w_o, ffn_norm=ffn_norm, ffn_w1=ffn_w1, ffn_w2=ffn_w2, ple_norm=ple_norm, ple_w_gate=ple_w_gate, ple_w_proj=ple_w_proj, final_norm=final_norm, loss_target=loss_target, m_mix_norm=m_mix_norm, m_conv_w_pw1=m_conv_w_pw1, m_conv_b_pw1=m_conv_b_pw1, m_conv_w_dw=m_conv_w_dw, m_conv_b_dw=m_conv_b_dw, m_conv_ln_g=m_conv_ln_g, m_conv_ln_b=m_conv_ln_b, m_conv_w_pw2=m_conv_w_pw2, m_conv_b_pw2=m_conv_b_pw2, m_kv_norm=m_kv_norm, m_w_kvf=m_w_kvf, m_b_f=m_b_f, m_attn_w_q=m_attn_w_q, m_attn_w_o=m_attn_w_o, m_ffn_norm=m_ffn_norm, m_ffn_w1=m_ffn_w1, m_ffn_w2=m_ffn_w2, m_ple_norm=m_ple_norm, m_ple_w_gate=m_ple_w_gate, m_ple_w_proj=m_ple_w_proj, m_final_norm=m_final_norm, v_mix_norm=v_mix_norm, v_conv_w_pw1=v_conv_w_pw1, v_conv_b_pw1=v_conv_b_pw1, v_conv_w_dw=v_conv_w_dw, v_conv_b_dw=v_conv_b_dw, v_conv_ln_g=v_conv_ln_g, v_conv_ln_b=v_conv_ln_b, v_conv_w_pw2=v_conv_w_pw2, v_conv_b_pw2=v_conv_b_pw2, v_kv_norm=v_kv_norm, v_w_kvf=v_w_kvf, v_b_f=v_b_f, v_attn_w_q=v_attn_w_q, v_attn_w_o=v_attn_w_o, v_ffn_norm=v_ffn_norm, v_ffn_w1=v_ffn_w1, v_ffn_w2=v_ffn_w2, v_ple_norm=v_ple_norm, v_ple_w_gate=v_ple_w_gate, v_ple_w_proj=v_ple_w_proj, v_final_norm=v_final_norm)
    weights = {n: given[n] for n in TWIN_WEIGHTS}
    shared = {n: given[n] for n in SHARED_INPUTS}
    per_example = {n: given[n] for n in ['x', 'p']}
    grad_fn = _jax.value_and_grad(_loss, argnums=(0, 1))

    def one_microbatch(ex, loss_target):
        ex = dict(ex)
        diff = ex.pop(TWIN_DIFF_INPUT)
        return grad_fn(weights, diff, {**shared, **ex}, loss_target)

    if N_MICROBATCH == 1:
        loss, (grad_w, grad_x) = one_microbatch(per_example, given["loss_target"])
    else:
        def body(carry, xs):
            loss_sum, grad_sum = carry
            l_k, (gw_k, gx_k) = one_microbatch(xs[0], xs[1])
            with _jax.named_scope("update"):
                return (loss_sum + l_k, _jax.tree.map(_jnp.add, grad_sum, gw_k)), gx_k

        init = (_jnp.zeros((), _jnp.float32), _jax.tree.map(_jnp.zeros_like, weights))
        (loss, grad_w), grad_x = _jax.lax.scan(body, init, (per_example, given["loss_target"]))
    with _jax.named_scope("update"):
        delta_w, new_m, new_v = {}, {}, {}
        for n in TWIN_WEIGHTS:
            delta_w[n], new_m[n], new_v[n] = _adamw(weights[n], grad_w[n], given["m_" + n], given["v_" + n])
    return (loss, grad_x, *[grad_w[n] for n in TWIN_WEIGHTS], *[delta_w[n] for n in TWIN_WEIGHTS],
            *[new_m[n] for n in TWIN_WEIGHTS], *[new_v[n] for n in TWIN_WEIGHTS])
```

```python
import functools

import jax
import jax.numpy as jnp
import numpy as np
from jax import lax
from jax.experimental import pallas as pl
from jax.experimental.pallas import tpu as pltpu

F32 = jnp.float32
BF16 = jnp.bfloat16
MESH = pl.DeviceIdType.MESH

N_CHIPS = 4
N_DEV = 8
HEAD_DIM = 64
LANES = 128
CONV_WIDTH = 31
CONV_PAD = 32
EPS = 1e-6
NEG_BIG = -1e30
VMEM_LIMIT = 56 * 1024 * 1024

RS_ROW_MULT = 1024

ADAM_LR, ADAM_B1, ADAM_B2, ADAM_EPS, ADAM_WD, ADAM_STEP = 0.001, 0.9, 0.999, 1e-08, 0.01, 10


def _pallas(body, **kw):
    return pl.pallas_call(body, **kw)


def _params(sem=None):
    return pltpu.CompilerParams(dimension_semantics=sem, vmem_limit_bytes=VMEM_LIMIT)


def _sds(shape, dtype):
    return jax.ShapeDtypeStruct(tuple(shape), dtype)


_DIMS = {"nn": (((1,), (0,)), ((), ())), "nt": (((1,), (1,)), ((), ())), "tn": (((0,), (0,)), ((), ()))}


def _mm(a, b, *, mode="nn", extras=(), epi=None, out_dtype=F32, tm=1024, tn=1024, tk=1024, name):
    if mode == "nn":
        (M, K), (K2, N) = a.shape, b.shape
    elif mode == "nt":
        (M, K), (N, K2) = a.shape, b.shape
    else:
        (K, M), (K2, N) = a.shape, b.shape
    assert K == K2, (name, a.shape, b.shape)
    tm, tn, tk = min(tm, M), min(tn, N), min(tk, K)
    assert M % tm == 0 and N % tn == 0 and K % tk == 0, (name, a.shape, b.shape)
    nk = K // tk
    if mode == "tn":
        a_spec = pl.BlockSpec((tk, tm), lambda i, j, k: (k, i))
    else:
        a_spec = pl.BlockSpec((tm, tk), lambda i, j, k: (i, k))
    if mode == "nt":
        b_spec = pl.BlockSpec((tn, tk), lambda i, j, k: (j, k))
    else:
        b_spec = pl.BlockSpec((tk, tn), lambda i, j, k: (k, j))
    ex_specs = []
    for e in extras:
        if e.shape[0] == 1:
            ex_specs.append(pl.BlockSpec((1, tn), lambda i, j, k: (0, j)))
        else:
            assert e.shape == (M, N), (name, e.shape)
            ex_specs.append(pl.BlockSpec((tm, tn), lambda i, j, k: (i, j)))
    ne = len(extras)
    dims = _DIMS[mode]

    def body(a_ref, b_ref, *rest):
        ex_refs, o_ref = rest[:ne], rest[ne]
        part = lax.dot_general(a_ref[...].astype(BF16), b_ref[...].astype(BF16), dims, preferred_element_type=F32)

        def finish(acc):
            ex = [r[...] for r in ex_refs]
            o_ref[...] = (epi(acc, *ex) if epi is not None else acc).astype(o_ref.dtype)

        if nk == 1:
            finish(part)
        else:
            acc_ref = rest[ne + 1]
            k = pl.program_id(2)

            @pl.when(k == 0)
            def _():
                acc_ref[...] = part

            @pl.when(k > 0)
            def _():
                acc_ref[...] += part

            @pl.when(k == nk - 1)
            def _():
                finish(acc_ref[...])

    return _pallas(
        body, name=name, grid=(M // tm, N // tn, nk),
        in_specs=[a_spec, b_spec] + ex_specs,
        out_specs=pl.BlockSpec((tm, tn), lambda i, j, k: (i, j)),
        out_shape=_sds((M, N), out_dtype),
        scratch_shapes=[pltpu.VMEM((tm, tn), F32)] if nk > 1 else [],
        compiler_params=_params(("parallel", "parallel", "arbitrary")),
    )(a, b, *extras)


def _rows(fn, ins, params, outs, reds=(), *, tm=256, name):
    S = ins[0].shape[0]
    tm = min(tm, S)
    assert S % tm == 0, (name, S, tm)
    ni, npar, no, nr = len(ins), len(params), len(outs), len(reds)

    def body(*refs):
        in_refs, p_refs = refs[:ni], refs[ni:ni + npar]
        o_refs, r_refs = refs[ni + npar:ni + npar + no], refs[ni + npar + no:]
        res = fn(*[r[...] for r in in_refs], *[r[...] for r in p_refs])
        if not isinstance(res, (tuple, list)):
            res = (res,)
        assert len(res) == no + nr, (name, len(res))
        for r, v in zip(o_refs, res[:no]):
            r[...] = v.astype(r.dtype)
        i = pl.program_id(0)
        for r, v in zip(r_refs, res[no:]):
            @pl.when(i == 0)
            def _(r=r, v=v):
                r[...] = v

            @pl.when(i > 0)
            def _(r=r, v=v):
                r[...] += v

    res = _pallas(
        body, name=name, grid=(S // tm,),
        in_specs=[pl.BlockSpec((tm, a.shape[1]), lambda i: (i, 0)) for a in ins]
        + [pl.BlockSpec(p.shape, lambda i: (0, 0)) for p in params],
        out_specs=[pl.BlockSpec((tm, o.shape[1]), lambda i: (i, 0)) for o in outs]
        + [pl.BlockSpec(r.shape, lambda i: (0, 0)) for r in reds],
        out_shape=list(outs) + list(reds),
        compiler_params=_params(("arbitrary",)),
    )(*ins, *params)
    return res


def _colsum(v):
    return jnp.sum(v, axis=0, keepdims=True)


def _sigmoid(v):
    return 1.0 / (1.0 + jnp.exp(-v))


def _rms_stats(x):
    r = lax.rsqrt(jnp.mean(x * x, axis=-1, keepdims=True) + EPS)
    return x * r, r


def _rms_fwd(x, g):
    xh, _ = _rms_stats(x)
    return (xh * g,)


def _rms_bwd(dy, x, dres, g):
    xh, r = _rms_stats(x)
    dyg = dy * g
    dx = r * (dyg - xh * jnp.mean(dyg * xh, axis=-1, keepdims=True))
    return dres + dx, _colsum(dy * xh)


def _rms_bwd_bias(dy, x, dres, g):
    dx, dg = _rms_bwd(dy, x, dres, g)
    return dx, dg, _colsum(dx)


def _glu_fwd(u):
    d = u.shape[1] // 2
    return (u[:, :d] * _sigmoid(u[:, d:]),)


def _glu_bwd(dglu, u):
    d = u.shape[1] // 2
    a, sig = u[:, :d], _sigmoid(u[:, d:])
    du = jnp.concatenate([dglu * sig, dglu * a * sig * (1.0 - sig)], axis=1)
    return du, _colsum(du)


def _ln_parts(x, g, b):
    mu = jnp.mean(x, axis=-1, keepdims=True)
    xc = x - mu
    r = lax.rsqrt(jnp.mean(xc * xc, axis=-1, keepdims=True) + EPS)
    xh = xc * r
    return xh, r, xh * g + b


def _ln_silu_fwd(x, g, b):
    _, _, y = _ln_parts(x, g, b)
    return (y * _sigmoid(y),)


def _ln_silu_bwd(dsw, x, g, b):
    xh, r, y = _ln_parts(x, g, b)
    sig = _sigmoid(y)
    dy = dsw * sig * (1.0 + y * (1.0 - sig))
    dxh = dy * g
    dx = r * (dxh - jnp.mean(dxh, axis=-1, keepdims=True) - xh * jnp.mean(dxh * xh, axis=-1, keepdims=True))
    return dx, _colsum(dy * xh), _colsum(dy)


def _relu2(z):
    zp = jnp.maximum(z, 0.0)
    return (zp * zp,)


def _ple_fwd(h, zg, pp):
    return (h + _sigmoid(zg) * pp,)


def _ple_bwd(dh, zg, pp):
    gate = _sigmoid(zg)
    return dh * pp * gate * (1.0 - gate), dh * gate


def _final_fn(h, t, g):
    xh, r = _rms_stats(h)
    err = xh * g - t
    dy = err * (1.0 / h.shape[1])
    dyg = dy * g
    dh = r * (dyg - xh * jnp.mean(dyg * xh, axis=-1, keepdims=True))
    return dh, _colsum(err * err), _colsum(dy * xh)


def _add2(a, b):
    return (a + b,)


def _adamw_fn(w, g, m, v):
    m = ADAM_B1 * m + (1.0 - ADAM_B1) * g
    v = ADAM_B2 * v + (1.0 - ADAM_B2) * (g * g)
    m_hat = m / (1.0 - ADAM_B1 ** ADAM_STEP)
    v_hat = v / (1.0 - ADAM_B2 ** ADAM_STEP)
    delta = -ADAM_LR * (m_hat / (jnp.sqrt(v_hat) + ADAM_EPS) + ADAM_WD * w)
    return g, delta, m, v


def _adamw(w, g, m, v, name):
    shape = w.shape
    cols = shape[-1] if len(shape) > 1 else shape[0]
    two = lambda t: t.reshape(-1, cols)
    o = _sds(two(w).shape, F32)
    res = _rows(_adamw_fn, [two(w), two(g), two(m), two(v)], [], [o, o, o, o], tm=512, name=name)
    return [r.reshape(shape) for r in res]


def _dwconv_fwd(u, w, b, *, tm=512, name):
    S, D = u.shape
    tm = min(tm, S)
    rc = min(128, tm)
    per = tm // CONV_PAD

    def body(prev_ref, cur_ref, w_ref, b_ref, o_ref, win):
        i = pl.program_id(0)

        @pl.when(i == 0)
        def _():
            win[0:CONV_PAD, :] = jnp.zeros((CONV_PAD, D), F32)

        @pl.when(i > 0)
        def _():
            win[0:CONV_PAD, :] = prev_ref[...]

        win[CONV_PAD:CONV_PAD + tm, :] = cur_ref[...]
        for lc in range(D // LANES):
            ls = slice(lc * LANES, (lc + 1) * LANES)
            for r0 in range(0, tm, rc):
                acc = jnp.zeros((rc, LANES), F32) + b_ref[:, ls]
                for k in range(CONV_WIDTH):
                    acc = acc + win[r0 + 2 + k:r0 + 2 + k + rc, ls] * w_ref[k:k + 1, ls]
                o_ref[r0:r0 + rc, ls] = acc

    return _pallas(
        body, name=name, grid=(S // tm,),
        in_specs=[pl.BlockSpec((CONV_PAD, D), lambda i: (jnp.maximum(i * per - 1, 0), 0)),
                  pl.BlockSpec((tm, D), lambda i: (i, 0)),
                  pl.BlockSpec((CONV_PAD, D), lambda i: (0, 0)),
                  pl.BlockSpec((1, D), lambda i: (0, 0))],
        out_specs=pl.BlockSpec((tm, D), lambda i: (i, 0)),
        out_shape=_sds((S, D), F32),
        scratch_shapes=[pltpu.VMEM((tm + CONV_PAD, D), F32)],
        compiler_params=_params(("arbitrary",)),
    )(u, u, w, b)


def _dwconv_bwd(dy, u, w, *, tm=512, name):
    S, D = u.shape
    tm = min(tm, S)
    rc = min(128, tm)
    per = tm // CONV_PAD
    n = S // tm
    nxt = S // CONV_PAD - 1

    def body(dy_ref, dyn_ref, up_ref, u_ref, w_ref, du_ref, dw_ref, db_ref, wd, wu, dwacc, dbacc):
        i = pl.program_id(0)

        @pl.when(i == 0)
        def _():
            wu[0:CONV_PAD, :] = jnp.zeros((CONV_PAD, D), F32)
            dwacc[...] = jnp.zeros(dwacc.shape, F32)
            dbacc[...] = jnp.zeros(dbacc.shape, F32)

        @pl.when(i > 0)
        def _():
            wu[0:CONV_PAD, :] = up_ref[...]

        @pl.when(i == n - 1)
        def _():
            wd[tm:tm + CONV_PAD, :] = jnp.zeros((CONV_PAD, D), F32)

        @pl.when(i < n - 1)
        def _():
            wd[tm:tm + CONV_PAD, :] = dyn_ref[...]

        wu[CONV_PAD:CONV_PAD + tm, :] = u_ref[...]
        wd[0:tm, :] = dy_ref[...]
        for lc in range(D // LANES):
            ls = slice(lc * LANES, (lc + 1) * LANES)
            for r0 in range(0, tm, rc):
                acc = jnp.zeros((rc, LANES), F32)
                for k in range(CONV_WIDTH):
                    acc = acc + wd[r0 + 30 - k:r0 + 30 - k + rc, ls] * w_ref[k:k + 1, ls]
                du_ref[r0:r0 + rc, ls] = acc
                dyc = wd[r0:r0 + rc, ls]
                dbacc[:, ls] += jnp.sum(dyc.reshape(rc // 8, 8, LANES), axis=0)
                for k in range(CONV_WIDTH):
                    prod = dyc * wu[r0 + 2 + k:r0 + 2 + k + rc, ls]
                    dwacc[8 * k:8 * k + 8, ls] += jnp.sum(prod.reshape(rc // 8, 8, LANES), axis=0)

        @pl.when(i == n - 1)
        def _():
            dw_ref[...] = jnp.zeros(dw_ref.shape, F32)
            for k in range(CONV_WIDTH):
                dw_ref[k:k + 1, :] = jnp.sum(dwacc[8 * k:8 * k + 8, :], axis=0, keepdims=True)
            db_ref[...] = jnp.sum(dbacc[...], axis=0, keepdims=True)

    return _pallas(
        body, name=name, grid=(n,),
        in_specs=[pl.BlockSpec((tm, D), lambda i: (i, 0)),
                  pl.BlockSpec((CONV_PAD, D), lambda i: (jnp.minimum((i + 1) * per, nxt), 0)),
                  pl.BlockSpec((CONV_PAD, D), lambda i: (jnp.maximum(i * per - 1, 0), 0)),
                  pl.BlockSpec((tm, D), lambda i: (i, 0)),
                  pl.BlockSpec((CONV_PAD, D), lambda i: (0, 0))],
        out_specs=[pl.BlockSpec((tm, D), lambda i: (i, 0)),
                   pl.BlockSpec((CONV_PAD, D), lambda i: (0, 0)),
                   pl.BlockSpec((1, D), lambda i: (0, 0))],
        out_shape=[_sds((S, D), F32), _sds((CONV_PAD, D), F32), _sds((1, D), F32)],
        scratch_shapes=[pltpu.VMEM((tm + CONV_PAD, D), F32), pltpu.VMEM((tm + CONV_PAD, D), F32),
                        pltpu.VMEM((8 * CONV_PAD, D), F32), pltpu.VMEM((8, D), F32)],
        compiler_params=_params(("arbitrary",)),
    )(dy, dy, u, u, w)


def _tri_dot(tri, x):
    x1 = x.astype(BF16)
    r1 = x - x1.astype(F32)
    x2 = r1.astype(BF16)
    x3 = (r1 - x2.astype(F32)).astype(BF16)
    d = lambda v: jnp.dot(tri, v, preferred_element_type=F32)
    return d(x1) + d(x2) + d(x3)


def _log_sigmoid(x):
    return jnp.minimum(x, 0.0) - jnp.log(1.0 + jnp.exp(-jnp.abs(x)))


def _gate_cumsum(fl, bf, *, tm=256, name):
    S, W = fl.shape
    tm = min(tm, S)

    def body(fl_ref, bf_ref, c_ref, carry):
        i = pl.program_id(0)

        @pl.when(i == 0)
        def _():
            carry[...] = jnp.zeros(carry.shape, F32)

        x = _log_sigmoid(fl_ref[...] + bf_ref[...])
        row = lax.broadcasted_iota(jnp.int32, (tm, tm), 0)
        col = lax.broadcasted_iota(jnp.int32, (tm, tm), 1)
        tri = jnp.where(row >= col, 1.0, 0.0).astype(BF16)
        cs = _tri_dot(tri, x) + carry[0:1, :]
        c_ref[...] = cs
        carry[...] = jnp.broadcast_to(cs[tm - 1:tm, :], carry.shape)

    return _pallas(
        body, name=name, grid=(S // tm,),
        in_specs=[pl.BlockSpec((tm, W), lambda i: (i, 0)), pl.BlockSpec((1, W), lambda i: (0, 0))],
        out_specs=pl.BlockSpec((tm, W), lambda i: (i, 0)),
        out_shape=_sds((S, W), F32),
        scratch_shapes=[pltpu.VMEM((8, W), F32)],
        compiler_params=_params(("arbitrary",)),
    )(fl, bf)


def _gate_cumsum_bwd(sums, fl, bf, *, tm=256, name):
    S, W = fl.shape
    tm = min(tm, S)
    n = S // tm
    ns = len(sums)
    assert ns % 2 == 0

    def body(*refs):
        sum_refs = refs[:ns]
        fl_ref, bf_ref, o_ref, s_ref, carry = refs[ns:]
        i = pl.program_id(0)

        @pl.when(i == 0)
        def _():
            carry[...] = jnp.zeros(carry.shape, F32)
            s_ref[...] = jnp.zeros(s_ref.shape, F32)

        dc = sum_refs[0][...] - sum_refs[1][...]
        for a in range(2, ns, 2):
            dc = dc + (sum_refs[a][...] - sum_refs[a + 1][...])
        row = lax.broadcasted_iota(jnp.int32, (tm, tm), 0)
        col = lax.broadcasted_iota(jnp.int32, (tm, tm), 1)
        tri = jnp.where(col >= row, 1.0, 0.0).astype(BF16)
        rs = _tri_dot(tri, dc) + carry[0:1, :]
        carry[...] = jnp.broadcast_to(rs[0:1, :], carry.shape)
        dfl = rs * _sigmoid(-(fl_ref[...] + bf_ref[...]))
        o_ref[...] = dfl
        s_ref[...] += _colsum(dfl)

    rev = lambda i: (n - 1 - i, 0)
    return _pallas(
        body, name=name, grid=(n,),
        in_specs=[pl.BlockSpec((tm, W), rev)] * (ns + 1) + [pl.BlockSpec((1, W), lambda i: (0, 0))],
        out_specs=[pl.BlockSpec((tm, W), rev), pl.BlockSpec((1, W), lambda i: (0, 0))],
        out_shape=[_sds((S, W), F32), _sds((1, W), F32)],
        scratch_shapes=[pltpu.VMEM((8, W), F32)],
        compiler_params=_params(("arbitrary",)),
    )(*sums, fl, bf)


def _tri_tables(nb, by_query):
    ii, jj = [], []
    if by_query:
        for i in range(nb):
            for j in range(i + 1):
                ii.append(i)
                jj.append(j)
    else:
        for j in range(nb):
            for i in range(j, nb):
                ii.append(i)
                jj.append(j)
    return jnp.asarray(np.array(ii, np.int32)), jnp.asarray(np.array(jj, np.int32))


def _rep(v, t):
    return jnp.tile(v, (1, t // LANES))


def _attn_fwd(q, kv, ck, *, tb=512, name):
    S, D = q.shape
    HP = D // LANES
    T = min(tb, S)
    nb = S // T
    it, jt = _tri_tables(nb, True)

    def body(it_ref, jt_ref, q_ref, k_ref, v_ref, ck_ref, o_ref, lse_ref, m_s, l_s, acc_s):
        s_id = pl.program_id(1)
        i, j = it_ref[s_id], jt_ref[s_id]
        lane = lax.broadcasted_iota(jnp.int32, (T, LANES), 1)
        head0 = lane < HEAD_DIM

        @pl.when(j == 0)
        def _():
            m_s[...] = jnp.full(m_s.shape, NEG_BIG, F32)
            l_s[...] = jnp.zeros(l_s.shape, F32)
            acc_s[...] = jnp.zeros(acc_s.shape, F32)

        def step(masked):
            qv, kvv, vv = q_ref[...], k_ref[...], v_ref[...]
            alphas, pvs = [], []
            for h in range(2):
                hm = head0 if h == 0 else jnp.logical_not(head0)
                qm = jnp.where(hm, qv, jnp.zeros_like(qv))
                s = lax.dot_general(qm, kvv, _DIMS["nt"], preferred_element_type=F32) - ck_ref[h:h + 1, :]
                if masked:
                    row = lax.broadcasted_iota(jnp.int32, (T, T), 0)
                    col = lax.broadcasted_iota(jnp.int32, (T, T), 1)
                    s = jnp.where(row >= col, s, NEG_BIG)
                m_prev = m_s[h]
                m_new = jnp.maximum(m_prev, jnp.max(s, axis=1, keepdims=True))
                p = jnp.exp(s - _rep(m_new, T))
                alpha = jnp.exp(m_prev - m_new)
                l_s[h] = alpha * l_s[h] + jnp.sum(p, axis=1, keepdims=True)
                m_s[h] = m_new
                alphas.append(alpha)
                pvs.append(jnp.dot(p.astype(BF16), vv, preferred_element_type=F32))
            acc_s[...] = jnp.where(head0, alphas[0], alphas[1]) * acc_s[...] + jnp.where(head0, pvs[0], pvs[1])

        @pl.when(j < i)
        def _():
            step(False)

        @pl.when(j == i)
        def _():
            step(True)
            o_ref[...] = acc_s[...] / jnp.where(head0, l_s[0], l_s[1])
            lse_ref[0] = m_s[0] + jnp.log(l_s[0])
            lse_ref[1] = m_s[1] + jnp.log(l_s[1])

    grid_spec = pltpu.PrefetchScalarGridSpec(
        num_scalar_prefetch=2, grid=(HP, it.shape[0]),
        in_specs=[pl.BlockSpec((T, LANES), lambda h, s, it, jt: (it[s], h)),
                  pl.BlockSpec((T, LANES), lambda h, s, it, jt: (jt[s], h)),
                  pl.BlockSpec((T, LANES), lambda h, s, it, jt: (jt[s], HP + h)),
                  pl.BlockSpec((None, 8, T), lambda h, s, it, jt: (h, 0, jt[s]))],
        out_specs=[pl.BlockSpec((T, LANES), lambda h, s, it, jt: (it[s], h)),
                   pl.BlockSpec((2, T, LANES), lambda h, s, it, jt: (h, it[s], 0))],
        scratch_shapes=[pltpu.VMEM((2, T, LANES), F32), pltpu.VMEM((2, T, LANES), F32), pltpu.VMEM((T, LANES), F32)],
    )
    return _pallas(
        body, name=name, grid_spec=grid_spec,
        out_shape=[_sds((S, D), F32), _sds((2 * HP, S, LANES), F32)],
        compiler_params=_params(("parallel", "arbitrary")),
    )(it, jt, q, kv, kv, ck)


def _attn_bwd(q, kv, o, do, lse, ck, *, tb=512, name):
    S, D = q.shape
    HP = D // LANES
    T = min(tb, S)
    nb = S // T
    it, jt = _tri_tables(nb, False)

    def body(it_ref, jt_ref, q_ref, k_ref, v_ref, o_ref, do_ref, lse_ref, ck_ref, dq_ref, drs_ref, dk_ref, dv_ref, dcs_ref):
        s_id = pl.program_id(1)
        i, j = it_ref[s_id], jt_ref[s_id]
        lane = lax.broadcasted_iota(jnp.int32, (T, LANES), 1)
        head0 = lane < HEAD_DIM

        @pl.when(s_id == 0)
        def _():
            dq_ref[...] = jnp.zeros(dq_ref.shape, F32)
            drs_ref[...] = jnp.zeros(drs_ref.shape, F32)

        @pl.when(i == j)
        def _():
            dk_ref[...] = jnp.zeros(dk_ref.shape, F32)
            dv_ref[...] = jnp.zeros(dv_ref.shape, F32)
            dcs_ref[...] = jnp.zeros(dcs_ref.shape, F32)

        def step(masked):
            qv, kvv, vv = q_ref[...], k_ref[...], v_ref[...]
            dob = do_ref[...].astype(BF16)
            prod = dob.astype(F32) * o_ref[...]
            one = jnp.ones_like(qv)
            dqs, dks, dvs = [], [], []
            for h in range(2):
                hm = head0 if h == 0 else jnp.logical_not(head0)
                qm = jnp.where(hm, qv, jnp.zeros_like(qv))
                dom = jnp.where(hm, dob, jnp.zeros_like(dob))
                s = lax.dot_general(qm, kvv, _DIMS["nt"], preferred_element_type=F32) - ck_ref[h:h + 1, :]
                if masked:
                    row = lax.broadcasted_iota(jnp.int32, (T, T), 0)
                    col = lax.broadcasted_iota(jnp.int32, (T, T), 1)
                    s = jnp.where(row >= col, s, NEG_BIG)
                p = jnp.exp(s - _rep(lse_ref[h], T))
                dp = lax.dot_general(dom, vv, _DIMS["nt"], preferred_element_type=F32)
                delta = jnp.sum(jnp.where(hm, prod, 0.0), axis=1, keepdims=True)
                pb, dsb = p.astype(BF16), (p * (dp - delta)).astype(BF16)
                dvs.append(lax.dot_general(pb, dob, _DIMS["tn"], preferred_element_type=F32))
                dks.append(lax.dot_general(dsb, jnp.where(hm, qv, one), _DIMS["tn"], preferred_element_type=F32))
                dqs.append(jnp.dot(dsb, jnp.where(hm, kvv, one), preferred_element_type=F32))
            dv_ref[...] += jnp.where(head0, dvs[0], dvs[1])
            dk_ref[...] += jnp.where(head0, dks[0], dks[1])
            dcs_ref[...] += jnp.where(head0, dks[1], dks[0])
            rows = pl.ds(pl.multiple_of(i * T, T), T)
            dq_ref[rows, :] += jnp.where(head0, dqs[0], dqs[1])
            drs_ref[rows, :] += jnp.where(head0, dqs[1], dqs[0])

        @pl.when(i > j)
        def _():
            step(False)

        @pl.when(i == j)
        def _():
            step(True)

    by_q = lambda h, s, it, jt: (it[s], h)
    by_k = lambda h, s, it, jt: (jt[s], h)
    whole = lambda h, s, it, jt: (0, h)
    grid_spec = pltpu.PrefetchScalarGridSpec(
        num_scalar_prefetch=2, grid=(HP, it.shape[0]),
        in_specs=[pl.BlockSpec((T, LANES), by_q),
                  pl.BlockSpec((T, LANES), by_k),
                  pl.BlockSpec((T, LANES), lambda h, s, it, jt: (jt[s], HP + h)),
                  pl.BlockSpec((T, LANES), by_q),
                  pl.BlockSpec((T, LANES), by_q),
                  pl.BlockSpec((2, T, LANES), lambda h, s, it, jt: (h, it[s], 0)),
                  pl.BlockSpec((None, 8, T), lambda h, s, it, jt: (h, 0, jt[s]))],
        out_specs=[pl.BlockSpec((S, LANES), whole), pl.BlockSpec((S, LANES), whole),
                   pl.BlockSpec((T, LANES), by_k), pl.BlockSpec((T, LANES), by_k), pl.BlockSpec((T, LANES), by_k)],
        scratch_shapes=[],
    )
    return _pallas(
        body, name=name, grid_spec=grid_spec,
        out_shape=[_sds((S, D), F32)] * 5,
        compiler_params=_params(("parallel", "arbitrary")),
    )(it, jt, q, kv, kv, o, do, lse, ck)


ANY = pl.BlockSpec(memory_space=pl.ANY)


def _coords():
    x, y, c = lax.axis_index("x"), lax.axis_index("y"), lax.axis_index("c")
    return x, y, c


def _remote(src, dst, send_sems, recv_sems, k, to):
    return pltpu.make_async_remote_copy(src_ref=src, dst_ref=dst, send_sem=send_sems.at[k], recv_sem=recv_sems.at[k],
                                        device_id=to, device_id_type=MESH)


def _all_gather_shards(pack, *, name):
    R, C = pack.shape
    assert R % 4 == 0
    H, Q = R // 2, R // 4

    def body(in_ref, out_ref, send_sems, recv_sems, local_sem):
        x, y, c = _coords()
        me, sib = (x, y, c), (x, y, 1 - c)
        xn, yn = (1 - x, y, c), (x, 1 - y, c)
        s, sx, sy, sd = 2 * x + y, 2 * (1 - x) + y, 2 * x + 1 - y, 2 * (1 - x) + 1 - y
        half = pl.ds(c * H, H)
        other = pl.ds((1 - c) * H, H)
        q0 = pl.ds(c * H, Q)
        q1 = pl.ds(c * H + Q, Q)
        rc = functools.partial(_remote, send_sems=send_sems, recv_sems=recv_sems)

        mine = pltpu.make_async_copy(in_ref, out_ref.at[s], local_sem)
        mine.start()
        sends = [rc(in_ref.at[half], out_ref.at[s, half], k=0, to=xn),
                 rc(in_ref.at[half], out_ref.at[s, half], k=1, to=yn)]
        sends[0].start()
        sends[1].start()
        rc(in_ref.at[half], out_ref.at[sx, half], k=0, to=me).wait_recv()
        sends.append(rc(out_ref.at[sx, q0], out_ref.at[sx, q0], k=2, to=yn))
        sends[-1].start()
        sends.append(rc(out_ref.at[sx, half], out_ref.at[sx, half], k=4, to=sib))
        sends[-1].start()
        rc(in_ref.at[half], out_ref.at[sy, half], k=1, to=me).wait_recv()
        sends.append(rc(out_ref.at[sy, q1], out_ref.at[sy, q1], k=3, to=xn))
        sends[-1].start()
        sends.append(rc(out_ref.at[sy, half], out_ref.at[sy, half], k=5, to=sib))
        sends[-1].start()
        rc(out_ref.at[sd, q0], out_ref.at[sd, q0], k=2, to=me).wait_recv()
        rc(out_ref.at[sd, q1], out_ref.at[sd, q1], k=3, to=me).wait_recv()
        sends.append(rc(out_ref.at[sd, half], out_ref.at[sd, half], k=6, to=sib))
        sends[-1].start()
        for k, sh in ((4, sx), (5, sy), (6, sd)):
            rc(out_ref.at[sh, other], out_ref.at[sh, other], k=k, to=me).wait_recv()
        for cp in sends:
            cp.wait_send()
        mine.wait()

    return _pallas(
        body, name=name, in_specs=[ANY], out_specs=ANY,
        out_shape=_sds((N_CHIPS, R, C), pack.dtype),
        scratch_shapes=[pltpu.SemaphoreType.DMA((7,)), pltpu.SemaphoreType.DMA((7,)), pltpu.SemaphoreType.DMA],
    )(pack)


def _rs_pair(g, *, name):
    n, R, C = g.shape
    H = R // 2

    def body(g_ref, land_ref, send_sems, recv_sems):
        x, y, c = _coords()
        other = pl.ds((1 - c) * H, H)
        cps = [_remote(g_ref.at[sh, other], land_ref.at[sh], send_sems, recv_sems, sh, (x, y, 1 - c)) for sh in range(n)]
        for cp in cps:
            cp.start()
        for cp in cps:
            cp.wait_recv()
        for cp in cps:
            cp.wait_send()

    return _pallas(
        body, name=name, in_specs=[ANY], out_specs=ANY, out_shape=_sds((n, H, C), g.dtype),
        scratch_shapes=[pltpu.SemaphoreType.DMA((n,)), pltpu.SemaphoreType.DMA((n,))],
    )(g)


def _rs_quarters(p, *, name):
    n, H, C = p.shape
    Q = H // 2

    def body(p_ref, la_ref, lb_ref, send_sems, recv_sems):
        x, y, c = _coords()
        sd = 2 * (1 - x) + 1 - y
        a = _remote(p_ref.at[sd, pl.ds(0, Q)], la_ref, send_sems, recv_sems, 0, (x, 1 - y, c))
        b = _remote(p_ref.at[sd, pl.ds(Q, Q)], lb_ref, send_sems, recv_sems, 1, (1 - x, y, c))
        a.start()
        b.start()
        a.wait_recv()
        b.wait_recv()
        a.wait_send()
        b.wait_send()

    return _pallas(
        body, name=name, in_specs=[ANY], out_specs=[ANY, ANY],
        out_shape=[_sds((Q, C), p.dtype), _sds((Q, C), p.dtype)],
        scratch_shapes=[pltpu.SemaphoreType.DMA((2,)), pltpu.SemaphoreType.DMA((2,))],
    )(p)


def _rs_halves(px, py, *, name):
    H, C = px.shape

    def body(px_ref, py_ref, la_ref, lb_ref, send_sems, recv_sems):
        x, y, c = _coords()
        a = _remote(px_ref, la_ref, send_sems, recv_sems, 0, (1 - x, y, c))
        b = _remote(py_ref, lb_ref, send_sems, recv_sems, 1, (x, 1 - y, c))
        a.start()
        b.start()
        a.wait_recv()
        b.wait_recv()
        a.wait_send()
        b.wait_send()

    return _pallas(
        body, name=name, in_specs=[ANY, ANY], out_specs=[ANY, ANY],
        out_shape=[_sds((H, C), px.dtype), _sds((H, C), px.dtype)],
        scratch_shapes=[pltpu.SemaphoreType.DMA((2,)), pltpu.SemaphoreType.DMA((2,))],
    )(px, py)


def _rs_join(rh, *, name):
    H, C = rh.shape

    def body(rh_ref, out_ref, send_sems, recv_sems, local_sem):
        x, y, c = _coords()
        half = pl.ds(c * H, H)
        other = pl.ds((1 - c) * H, H)
        mine = pltpu.make_async_copy(rh_ref, out_ref.at[half], local_sem)
        mine.start()
        cp = _remote(rh_ref, out_ref.at[half], send_sems, recv_sems, 0, (x, y, 1 - c))
        cp.start()
        _remote(rh_ref, out_ref.at[other], send_sems, recv_sems, 0, (x, y, c)).wait_recv()
        cp.wait_send()
        mine.wait()

    return _pallas(
        body, name=name, in_specs=[ANY], out_specs=ANY, out_shape=_sds((2 * H, C), rh.dtype),
        scratch_shapes=[pltpu.SemaphoreType.DMA((1,)), pltpu.SemaphoreType.DMA((1,)), pltpu.SemaphoreType.DMA],
    )(rh)


def _tile_add(ins_specs, arrays, out_rows, C, scalars, body_fn, *, tm, name):
    grid_spec = pltpu.PrefetchScalarGridSpec(
        num_scalar_prefetch=1, grid=(out_rows // tm,), in_specs=ins_specs,
        out_specs=pl.BlockSpec((tm, C), lambda i, sc: (i, 0)), scratch_shapes=[])

    def body(sc_ref, *refs):
        refs[-1][...] = body_fn(*[r[...] for r in refs[:-1]])

    return _pallas(body, name=name, grid_spec=grid_spec, out_shape=_sds((out_rows, C), F32),
                   compiler_params=_params(("arbitrary",)))(scalars, *arrays)


def _reduce_scatter(g, *, name):
    n, R, C = g.shape
    H, Q = R // 2, R // 4
    tm = RS_ROW_MULT // 4
    assert Q % tm == 0, (R, tm)
    x, y, c = _coords()
    sx, sy, s = 2 * (1 - x) + y, 2 * x + 1 - y, 2 * x + y
    sc = jnp.stack([c, sx, sy, s]).astype(jnp.int32)
    hb, qb = H // tm, Q // tm

    land = _rs_pair(g, name=name + "_pair")
    p = _tile_add(
        [pl.BlockSpec((None, tm, C), lambda i, sc: (i // hb, sc[0] * hb + i % hb, 0)),
         pl.BlockSpec((None, tm, C), lambda i, sc: (i // hb, i % hb, 0))],
        [g, land], n * H, C, sc, lambda a, b: a + b, tm=tm, name=name + "_add0").reshape(n, H, C)
    la, lb = _rs_quarters(p, name=name + "_quarters")
    zero = jnp.zeros((Q, C), F32)
    px = _tile_add(
        [pl.BlockSpec((None, tm, C), lambda i, sc: (sc[1], i, 0)),
         pl.BlockSpec((tm, C), lambda i, sc: (i % (2 * qb), 0))],
        [p, jnp.concatenate([la, zero], axis=0)], H, C, sc, lambda a, b: a + b, tm=tm, name=name + "_add1x")
    py = _tile_add(
        [pl.BlockSpec((None, tm, C), lambda i, sc: (sc[2], i, 0)),
         pl.BlockSpec((tm, C), lambda i, sc: (i % (2 * qb), 0))],
        [p, jnp.concatenate([zero, lb], axis=0)], H, C, sc, lambda a, b: a + b, tm=tm, name=name + "_add1y")
    fa, fb = _rs_halves(px, py, name=name + "_halves")
    rh = _tile_add(
        [pl.BlockSpec((None, tm, C), lambda i, sc: (sc[3], i, 0)),
         pl.BlockSpec((tm, C), lambda i, sc: (i, 0)), pl.BlockSpec((tm, C), lambda i, sc: (i, 0))],
        [p, fa, fb], H, C, sc, lambda a, b, d: (a + b) + d, tm=tm, name=name + "_add2")
    return _rs_join(rh, name=name + "_join")


def _all_reduce_small(v, *, name):
    M, N = v.shape

    def body(x_ref, out_ref, send_sems, recv_sems, local_sem):
        x, y, c = _coords()
        me, sibling = (x, y, c), (x, y, 1 - c)
        chips = [(1 - x, y), (x, 1 - y), (1 - x, 1 - y)]

        def rows(px, py, pc):
            return out_ref.at[pl.ds((4 * px + 2 * py + pc) * M, M), :]

        def copy(k, block, to, src=None):
            return pltpu.make_async_remote_copy(
                src_ref=rows(*block) if src is None else src, dst_ref=rows(*block),
                send_sem=send_sems.at[k], recv_sem=recv_sems.at[k], device_id=to, device_id_type=MESH)

        mine = pltpu.make_async_copy(x_ref, rows(*me), local_sem)
        mine.start()
        first = [copy(0, me, sibling, src=x_ref)]
        first += [copy(1 + j, me, (*chip, c), src=x_ref) for j, chip in enumerate(chips)]
        for cp in first:
            cp.start()
        passed = [copy(4 + j, (*chip, c), sibling) for j, chip in enumerate(chips)]
        for j, chip in enumerate(chips):
            copy(1 + j, (*chip, c), me).wait_recv()
            passed[j].start()
        copy(0, sibling, me).wait_recv()
        for j, chip in enumerate(chips):
            copy(4 + j, (*chip, 1 - c), me).wait_recv()
        for cp in first + passed:
            cp.wait_send()
        mine.wait()

    gathered = _pallas(
        body, name=name + "_gather",
        out_shape=_sds((N_DEV * M, N), F32),
        in_specs=[pl.BlockSpec(memory_space=pltpu.VMEM)],
        out_specs=pl.BlockSpec(memory_space=pltpu.VMEM),
        scratch_shapes=[pltpu.SemaphoreType.DMA((7,)), pltpu.SemaphoreType.DMA((7,)), pltpu.SemaphoreType.DMA],
    )(v)

    def sum_body(g_ref, o_ref):
        acc = g_ref[0:M, :]
        for d in range(1, N_DEV):
            acc = acc + g_ref[d * M:(d + 1) * M, :]
        o_ref[...] = acc

    return _pallas(sum_body, name=name + "_sum", out_shape=_sds((M, N), F32))(gathered)


MATS = [("conv_w_pw1", 2), ("conv_w_pw2", 1), ("attn_w_q", 1), ("attn_w_o", 1), ("ffn_w1", 2), ("ffn_w2", 1),
        ("ple_w_gate", 1), ("ple_w_proj", 2), ("w_kvf", 1)]
VECS = [("conv_b_pw1", 1), ("conv_w_dw", 2), ("conv_b_dw", 1), ("conv_ln_g", 1), ("conv_ln_b", 1), ("conv_b_pw2", 1)]
REPL = ["mix_norm", "ffn_norm", "ple_norm", "kv_norm", "final_norm", "b_f"]
WEIGHTS = ["mix_norm", "conv_w_pw1", "conv_b_pw1", "conv_w_dw", "conv_b_dw", "conv_ln_g", "conv_ln_b", "conv_w_pw2",
           "conv_b_pw2", "kv_norm", "w_kvf", "b_f", "attn_w_q", "attn_w_o", "ffn_norm", "ffn_w1", "ffn_w2", "ple_norm",
           "ple_w_gate", "ple_w_proj", "final_norm"]


def _round_up(n, m):
    return -(-n // m) * m


def _to_rows(t, C, mult):
    flat = t.reshape(-1)
    rows = _round_up(_round_up(flat.shape[0], C) // C, mult)
    flat = jnp.pad(flat, (0, rows * C - flat.shape[0]))
    return flat.reshape(rows, C)


def _pack(tensors, C, mult, total_mult):
    parts = [_to_rows(t, C, mult) for t in tensors]
    rows = sum(p.shape[0] for p in parts)
    pad = _round_up(rows, total_mult) - rows
    if pad:
        parts.append(jnp.zeros((pad, C), parts[0].dtype))
    return jnp.concatenate(parts, axis=0)


def _row_counts(shapes, C, mult):
    return [_round_up(_round_up(int(np.prod(s)), C) // C, mult) for s in shapes]


def _unpack(packed, shapes, C, mult):
    outs, r0 = [], 0
    lead = packed.shape[:-2]
    for shp, nr in zip(shapes, _row_counts(shapes, C, mult)):
        n = int(np.prod(shp))
        seg = packed[..., r0:r0 + nr, :].reshape(lead + (nr * C,))[..., :n]
        outs.append(seg.reshape(lead + tuple(shp)))
        r0 += nr
    return outs


def _unshard(t, axis):
    return jnp.concatenate([t[s] for s in range(N_CHIPS)], axis=axis)


def _shards(t, axis):
    return jnp.split(t, N_CHIPS, axis=axis)


def _local_step(x, p, tgt, w):
    S, D = x.shape
    L = p.shape[0]
    NA = w["conv_w_pw1"].shape[0]
    H = w["b_f"].shape[0]
    HP = D // LANES
    row = lambda v: v.reshape(1, -1)
    act = lambda dt, n=D: _sds((S, n), dt)
    g = {}

    def rms(hh, gain, name):
        return _rows(_rms_fwd, [hh], [row(gain)], [act(BF16)], name=name)[0]

    saved = []
    h = x
    kv = ck = fl = nkv = h_kv = None
    bfp = jnp.pad(w["b_f"], (0, LANES - H)).reshape(1, LANES)
    wk = w["w_kvf"][:, :D]
    wv = w["w_kvf"][:, D:2 * D]
    wkv = w["w_kvf"][:, :2 * D]
    wf = jnp.pad(w["w_kvf"][:, 2 * D:], ((0, 0), (0, LANES - H)))
    for i in range(L):
        sv = {"h0": h}
        hn = rms(h, w["mix_norm"][i], "rms_mix")
        sv["hn"] = hn
        if i < NA:
            u = _mm(hn, w["conv_w_pw1"][i], extras=[row(w["conv_b_pw1"][i])], epi=lambda acc, b: acc + b, name="mm_pw1")
            glu = _rows(_glu_fwd, [u], [], [act(F32)], name="glu_fwd")[0]
            wdw = jnp.pad(w["conv_w_dw"][i], ((0, CONV_PAD - CONV_WIDTH), (0, 0)))
            cv = _dwconv_fwd(glu, wdw, row(w["conv_b_dw"][i]), name="dwconv_fwd")
            sw = _rows(_ln_silu_fwd, [cv], [row(w["conv_ln_g"][i]), row(w["conv_ln_b"][i])], [act(BF16)], name="ln_silu_fwd")[0]
            h1 = _mm(sw, w["conv_w_pw2"][i], extras=[row(w["conv_b_pw2"][i]), h], epi=lambda acc, b, r: acc + b + r, name="mm_pw2")
            sv.update(u=u, glu=glu, cv=cv, sw=sw, wdw=wdw)
        else:
            if i == NA:
                h_kv = h
                nkv = rms(h, w["kv_norm"], "rms_kv")
                kv = _mm(nkv, wkv, out_dtype=BF16, name="mm_kv")
                fl = _mm(nkv, wf, name="mm_f")
                c = _gate_cumsum(fl, bfp, name="gate_cumsum")
                ck = jnp.pad(c[:, :H].T.reshape(HP, 2, S), ((0, 0), (0, 6), (0, 0)))
            j = i - NA
            q = _mm(hn, w["attn_w_q"][j], epi=lambda acc: acc * (HEAD_DIM ** -0.5), out_dtype=BF16, name="mm_q")
            o, lse = _attn_fwd(q, kv, ck, name="attn_fwd")
            h1 = _mm(o, w["attn_w_o"][j], extras=[h], epi=lambda acc, r: acc + r, name="mm_o")
            sv.update(q=q, o=o, lse=lse)
        sv["h1"] = h1
        hn2 = rms(h1, w["ffn_norm"][i], "rms_ffn")
        z = _mm(hn2, w["ffn_w1"][i], name="mm_ffn1")
        f = _rows(_relu2, [z], [], [act(BF16, z.shape[1])], name="relu2")[0]
        h2 = _mm(f, w["ffn_w2"][i], extras=[h1], epi=lambda acc, r: acc + r, name="mm_ffn2")
        n3 = rms(h2, w["ple_norm"][i], "rms_ple")
        zg = _mm(n3, w["ple_w_gate"][i], name="mm_gate")
        pp = _mm(p[i], w["ple_w_proj"][i], name="mm_proj")
        h = _rows(_ple_fwd, [h2, zg, pp], [], [act(F32)], name="ple_fwd")[0]
        sv.update(hn2=hn2, z=z, f=f, h2=h2, n3=n3, zg=zg, pp=pp)
        saved.append(sv)

    dh, err2, g_final = _rows(_final_fn, [h, tgt], [row(w["final_norm"])], [act(F32)], [_sds((1, D), F32), _sds((1, D), F32)],
                              name="final")
    loss = 0.5 * jnp.sum(err2) / D
    g["final_norm"] = g_final.reshape(-1)

    red = _sds((1, D), F32)
    stack = {k: [None] * n for k, n in (("mix_norm", L), ("ffn_norm", L), ("ple_norm", L), ("ffn_w1", L), ("ffn_w2", L),
                                        ("ple_w_gate", L), ("ple_w_proj", L), ("conv_w_pw1", NA), ("conv_b_pw1", NA),
                                        ("conv_w_dw", NA), ("conv_b_dw", NA), ("conv_ln_g", NA), ("conv_ln_b", NA),
                                        ("conv_w_pw2", NA), ("conv_b_pw2", NA), ("attn_w_q", L - NA), ("attn_w_o", L - NA))}
    dk_sum = dv_sum = None
    dcks = []
    for i in reversed(range(L)):
        sv = saved[i]
        dzg, dpp = _rows(_ple_bwd, [dh, sv["zg"], sv["pp"]], [], [act(BF16), act(BF16)], name="ple_bwd")
        stack["ple_w_proj"][i] = _mm(p[i], dpp, mode="tn", tk=512, name="mm_dproj")
        stack["ple_w_gate"][i] = _mm(sv["n3"], dzg, mode="tn", tk=512, name="mm_dgate")
        dn3 = _mm(dzg, w["ple_w_gate"][i], mode="nt", name="mm_dn3")
        dh, dgain = _rows(_rms_bwd, [dn3, sv["h2"], dh], [row(w["ple_norm"][i])], [act(F32)], [red], name="rms_bwd")
        stack["ple_norm"][i] = dgain.reshape(-1)
        dz = _mm(dh, w["ffn_w2"][i], mode="nt", extras=[sv["z"]], epi=lambda acc, zz: acc * (2.0 * jnp.maximum(zz, 0.0)),
                 out_dtype=BF16, name="mm_dz")
        stack["ffn_w2"][i] = _mm(sv["f"], dh, mode="tn", tk=512, name="mm_dffn2")
        stack["ffn_w1"][i] = _mm(sv["hn2"], dz, mode="tn", tk=512, name="mm_dffn1")
        dhn2 = _mm(dz, w["ffn_w1"][i], mode="nt", name="mm_dhn2")
        if i < NA:
            dh, dgain, dbias = _rows(_rms_bwd_bias, [dhn2, sv["h1"], dh], [row(w["ffn_norm"][i])], [act(F32)], [red, red],
                                     name="rms_bwd_bias")
            stack["conv_b_pw2"][i] = dbias.reshape(-1)
        else:
            dh, dgain = _rows(_rms_bwd, [dhn2, sv["h1"], dh], [row(w["ffn_norm"][i])], [act(F32)], [red], name="rms_bwd")
        stack["ffn_norm"][i] = dgain.reshape(-1)
        if i < NA:
            dsw = _mm(dh, w["conv_w_pw2"][i], mode="nt", name="mm_dsw")
            stack["conv_w_pw2"][i] = _mm(sv["sw"], dh, mode="tn", tk=512, name="mm_dpw2")
            dcv, dlg, dlb = _rows(_ln_silu_bwd, [dsw, sv["cv"]], [row(w["conv_ln_g"][i]), row(w["conv_ln_b"][i])], [act(F32)],
                                  [red, red], name="ln_silu_bwd")
            stack["conv_ln_g"][i], stack["conv_ln_b"][i] = dlg.reshape(-1), dlb.reshape(-1)
            dglu, dwdw, dbdw = _dwconv_bwd(dcv, sv["glu"], sv["wdw"], name="dwconv_bwd")
            stack["conv_w_dw"][i], stack["conv_b_dw"][i] = dwdw[:CONV_WIDTH], dbdw.reshape(-1)
            du, dbu = _rows(_glu_bwd, [dglu, sv["u"]], [], [act(BF16, 2 * D)], [_sds((1, 2 * D), F32)], name="glu_bwd")
            stack["conv_b_pw1"][i] = dbu.reshape(-1)
            stack["conv_w_pw1"][i] = _mm(sv["hn"], du, mode="tn", tk=512, name="mm_dpw1")
            dhn = _mm(du, w["conv_w_pw1"][i], mode="nt", name="mm_dhn_a")
        else:
            j = i - NA
            do = _mm(dh, w["attn_w_o"][j], mode="nt", name="mm_do")
            stack["attn_w_o"][j] = _mm(sv["o"], dh, mode="tn", tk=512, name="mm_dwo")
            dq, drs, dk, dv, dcs = _attn_bwd(sv["q"], kv, sv["o"], do, sv["lse"], ck, name="attn_bwd")
            scale = lambda acc: acc * (HEAD_DIM ** -0.5)
            stack["attn_w_q"][j] = _mm(sv["hn"], dq, mode="tn", tk=512, epi=scale, name="mm_dwq")
            dhn = _mm(dq, w["attn_w_q"][j], mode="nt", epi=scale, name="mm_dhn_b")
            pick = lambda t: jnp.pad(t.reshape(S, HP, 2, HEAD_DIM)[:, :, ::-1, 0].reshape(S, H), ((0, 0), (0, LANES - H)))
            dcks += [pick(drs), pick(dcs)]
            if dk_sum is None:
                dk_sum, dv_sum = dk, dv
            else:
                dk_sum = _rows(_add2, [dk_sum, dk], [], [act(F32)], name="add_dk")[0]
                dv_sum = _rows(_add2, [dv_sum, dv], [], [act(F32)], name="add_dk")[0]
        dh, dgain = _rows(_rms_bwd, [dhn, sv["h0"], dh], [row(w["mix_norm"][i])], [act(F32)], [red], name="rms_bwd")
        stack["mix_norm"][i] = dgain.reshape(-1)
        if i == NA:
            dfl, dbf = _gate_cumsum_bwd(dcks, fl, bfp, name="gate_cumsum_bwd")
            g["b_f"] = dbf[0, :H]
            gk = _mm(nkv, dk_sum, mode="tn", tk=512, name="mm_dwk")
            gv = _mm(nkv, dv_sum, mode="tn", tk=512, name="mm_dwk")
            gf = _mm(nkv, dfl, mode="tn", tk=512, name="mm_dwf")
            g["w_kvf"] = jnp.concatenate([gk, gv, gf[:, :H]], axis=1)
            dn = _mm(dk_sum, wk, mode="nt", name="mm_dnk")
            dn = _mm(dv_sum, wv, mode="nt", extras=[dn], epi=lambda acc, r: acc + r, name="mm_dnv")
            dn = _mm(dfl, wf, mode="nt", extras=[dn], epi=lambda acc, r: acc + r, name="mm_dnf")
            dh, dgain = _rows(_rms_bwd, [dn, h_kv, dh], [row(w["kv_norm"])], [act(F32)], [red], name="rms_bwd")
            g["kv_norm"] = dgain.reshape(-1)
    for k, v in stack.items():
        g[k] = jnp.stack(v, axis=0)
    return loss, dh, g


def kernel(x, p, mix_norm, conv_w_pw1, conv_b_pw1, conv_w_dw, conv_b_dw, conv_ln_g, conv_ln_b, conv_w_pw2, conv_b_pw2, kv_norm, w_kvf, b_f, attn_w_q, attn_w_o, ffn_norm, ffn_w1, ffn_w2, ple_norm, ple_w_gate, ple_w_proj, final_norm, loss_target, m_mix_norm, m_conv_w_pw1, m_conv_b_pw1, m_conv_w_dw, m_conv_b_dw, m_conv_ln_g, m_conv_ln_b, m_conv_w_pw2, m_conv_b_pw2, m_kv_norm, m_w_kvf, m_b_f, m_attn_w_q, m_attn_w_o, m_ffn_norm, m_ffn_w1, m_ffn_w2, m_ple_norm, m_ple_w_gate, m_ple_w_proj, m_final_norm, v_mix_norm, v_conv_w_pw1, v_conv_b_pw1, v_conv_w_dw, v_conv_b_dw, v_conv_ln_g, v_conv_ln_b, v_conv_w_pw2, v_conv_b_pw2, v_kv_norm, v_w_kvf, v_b_f, v_attn_w_q, v_attn_w_o, v_ffn_norm, v_ffn_w1, v_ffn_w2, v_ple_norm, v_ple_w_gate, v_ple_w_proj, v_final_norm):
    args = dict(locals())
    wl = {n: args[n] for n in WEIGHTS}
    ml = {n: args["m_" + n] for n in WEIGHTS}
    vl = {n: args["v_" + n] for n in WEIGHTS}
    S, D = x.shape[1], x.shape[2]
    C = D

    mat_shapes = [wl[n].shape for n, _ in MATS]
    vec_shapes = [wl[n].shape for n, _ in VECS]
    mats = _all_gather_shards(_pack([wl[n].astype(BF16) for n, _ in MATS], C, 16, 64), name="ag_mats")
    vecs = _all_gather_shards(_pack([wl[n] for n, _ in VECS], C, 1, 32), name="ag_vecs")
    full = {n: wl[n] for n in REPL}
    for (n, ax), t in zip(MATS, _unpack(mats, mat_shapes, C, 16)):
        full[n] = _unshard(t, ax)
    for (n, ax), t in zip(VECS, _unpack(vecs, vec_shapes, C, 1)):
        full[n] = _unshard(t, ax)

    loss, dx, g = _local_step(x[0], p[:, 0], loss_target[0], full)
    loss = lax.psum(loss, ("x", "y", "c"))

    names = [n for n, _ in MATS] + [n for n, _ in VECS]
    axes = dict(MATS + VECS)
    shard_shapes = [wl[n].shape for n in names]
    per_shard = [[] for _ in range(N_CHIPS)]
    for n in names:
        for s, piece in enumerate(_shards(g[n], axes[n])):
            per_shard[s].append(piece)
    gpack = jnp.stack([_pack(ts, C, 8, RS_ROW_MULT) for ts in per_shard], axis=0)
    gred = _reduce_scatter(gpack, name="rs")
    gl = dict(zip(names, _unpack(gred, shard_shapes, C, 8)))

    rep_shapes = [wl[n].shape for n in REPL]
    rpack = _pack([g[n] for n in REPL], C, 1, 8)
    for n, t in zip(REPL, _unpack(_all_reduce_small(rpack, name="ar"), rep_shapes, C, 1)):
        gl[n] = t

    grads, deltas, new_m, new_v = {}, {}, {}, {}
    rep_w, rep_m, rep_v = (_pack([d[n] for n in REPL], C, 1, 8) for d in (wl, ml, vl))
    rep_out = _adamw(rep_w, rpack_like(gl, rep_shapes, C), rep_m, rep_v, "adamw_rep")
    for dst, packed in zip((grads, deltas, new_m, new_v), rep_out):
        for n, t in zip(REPL, _unpack(packed, rep_shapes, C, 1)):
            dst[n] = t
    for n in names:
        res = _adamw(wl[n], gl[n], ml[n], vl[n], "adamw_" + n)
        for dst, t in zip((grads, deltas, new_m, new_v), res):
            dst[n] = t
    out = [loss, dx[None]]
    for d in (grads, deltas, new_m, new_v):
        out += [d[n] for n in WEIGHTS]
    return tuple(out)


def rpack_like(gl, rep_shapes, C):
    return _pack([gl[n] for n in REPL], C, 1, 8)
```

```python
import functools

import jax
import jax.numpy as jnp
import numpy as np
from jax import lax
from jax.experimental import pallas as pl
from jax.experimental.pallas import tpu as pltpu

F32 = jnp.float32
BF16 = jnp.bfloat16
MESH = pl.DeviceIdType.MESH

N_CHIPS = 4
N_DEV = 8
HEAD_DIM = 64
LANES = 128
CONV_WIDTH = 31
CONV_PAD = 32
EPS = 1e-6
NEG_BIG = -1e30
VMEM_LIMIT = 56 * 1024 * 1024

RS_ROW_MULT = 1024

ADAM_LR, ADAM_B1, ADAM_B2, ADAM_EPS, ADAM_WD, ADAM_STEP = 0.001, 0.9, 0.999, 1e-08, 0.01, 10


def _pallas(body, **kw):
    return pl.pallas_call(body, **kw)


def _params(sem=None):
    return pltpu.CompilerParams(dimension_semantics=sem, vmem_limit_bytes=VMEM_LIMIT)


def _sds(shape, dtype):
    return jax.ShapeDtypeStruct(tuple(shape), dtype)


_DIMS = {"nn": (((1,), (0,)), ((), ())), "nt": (((1,), (1,)), ((), ())), "tn": (((0,), (0,)), ((), ()))}


def _mm(a, b, *, mode="nn", extras=(), epi=None, out_dtype=F32, reds=0, tm=1024, tn=1024, tk=1024, name):
    if mode == "nn":
        (M, K), (K2, N) = a.shape, b.shape
    elif mode == "nt":
        (M, K), (N, K2) = a.shape, b.shape
    else:
        (K, M), (K2, N) = a.shape, b.shape
    assert K == K2, (name, a.shape, b.shape)
    tm, tn, tk = min(tm, M), min(tn, N), min(tk, K)
    assert M % tm == 0 and N % tn == 0 and K % tk == 0, (name, a.shape, b.shape)
    nk = K // tk
    if mode == "tn":
        a_spec = pl.BlockSpec((tk, tm), lambda i, j, k: (k, i))
    else:
        a_spec = pl.BlockSpec((tm, tk), lambda i, j, k: (i, k))
    if mode == "nt":
        b_spec = pl.BlockSpec((tn, tk), lambda i, j, k: (j, k))
    else:
        b_spec = pl.BlockSpec((tk, tn), lambda i, j, k: (k, j))
    ex_specs = []
    for e in extras:
        if e.shape[0] == 1:
            ex_specs.append(pl.BlockSpec((1, tn), lambda i, j, k: (0, j)))
        else:
            assert e.shape == (M, N), (name, e.shape)
            ex_specs.append(pl.BlockSpec((tm, tn), lambda i, j, k: (i, j)))
    ne = len(extras)
    dims = _DIMS[mode]
    many = isinstance(out_dtype, (list, tuple))
    out_dtypes = list(out_dtype) if many else [out_dtype]
    no = len(out_dtypes)
    assert not reds or tn == N, name

    def body(a_ref, b_ref, *rest):
        ex_refs, o_refs, r_refs = rest[:ne], rest[ne:ne + no], rest[ne + no:ne + no + reds]
        part = lax.dot_general(a_ref[...].astype(BF16), b_ref[...].astype(BF16), dims, preferred_element_type=F32)
        i = pl.program_id(0)

        def finish(acc):
            res = epi(acc, *[r[...] for r in ex_refs]) if epi is not None else acc
            if not isinstance(res, (tuple, list)):
                res = (res,)
            assert len(res) == no + reds, (name, len(res))
            for r, v in zip(o_refs, res[:no]):
                r[...] = v.astype(r.dtype)
            for r, v in zip(r_refs, res[no:]):
                @pl.when(i == 0)
                def _(r=r, v=v):
                    r[...] = v

                @pl.when(i > 0)
                def _(r=r, v=v):
                    r[...] += v

        if nk == 1:
            finish(part)
        else:
            acc_ref = rest[ne + no + reds]
            k = pl.program_id(2)

            @pl.when(k == 0)
            def _():
                acc_ref[...] = part

            @pl.when(k > 0)
            def _():
                acc_ref[...] += part

            @pl.when(k == nk - 1)
            def _():
                finish(acc_ref[...])

    res = _pallas(
        body, name=name, grid=(M // tm, N // tn, nk),
        in_specs=[a_spec, b_spec] + ex_specs,
        out_specs=[pl.BlockSpec((tm, tn), lambda i, j, k: (i, j))] * no + [pl.BlockSpec((1, tn), lambda i, j, k: (0, j))] * reds,
        out_shape=[_sds((M, N), dt) for dt in out_dtypes] + [_sds((1, N), F32)] * reds,
        scratch_shapes=[pltpu.VMEM((tm, tn), F32)] if nk > 1 else [],
        compiler_params=_params(("arbitrary",) * 3 if reds else ("parallel", "parallel", "arbitrary")),
    )(a, b, *extras)
    return res if (many or reds) else res[0]


def _rows(fn, ins, params, outs, reds=(), *, tm=256, name):
    S = ins[0].shape[0]
    tm = min(tm, S)
    assert S % tm == 0, (name, S, tm)
    ni, npar, no, nr = len(ins), len(params), len(outs), len(reds)

    def body(*refs):
        in_refs, p_refs = refs[:ni], refs[ni:ni + npar]
        o_refs, r_refs = refs[ni + npar:ni + npar + no], refs[ni + npar + no:]
        res = fn(*[r[...] for r in in_refs], *[r[...] for r in p_refs])
        if not isinstance(res, (tuple, list)):
            res = (res,)
        assert len(res) == no + nr, (name, len(res))
        for r, v in zip(o_refs, res[:no]):
            r[...] = v.astype(r.dtype)
        i = pl.program_id(0)
        for r, v in zip(r_refs, res[no:]):
            @pl.when(i == 0)
            def _(r=r, v=v):
                r[...] = v

            @pl.when(i > 0)
            def _(r=r, v=v):
                r[...] += v

    res = _pallas(
        body, name=name, grid=(S // tm,),
        in_specs=[pl.BlockSpec((tm, a.shape[1]), lambda i: (i, 0)) for a in ins]
        + [pl.BlockSpec(p.shape, lambda i: (0, 0)) for p in params],
        out_specs=[pl.BlockSpec((tm, o.shape[1]), lambda i: (i, 0)) for o in outs]
        + [pl.BlockSpec(r.shape, lambda i: (0, 0)) for r in reds],
        out_shape=list(outs) + list(reds),
        compiler_params=_params(("arbitrary",)),
    )(*ins, *params)
    return res


def _colsum(v):
    return jnp.sum(v, axis=0, keepdims=True)


def _sigmoid(v):
    return 1.0 / (1.0 + jnp.exp(-v))


def _rms_stats(x):
    r = lax.rsqrt(jnp.mean(x * x, axis=-1, keepdims=True) + EPS)
    return x * r, r


def _rms_fwd(x, g):
    xh, _ = _rms_stats(x)
    return (xh * g,)


def _with_rms(h, g):
    xh, _ = _rms_stats(h)
    return h, xh * g


def _rms_bwd(dy, x, dres, g):
    xh, r = _rms_stats(x)
    dyg = dy * g
    dx = r * (dyg - xh * jnp.mean(dyg * xh, axis=-1, keepdims=True))
    return dres + dx, _colsum(dy * xh)


def _rms_bwd_bias(dy, x, dres, g):
    dx, dg = _rms_bwd(dy, x, dres, g)
    return dx, dg, _colsum(dx)


def _glu_fwd(u):
    d = u.shape[1] // 2
    return (u[:, :d] * _sigmoid(u[:, d:]),)


def _glu_bwd(dglu, u):
    d = u.shape[1] // 2
    a, sig = u[:, :d], _sigmoid(u[:, d:])
    du = jnp.concatenate([dglu * sig, dglu * a * sig * (1.0 - sig)], axis=1)
    return du, _colsum(du)


def _ln_parts(x, g, b):
    mu = jnp.mean(x, axis=-1, keepdims=True)
    xc = x - mu
    r = lax.rsqrt(jnp.mean(xc * xc, axis=-1, keepdims=True) + EPS)
    xh = xc * r
    return xh, r, xh * g + b


def _ln_silu_fwd(x, g, b):
    _, _, y = _ln_parts(x, g, b)
    return (y * _sigmoid(y),)


def _ln_silu_bwd(dsw, x, g, b):
    xh, r, y = _ln_parts(x, g, b)
    sig = _sigmoid(y)
    dy = dsw * sig * (1.0 + y * (1.0 - sig))
    dxh = dy * g
    dx = r * (dxh - jnp.mean(dxh, axis=-1, keepdims=True) - xh * jnp.mean(dxh * xh, axis=-1, keepdims=True))
    return dx, _colsum(dy * xh), _colsum(dy)


def _relu2(z):
    zp = jnp.maximum(z, 0.0)
    return (zp * zp,)


def _ple_fwd(h, zg, pp):
    return (h + _sigmoid(zg) * pp,)


def _ple_bwd(dh, zg, pp):
    gate = _sigmoid(zg)
    return dh * pp * gate * (1.0 - gate), dh * gate


def _final_fn(h, t, g):
    xh, r = _rms_stats(h)
    err = xh * g - t
    dy = err * (1.0 / h.shape[1])
    dyg = dy * g
    dh = r * (dyg - xh * jnp.mean(dyg * xh, axis=-1, keepdims=True))
    return dh, _colsum(err * err), _colsum(dy * xh)


def _add2(a, b):
    return (a + b,)


def _adamw_fn(w, g, m, v):
    m = ADAM_B1 * m + (1.0 - ADAM_B1) * g
    v = ADAM_B2 * v + (1.0 - ADAM_B2) * (g * g)
    m_hat = m / (1.0 - ADAM_B1 ** ADAM_STEP)
    v_hat = v / (1.0 - ADAM_B2 ** ADAM_STEP)
    delta = -ADAM_LR * (m_hat / (jnp.sqrt(v_hat) + ADAM_EPS) + ADAM_WD * w)
    return g, delta, m, v


def _adamw(w, g, m, v, name):
    shape = w.shape
    cols = shape[-1] if len(shape) > 1 else shape[0]
    two = lambda t: t.reshape(-1, cols)
    o = _sds(two(w).shape, F32)
    res = _rows(_adamw_fn, [two(w), two(g), two(m), two(v)], [], [o, o, o, o], tm=512, name=name)
    return [r.reshape(shape) for r in res]


def _dwconv_fwd(u, w, b, *, tm=512, name):
    S, D = u.shape
    tm = min(tm, S)
    rc = min(128, tm)
    per = tm // CONV_PAD

    def body(prev_ref, cur_ref, w_ref, b_ref, o_ref, win):
        i = pl.program_id(0)

        @pl.when(i == 0)
        def _():
            win[0:CONV_PAD, :] = jnp.zeros((CONV_PAD, D), F32)

        @pl.when(i > 0)
        def _():
            win[0:CONV_PAD, :] = prev_ref[...]

        win[CONV_PAD:CONV_PAD + tm, :] = cur_ref[...]
        for lc in range(D // LANES):
            ls = slice(lc * LANES, (lc + 1) * LANES)
            for r0 in range(0, tm, rc):
                acc = jnp.zeros((rc, LANES), F32) + b_ref[:, ls]
                for k in range(CONV_WIDTH):
                    acc = acc + win[r0 + 2 + k:r0 + 2 + k + rc, ls] * w_ref[k:k + 1, ls]
                o_ref[r0:r0 + rc, ls] = acc

    return _pallas(
        body, name=name, grid=(S // tm,),
        in_specs=[pl.BlockSpec((CONV_PAD, D), lambda i: (jnp.maximum(i * per - 1, 0), 0)),
                  pl.BlockSpec((tm, D), lambda i: (i, 0)),
                  pl.BlockSpec((CONV_PAD, D), lambda i: (0, 0)),
                  pl.BlockSpec((1, D), lambda i: (0, 0))],
        out_specs=pl.BlockSpec((tm, D), lambda i: (i, 0)),
        out_shape=_sds((S, D), F32),
        scratch_shapes=[pltpu.VMEM((tm + CONV_PAD, D), F32)],
        compiler_params=_params(("arbitrary",)),
    )(u, u, w, b)


def _dwconv_bwd(dy, u, w, *, tm=512, name):
    S, D = u.shape
    tm = min(tm, S)
    rc = min(128, tm)
    per = tm // CONV_PAD
    n = S // tm
    nxt = S // CONV_PAD - 1

    def body(dy_ref, dyn_ref, up_ref, u_ref, w_ref, du_ref, dw_ref, db_ref, wd, wu, dwacc, dbacc):
        i = pl.program_id(0)

        @pl.when(i == 0)
        def _():
            wu[0:CONV_PAD, :] = jnp.zeros((CONV_PAD, D), F32)
            dwacc[...] = jnp.zeros(dwacc.shape, F32)
            dbacc[...] = jnp.zeros(dbacc.shape, F32)

        @pl.when(i > 0)
        def _():
            wu[0:CONV_PAD, :] = up_ref[...]

        @pl.when(i == n - 1)
        def _():
            wd[tm:tm + CONV_PAD, :] = jnp.zeros((CONV_PAD, D), F32)

        @pl.when(i < n - 1)
        def _():
            wd[tm:tm + CONV_PAD, :] = dyn_ref[...]

        wu[CONV_PAD:CONV_PAD + tm, :] = u_ref[...]
        wd[0:tm, :] = dy_ref[...]
        for lc in range(D // LANES):
            ls = slice(lc * LANES, (lc + 1) * LANES)
            for r0 in range(0, tm, rc):
                acc = jnp.zeros((rc, LANES), F32)
                for k in range(CONV_WIDTH):
                    acc = acc + wd[r0 + 30 - k:r0 + 30 - k + rc, ls] * w_ref[k:k + 1, ls]
                du_ref[r0:r0 + rc, ls] = acc
                dyc = wd[r0:r0 + rc, ls]
                dbacc[:, ls] += jnp.sum(dyc.reshape(rc // 8, 8, LANES), axis=0)
                for k in range(CONV_WIDTH):
                    prod = dyc * wu[r0 + 2 + k:r0 + 2 + k + rc, ls]
                    dwacc[8 * k:8 * k + 8, ls] += jnp.sum(prod.reshape(rc // 8, 8, LANES), axis=0)

        @pl.when(i == n - 1)
        def _():
            dw_ref[...] = jnp.zeros(dw_ref.shape, F32)
            for k in range(CONV_WIDTH):
                dw_ref[k:k + 1, :] = jnp.sum(dwacc[8 * k:8 * k + 8, :], axis=0, keepdims=True)
            db_ref[...] = jnp.sum(dbacc[...], axis=0, keepdims=True)

    return _pallas(
        body, name=name, grid=(n,),
        in_specs=[pl.BlockSpec((tm, D), lambda i: (i, 0)),
                  pl.BlockSpec((CONV_PAD, D), lambda i: (jnp.minimum((i + 1) * per, nxt), 0)),
                  pl.BlockSpec((CONV_PAD, D), lambda i: (jnp.maximum(i * per - 1, 0), 0)),
                  pl.BlockSpec((tm, D), lambda i: (i, 0)),
                  pl.BlockSpec((CONV_PAD, D), lambda i: (0, 0))],
        out_specs=[pl.BlockSpec((tm, D), lambda i: (i, 0)),
                   pl.BlockSpec((CONV_PAD, D), lambda i: (0, 0)),
                   pl.BlockSpec((1, D), lambda i: (0, 0))],
        out_shape=[_sds((S, D), F32), _sds((CONV_PAD, D), F32), _sds((1, D), F32)],
        scratch_shapes=[pltpu.VMEM((tm + CONV_PAD, D), F32), pltpu.VMEM((tm + CONV_PAD, D), F32),
                        pltpu.VMEM((8 * CONV_PAD, D), F32), pltpu.VMEM((8, D), F32)],
        compiler_params=_params(("arbitrary",)),
    )(dy, dy, u, u, w)


def _tri_dot(tri, x):
    x1 = x.astype(BF16)
    r1 = x - x1.astype(F32)
    x2 = r1.astype(BF16)
    x3 = (r1 - x2.astype(F32)).astype(BF16)
    d = lambda v: jnp.dot(tri, v, preferred_element_type=F32)
    return d(x1) + d(x2) + d(x3)


def _log_sigmoid(x):
    return jnp.minimum(x, 0.0) - jnp.log(1.0 + jnp.exp(-jnp.abs(x)))


def _gate_cumsum(fl, bf, *, tm=256, name):
    S, W = fl.shape
    tm = min(tm, S)

    def body(fl_ref, bf_ref, c_ref, carry):
        i = pl.program_id(0)

        @pl.when(i == 0)
        def _():
            carry[...] = jnp.zeros(carry.shape, F32)

        x = _log_sigmoid(fl_ref[...] + bf_ref[...])
        row = lax.broadcasted_iota(jnp.int32, (tm, tm), 0)
        col = lax.broadcasted_iota(jnp.int32, (tm, tm), 1)
        tri = jnp.where(row >= col, 1.0, 0.0).astype(BF16)
        cs = _tri_dot(tri, x) + carry[0:1, :]
        c_ref[...] = cs
        carry[...] = jnp.broadcast_to(cs[tm - 1:tm, :], carry.shape)

    return _pallas(
        body, name=name, grid=(S // tm,),
        in_specs=[pl.BlockSpec((tm, W), lambda i: (i, 0)), pl.BlockSpec((1, W), lambda i: (0, 0))],
        out_specs=pl.BlockSpec((tm, W), lambda i: (i, 0)),
        out_shape=_sds((S, W), F32),
        scratch_shapes=[pltpu.VMEM((8, W), F32)],
        compiler_params=_params(("arbitrary",)),
    )(fl, bf)


def _gate_cumsum_bwd(sums, fl, bf, *, tm=256, name):
    S, W = fl.shape
    tm = min(tm, S)
    n = S // tm
    ns = len(sums)
    assert ns % 2 == 0

    def body(*refs):
        sum_refs = refs[:ns]
        fl_ref, bf_ref, o_ref, s_ref, carry = refs[ns:]
        i = pl.program_id(0)

        @pl.when(i == 0)
        def _():
            carry[...] = jnp.zeros(carry.shape, F32)
            s_ref[...] = jnp.zeros(s_ref.shape, F32)

        dc = sum_refs[0][...] - sum_refs[1][...]
        for a in range(2, ns, 2):
            dc = dc + (sum_refs[a][...] - sum_refs[a + 1][...])
        row = lax.broadcasted_iota(jnp.int32, (tm, tm), 0)
        col = lax.broadcasted_iota(jnp.int32, (tm, tm), 1)
        tri = jnp.where(col >= row, 1.0, 0.0).astype(BF16)
        rs = _tri_dot(tri, dc) + carry[0:1, :]
        carry[...] = jnp.broadcast_to(rs[0:1, :], carry.shape)
        dfl = rs * _sigmoid(-(fl_ref[...] + bf_ref[...]))
        o_ref[...] = dfl
        s_ref[...] += _colsum(dfl)

    rev = lambda i: (n - 1 - i, 0)
    return _pallas(
        body, name=name, grid=(n,),
        in_specs=[pl.BlockSpec((tm, W), rev)] * (ns + 1) + [pl.BlockSpec((1, W), lambda i: (0, 0))],
        out_specs=[pl.BlockSpec((tm, W), rev), pl.BlockSpec((1, W), lambda i: (0, 0))],
        out_shape=[_sds((S, W), F32), _sds((1, W), F32)],
        scratch_shapes=[pltpu.VMEM((8, W), F32)],
        compiler_params=_params(("arbitrary",)),
    )(*sums, fl, bf)


def _tri_tables(nb, by_query):
    ii, jj = [], []
    if by_query:
        for i in range(nb):
            for j in range(i + 1):
                ii.append(i)
                jj.append(j)
    else:
        for j in range(nb):
            for i in range(j, nb):
                ii.append(i)
                jj.append(j)
    return jnp.asarray(np.array(ii, np.int32)), jnp.asarray(np.array(jj, np.int32))


def _rep(v, t):
    return jnp.tile(v, (1, t // LANES))


def _attn_fwd(q, kv, ck, *, tb=512, name):
    S, D = q.shape
    HP = D // LANES
    T = min(tb, S)
    nb = S // T
    it, jt = _tri_tables(nb, True)

    def body(it_ref, jt_ref, q_ref, k_ref, v_ref, ck_ref, o_ref, lse_ref, m_s, l_s, acc_s):
        s_id = pl.program_id(1)
        i, j = it_ref[s_id], jt_ref[s_id]
        lane = lax.broadcasted_iota(jnp.int32, (T, LANES), 1)
        head0 = lane < HEAD_DIM

        @pl.when(j == 0)
        def _():
            m_s[...] = jnp.full(m_s.shape, NEG_BIG, F32)
            l_s[...] = jnp.zeros(l_s.shape, F32)
            acc_s[...] = jnp.zeros(acc_s.shape, F32)

        def step(masked):
            qv, kvv, vv = q_ref[...], k_ref[...], v_ref[...]
            alphas, pvs = [], []
            for h in range(2):
                hm = head0 if h == 0 else jnp.logical_not(head0)
                qm = jnp.where(hm, qv, jnp.zeros_like(qv))
                s = lax.dot_general(qm, kvv, _DIMS["nt"], preferred_element_type=F32) - ck_ref[h:h + 1, :]
                if masked:
                    row = lax.broadcasted_iota(jnp.int32, (T, T), 0)
                    col = lax.broadcasted_iota(jnp.int32, (T, T), 1)
                    s = jnp.where(row >= col, s, NEG_BIG)
                m_prev = m_s[h]
                m_new = jnp.maximum(m_prev, jnp.max(s, axis=1, keepdims=True))
                p = jnp.exp(s - _rep(m_new, T))
                alpha = jnp.exp(m_prev - m_new)
                l_s[h] = alpha * l_s[h] + jnp.sum(p, axis=1, keepdims=True)
                m_s[h] = m_new
                alphas.append(alpha)
                pvs.append(jnp.dot(p.astype(BF16), vv, preferred_element_type=F32))
            acc_s[...] = jnp.where(head0, alphas[0], alphas[1]) * acc_s[...] + jnp.where(head0, pvs[0], pvs[1])

        @pl.when(j < i)
        def _():
            step(False)

        @pl.when(j == i)
        def _():
            step(True)
            o_ref[...] = acc_s[...] / jnp.where(head0, l_s[0], l_s[1])
            lse_ref[0] = m_s[0] + jnp.log(l_s[0])
            lse_ref[1] = m_s[1] + jnp.log(l_s[1])

    grid_spec = pltpu.PrefetchScalarGridSpec(
        num_scalar_prefetch=2, grid=(HP, it.shape[0]),
        in_specs=[pl.BlockSpec((T, LANES), lambda h, s, it, jt: (it[s], h)),
                  pl.BlockSpec((T, LANES), lambda h, s, it, jt: (jt[s], h)),
                  pl.BlockSpec((T, LANES), lambda h, s, it, jt: (jt[s], HP + h)),
                  pl.BlockSpec((None, 8, T), lambda h, s, it, jt: (h, 0, jt[s]))],
        out_specs=[pl.BlockSpec((T, LANES), lambda h, s, it, jt: (it[s], h)),
                   pl.BlockSpec((2, T, LANES), lambda h, s, it, jt: (h, it[s], 0))],
        scratch_shapes=[pltpu.VMEM((2, T, LANES), F32), pltpu.VMEM((2, T, LANES), F32), pltpu.VMEM((T, LANES), F32)],
    )
    return _pallas(
        body, name=name, grid_spec=grid_spec,
        out_shape=[_sds((S, D), F32), _sds((2 * HP, S, LANES), F32)],
        compiler_params=_params(("parallel", "arbitrary")),
    )(it, jt, q, kv, kv, ck)


def _attn_bwd(q, kv, o, do, lse, ck, *, tb=512, name):
    S, D = q.shape
    HP = D // LANES
    T = min(tb, S)
    nb = S // T
    it, jt = _tri_tables(nb, False)

    def body(it_ref, jt_ref, q_ref, k_ref, v_ref, o_ref, do_ref, lse_ref, ck_ref, dq_ref, drs_ref, dk_ref, dv_ref, dcs_ref):
        s_id = pl.program_id(1)
        i, j = it_ref[s_id], jt_ref[s_id]
        lane = lax.broadcasted_iota(jnp.int32, (T, LANES), 1)
        head0 = lane < HEAD_DIM

        @pl.when(s_id == 0)
        def _():
            dq_ref[...] = jnp.zeros(dq_ref.shape, F32)
            drs_ref[...] = jnp.zeros(drs_ref.shape, F32)

        @pl.when(i == j)
        def _():
            dk_ref[...] = jnp.zeros(dk_ref.shape, F32)
            dv_ref[...] = jnp.zeros(dv_ref.shape, F32)
            dcs_ref[...] = jnp.zeros(dcs_ref.shape, F32)

        def step(masked):
            qv, kvv, vv = q_ref[...], k_ref[...], v_ref[...]
            dob = do_ref[...].astype(BF16)
            prod = dob.astype(F32) * o_ref[...]
            one = jnp.ones_like(qv)
            dqs, dks, dvs = [], [], []
            for h in range(2):
                hm = head0 if h == 0 else jnp.logical_not(head0)
                qm = jnp.where(hm, qv, jnp.zeros_like(qv))
                dom = jnp.where(hm, dob, jnp.zeros_like(dob))
                s = lax.dot_general(qm, kvv, _DIMS["nt"], preferred_element_type=F32) - ck_ref[h:h + 1, :]
                if masked:
                    row = lax.broadcasted_iota(jnp.int32, (T, T), 0)
                    col = lax.broadcasted_iota(jnp.int32, (T, T), 1)
                    s = jnp.where(row >= col, s, NEG_BIG)
                p = jnp.exp(s - _rep(lse_ref[h], T))
                dp = lax.dot_general(dom, vv, _DIMS["nt"], preferred_element_type=F32)
                delta = jnp.sum(jnp.where(hm, prod, 0.0), axis=1, keepdims=True)
                pb, dsb = p.astype(BF16), (p * (dp - delta)).astype(BF16)
                dvs.append(lax.dot_general(pb, dob, _DIMS["tn"], preferred_element_type=F32))
                dks.append(lax.dot_general(dsb, jnp.where(hm, qv, one), _DIMS["tn"], preferred_element_type=F32))
                dqs.append(jnp.dot(dsb, jnp.where(hm, kvv, one), preferred_element_type=F32))
            dv_ref[...] += jnp.where(head0, dvs[0], dvs[1])
            dk_ref[...] += jnp.where(head0, dks[0], dks[1])
            dcs_ref[...] += jnp.where(head0, dks[1], dks[0])
            rows = pl.ds(pl.multiple_of(i * T, T), T)
            dq_ref[rows, :] += jnp.where(head0, dqs[0], dqs[1])
            drs_ref[rows, :] += jnp.where(head0, dqs[1], dqs[0])

        @pl.when(i > j)
        def _():
            step(False)

        @pl.when(i == j)
        def _():
            step(True)

    by_q = lambda h, s, it, jt: (it[s], h)
    by_k = lambda h, s, it, jt: (jt[s], h)
    whole = lambda h, s, it, jt: (0, h)
    grid_spec = pltpu.PrefetchScalarGridSpec(
        num_scalar_prefetch=2, grid=(HP, it.shape[0]),
        in_specs=[pl.BlockSpec((T, LANES), by_q),
                  pl.BlockSpec((T, LANES), by_k),
                  pl.BlockSpec((T, LANES), lambda h, s, it, jt: (jt[s], HP + h)),
                  pl.BlockSpec((T, LANES), by_q),
                  pl.BlockSpec((T, LANES), by_q),
                  pl.BlockSpec((2, T, LANES), lambda h, s, it, jt: (h, it[s], 0)),
                  pl.BlockSpec((None, 8, T), lambda h, s, it, jt: (h, 0, jt[s]))],
        out_specs=[pl.BlockSpec((S, LANES), whole), pl.BlockSpec((S, LANES), whole),
                   pl.BlockSpec((T, LANES), by_k), pl.BlockSpec((T, LANES), by_k), pl.BlockSpec((T, LANES), by_k)],
        scratch_shapes=[],
    )
    return _pallas(
        body, name=name, grid_spec=grid_spec,
        out_shape=[_sds((S, D), F32)] * 5,
        compiler_params=_params(("parallel", "arbitrary")),
    )(it, jt, q, kv, kv, o, do, lse, ck)


ANY = pl.BlockSpec(memory_space=pl.ANY)


def _coords():
    x, y, c = lax.axis_index("x"), lax.axis_index("y"), lax.axis_index("c")
    return x, y, c


def _remote(src, dst, send_sems, recv_sems, k, to):
    return pltpu.make_async_remote_copy(src_ref=src, dst_ref=dst, send_sem=send_sems.at[k], recv_sem=recv_sems.at[k],
                                        device_id=to, device_id_type=MESH)


def _all_gather_shards(pack, *, name):
    R, C = pack.shape
    assert R % 4 == 0
    H, Q = R // 2, R // 4

    def body(in_ref, out_ref, send_sems, recv_sems):
        x, y, c = _coords()
        me, sib = (x, y, c), (x, y, 1 - c)
        xn, yn = (1 - x, y, c), (x, 1 - y, c)
        s, sx, sy, sd = 2 * x + y, 2 * (1 - x) + y, 2 * x + 1 - y, 2 * (1 - x) + 1 - y
        half = pl.ds(c * H, H)
        other = pl.ds((1 - c) * H, H)
        q0 = pl.ds(c * H, Q)
        q1 = pl.ds(c * H + Q, Q)
        rc = functools.partial(_remote, send_sems=send_sems, recv_sems=recv_sems)

        sends = [rc(in_ref.at[half], out_ref.at[s, half], k=0, to=xn),
                 rc(in_ref.at[half], out_ref.at[s, half], k=1, to=yn),
                 rc(in_ref, out_ref.at[s], k=7, to=sib)]
        for cp in sends:
            cp.start()
        rc(in_ref.at[half], out_ref.at[sx, half], k=0, to=me).wait_recv()
        sends.append(rc(out_ref.at[sx, q0], out_ref.at[sx, q0], k=2, to=yn))
        sends[-1].start()
        sends.append(rc(out_ref.at[sx, half], out_ref.at[sx, half], k=4, to=sib))
        sends[-1].start()
        rc(in_ref.at[half], out_ref.at[sy, half], k=1, to=me).wait_recv()
        sends.append(rc(out_ref.at[sy, q1], out_ref.at[sy, q1], k=3, to=xn))
        sends[-1].start()
        sends.append(rc(out_ref.at[sy, half], out_ref.at[sy, half], k=5, to=sib))
        sends[-1].start()
        rc(out_ref.at[sd, q0], out_ref.at[sd, q0], k=2, to=me).wait_recv()
        rc(out_ref.at[sd, q1], out_ref.at[sd, q1], k=3, to=me).wait_recv()
        sends.append(rc(out_ref.at[sd, half], out_ref.at[sd, half], k=6, to=sib))
        sends[-1].start()
        for k, sh in ((4, sx), (5, sy), (6, sd)):
            rc(out_ref.at[sh, other], out_ref.at[sh, other], k=k, to=me).wait_recv()
        rc(in_ref, out_ref.at[s], k=7, to=me).wait_recv()
        for cp in sends:
            cp.wait_send()

    return _pallas(
        body, name=name, in_specs=[ANY], out_specs=ANY,
        out_shape=_sds((N_CHIPS, R, C), pack.dtype),
        scratch_shapes=[pltpu.SemaphoreType.DMA((8,)), pltpu.SemaphoreType.DMA((8,))],
    )(pack)


def _rs_pair(g, *, name):
    n, R, C = g.shape
    H = R // 2

    def body(g_ref, land_ref, send_sems, recv_sems):
        x, y, c = _coords()
        other = pl.ds((1 - c) * H, H)
        cps = [_remote(g_ref.at[sh, other], land_ref.at[sh], send_sems, recv_sems, sh, (x, y, 1 - c)) for sh in range(n)]
        for cp in cps:
            cp.start()
        for cp in cps:
            cp.wait_recv()
        for cp in cps:
            cp.wait_send()

    return _pallas(
        body, name=name, in_specs=[ANY], out_specs=ANY, out_shape=_sds((n, H, C), g.dtype),
        scratch_shapes=[pltpu.SemaphoreType.DMA((n,)), pltpu.SemaphoreType.DMA((n,))],
    )(g)


def _rs_quarters(p, *, name):
    n, H, C = p.shape
    Q = H // 2

    def body(p_ref, la_ref, lb_ref, send_sems, recv_sems):
        x, y, c = _coords()
        sd = 2 * (1 - x) + 1 - y
        a = _remote(p_ref.at[sd, pl.ds(0, Q)], la_ref, send_sems, recv_sems, 0, (x, 1 - y, c))
        b = _remote(p_ref.at[sd, pl.ds(Q, Q)], lb_ref, send_sems, recv_sems, 1, (1 - x, y, c))
        a.start()
        b.start()
        a.wait_recv()
        b.wait_recv()
        a.wait_send()
        b.wait_send()

    return _pallas(
        body, name=name, in_specs=[ANY], out_specs=[ANY, ANY],
        out_shape=[_sds((Q, C), p.dtype), _sds((Q, C), p.dtype)],
        scratch_shapes=[pltpu.SemaphoreType.DMA((2,)), pltpu.SemaphoreType.DMA((2,))],
    )(p)


def _rs_halves(p, ax, ay, *, name):
    n, H, C = p.shape
    Q = H // 2

    def body(p_ref, ax_ref, ay_ref, la_ref, lb_ref, send_sems, recv_sems):
        x, y, c = _coords()
        sx, sy = 2 * (1 - x) + y, 2 * x + 1 - y
        xn, yn = (1 - x, y, c), (x, 1 - y, c)
        lo, hi = pl.ds(0, Q), pl.ds(Q, Q)
        cps = [_remote(ax_ref, la_ref.at[lo], send_sems, recv_sems, 0, xn),
               _remote(p_ref.at[sx, hi], la_ref.at[hi], send_sems, recv_sems, 1, xn),
               _remote(p_ref.at[sy, lo], lb_ref.at[lo], send_sems, recv_sems, 2, yn),
               _remote(ay_ref, lb_ref.at[hi], send_sems, recv_sems, 3, yn)]
        for cp in cps:
            cp.start()
        for cp in cps:
            cp.wait_recv()
        for cp in cps:
            cp.wait_send()

    return _pallas(
        body, name=name, in_specs=[ANY, ANY, ANY], out_specs=[ANY, ANY],
        out_shape=[_sds((H, C), p.dtype), _sds((H, C), p.dtype)],
        scratch_shapes=[pltpu.SemaphoreType.DMA((4,)), pltpu.SemaphoreType.DMA((4,))],
    )(p, ax, ay)


def _rs_join(buf, *, name):
    R, C = buf.shape
    H = R // 2

    def body(in_ref, out_ref, send_sems, recv_sems):
        x, y, c = _coords()
        half = pl.ds(c * H, H)
        other = pl.ds((1 - c) * H, H)
        cp = _remote(in_ref.at[half], out_ref.at[half], send_sems, recv_sems, 0, (x, y, 1 - c))
        cp.start()
        _remote(in_ref.at[other], out_ref.at[other], send_sems, recv_sems, 0, (x, y, c)).wait_recv()
        cp.wait_send()

    return _pallas(
        body, name=name, in_specs=[ANY], out_specs=ANY, out_shape=_sds((R, C), buf.dtype),
        input_output_aliases={0: 0},
        scratch_shapes=[pltpu.SemaphoreType.DMA((1,)), pltpu.SemaphoreType.DMA((1,))],
    )(buf)


def _tile_add(ins_specs, arrays, n_steps, out_spec, out_shape, scalars, *, name):
    grid_spec = pltpu.PrefetchScalarGridSpec(
        num_scalar_prefetch=1, grid=(n_steps,), in_specs=ins_specs, out_specs=out_spec, scratch_shapes=[])

    def body(sc_ref, *refs):
        acc = refs[0][...].astype(F32)
        for r in refs[1:-1]:
            acc = acc + r[...].astype(F32)
        refs[-1][...] = acc.astype(refs[-1].dtype)

    return _pallas(body, name=name, grid_spec=grid_spec, out_shape=out_shape,
                   compiler_params=_params(("arbitrary",)))(scalars, *arrays)


def _reduce_scatter(g, *, name):
    n, R, C = g.shape
    H, Q = R // 2, R // 4
    tm = RS_ROW_MULT // 4
    assert Q % tm == 0, (R, tm)
    x, y, c = _coords()
    sx, sy, s = 2 * (1 - x) + y, 2 * x + 1 - y, 2 * x + y
    sc = jnp.stack([c, sx, sy, s]).astype(jnp.int32)
    hb, qb = H // tm, Q // tm
    blk = lambda f: pl.BlockSpec((None, tm, C), f)
    flat = lambda f: pl.BlockSpec((tm, C), f)

    land = _rs_pair(g, name=name + "_pair")
    p = _tile_add([blk(lambda i, sc: (i // hb, sc[0] * hb + i % hb, 0)), blk(lambda i, sc: (i // hb, i % hb, 0))],
                  [g, land], n * hb, blk(lambda i, sc: (i // hb, i % hb, 0)), _sds((n, H, C), g.dtype), sc, name=name + "_add0")
    la, lb = _rs_quarters(p, name=name + "_quarters")
    ax = _tile_add([blk(lambda i, sc: (sc[1], i, 0)), flat(lambda i, sc: (i, 0))], [p, la], qb,
                   flat(lambda i, sc: (i, 0)), _sds((Q, C), g.dtype), sc, name=name + "_add1x")
    ay = _tile_add([blk(lambda i, sc: (sc[2], qb + i, 0)), flat(lambda i, sc: (i, 0))], [p, lb], qb,
                   flat(lambda i, sc: (i, 0)), _sds((Q, C), g.dtype), sc, name=name + "_add1y")
    fa, fb = _rs_halves(p, ax, ay, name=name + "_halves")
    buf = _tile_add([blk(lambda i, sc: (sc[3], i, 0)), flat(lambda i, sc: (i, 0)), flat(lambda i, sc: (i, 0))],
                    [p, fa, fb], hb, flat(lambda i, sc: (sc[0] * hb + i, 0)), _sds((R, C), F32), sc, name=name + "_add2")
    return _rs_join(buf, name=name + "_join")


def _all_reduce_small(v, *, name):
    M, N = v.shape

    def body(x_ref, out_ref, send_sems, recv_sems, local_sem):
        x, y, c = _coords()
        me, sibling = (x, y, c), (x, y, 1 - c)
        chips = [(1 - x, y), (x, 1 - y), (1 - x, 1 - y)]

        def rows(px, py, pc):
            return out_ref.at[pl.ds((4 * px + 2 * py + pc) * M, M), :]

        def copy(k, block, to, src=None):
            return pltpu.make_async_remote_copy(
                src_ref=rows(*block) if src is None else src, dst_ref=rows(*block),
                send_sem=send_sems.at[k], recv_sem=recv_sems.at[k], device_id=to, device_id_type=MESH)

        mine = pltpu.make_async_copy(x_ref, rows(*me), local_sem)
        mine.start()
        first = [copy(0, me, sibling, src=x_ref)]
        first += [copy(1 + j, me, (*chip, c), src=x_ref) for j, chip in enumerate(chips)]
        for cp in first:
            cp.start()
        passed = [copy(4 + j, (*chip, c), sibling) for j, chip in enumerate(chips)]
        for j, chip in enumerate(chips):
            copy(1 + j, (*chip, c), me).wait_recv()
            passed[j].start()
        copy(0, sibling, me).wait_recv()
        for j, chip in enumerate(chips):
            copy(4 + j, (*chip, 1 - c), me).wait_recv()
        for cp in first + passed:
            cp.wait_send()
        mine.wait()

    gathered = _pallas(
        body, name=name + "_gather",
        out_shape=_sds((N_DEV * M, N), F32),
        in_specs=[pl.BlockSpec(memory_space=pltpu.VMEM)],
        out_specs=pl.BlockSpec(memory_space=pltpu.VMEM),
        scratch_shapes=[pltpu.SemaphoreType.DMA((7,)), pltpu.SemaphoreType.DMA((7,)), pltpu.SemaphoreType.DMA],
    )(v)

    def sum_body(g_ref, o_ref):
        acc = g_ref[0:M, :]
        for d in range(1, N_DEV):
            acc = acc + g_ref[d * M:(d + 1) * M, :]
        o_ref[...] = acc

    return _pallas(sum_body, name=name + "_sum", out_shape=_sds((M, N), F32))(gathered)


MATS = [("conv_w_pw1", 2), ("conv_w_pw2", 1), ("attn_w_q", 1), ("attn_w_o", 1), ("ffn_w1", 2), ("ffn_w2", 1),
        ("ple_w_gate", 1), ("ple_w_proj", 2), ("w_kvf", 1)]
VECS = [("conv_b_pw1", 1), ("conv_w_dw", 2), ("conv_b_dw", 1), ("conv_ln_g", 1), ("conv_ln_b", 1), ("conv_b_pw2", 1)]
REPL = ["mix_norm", "ffn_norm", "ple_norm", "kv_norm", "final_norm", "b_f"]
WEIGHTS = ["mix_norm", "conv_w_pw1", "conv_b_pw1", "conv_w_dw", "conv_b_dw", "conv_ln_g", "conv_ln_b", "conv_w_pw2",
           "conv_b_pw2", "kv_norm", "w_kvf", "b_f", "attn_w_q", "attn_w_o", "ffn_norm", "ffn_w1", "ffn_w2", "ple_norm",
           "ple_w_gate", "ple_w_proj", "final_norm"]


def _round_up(n, m):
    return -(-n // m) * m


def _to_rows(t, C, mult):
    flat = t.reshape(-1)
    rows = _round_up(_round_up(flat.shape[0], C) // C, mult)
    flat = jnp.pad(flat, (0, rows * C - flat.shape[0]))
    return flat.reshape(rows, C)


def _pack(tensors, C, mult, total_mult):
    parts = [_to_rows(t, C, mult) for t in tensors]
    rows = sum(p.shape[0] for p in parts)
    pad = _round_up(rows, total_mult) - rows
    if pad:
        parts.append(jnp.zeros((pad, C), parts[0].dtype))
    return jnp.concatenate(parts, axis=0)


def _row_counts(shapes, C, mult):
    return [_round_up(_round_up(int(np.prod(s)), C) // C, mult) for s in shapes]


def _unpack(packed, shapes, C, mult):
    outs, r0 = [], 0
    lead = packed.shape[:-2]
    for shp, nr in zip(shapes, _row_counts(shapes, C, mult)):
        n = int(np.prod(shp))
        seg = packed[..., r0:r0 + nr, :].reshape(lead + (nr * C,))[..., :n]
        outs.append(seg.reshape(lead + tuple(shp)))
        r0 += nr
    return outs


def _unshard(t, axis):
    return jnp.concatenate([t[s] for s in range(N_CHIPS)], axis=axis)


def _shards(t, axis):
    return jnp.split(t, N_CHIPS, axis=axis)


def _local_step(x, p, tgt, w):
    S, D = x.shape
    L = p.shape[0]
    NA = w["conv_w_pw1"].shape[0]
    H = w["b_f"].shape[0]
    HP = D // LANES
    row = lambda v: v.reshape(1, -1)
    act = lambda dt, n=D: _sds((S, n), dt)
    g = {}

    def rms(hh, gain, name):
        return _rows(_rms_fwd, [hh], [row(gain)], [act(BF16)], name=name)[0]

    saved = []
    h = x
    kv = ck = fl = nkv = h_kv = None
    bfp = jnp.pad(w["b_f"], (0, LANES - H)).reshape(1, LANES)
    wk = w["w_kvf"][:, :D]
    wv = w["w_kvf"][:, D:2 * D]
    wkv = w["w_kvf"][:, :2 * D]
    wf = jnp.pad(w["w_kvf"][:, 2 * D:], ((0, 0), (0, LANES - H)))
    res_rms = lambda acc, r, gn: _with_rms(acc + r, gn)
    hn = rms(h, w["mix_norm"][0], "rms_mix")
    for i in range(L):
        sv = {"h0": h, "hn": hn}
        g_ffn, g_ple = row(w["ffn_norm"][i]), row(w["ple_norm"][i])
        if i < NA:
            u = _mm(hn, w["conv_w_pw1"][i], extras=[row(w["conv_b_pw1"][i])], epi=lambda acc, b: acc + b, name="mm_pw1")
            glu = _rows(_glu_fwd, [u], [], [act(F32)], name="glu_fwd")[0]
            wdw = jnp.pad(w["conv_w_dw"][i], ((0, CONV_PAD - CONV_WIDTH), (0, 0)))
            cv = _dwconv_fwd(glu, wdw, row(w["conv_b_dw"][i]), name="dwconv_fwd")
            sw = _rows(_ln_silu_fwd, [cv], [row(w["conv_ln_g"][i]), row(w["conv_ln_b"][i])], [act(BF16)], name="ln_silu_fwd")[0]
            h1, hn2 = _mm(sw, w["conv_w_pw2"][i], extras=[row(w["conv_b_pw2"][i]), h, g_ffn],
                          epi=lambda acc, b, r, gn: _with_rms(acc + b + r, gn), out_dtype=[F32, BF16], tm=512, tn=D, name="mm_pw2")
            sv.update(u=u, glu=glu, cv=cv, sw=sw, wdw=wdw)
        else:
            if i == NA:
                h_kv = h
                nkv = rms(h, w["kv_norm"], "rms_kv")
                kv = _mm(nkv, wkv, out_dtype=BF16, name="mm_kv")
                fl = _mm(nkv, wf, name="mm_f")
                c = _gate_cumsum(fl, bfp, name="gate_cumsum")
                ck = jnp.pad(c[:, :H].T.reshape(HP, 2, S), ((0, 0), (0, 6), (0, 0)))
            j = i - NA
            q = _mm(hn, w["attn_w_q"][j], epi=lambda acc: acc * (HEAD_DIM ** -0.5), out_dtype=BF16, name="mm_q")
            o, lse = _attn_fwd(q, kv, ck, name="attn_fwd")
            h1, hn2 = _mm(o, w["attn_w_o"][j], extras=[h, g_ffn], epi=res_rms, out_dtype=[F32, BF16], tm=512, tn=D, name="mm_o")
            sv.update(q=q, o=o, lse=lse)
        zb, f = _mm(hn2, w["ffn_w1"][i], epi=lambda acc: (acc, jnp.square(jnp.maximum(acc, 0.0))), out_dtype=[BF16, BF16],
                    name="mm_ffn1")
        h2, n3 = _mm(f, w["ffn_w2"][i], extras=[h1, g_ple], epi=res_rms, out_dtype=[F32, BF16], tm=512, tn=D, name="mm_ffn2")
        zg = _mm(n3, w["ple_w_gate"][i], name="mm_gate")
        ple = lambda acc, r, zz: r + _sigmoid(zz) * acc
        if i + 1 < L:
            h, hn = _mm(p[i], w["ple_w_proj"][i], extras=[h2, zg, row(w["mix_norm"][i + 1])],
                        epi=lambda acc, r, zz, gn: _with_rms(ple(acc, r, zz), gn), out_dtype=[F32, BF16], tm=512, tn=D, name="mm_proj")
        else:
            h = _mm(p[i], w["ple_w_proj"][i], extras=[h2, zg], epi=ple, tm=512, tn=D, name="mm_proj_last")
        sv.update(h1=h1, hn2=hn2, zb=zb, f=f, h2=h2, n3=n3, zg=zg)
        saved.append(sv)

    dh, err2, g_final = _rows(_final_fn, [h, tgt], [row(w["final_norm"])], [act(F32)], [_sds((1, D), F32), _sds((1, D), F32)],
                              name="final")
    loss = 0.5 * jnp.sum(err2) / D
    g["final_norm"] = g_final.reshape(-1)

    red = _sds((1, D), F32)
    stack = {k: [None] * n for k, n in (("mix_norm", L), ("ffn_norm", L), ("ple_norm", L), ("ffn_w1", L), ("ffn_w2", L),
                                        ("ple_w_gate", L), ("ple_w_proj", L), ("conv_w_pw1", NA), ("conv_b_pw1", NA),
                                        ("conv_w_dw", NA), ("conv_b_dw", NA), ("conv_ln_g", NA), ("conv_ln_b", NA),
                                        ("conv_w_pw2", NA), ("conv_b_pw2", NA), ("attn_w_q", L - NA), ("attn_w_o", L - NA))}
    dk_sum = dv_sum = None
    dcks = []
    for i in reversed(range(L)):
        sv = saved[i]
        dzg, dpp = _mm(p[i], w["ple_w_proj"][i], extras=[dh, sv["zg"]], epi=lambda acc, d, zz: _ple_bwd(d, zz, acc),
                       out_dtype=[BF16, BF16], name="mm_ple_bwd")
        stack["ple_w_proj"][i] = _mm(p[i], dpp, mode="tn", tk=512, out_dtype=BF16, name="mm_dproj")
        stack["ple_w_gate"][i] = _mm(sv["n3"], dzg, mode="tn", tk=512, out_dtype=BF16, name="mm_dgate")
        dh, dgain = _mm(dzg, w["ple_w_gate"][i], mode="nt", extras=[sv["h2"], dh, row(w["ple_norm"][i])], epi=_rms_bwd,
                        reds=1, tm=512, tn=D, name="mm_dn3")
        stack["ple_norm"][i] = dgain.reshape(-1)
        dz = _mm(dh, w["ffn_w2"][i], mode="nt", extras=[sv["zb"]], epi=lambda acc, zz: acc * (2.0 * jnp.maximum(zz, 0.0).astype(F32)),
                 out_dtype=BF16, name="mm_dz")
        stack["ffn_w2"][i] = _mm(sv["f"], dh, mode="tn", tk=512, out_dtype=BF16, name="mm_dffn2")
        stack["ffn_w1"][i] = _mm(sv["hn2"], dz, mode="tn", tk=512, out_dtype=BF16, name="mm_dffn1")
        if i < NA:
            dh, dgain, dbias = _mm(dz, w["ffn_w1"][i], mode="nt", extras=[sv["h1"], dh, row(w["ffn_norm"][i])], epi=_rms_bwd_bias,
                                   reds=2, tm=512, tn=D, name="mm_dhn2_bias")
            stack["conv_b_pw2"][i] = dbias.reshape(-1)
        else:
            dh, dgain = _mm(dz, w["ffn_w1"][i], mode="nt", extras=[sv["h1"], dh, row(w["ffn_norm"][i])], epi=_rms_bwd,
                            reds=1, tm=512, tn=D, name="mm_dhn2")
        stack["ffn_norm"][i] = dgain.reshape(-1)
        if i < NA:
            dsw = _mm(dh, w["conv_w_pw2"][i], mode="nt", name="mm_dsw")
            stack["conv_w_pw2"][i] = _mm(sv["sw"], dh, mode="tn", tk=512, out_dtype=BF16, name="mm_dpw2")
            dcv, dlg, dlb = _rows(_ln_silu_bwd, [dsw, sv["cv"]], [row(w["conv_ln_g"][i]), row(w["conv_ln_b"][i])], [act(F32)],
                                  [red, red], name="ln_silu_bwd")
            stack["conv_ln_g"][i], stack["conv_ln_b"][i] = dlg.reshape(-1), dlb.reshape(-1)
            dglu, dwdw, dbdw = _dwconv_bwd(dcv, sv["glu"], sv["wdw"], name="dwconv_bwd")
            stack["conv_w_dw"][i], stack["conv_b_dw"][i] = dwdw[:CONV_WIDTH], dbdw.reshape(-1)
            du, dbu = _rows(_glu_bwd, [dglu, sv["u"]], [], [act(BF16, 2 * D)], [_sds((1, 2 * D), F32)], name="glu_bwd")
            stack["conv_b_pw1"][i] = dbu.reshape(-1)
            stack["conv_w_pw1"][i] = _mm(sv["hn"], du, mode="tn", tk=512, out_dtype=BF16, name="mm_dpw1")
            dh, dgain = _mm(du, w["conv_w_pw1"][i], mode="nt", extras=[sv["h0"], dh, row(w["mix_norm"][i])], epi=_rms_bwd,
                            reds=1, tm=512, tn=D, name="mm_dhn_a")
        else:
            j = i - NA
            do = _mm(dh, w["attn_w_o"][j], mode="nt", name="mm_do")
            stack["attn_w_o"][j] = _mm(sv["o"], dh, mode="tn", tk=512, out_dtype=BF16, name="mm_dwo")
            dq, drs, dk, dv, dcs = _attn_bwd(sv["q"], kv, sv["o"], do, sv["lse"], ck, name="attn_bwd")
            scale = lambda acc: acc * (HEAD_DIM ** -0.5)
            stack["attn_w_q"][j] = _mm(sv["hn"], dq, mode="tn", tk=512, out_dtype=BF16, epi=scale, name="mm_dwq")
            dh_in = dh
            dh, dgain = _mm(dq, w["attn_w_q"][j], mode="nt", extras=[sv["h0"], dh_in, row(w["mix_norm"][i])],
                            epi=lambda acc, xx, dr, gn: _rms_bwd(scale(acc), xx, dr, gn), reds=1, tm=512, tn=D, name="mm_dhn_b")
            pick = lambda t: jnp.pad(t.reshape(S, HP, 2, HEAD_DIM)[:, :, ::-1, 0].reshape(S, H), ((0, 0), (0, LANES - H)))
            dcks += [pick(drs), pick(dcs)]
            if dk_sum is None:
                dk_sum, dv_sum = dk, dv
            else:
                dk_sum = _rows(_add2, [dk_sum, dk], [], [act(F32)], name="add_dk")[0]
                dv_sum = _rows(_add2, [dv_sum, dv], [], [act(F32)], name="add_dk")[0]
        stack["mix_norm"][i] = dgain.reshape(-1)
        if i == NA:
            dfl, dbf = _gate_cumsum_bwd(dcks, fl, bfp, name="gate_cumsum_bwd")
            g["b_f"] = dbf[0, :H]
            gk = _mm(nkv, dk_sum, mode="tn", tk=512, out_dtype=BF16, name="mm_dwk")
            gv = _mm(nkv, dv_sum, mode="tn", tk=512, out_dtype=BF16, name="mm_dwk")
            gf = _mm(nkv, dfl, mode="tn", tk=512, out_dtype=BF16, name="mm_dwf")
            g["w_kvf"] = jnp.concatenate([gk, gv, gf[:, :H]], axis=1)
            dn = _mm(dk_sum, wk, mode="nt", name="mm_dnk")
            dn = _mm(dv_sum, wv, mode="nt", extras=[dn], epi=lambda acc, r: acc + r, name="mm_dnv")
            dh, dgain = _mm(dfl, wf, mode="nt", extras=[dn, h_kv, dh, row(w["kv_norm"])],
                            epi=lambda acc, r, xx, dr, gn: _rms_bwd(acc + r, xx, dr, gn), reds=1, tm=512, tn=D, name="mm_dnf")
            g["kv_norm"] = dgain.reshape(-1)
    for k, v in stack.items():
        g[k] = jnp.stack(v, axis=0)
    return loss, dh, g


def kernel(x, p, mix_norm, conv_w_pw1, conv_b_pw1, conv_w_dw, conv_b_dw, conv_ln_g, conv_ln_b, conv_w_pw2, conv_b_pw2, kv_norm, w_kvf, b_f, attn_w_q, attn_w_o, ffn_norm, ffn_w1, ffn_w2, ple_norm, ple_w_gate, ple_w_proj, final_norm, loss_target, m_mix_norm, m_conv_w_pw1, m_conv_b_pw1, m_conv_w_dw, m_conv_b_dw, m_conv_ln_g, m_conv_ln_b, m_conv_w_pw2, m_conv_b_pw2, m_kv_norm, m_w_kvf, m_b_f, m_attn_w_q, m_attn_w_o, m_ffn_norm, m_ffn_w1, m_ffn_w2, m_ple_norm, m_ple_w_gate, m_ple_w_proj, m_final_norm, v_mix_norm, v_conv_w_pw1, v_conv_b_pw1, v_conv_w_dw, v_conv_b_dw, v_conv_ln_g, v_conv_ln_b, v_conv_w_pw2, v_conv_b_pw2, v_kv_norm, v_w_kvf, v_b_f, v_attn_w_q, v_attn_w_o, v_ffn_norm, v_ffn_w1, v_ffn_w2, v_ple_norm, v_ple_w_gate, v_ple_w_proj, v_final_norm):
    args = dict(locals())
    wl = {n: args[n] for n in WEIGHTS}
    ml = {n: args["m_" + n] for n in WEIGHTS}
    vl = {n: args["v_" + n] for n in WEIGHTS}
    S, D = x.shape[1], x.shape[2]
    C = D

    mat_shapes = [wl[n].shape for n, _ in MATS]
    vec_shapes = [wl[n].shape for n, _ in VECS]
    mats = _all_gather_shards(_pack([wl[n].astype(BF16) for n, _ in MATS], C, 16, 64), name="ag_mats")
    vecs = _all_gather_shards(_pack([wl[n] for n, _ in VECS], C, 1, 32), name="ag_vecs")
    full = {n: wl[n] for n in REPL}
    for (n, ax), t in zip(MATS, _unpack(mats, mat_shapes, C, 16)):
        full[n] = _unshard(t, ax)
    for (n, ax), t in zip(VECS, _unpack(vecs, vec_shapes, C, 1)):
        full[n] = _unshard(t, ax)

    loss, dx, g = _local_step(x[0], p[:, 0], loss_target[0], full)
    loss = lax.psum(loss, ("x", "y", "c"))

    names = [n for n, _ in MATS] + [n for n, _ in VECS]
    axes = dict(MATS + VECS)
    shard_shapes = [wl[n].shape for n in names]
    per_shard = [[] for _ in range(N_CHIPS)]
    for n in names:
        for s, piece in enumerate(_shards(g[n].astype(BF16), axes[n])):
            per_shard[s].append(piece)
    gpack = jnp.stack([_pack(ts, C, 16, RS_ROW_MULT) for ts in per_shard], axis=0)
    gred = _reduce_scatter(gpack, name="rs")
    gl = dict(zip(names, _unpack(gred, shard_shapes, C, 16)))

    rep_shapes = [wl[n].shape for n in REPL]
    rpack = _pack([g[n] for n in REPL], C, 1, 8)
    for n, t in zip(REPL, _unpack(_all_reduce_small(rpack, name="ar"), rep_shapes, C, 1)):
        gl[n] = t

    grads, deltas, new_m, new_v = {}, {}, {}, {}
    rep_w, rep_m, rep_v = (_pack([d[n] for n in REPL], C, 1, 8) for d in (wl, ml, vl))
    rep_out = _adamw(rep_w, rpack_like(gl, rep_shapes, C), rep_m, rep_v, "adamw_rep")
    for dst, packed in zip((grads, deltas, new_m, new_v), rep_out):
        for n, t in zip(REPL, _unpack(packed, rep_shapes, C, 1)):
            dst[n] = t
    for n in names:
        res = _adamw(wl[n], gl[n], ml[n], vl[n], "adamw_" + n)
        for dst, t in zip((grads, deltas, new_m, new_v), res):
            dst[n] = t
    out = [loss, dx[None]]
    for d in (grads, deltas, new_m, new_v):
        out += [d[n] for n in WEIGHTS]
    return tuple(out)


def rpack_like(gl, rep_shapes, C):
    return _pack([gl[n] for n in REPL], C, 1, 8)
```

```python
import functools

import jax
import jax.numpy as jnp
import numpy as np
from jax import lax
from jax.experimental import pallas as pl
from jax.experimental.pallas import tpu as pltpu

F32 = jnp.float32
BF16 = jnp.bfloat16
MESH = pl.DeviceIdType.MESH

N_CHIPS = 4
N_DEV = 8
HEAD_DIM = 64
LANES = 128
CONV_WIDTH = 31
CONV_PAD = 32
EPS = 1e-6
NEG_BIG = -1e30
VMEM_LIMIT = 56 * 1024 * 1024

RS_ROW_MULT = 1024

ADAM_LR, ADAM_B1, ADAM_B2, ADAM_EPS, ADAM_WD, ADAM_STEP = 0.001, 0.9, 0.999, 1e-08, 0.01, 10


def _pallas(body, **kw):
    return pl.pallas_call(body, **kw)


def _params(sem=None):
    return pltpu.CompilerParams(dimension_semantics=sem, vmem_limit_bytes=VMEM_LIMIT)


def _sds(shape, dtype):
    return jax.ShapeDtypeStruct(tuple(shape), dtype)


_DIMS = {"nn": (((1,), (0,)), ((), ())), "nt": (((1,), (1,)), ((), ())), "tn": (((0,), (0,)), ((), ()))}


def _mm(a, b, *, mode="nn", extras=(), epi=None, out_dtype=F32, reds=0, tm=1024, tn=1024, tk=1024, name):
    if mode == "nn":
        (M, K), (K2, N) = a.shape, b.shape
    elif mode == "nt":
        (M, K), (N, K2) = a.shape, b.shape
    else:
        (K, M), (K2, N) = a.shape, b.shape
    assert K == K2, (name, a.shape, b.shape)
    tm, tn, tk = min(tm, M), min(tn, N), min(tk, K)
    assert M % tm == 0 and N % tn == 0 and K % tk == 0, (name, a.shape, b.shape)
    nk = K // tk
    if mode == "tn":
        a_spec = pl.BlockSpec((tk, tm), lambda i, j, k: (k, i))
    else:
        a_spec = pl.BlockSpec((tm, tk), lambda i, j, k: (i, k))
    if mode == "nt":
        b_spec = pl.BlockSpec((tn, tk), lambda i, j, k: (j, k))
    else:
        b_spec = pl.BlockSpec((tk, tn), lambda i, j, k: (k, j))
    ex_specs = []
    for e in extras:
        if e.shape[0] == 1:
            ex_specs.append(pl.BlockSpec((1, tn), lambda i, j, k: (0, j)))
        else:
            assert e.shape == (M, N), (name, e.shape)
            ex_specs.append(pl.BlockSpec((tm, tn), lambda i, j, k: (i, j)))
    ne = len(extras)
    dims = _DIMS[mode]
    many = isinstance(out_dtype, (list, tuple))
    out_dtypes = list(out_dtype) if many else [out_dtype]
    no = len(out_dtypes)
    assert not reds or tn == N, name

    def body(a_ref, b_ref, *rest):
        ex_refs, o_refs, r_refs = rest[:ne], rest[ne:ne + no], rest[ne + no:ne + no + reds]
        part = lax.dot_general(a_ref[...].astype(BF16), b_ref[...].astype(BF16), dims, preferred_element_type=F32)
        i = pl.program_id(0)

        def finish(acc):
            res = epi(acc, *[r[...] for r in ex_refs]) if epi is not None else acc
            if not isinstance(res, (tuple, list)):
                res = (res,)
            assert len(res) == no + reds, (name, len(res))
            for r, v in zip(o_refs, res[:no]):
                r[...] = v.astype(r.dtype)
            for r, v in zip(r_refs, res[no:]):
                @pl.when(i == 0)
                def _(r=r, v=v):
                    r[...] = v

                @pl.when(i > 0)
                def _(r=r, v=v):
                    r[...] += v

        if nk == 1:
            finish(part)
        else:
            acc_ref = rest[ne + no + reds]
            k = pl.program_id(2)

            @pl.when(k == 0)
            def _():
                acc_ref[...] = part

            @pl.when(k > 0)
            def _():
                acc_ref[...] += part

            @pl.when(k == nk - 1)
            def _():
                finish(acc_ref[...])

    res = _pallas(
        body, name=name, grid=(M // tm, N // tn, nk),
        in_specs=[a_spec, b_spec] + ex_specs,
        out_specs=[pl.BlockSpec((tm, tn), lambda i, j, k: (i, j))] * no + [pl.BlockSpec((1, tn), lambda i, j, k: (0, j))] * reds,
        out_shape=[_sds((M, N), dt) for dt in out_dtypes] + [_sds((1, N), F32)] * reds,
        scratch_shapes=[pltpu.VMEM((tm, tn), F32)] if nk > 1 else [],
        compiler_params=_params(("arbitrary",) * 3 if reds else ("parallel", "parallel", "arbitrary")),
    )(a, b, *extras)
    return res if (many or reds) else res[0]


def _rows(fn, ins, params, outs, reds=(), *, tm=256, name):
    S = ins[0].shape[0]
    tm = min(tm, S)
    assert S % tm == 0, (name, S, tm)
    ni, npar, no, nr = len(ins), len(params), len(outs), len(reds)

    def body(*refs):
        in_refs, p_refs = refs[:ni], refs[ni:ni + npar]
        o_refs, r_refs = refs[ni + npar:ni + npar + no], refs[ni + npar + no:]
        res = fn(*[r[...] for r in in_refs], *[r[...] for r in p_refs])
        if not isinstance(res, (tuple, list)):
            res = (res,)
        assert len(res) == no + nr, (name, len(res))
        for r, v in zip(o_refs, res[:no]):
            r[...] = v.astype(r.dtype)
        i = pl.program_id(0)
        for r, v in zip(r_refs, res[no:]):
            @pl.when(i == 0)
            def _(r=r, v=v):
                r[...] = v

            @pl.when(i > 0)
            def _(r=r, v=v):
                r[...] += v

    res = _pallas(
        body, name=name, grid=(S // tm,),
        in_specs=[pl.BlockSpec((tm, a.shape[1]), lambda i: (i, 0)) for a in ins]
        + [pl.BlockSpec(p.shape, lambda i: (0, 0)) for p in params],
        out_specs=[pl.BlockSpec((tm, o.shape[1]), lambda i: (i, 0)) for o in outs]
        + [pl.BlockSpec(r.shape, lambda i: (0, 0)) for r in reds],
        out_shape=list(outs) + list(reds),
        compiler_params=_params(("arbitrary",)),
    )(*ins, *params)
    return res


def _colsum(v):
    return jnp.sum(v, axis=0, keepdims=True)


def _sigmoid(v):
    return 1.0 / (1.0 + jnp.exp(-v))


def _rms_stats(x):
    r = lax.rsqrt(jnp.mean(x * x, axis=-1, keepdims=True) + EPS)
    return x * r, r


def _rms_fwd(x, g):
    xh, _ = _rms_stats(x)
    return (xh * g,)


def _with_rms(h, g):
    xh, _ = _rms_stats(h)
    return h, xh * g


def _rms_bwd(dy, x, dres, g):
    xh, r = _rms_stats(x)
    dyg = dy * g
    dx = r * (dyg - xh * jnp.mean(dyg * xh, axis=-1, keepdims=True))
    return dres + dx, _colsum(dy * xh)


def _dup(fn):
    def wrapped(*a):
        r = fn(*a)
        return (r[0], r[0]) + tuple(r[1:])
    return wrapped


def _rms_bwd_bias(dy, x, dres, g):
    dx, dg = _rms_bwd(dy, x, dres, g)
    return dx, dg, _colsum(dx)


def _glu_fwd(u):
    d = u.shape[1] // 2
    return (u[:, :d] * _sigmoid(u[:, d:]),)


def _glu_bwd(dglu, u):
    d = u.shape[1] // 2
    a, sig = u[:, :d], _sigmoid(u[:, d:])
    du = jnp.concatenate([dglu * sig, dglu * a * sig * (1.0 - sig)], axis=1)
    return du, _colsum(du)


def _ln_parts(x, g, b):
    mu = jnp.mean(x, axis=-1, keepdims=True)
    xc = x - mu
    r = lax.rsqrt(jnp.mean(xc * xc, axis=-1, keepdims=True) + EPS)
    xh = xc * r
    return xh, r, xh * g + b


def _ln_silu_fwd(x, g, b):
    _, _, y = _ln_parts(x, g, b)
    return (y * _sigmoid(y),)


def _ln_silu_bwd(dsw, x, g, b):
    xh, r, y = _ln_parts(x, g, b)
    sig = _sigmoid(y)
    dy = dsw * sig * (1.0 + y * (1.0 - sig))
    dxh = dy * g
    dx = r * (dxh - jnp.mean(dxh, axis=-1, keepdims=True) - xh * jnp.mean(dxh * xh, axis=-1, keepdims=True))
    return dx, _colsum(dy * xh), _colsum(dy)


def _relu2(z):
    zp = jnp.maximum(z, 0.0)
    return (zp * zp,)


def _ple_fwd(h, zg, pp):
    return (h + _sigmoid(zg) * pp,)


def _ple_bwd(dh, zg, pp):
    gate = _sigmoid(zg)
    return dh * pp * gate * (1.0 - gate), dh * gate


def _final_fn(h, t, g):
    xh, r = _rms_stats(h)
    err = xh * g - t
    dy = err * (1.0 / h.shape[1])
    dyg = dy * g
    dh = r * (dyg - xh * jnp.mean(dyg * xh, axis=-1, keepdims=True))
    return dh, _colsum(err * err), _colsum(dy * xh)


def _add2(a, b):
    return (a + b,)


def _adamw_fn(w, g, m, v):
    m = ADAM_B1 * m + (1.0 - ADAM_B1) * g
    v = ADAM_B2 * v + (1.0 - ADAM_B2) * (g * g)
    m_hat = m / (1.0 - ADAM_B1 ** ADAM_STEP)
    v_hat = v / (1.0 - ADAM_B2 ** ADAM_STEP)
    delta = -ADAM_LR * (m_hat / (jnp.sqrt(v_hat) + ADAM_EPS) + ADAM_WD * w)
    return g, delta, m, v


def _adamw(w, g, m, v, name):
    shape = w.shape
    cols = shape[-1] if len(shape) > 1 else shape[0]
    two = lambda t: t.reshape(-1, cols)
    o = _sds(two(w).shape, F32)
    res = _rows(_adamw_fn, [two(w), two(g), two(m), two(v)], [], [o, o, o, o], tm=512, name=name)
    return [r.reshape(shape) for r in res]


def _dwconv_fwd(u, w, b, *, tm=512, name):
    S, D = u.shape
    tm = min(tm, S)
    rc = min(128, tm)
    per = tm // CONV_PAD

    def body(prev_ref, cur_ref, w_ref, b_ref, o_ref, win):
        i = pl.program_id(0)

        @pl.when(i == 0)
        def _():
            win[0:CONV_PAD, :] = jnp.zeros((CONV_PAD, D), F32)

        @pl.when(i > 0)
        def _():
            win[0:CONV_PAD, :] = prev_ref[...]

        win[CONV_PAD:CONV_PAD + tm, :] = cur_ref[...]
        for lc in range(D // LANES):
            ls = slice(lc * LANES, (lc + 1) * LANES)
            for r0 in range(0, tm, rc):
                acc = jnp.zeros((rc, LANES), F32) + b_ref[:, ls]
                for k in range(CONV_WIDTH):
                    acc = acc + win[r0 + 2 + k:r0 + 2 + k + rc, ls] * w_ref[k:k + 1, ls]
                o_ref[r0:r0 + rc, ls] = acc

    return _pallas(
        body, name=name, grid=(S // tm,),
        in_specs=[pl.BlockSpec((CONV_PAD, D), lambda i: (jnp.maximum(i * per - 1, 0), 0)),
                  pl.BlockSpec((tm, D), lambda i: (i, 0)),
                  pl.BlockSpec((CONV_PAD, D), lambda i: (0, 0)),
                  pl.BlockSpec((1, D), lambda i: (0, 0))],
        out_specs=pl.BlockSpec((tm, D), lambda i: (i, 0)),
        out_shape=_sds((S, D), F32),
        scratch_shapes=[pltpu.VMEM((tm + CONV_PAD, D), F32)],
        compiler_params=_params(("arbitrary",)),
    )(u, u, w, b)


def _dwconv_bwd(dy, u, w, *, tm=512, name):
    S, D = u.shape
    tm = min(tm, S)
    rc = min(128, tm)
    per = tm // CONV_PAD
    n = S // tm
    nxt = S // CONV_PAD - 1

    def body(dy_ref, dyn_ref, up_ref, u_ref, w_ref, du_ref, dw_ref, db_ref, wd, wu, dwacc, dbacc):
        i = pl.program_id(0)

        @pl.when(i == 0)
        def _():
            wu[0:CONV_PAD, :] = jnp.zeros((CONV_PAD, D), F32)
            dwacc[...] = jnp.zeros(dwacc.shape, F32)
            dbacc[...] = jnp.zeros(dbacc.shape, F32)

        @pl.when(i > 0)
        def _():
            wu[0:CONV_PAD, :] = up_ref[...]

        @pl.when(i == n - 1)
        def _():
            wd[tm:tm + CONV_PAD, :] = jnp.zeros((CONV_PAD, D), F32)

        @pl.when(i < n - 1)
        def _():
            wd[tm:tm + CONV_PAD, :] = dyn_ref[...]

        wu[CONV_PAD:CONV_PAD + tm, :] = u_ref[...]
        wd[0:tm, :] = dy_ref[...]
        for lc in range(D // LANES):
            ls = slice(lc * LANES, (lc + 1) * LANES)
            for r0 in range(0, tm, rc):
                acc = jnp.zeros((rc, LANES), F32)
                for k in range(CONV_WIDTH):
                    acc = acc + wd[r0 + 30 - k:r0 + 30 - k + rc, ls] * w_ref[k:k + 1, ls]
                du_ref[r0:r0 + rc, ls] = acc
                dyc = wd[r0:r0 + rc, ls]
                dbacc[:, ls] += jnp.sum(dyc.reshape(rc // 8, 8, LANES), axis=0)
                for k in range(CONV_WIDTH):
                    prod = dyc * wu[r0 + 2 + k:r0 + 2 + k + rc, ls]
                    dwacc[8 * k:8 * k + 8, ls] += jnp.sum(prod.reshape(rc // 8, 8, LANES), axis=0)

        @pl.when(i == n - 1)
        def _():
            dw_ref[...] = jnp.zeros(dw_ref.shape, F32)
            for k in range(CONV_WIDTH):
                dw_ref[k:k + 1, :] = jnp.sum(dwacc[8 * k:8 * k + 8, :], axis=0, keepdims=True)
            db_ref[...] = jnp.sum(dbacc[...], axis=0, keepdims=True)

    return _pallas(
        body, name=name, grid=(n,),
        in_specs=[pl.BlockSpec((tm, D), lambda i: (i, 0)),
                  pl.BlockSpec((CONV_PAD, D), lambda i: (jnp.minimum((i + 1) * per, nxt), 0)),
                  pl.BlockSpec((CONV_PAD, D), lambda i: (jnp.maximum(i * per - 1, 0), 0)),
                  pl.BlockSpec((tm, D), lambda i: (i, 0)),
                  pl.BlockSpec((CONV_PAD, D), lambda i: (0, 0))],
        out_specs=[pl.BlockSpec((tm, D), lambda i: (i, 0)),
                   pl.BlockSpec((CONV_PAD, D), lambda i: (0, 0)),
                   pl.BlockSpec((1, D), lambda i: (0, 0))],
        out_shape=[_sds((S, D), F32), _sds((CONV_PAD, D), F32), _sds((1, D), F32)],
        scratch_shapes=[pltpu.VMEM((tm + CONV_PAD, D), F32), pltpu.VMEM((tm + CONV_PAD, D), F32),
                        pltpu.VMEM((8 * CONV_PAD, D), F32), pltpu.VMEM((8, D), F32)],
        compiler_params=_params(("arbitrary",)),
    )(dy, dy, u, u, w)


def _tri_dot(tri, x):
    x1 = x.astype(BF16)
    r1 = x - x1.astype(F32)
    x2 = r1.astype(BF16)
    x3 = (r1 - x2.astype(F32)).astype(BF16)
    d = lambda v: jnp.dot(tri, v, preferred_element_type=F32)
    return d(x1) + d(x2) + d(x3)


def _log_sigmoid(x):
    return jnp.minimum(x, 0.0) - jnp.log(1.0 + jnp.exp(-jnp.abs(x)))


def _gate_cumsum(fl, bf, *, tm=256, name):
    S, W = fl.shape
    tm = min(tm, S)

    def body(fl_ref, bf_ref, c_ref, carry):
        i = pl.program_id(0)

        @pl.when(i == 0)
        def _():
            carry[...] = jnp.zeros(carry.shape, F32)

        x = _log_sigmoid(fl_ref[...] + bf_ref[...])
        row = lax.broadcasted_iota(jnp.int32, (tm, tm), 0)
        col = lax.broadcasted_iota(jnp.int32, (tm, tm), 1)
        tri = jnp.where(row >= col, 1.0, 0.0).astype(BF16)
        cs = _tri_dot(tri, x) + carry[0:1, :]
        c_ref[...] = cs
        carry[...] = jnp.broadcast_to(cs[tm - 1:tm, :], carry.shape)

    return _pallas(
        body, name=name, grid=(S // tm,),
        in_specs=[pl.BlockSpec((tm, W), lambda i: (i, 0)), pl.BlockSpec((1, W), lambda i: (0, 0))],
        out_specs=pl.BlockSpec((tm, W), lambda i: (i, 0)),
        out_shape=_sds((S, W), F32),
        scratch_shapes=[pltpu.VMEM((8, W), F32)],
        compiler_params=_params(("arbitrary",)),
    )(fl, bf)


def _gate_cumsum_bwd(sums, fl, bf, *, tm=256, name):
    S, W = fl.shape
    tm = min(tm, S)
    n = S // tm
    ns = len(sums)
    assert ns % 2 == 0

    def body(*refs):
        sum_refs = refs[:ns]
        fl_ref, bf_ref, o_ref, s_ref, carry = refs[ns:]
        i = pl.program_id(0)

        @pl.when(i == 0)
        def _():
            carry[...] = jnp.zeros(carry.shape, F32)
            s_ref[...] = jnp.zeros(s_ref.shape, F32)

        dc = sum_refs[0][...] - sum_refs[1][...]
        for a in range(2, ns, 2):
            dc = dc + (sum_refs[a][...] - sum_refs[a + 1][...])
        row = lax.broadcasted_iota(jnp.int32, (tm, tm), 0)
        col = lax.broadcasted_iota(jnp.int32, (tm, tm), 1)
        tri = jnp.where(col >= row, 1.0, 0.0).astype(BF16)
        rs = _tri_dot(tri, dc) + carry[0:1, :]
        carry[...] = jnp.broadcast_to(rs[0:1, :], carry.shape)
        dfl = rs * _sigmoid(-(fl_ref[...] + bf_ref[...]))
        o_ref[...] = dfl
        s_ref[...] += _colsum(dfl)

    rev = lambda i: (n - 1 - i, 0)
    return _pallas(
        body, name=name, grid=(n,),
        in_specs=[pl.BlockSpec((tm, W), rev)] * (ns + 1) + [pl.BlockSpec((1, W), lambda i: (0, 0))],
        out_specs=[pl.BlockSpec((tm, W), rev), pl.BlockSpec((1, W), lambda i: (0, 0))],
        out_shape=[_sds((S, W), F32), _sds((1, W), F32)],
        scratch_shapes=[pltpu.VMEM((8, W), F32)],
        compiler_params=_params(("arbitrary",)),
    )(*sums, fl, bf)


def _tri_tables(nb, by_query):
    ii, jj = [], []
    if by_query:
        for i in range(nb):
            for j in range(i + 1):
                ii.append(i)
                jj.append(j)
    else:
        for j in range(nb):
            for i in range(j, nb):
                ii.append(i)
                jj.append(j)
    return jnp.asarray(np.array(ii, np.int32)), jnp.asarray(np.array(jj, np.int32))


def _rep(v, t):
    return jnp.tile(v, (1, t // LANES))


def _attn_fwd(q, kv, ck, *, tb=512, row_chunks=2, name):
    S, D = q.shape
    HP = D // LANES
    T = min(tb, S)
    ROW_CHUNKS = row_chunks
    nb = S // T
    it, jt = _tri_tables(nb, True)

    def body(it_ref, jt_ref, q_ref, k_ref, v_ref, ck_ref, o_ref, lse_ref, st):
        s_id = pl.program_id(1)
        i, j = it_ref[s_id], jt_ref[s_id]
        lane = lax.broadcasted_iota(jnp.int32, (T, LANES), 1)
        head0 = lane < HEAD_DIM

        @pl.when(j == 0)
        def _():
            st[0:2] = jnp.full((2, T, LANES), NEG_BIG, F32)
            st[2:4] = jnp.zeros((2, T, LANES), F32)

        def step(masked):
            kvv, vv = k_ref[...], v_ref[...]
            one = jnp.ones_like(vv)
            RQ = T // ROW_CHUNKS
            h0 = head0[:RQ]
            hms = (h0, jnp.logical_not(h0))
            vaug = [jnp.where(head0 if h == 0 else jnp.logical_not(head0), vv, one) for h in range(2)]
            old = st[...]
            new = [[], [], [], []]
            chains = [(r, h) for r in range(ROW_CHUNKS) for h in range(2)]
            rows = lambda r: slice(r * RQ, (r + 1) * RQ)
            ss = {}
            for r, h in chains:
                qv = q_ref[rows(r), :]
                qm = jnp.where(hms[h], qv, jnp.zeros_like(qv))
                ss[r, h] = lax.dot_general(qm, kvv, _DIMS["nt"], preferred_element_type=F32)
            ps, alphas = {}, {}
            for r, h in chains:
                s = ss[r, h] - ck_ref[h:h + 1, :]
                if masked:
                    row = lax.broadcasted_iota(jnp.int32, (RQ, T), 0) + r * RQ
                    col = lax.broadcasted_iota(jnp.int32, (RQ, T), 1)
                    s = jnp.where(row >= col, s, NEG_BIG)
                m_prev = old[h, rows(r), :]
                m_new = jnp.maximum(m_prev, jnp.max(s, axis=1, keepdims=True))
                ps[r, h] = jnp.exp(s - _rep(m_new, T)).astype(BF16)
                alphas[r, h] = jnp.exp(m_prev - m_new)
                new[h].append(m_new)
            for r in range(ROW_CHUNKS):
                pv = [jnp.dot(ps[r, h], vaug[h], preferred_element_type=F32) for h in range(2)]
                a0, a1 = alphas[r, 0], alphas[r, 1]
                new[2].append(jnp.where(h0, a0, a1) * old[2, rows(r), :] + jnp.where(h0, pv[0], pv[1]))
                new[3].append(jnp.where(h0, a1, a0) * old[3, rows(r), :] + jnp.where(h0, pv[1], pv[0]))
            res = jnp.stack([jnp.concatenate(n, axis=0) for n in new], axis=0)
            st[...] = res
            return res

        @pl.when(j < i)
        def _():
            step(False)

        @pl.when(j == i)
        def _():
            res = step(True)
            lr = pltpu.roll(res[3], HEAD_DIM, axis=1)
            o_ref[...] = res[2] / lr
            lse_ref[0] = res[0] + jnp.log(jnp.where(head0, lr, res[3]))
            lse_ref[1] = res[1] + jnp.log(jnp.where(head0, res[3], lr))

    grid_spec = pltpu.PrefetchScalarGridSpec(
        num_scalar_prefetch=2, grid=(HP, it.shape[0]),
        in_specs=[pl.BlockSpec((T, LANES), lambda h, s, it, jt: (it[s], h)),
                  pl.BlockSpec((T, LANES), lambda h, s, it, jt: (jt[s], h)),
                  pl.BlockSpec((T, LANES), lambda h, s, it, jt: (jt[s], HP + h)),
                  pl.BlockSpec((None, 8, T), lambda h, s, it, jt: (h, 0, jt[s]))],
        out_specs=[pl.BlockSpec((T, LANES), lambda h, s, it, jt: (it[s], h)),
                   pl.BlockSpec((2, T, LANES), lambda h, s, it, jt: (h, it[s], 0))],
        scratch_shapes=[pltpu.VMEM((4, T, LANES), F32)],
    )
    return _pallas(
        body, name=name, grid_spec=grid_spec,
        out_shape=[_sds((S, D), F32), _sds((2 * HP, S, LANES), F32)],
        compiler_params=_params(("parallel", "arbitrary")),
    )(it, jt, q, kv, kv, ck)


def _attn_bwd(q, kv, o, do, lse, ck, *, tb=512, name):
    S, D = q.shape
    HP = D // LANES
    T = min(tb, S)
    nb = S // T
    it, jt = _tri_tables(nb, False)

    def body(it_ref, jt_ref, q_ref, k_ref, v_ref, o_ref, do_ref, lse_ref, ck_ref, dq_ref, drs_ref, dk_ref, dv_ref, dcs_ref):
        s_id = pl.program_id(1)
        i, j = it_ref[s_id], jt_ref[s_id]
        lane = lax.broadcasted_iota(jnp.int32, (T, LANES), 1)
        head0 = lane < HEAD_DIM

        @pl.when(s_id == 0)
        def _():
            dq_ref[...] = jnp.zeros(dq_ref.shape, F32)
            drs_ref[...] = jnp.zeros(drs_ref.shape, F32)

        @pl.when(i == j)
        def _():
            dk_ref[...] = jnp.zeros(dk_ref.shape, F32)
            dv_ref[...] = jnp.zeros(dv_ref.shape, F32)
            dcs_ref[...] = jnp.zeros(dcs_ref.shape, F32)

        def step(masked):
            qv, kvv, vv = q_ref[...], k_ref[...], v_ref[...]
            dob = do_ref[...].astype(BF16)
            prod = dob.astype(F32) * o_ref[...]
            one = jnp.ones_like(qv)
            zero = jnp.zeros_like(qv)
            hms = (head0, jnp.logical_not(head0))
            ss = [lax.dot_general(jnp.where(hms[h], qv, zero), kvv, _DIMS["nt"], preferred_element_type=F32) for h in range(2)]
            dps = [lax.dot_general(jnp.where(hms[h], dob, zero), vv, _DIMS["nt"], preferred_element_type=F32) for h in range(2)]
            pbs, dsbs = [], []
            for h in range(2):
                s = ss[h] - ck_ref[h:h + 1, :]
                if masked:
                    row = lax.broadcasted_iota(jnp.int32, (T, T), 0)
                    col = lax.broadcasted_iota(jnp.int32, (T, T), 1)
                    s = jnp.where(row >= col, s, NEG_BIG)
                p = jnp.exp(s - _rep(lse_ref[h], T))
                delta = jnp.sum(jnp.where(hms[h], prod, 0.0), axis=1, keepdims=True)
                pbs.append(p.astype(BF16))
                dsbs.append((p * (dps[h] - delta)).astype(BF16))
            dvs, dks, dqs = [], [], []
            for h in range(2):
                dqs.append(jnp.dot(dsbs[h], jnp.where(hms[h], kvv, one), preferred_element_type=F32))
                dvs.append(lax.dot_general(pbs[h], dob, _DIMS["tn"], preferred_element_type=F32))
                dks.append(lax.dot_general(dsbs[h], jnp.where(hms[h], qv, one), _DIMS["tn"], preferred_element_type=F32))
            dv_ref[...] += jnp.where(head0, dvs[0], dvs[1])
            dk_ref[...] += jnp.where(head0, dks[0], dks[1])
            dcs_ref[...] += jnp.where(head0, dks[1], dks[0])
            rows = pl.ds(pl.multiple_of(i * T, T), T)
            dq_ref[rows, :] += jnp.where(head0, dqs[0], dqs[1])
            drs_ref[rows, :] += jnp.where(head0, dqs[1], dqs[0])

        @pl.when(i > j)
        def _():
            step(False)

        @pl.when(i == j)
        def _():
            step(True)

    by_q = lambda h, s, it, jt: (it[s], h)
    by_k = lambda h, s, it, jt: (jt[s], h)
    whole = lambda h, s, it, jt: (0, h)
    grid_spec = pltpu.PrefetchScalarGridSpec(
        num_scalar_prefetch=2, grid=(HP, it.shape[0]),
        in_specs=[pl.BlockSpec((T, LANES), by_q),
                  pl.BlockSpec((T, LANES), by_k),
                  pl.BlockSpec((T, LANES), lambda h, s, it, jt: (jt[s], HP + h)),
                  pl.BlockSpec((T, LANES), by_q),
                  pl.BlockSpec((T, LANES), by_q),
                  pl.BlockSpec((2, T, LANES), lambda h, s, it, jt: (h, it[s], 0)),
                  pl.BlockSpec((None, 8, T), lambda h, s, it, jt: (h, 0, jt[s]))],
        out_specs=[pl.BlockSpec((S, LANES), whole), pl.BlockSpec((S, LANES), whole),
                   pl.BlockSpec((T, LANES), by_k), pl.BlockSpec((T, LANES), by_k), pl.BlockSpec((T, LANES), by_k)],
        scratch_shapes=[],
    )
    return _pallas(
        body, name=name, grid_spec=grid_spec,
        out_shape=[_sds((S, D), F32)] * 5,
        compiler_params=_params(("parallel", "arbitrary")),
    )(it, jt, q, kv, kv, o, do, lse, ck)


ANY = pl.BlockSpec(memory_space=pl.ANY)


def _coords():
    x, y, c = lax.axis_index("x"), lax.axis_index("y"), lax.axis_index("c")
    return x, y, c


def _remote(src, dst, send_sems, recv_sems, k, to):
    return pltpu.make_async_remote_copy(src_ref=src, dst_ref=dst, send_sem=send_sems.at[k], recv_sem=recv_sems.at[k],
                                        device_id=to, device_id_type=MESH)


def _all_gather_shards(pack, *, name):
    R, C = pack.shape
    assert R % 4 == 0
    H, Q = R // 2, R // 4

    def body(in_ref, out_ref, send_sems, recv_sems):
        x, y, c = _coords()
        me, sib = (x, y, c), (x, y, 1 - c)
        xn, yn = (1 - x, y, c), (x, 1 - y, c)
        s, sx, sy, sd = 2 * x + y, 2 * (1 - x) + y, 2 * x + 1 - y, 2 * (1 - x) + 1 - y
        half = pl.ds(c * H, H)
        other = pl.ds((1 - c) * H, H)
        q0 = pl.ds(c * H, Q)
        q1 = pl.ds(c * H + Q, Q)
        rc = functools.partial(_remote, send_sems=send_sems, recv_sems=recv_sems)

        sends = [rc(in_ref.at[half], out_ref.at[s, half], k=0, to=xn),
                 rc(in_ref.at[half], out_ref.at[s, half], k=1, to=yn),
                 rc(in_ref, out_ref.at[s], k=7, to=sib)]
        for cp in sends:
            cp.start()
        rc(in_ref.at[half], out_ref.at[sx, half], k=0, to=me).wait_recv()
        sends.append(rc(out_ref.at[sx, q0], out_ref.at[sx, q0], k=2, to=yn))
        sends[-1].start()
        sends.append(rc(out_ref.at[sx, half], out_ref.at[sx, half], k=4, to=sib))
        sends[-1].start()
        rc(in_ref.at[half], out_ref.at[sy, half], k=1, to=me).wait_recv()
        sends.append(rc(out_ref.at[sy, q1], out_ref.at[sy, q1], k=3, to=xn))
        sends[-1].start()
        sends.append(rc(out_ref.at[sy, half], out_ref.at[sy, half], k=5, to=sib))
        sends[-1].start()
        rc(out_ref.at[sd, q0], out_ref.at[sd, q0], k=2, to=me).wait_recv()
        rc(out_ref.at[sd, q1], out_ref.at[sd, q1], k=3, to=me).wait_recv()
        sends.append(rc(out_ref.at[sd, half], out_ref.at[sd, half], k=6, to=sib))
        sends[-1].start()
        for k, sh in ((4, sx), (5, sy), (6, sd)):
            rc(out_ref.at[sh, other], out_ref.at[sh, other], k=k, to=me).wait_recv()
        rc(in_ref, out_ref.at[s], k=7, to=me).wait_recv()
        for cp in sends:
            cp.wait_send()

    return _pallas(
        body, name=name, in_specs=[ANY], out_specs=ANY,
        out_shape=_sds((N_CHIPS, R, C), pack.dtype),
        scratch_shapes=[pltpu.SemaphoreType.DMA((8,)), pltpu.SemaphoreType.DMA((8,))],
    )(pack)


def _rs_pair(g, *, name):
    n, R, C = g.shape
    H = R // 2

    def body(g_ref, land_ref, send_sems, recv_sems):
        x, y, c = _coords()
        other = pl.ds((1 - c) * H, H)
        cps = [_remote(g_ref.at[sh, other], land_ref.at[sh], send_sems, recv_sems, sh, (x, y, 1 - c)) for sh in range(n)]
        for cp in cps:
            cp.start()
        for cp in cps:
            cp.wait_recv()
        for cp in cps:
            cp.wait_send()

    return _pallas(
        body, name=name, in_specs=[ANY], out_specs=ANY, out_shape=_sds((n, H, C), g.dtype),
        scratch_shapes=[pltpu.SemaphoreType.DMA((n,)), pltpu.SemaphoreType.DMA((n,))],
    )(g)


def _rs_quarters(p, *, name):
    n, H, C = p.shape
    Q = H // 2

    def body(p_ref, la_ref, lb_ref, send_sems, recv_sems):
        x, y, c = _coords()
        sd = 2 * (1 - x) + 1 - y
        a = _remote(p_ref.at[sd, pl.ds(0, Q)], la_ref, send_sems, recv_sems, 0, (x, 1 - y, c))
        b = _remote(p_ref.at[sd, pl.ds(Q, Q)], lb_ref, send_sems, recv_sems, 1, (1 - x, y, c))
        a.start()
        b.start()
        a.wait_recv()
        b.wait_recv()
        a.wait_send()
        b.wait_send()

    return _pallas(
        body, name=name, in_specs=[ANY], out_specs=[ANY, ANY],
        out_shape=[_sds((Q, C), p.dtype), _sds((Q, C), p.dtype)],
        scratch_shapes=[pltpu.SemaphoreType.DMA((2,)), pltpu.SemaphoreType.DMA((2,))],
    )(p)


def _rs_halves(p, ax, ay, *, name):
    n, H, C = p.shape
    Q = H // 2

    def body(p_ref, ax_ref, ay_ref, la_ref, lb_ref, send_sems, recv_sems):
        x, y, c = _coords()
        sx, sy = 2 * (1 - x) + y, 2 * x + 1 - y
        xn, yn = (1 - x, y, c), (x, 1 - y, c)
        lo, hi = pl.ds(0, Q), pl.ds(Q, Q)
        cps = [_remote(ax_ref, la_ref.at[lo], send_sems, recv_sems, 0, xn),
               _remote(p_ref.at[sx, hi], la_ref.at[hi], send_sems, recv_sems, 1, xn),
               _remote(p_ref.at[sy, lo], lb_ref.at[lo], send_sems, recv_sems, 2, yn),
               _remote(ay_ref, lb_ref.at[hi], send_sems, recv_sems, 3, yn)]
        for cp in cps:
            cp.start()
        for cp in cps:
            cp.wait_recv()
        for cp in cps:
            cp.wait_send()

    return _pallas(
        body, name=name, in_specs=[ANY, ANY, ANY], out_specs=[ANY, ANY],
        out_shape=[_sds((H, C), p.dtype), _sds((H, C), p.dtype)],
        scratch_shapes=[pltpu.SemaphoreType.DMA((4,)), pltpu.SemaphoreType.DMA((4,))],
    )(p, ax, ay)


def _rs_join(buf, *, name):
    R, C = buf.shape
    H = R // 2

    def body(in_ref, out_ref, send_sems, recv_sems):
        x, y, c = _coords()
        half = pl.ds(c * H, H)
        other = pl.ds((1 - c) * H, H)
        cp = _remote(in_ref.at[half], out_ref.at[half], send_sems, recv_sems, 0, (x, y, 1 - c))
        cp.start()
        _remote(in_ref.at[other], out_ref.at[other], send_sems, recv_sems, 0, (x, y, c)).wait_recv()
        cp.wait_send()

    return _pallas(
        body, name=name, in_specs=[ANY], out_specs=ANY, out_shape=_sds((R, C), buf.dtype),
        input_output_aliases={0: 0},
        scratch_shapes=[pltpu.SemaphoreType.DMA((1,)), pltpu.SemaphoreType.DMA((1,))],
    )(buf)


def _tile_add(ins_specs, arrays, n_steps, out_spec, out_shape, scalars, *, name):
    grid_spec = pltpu.PrefetchScalarGridSpec(
        num_scalar_prefetch=1, grid=(n_steps,), in_specs=ins_specs, out_specs=out_spec, scratch_shapes=[])

    def body(sc_ref, *refs):
        acc = refs[0][...].astype(F32)
        for r in refs[1:-1]:
            acc = acc + r[...].astype(F32)
        refs[-1][...] = acc.astype(refs[-1].dtype)

    return _pallas(body, name=name, grid_spec=grid_spec, out_shape=out_shape,
                   compiler_params=_params(("arbitrary",)))(scalars, *arrays)


def _reduce_scatter(g, *, name):
    n, R, C = g.shape
    H, Q = R // 2, R // 4
    tm = RS_ROW_MULT // 4
    assert Q % tm == 0, (R, tm)
    x, y, c = _coords()
    sx, sy, s = 2 * (1 - x) + y, 2 * x + 1 - y, 2 * x + y
    sc = jnp.stack([c, sx, sy, s]).astype(jnp.int32)
    hb, qb = H // tm, Q // tm
    blk = lambda f: pl.BlockSpec((None, tm, C), f)
    flat = lambda f: pl.BlockSpec((tm, C), f)

    land = _rs_pair(g, name=name + "_pair")
    p = _tile_add([blk(lambda i, sc: (i // hb, sc[0] * hb + i % hb, 0)), blk(lambda i, sc: (i // hb, i % hb, 0))],
                  [g, land], n * hb, blk(lambda i, sc: (i // hb, i % hb, 0)), _sds((n, H, C), g.dtype), sc, name=name + "_add0")
    la, lb = _rs_quarters(p, name=name + "_quarters")
    ax = _tile_add([blk(lambda i, sc: (sc[1], i, 0)), flat(lambda i, sc: (i, 0))], [p, la], qb,
                   flat(lambda i, sc: (i, 0)), _sds((Q, C), g.dtype), sc, name=name + "_add1x")
    ay = _tile_add([blk(lambda i, sc: (sc[2], qb + i, 0)), flat(lambda i, sc: (i, 0))], [p, lb], qb,
                   flat(lambda i, sc: (i, 0)), _sds((Q, C), g.dtype), sc, name=name + "_add1y")
    fa, fb = _rs_halves(p, ax, ay, name=name + "_halves")
    buf = _tile_add([blk(lambda i, sc: (sc[3], i, 0)), flat(lambda i, sc: (i, 0)), flat(lambda i, sc: (i, 0))],
                    [p, fa, fb], hb, flat(lambda i, sc: (sc[0] * hb + i, 0)), _sds((R, C), F32), sc, name=name + "_add2")
    return _rs_join(buf, name=name + "_join")


def _all_reduce_small(v, *, name):
    M, N = v.shape

    def body(x_ref, out_ref, send_sems, recv_sems, local_sem):
        x, y, c = _coords()
        me, sibling = (x, y, c), (x, y, 1 - c)
        chips = [(1 - x, y), (x, 1 - y), (1 - x, 1 - y)]

        def rows(px, py, pc):
            return out_ref.at[pl.ds((4 * px + 2 * py + pc) * M, M), :]

        def copy(k, block, to, src=None):
            return pltpu.make_async_remote_copy(
                src_ref=rows(*block) if src is None else src, dst_ref=rows(*block),
                send_sem=send_sems.at[k], recv_sem=recv_sems.at[k], device_id=to, device_id_type=MESH)

        mine = pltpu.make_async_copy(x_ref, rows(*me), local_sem)
        mine.start()
        first = [copy(0, me, sibling, src=x_ref)]
        first += [copy(1 + j, me, (*chip, c), src=x_ref) for j, chip in enumerate(chips)]
        for cp in first:
            cp.start()
        passed = [copy(4 + j, (*chip, c), sibling) for j, chip in enumerate(chips)]
        for j, chip in enumerate(chips):
            copy(1 + j, (*chip, c), me).wait_recv()
            passed[j].start()
        copy(0, sibling, me).wait_recv()
        for j, chip in enumerate(chips):
            copy(4 + j, (*chip, 1 - c), me).wait_recv()
        for cp in first + passed:
            cp.wait_send()
        mine.wait()

    gathered = _pallas(
        body, name=name + "_gather",
        out_shape=_sds((N_DEV * M, N), F32),
        in_specs=[pl.BlockSpec(memory_space=pltpu.VMEM)],
        out_specs=pl.BlockSpec(memory_space=pltpu.VMEM),
        scratch_shapes=[pltpu.SemaphoreType.DMA((7,)), pltpu.SemaphoreType.DMA((7,)), pltpu.SemaphoreType.DMA],
    )(v)

    def sum_body(g_ref, o_ref):
        acc = g_ref[0:M, :]
        for d in range(1, N_DEV):
            acc = acc + g_ref[d * M:(d + 1) * M, :]
        o_ref[...] = acc

    return _pallas(sum_body, name=name + "_sum", out_shape=_sds((M, N), F32))(gathered)


MATS = [("conv_w_pw1", 2), ("conv_w_pw2", 1), ("attn_w_q", 1), ("attn_w_o", 1), ("ffn_w1", 2), ("ffn_w2", 1),
        ("ple_w_gate", 1), ("ple_w_proj", 2), ("w_kvf", 1)]
VECS = [("conv_b_pw1", 1), ("conv_w_dw", 2), ("conv_b_dw", 1), ("conv_ln_g", 1), ("conv_ln_b", 1), ("conv_b_pw2", 1)]
REPL = ["mix_norm", "ffn_norm", "ple_norm", "kv_norm", "final_norm", "b_f"]
WEIGHTS = ["mix_norm", "conv_w_pw1", "conv_b_pw1", "conv_w_dw", "conv_b_dw", "conv_ln_g", "conv_ln_b", "conv_w_pw2",
           "conv_b_pw2", "kv_norm", "w_kvf", "b_f", "attn_w_q", "attn_w_o", "ffn_norm", "ffn_w1", "ffn_w2", "ple_norm",
           "ple_w_gate", "ple_w_proj", "final_norm"]


def _round_up(n, m):
    return -(-n // m) * m


def _to_rows(t, C, mult):
    flat = t.reshape(-1)
    rows = _round_up(_round_up(flat.shape[0], C) // C, mult)
    flat = jnp.pad(flat, (0, rows * C - flat.shape[0]))
    return flat.reshape(rows, C)


def _pack(tensors, C, mult, total_mult):
    parts = [_to_rows(t, C, mult) for t in tensors]
    rows = sum(p.shape[0] for p in parts)
    pad = _round_up(rows, total_mult) - rows
    if pad:
        parts.append(jnp.zeros((pad, C), parts[0].dtype))
    return jnp.concatenate(parts, axis=0)


def _row_counts(shapes, C, mult):
    return [_round_up(_round_up(int(np.prod(s)), C) // C, mult) for s in shapes]


def _unpack(packed, shapes, C, mult):
    outs, r0 = [], 0
    lead = packed.shape[:-2]
    for shp, nr in zip(shapes, _row_counts(shapes, C, mult)):
        n = int(np.prod(shp))
        seg = packed[..., r0:r0 + nr, :].reshape(lead + (nr * C,))[..., :n]
        outs.append(seg.reshape(lead + tuple(shp)))
        r0 += nr
    return outs


def _unshard(t, axis):
    return jnp.concatenate([t[s] for s in range(N_CHIPS)], axis=axis)


def _shards(t, axis):
    return jnp.split(t, N_CHIPS, axis=axis)


def _local_step(x, p, tgt, w):
    S, D = x.shape
    L = p.shape[0]
    NA = w["conv_w_pw1"].shape[0]
    H = w["b_f"].shape[0]
    HP = D // LANES
    row = lambda v: v.reshape(1, -1)
    act = lambda dt, n=D: _sds((S, n), dt)
    g = {}

    def rms(hh, gain, name):
        return _rows(_rms_fwd, [hh], [row(gain)], [act(BF16)], name=name)[0]

    saved = []
    h = x
    kv = ck = fl = nkv = h_kv = None
    bfp = jnp.pad(w["b_f"], (0, LANES - H)).reshape(1, LANES)
    wk = w["w_kvf"][:, :D]
    wv = w["w_kvf"][:, D:2 * D]
    wkv = w["w_kvf"][:, :2 * D]
    wf = jnp.pad(w["w_kvf"][:, 2 * D:], ((0, 0), (0, LANES - H)))
    res_rms = lambda acc, r, gn: _with_rms(acc + r, gn)
    hn = rms(h, w["mix_norm"][0], "rms_mix")
    for i in range(L):
        sv = {"h0": h, "hn": hn}
        g_ffn, g_ple = row(w["ffn_norm"][i]), row(w["ple_norm"][i])
        if i < NA:
            u = _mm(hn, w["conv_w_pw1"][i], extras=[row(w["conv_b_pw1"][i])], epi=lambda acc, b: acc + b, name="mm_pw1")
            glu = _rows(_glu_fwd, [u], [], [act(F32)], name="glu_fwd")[0]
            wdw = jnp.pad(w["conv_w_dw"][i], ((0, CONV_PAD - CONV_WIDTH), (0, 0)))
            cv = _dwconv_fwd(glu, wdw, row(w["conv_b_dw"][i]), name="dwconv_fwd")
            sw = _rows(_ln_silu_fwd, [cv], [row(w["conv_ln_g"][i]), row(w["conv_ln_b"][i])], [act(BF16)], name="ln_silu_fwd")[0]
            h1, hn2 = _mm(sw, w["conv_w_pw2"][i], extras=[row(w["conv_b_pw2"][i]), h, g_ffn],
                          epi=lambda acc, b, r, gn: _with_rms(acc + b + r, gn), out_dtype=[F32, BF16], tm=512, tn=D, name="mm_pw2")
            sv.update(u=u, glu=glu, cv=cv, sw=sw, wdw=wdw)
        else:
            if i == NA:
                h_kv = h
                nkv = rms(h, w["kv_norm"], "rms_kv")
                kv = _mm(nkv, wkv, out_dtype=BF16, name="mm_kv")
                fl = _mm(nkv, wf, name="mm_f")
                c = _gate_cumsum(fl, bfp, name="gate_cumsum")
                ck = jnp.pad(c[:, :H].T.reshape(HP, 2, S), ((0, 0), (0, 6), (0, 0)))
            j = i - NA
            q = _mm(hn, w["attn_w_q"][j], epi=lambda acc: acc * (HEAD_DIM ** -0.5), out_dtype=BF16, name="mm_q")
            o, lse = _attn_fwd(q, kv, ck, name="attn_fwd")
            h1, hn2 = _mm(o, w["attn_w_o"][j], extras=[h, g_ffn], epi=res_rms, out_dtype=[F32, BF16], tm=512, tn=D, name="mm_o")
            sv.update(q=q, o=o, lse=lse)
        zb, f = _mm(hn2, w["ffn_w1"][i], epi=lambda acc: (acc, jnp.square(jnp.maximum(acc, 0.0))), out_dtype=[BF16, BF16],
                    name="mm_ffn1")
        h2, n3 = _mm(f, w["ffn_w2"][i], extras=[h1, g_ple], epi=res_rms, out_dtype=[F32, BF16], tm=512, tn=D, tk=4 * D, name="mm_ffn2")
        zg = _mm(n3, w["ple_w_gate"][i], name="mm_gate")
        ple = lambda acc, r, zz: r + _sigmoid(zz) * acc
        if i + 1 < L:
            h, hn = _mm(p[i], w["ple_w_proj"][i], extras=[h2, zg, row(w["mix_norm"][i + 1])],
                        epi=lambda acc, r, zz, gn: _with_rms(ple(acc, r, zz), gn), out_dtype=[F32, BF16], tm=512, tn=D, name="mm_proj")
        else:
            h = _mm(p[i], w["ple_w_proj"][i], extras=[h2, zg], epi=ple, tm=512, tn=D, name="mm_proj_last")
        sv.update(h1=h1, hn2=hn2, zb=zb, f=f, h2=h2, n3=n3, zg=zg)
        saved.append(sv)

    dh, err2, g_final = _rows(_final_fn, [h, tgt], [row(w["final_norm"])], [act(F32)], [_sds((1, D), F32), _sds((1, D), F32)],
                              name="final")
    loss = 0.5 * jnp.sum(err2) / D
    g["final_norm"] = g_final.reshape(-1)

    red = _sds((1, D), F32)
    stack = {k: [None] * n for k, n in (("mix_norm", L), ("ffn_norm", L), ("ple_norm", L), ("ffn_w1", L), ("ffn_w2", L),
                                        ("ple_w_gate", L), ("ple_w_proj", L), ("conv_w_pw1", NA), ("conv_b_pw1", NA),
                                        ("conv_w_dw", NA), ("conv_b_dw", NA), ("conv_ln_g", NA), ("conv_ln_b", NA),
                                        ("conv_w_pw2", NA), ("conv_b_pw2", NA), ("attn_w_q", L - NA), ("attn_w_o", L - NA))}
    dk_sum = dv_sum = None
    dcks = []
    for i in reversed(range(L)):
        sv = saved[i]
        dzg, dpp = _mm(p[i], w["ple_w_proj"][i], extras=[dh, sv["zg"]], epi=lambda acc, d, zz: _ple_bwd(d, zz, acc),
                       out_dtype=[BF16, BF16], name="mm_ple_bwd")
        stack["ple_w_proj"][i] = _mm(p[i], dpp, mode="tn", tk=2048, out_dtype=BF16, name="mm_dproj")
        stack["ple_w_gate"][i] = _mm(sv["n3"], dzg, mode="tn", tk=2048, out_dtype=BF16, name="mm_dgate")
        dh, dhb, dgain = _mm(dzg, w["ple_w_gate"][i], mode="nt", extras=[sv["h2"], dh, row(w["ple_norm"][i])], epi=_dup(_rms_bwd),
                             out_dtype=[F32, BF16], reds=1, tm=512, tn=D, name="mm_dn3")
        stack["ple_norm"][i] = dgain.reshape(-1)
        dz = _mm(dhb, w["ffn_w2"][i], mode="nt", extras=[sv["zb"]], epi=lambda acc, zz: acc * (2.0 * jnp.maximum(zz, 0.0).astype(F32)),
                 out_dtype=BF16, name="mm_dz")
        stack["ffn_w2"][i] = _mm(sv["f"], dhb, mode="tn", tk=2048, out_dtype=BF16, name="mm_dffn2")
        stack["ffn_w1"][i] = _mm(sv["hn2"], dz, mode="tn", tk=2048, out_dtype=BF16, name="mm_dffn1")
        if i < NA:
            dh, dhb, dgain, dbias = _mm(dz, w["ffn_w1"][i], mode="nt", extras=[sv["h1"], dh, row(w["ffn_norm"][i])],
                                        epi=_dup(_rms_bwd_bias), out_dtype=[F32, BF16], reds=2, tm=512, tn=D, tk=4 * D, name="mm_dhn2_bias")
            stack["conv_b_pw2"][i] = dbias.reshape(-1)
        else:
            dh, dhb, dgain = _mm(dz, w["ffn_w1"][i], mode="nt", extras=[sv["h1"], dh, row(w["ffn_norm"][i])], epi=_dup(_rms_bwd),
                                 out_dtype=[F32, BF16], reds=1, tm=512, tn=D, tk=4 * D, name="mm_dhn2")
        stack["ffn_norm"][i] = dgain.reshape(-1)
        if i < NA:
            dsw = _mm(dhb, w["conv_w_pw2"][i], mode="nt", name="mm_dsw")
            stack["conv_w_pw2"][i] = _mm(sv["sw"], dhb, mode="tn", tk=2048, out_dtype=BF16, name="mm_dpw2")
            dcv, dlg, dlb = _rows(_ln_silu_bwd, [dsw, sv["cv"]], [row(w["conv_ln_g"][i]), row(w["conv_ln_b"][i])], [act(F32)],
                                  [red, red], name="ln_silu_bwd")
            stack["conv_ln_g"][i], stack["conv_ln_b"][i] = dlg.reshape(-1), dlb.reshape(-1)
            dglu, dwdw, dbdw = _dwconv_bwd(dcv, sv["glu"], sv["wdw"], name="dwconv_bwd")
            stack["conv_w_dw"][i], stack["conv_b_dw"][i] = dwdw[:CONV_WIDTH], dbdw.reshape(-1)
            du, dbu = _rows(_glu_bwd, [dglu, sv["u"]], [], [act(BF16, 2 * D)], [_sds((1, 2 * D), F32)], name="glu_bwd")
            stack["conv_b_pw1"][i] = dbu.reshape(-1)
            stack["conv_w_pw1"][i] = _mm(sv["hn"], du, mode="tn", tk=2048, out_dtype=BF16, name="mm_dpw1")
            dh, dgain = _mm(du, w["conv_w_pw1"][i], mode="nt", extras=[sv["h0"], dh, row(w["mix_norm"][i])], epi=_rms_bwd,
                            reds=1, tm=512, tn=D, name="mm_dhn_a")
        else:
            j = i - NA
            do = _mm(dhb, w["attn_w_o"][j], mode="nt", out_dtype=BF16, name="mm_do")
            stack["attn_w_o"][j] = _mm(sv["o"], dhb, mode="tn", tk=2048, out_dtype=BF16, name="mm_dwo")
            dq, drs, dk, dv, dcs = _attn_bwd(sv["q"], kv, sv["o"], do, sv["lse"], ck, name="attn_bwd")
            scale = lambda acc: acc * (HEAD_DIM ** -0.5)
            stack["attn_w_q"][j] = _mm(sv["hn"], dq, mode="tn", tk=2048, out_dtype=BF16, epi=scale, name="mm_dwq")
            dh_in = dh
            dh, dgain = _mm(dq, w["attn_w_q"][j], mode="nt", extras=[sv["h0"], dh_in, row(w["mix_norm"][i])],
                            epi=lambda acc, xx, dr, gn: _rms_bwd(scale(acc), xx, dr, gn), reds=1, tm=512, tn=D, name="mm_dhn_b")
            pick = lambda t: jnp.pad(t.reshape(S, HP, 2, HEAD_DIM)[:, :, ::-1, 0].reshape(S, H), ((0, 0), (0, LANES - H)))
            dcks += [pick(drs), pick(dcs)]
            if dk_sum is None:
                dk_sum, dv_sum = dk, dv
            else:
                dk_sum = _rows(_add2, [dk_sum, dk], [], [act(F32)], name="add_dk")[0]
                dv_sum = _rows(_add2, [dv_sum, dv], [], [act(F32)], name="add_dk")[0]
        stack["mix_norm"][i] = dgain.reshape(-1)
        if i == NA:
            dfl, dbf = _gate_cumsum_bwd(dcks, fl, bfp, name="gate_cumsum_bwd")
            g["b_f"] = dbf[0, :H]
            gk = _mm(nkv, dk_sum, mode="tn", tk=2048, out_dtype=BF16, name="mm_dwk")
            gv = _mm(nkv, dv_sum, mode="tn", tk=2048, out_dtype=BF16, name="mm_dwk")
            gf = _mm(nkv, dfl, mode="tn", tk=2048, out_dtype=BF16, name="mm_dwf")
            g["w_kvf"] = jnp.concatenate([gk, gv, gf[:, :H]], axis=1)
            dn = _mm(dk_sum, wk, mode="nt", name="mm_dnk")
            dn = _mm(dv_sum, wv, mode="nt", extras=[dn], epi=lambda acc, r: acc + r, name="mm_dnv")
            dh, dgain = _mm(dfl, wf, mode="nt", extras=[dn, h_kv, dh, row(w["kv_norm"])],
                            epi=lambda acc, r, xx, dr, gn: _rms_bwd(acc + r, xx, dr, gn), reds=1, tm=512, tn=D, name="mm_dnf")
            g["kv_norm"] = dgain.reshape(-1)
    for k, v in stack.items():
        g[k] = jnp.stack(v, axis=0)
    return loss, dh, g


def kernel(x, p, mix_norm, conv_w_pw1, conv_b_pw1, conv_w_dw, conv_b_dw, conv_ln_g, conv_ln_b, conv_w_pw2, conv_b_pw2, kv_norm, w_kvf, b_f, attn_w_q, attn_w_o, ffn_norm, ffn_w1, ffn_w2, ple_norm, ple_w_gate, ple_w_proj, final_norm, loss_target, m_mix_norm, m_conv_w_pw1, m_conv_b_pw1, m_conv_w_dw, m_conv_b_dw, m_conv_ln_g, m_conv_ln_b, m_conv_w_pw2, m_conv_b_pw2, m_kv_norm, m_w_kvf, m_b_f, m_attn_w_q, m_attn_w_o, m_ffn_norm, m_ffn_w1, m_ffn_w2, m_ple_norm, m_ple_w_gate, m_ple_w_proj, m_final_norm, v_mix_norm, v_conv_w_pw1, v_conv_b_pw1, v_conv_w_dw, v_conv_b_dw, v_conv_ln_g, v_conv_ln_b, v_conv_w_pw2, v_conv_b_pw2, v_kv_norm, v_w_kvf, v_b_f, v_attn_w_q, v_attn_w_o, v_ffn_norm, v_ffn_w1, v_ffn_w2, v_ple_norm, v_ple_w_gate, v_ple_w_proj, v_final_norm):
    args = dict(locals())
    wl = {n: args[n] for n in WEIGHTS}
    ml = {n: args["m_" + n] for n in WEIGHTS}
    vl = {n: args["v_" + n] for n in WEIGHTS}
    S, D = x.shape[1], x.shape[2]
    C = D

    mat_shapes = [wl[n].shape for n, _ in MATS]
    vec_shapes = [wl[n].shape for n, _ in VECS]
    mats = _all_gather_shards(_pack([wl[n].astype(BF16) for n, _ in MATS], C, 16, 64), name="ag_mats")
    vecs = _all_gather_shards(_pack([wl[n] for n, _ in VECS], C, 1, 32), name="ag_vecs")
    full = {n: wl[n] for n in REPL}
    for (n, ax), t in zip(MATS, _unpack(mats, mat_shapes, C, 16)):
        full[n] = _unshard(t, ax)
    for (n, ax), t in zip(VECS, _unpack(vecs, vec_shapes, C, 1)):
        full[n] = _unshard(t, ax)

    loss, dx, g = _local_step(x[0], p[:, 0], loss_target[0], full)
    loss = lax.psum(loss, ("x", "y", "c"))

    names = [n for n, _ in MATS] + [n for n, _ in VECS]
    axes = dict(MATS + VECS)
    shard_shapes = [wl[n].shape for n in names]
    per_shard = [[] for _ in range(N_CHIPS)]
    for n in names:
        for s, piece in enumerate(_shards(g[n].astype(BF16), axes[n])):
            per_shard[s].append(piece)
    gpack = jnp.stack([_pack(ts, C, 16, RS_ROW_MULT) for ts in per_shard], axis=0)
    gred = _reduce_scatter(gpack, name="rs")
    gl = dict(zip(names, _unpack(gred, shard_shapes, C, 16)))

    rep_shapes = [wl[n].shape for n in REPL]
    rpack = _pack([g[n] for n in REPL], C, 1, 8)
    for n, t in zip(REPL, _unpack(_all_reduce_small(rpack, name="ar"), rep_shapes, C, 1)):
        gl[n] = t

    grads, deltas, new_m, new_v = {}, {}, {}, {}
    rep_w, rep_m, rep_v = (_pack([d[n] for n in REPL], C, 1, 8) for d in (wl, ml, vl))
    rep_out = _adamw(rep_w, rpack_like(gl, rep_shapes, C), rep_m, rep_v, "adamw_rep")
    for dst, packed in zip((grads, deltas, new_m, new_v), rep_out):
        for n, t in zip(REPL, _unpack(packed, rep_shapes, C, 1)):
            dst[n] = t
    for n in names:
        res = _adamw(wl[n], gl[n], ml[n], vl[n], "adamw_" + n)
        for dst, t in zip((grads, deltas, new_m, new_v), res):
            dst[n] = t
    out = [loss, dx[None]]
    for d in (grads, deltas, new_m, new_v):
        out += [d[n] for n in WEIGHTS]
    return tuple(out)


def rpack_like(gl, rep_shapes, C):
    return _pack([gl[n] for n in REPL], C, 1, 8)
```

```python
import functools

import jax
import jax.numpy as jnp
import numpy as np
from jax import lax
from jax.experimental import pallas as pl
from jax.experimental.pallas import tpu as pltpu

F32 = jnp.float32
BF16 = jnp.bfloat16
MESH = pl.DeviceIdType.MESH

N_CHIPS = 4
N_DEV = 8
HEAD_DIM = 64
LANES = 128
CONV_WIDTH = 31
CONV_PAD = 32
EPS = 1e-6
NEG_BIG = -1e30
VMEM_LIMIT = 56 * 1024 * 1024

RS_ROW_MULT = 1024

ADAM_LR, ADAM_B1, ADAM_B2, ADAM_EPS, ADAM_WD, ADAM_STEP = 0.001, 0.9, 0.999, 1e-08, 0.01, 10


def _pallas(body, **kw):
    return pl.pallas_call(body, **kw)


def _params(sem=None):
    return pltpu.CompilerParams(dimension_semantics=sem, vmem_limit_bytes=VMEM_LIMIT)


def _sds(shape, dtype):
    return jax.ShapeDtypeStruct(tuple(shape), dtype)


_DIMS = {"nn": (((1,), (0,)), ((), ())), "nt": (((1,), (1,)), ((), ())), "tn": (((0,), (0,)), ((), ()))}


def _mm(a, b, *, mode="nn", extras=(), epi=None, out_dtype=F32, reds=0, tm=1024, tn=1024, tk=1024, name):
    if mode == "nn":
        (M, K), (K2, N) = a.shape, b.shape
    elif mode == "nt":
        (M, K), (N, K2) = a.shape, b.shape
    else:
        (K, M), (K2, N) = a.shape, b.shape
    assert K == K2, (name, a.shape, b.shape)
    tm, tn, tk = min(tm, M), min(tn, N), min(tk, K)
    assert M % tm == 0 and N % tn == 0 and K % tk == 0, (name, a.shape, b.shape)
    nk = K // tk
    if mode == "tn":
        a_spec = pl.BlockSpec((tk, tm), lambda i, j, k: (k, i))
    else:
        a_spec = pl.BlockSpec((tm, tk), lambda i, j, k: (i, k))
    if mode == "nt":
        b_spec = pl.BlockSpec((tn, tk), lambda i, j, k: (j, k))
    else:
        b_spec = pl.BlockSpec((tk, tn), lambda i, j, k: (k, j))
    ex_specs = []
    for e in extras:
        if e.shape[0] == 1:
            ex_specs.append(pl.BlockSpec((1, tn), lambda i, j, k: (0, j)))
        else:
            assert e.shape == (M, N), (name, e.shape)
            ex_specs.append(pl.BlockSpec((tm, tn), lambda i, j, k: (i, j)))
    ne = len(extras)
    dims = _DIMS[mode]
    many = isinstance(out_dtype, (list, tuple))
    out_dtypes = list(out_dtype) if many else [out_dtype]
    no = len(out_dtypes)
    assert not reds or tn == N, name

    def body(a_ref, b_ref, *rest):
        ex_refs, o_refs, r_refs = rest[:ne], rest[ne:ne + no], rest[ne + no:ne + no + reds]
        part = lax.dot_general(a_ref[...].astype(BF16), b_ref[...].astype(BF16), dims, preferred_element_type=F32)
        i = pl.program_id(0)

        def finish(acc):
            res = epi(acc, *[r[...] for r in ex_refs]) if epi is not None else acc
            if not isinstance(res, (tuple, list)):
                res = (res,)
            assert len(res) == no + reds, (name, len(res))
            for r, v in zip(o_refs, res[:no]):
                r[...] = v.astype(r.dtype)
            for r, v in zip(r_refs, res[no:]):
                @pl.when(i == 0)
                def _(r=r, v=v):
                    r[...] = v

                @pl.when(i > 0)
                def _(r=r, v=v):
                    r[...] += v

        if nk == 1:
            finish(part)
        else:
            acc_ref = rest[ne + no + reds]
            k = pl.program_id(2)

            @pl.when(k == 0)
            def _():
                acc_ref[...] = part

            @pl.when(k > 0)
            def _():
                acc_ref[...] += part

            @pl.when(k == nk - 1)
            def _():
                finish(acc_ref[...])

    res = _pallas(
        body, name=name, grid=(M // tm, N // tn, nk),
        in_specs=[a_spec, b_spec] + ex_specs,
        out_specs=[pl.BlockSpec((tm, tn), lambda i, j, k: (i, j))] * no + [pl.BlockSpec((1, tn), lambda i, j, k: (0, j))] * reds,
        out_shape=[_sds((M, N), dt) for dt in out_dtypes] + [_sds((1, N), F32)] * reds,
        scratch_shapes=[pltpu.VMEM((tm, tn), F32)] if nk > 1 else [],
        compiler_params=_params(("arbitrary",) * 3 if reds else ("parallel", "parallel", "arbitrary")),
    )(a, b, *extras)
    return res if (many or reds) else res[0]


def _rows(fn, ins, params, outs, reds=(), *, tm=256, name):
    S = ins[0].shape[0]
    tm = min(tm, S)
    assert S % tm == 0, (name, S, tm)
    ni, npar, no, nr = len(ins), len(params), len(outs), len(reds)

    def body(*refs):
        in_refs, p_refs = refs[:ni], refs[ni:ni + npar]
        o_refs, r_refs = refs[ni + npar:ni + npar + no], refs[ni + npar + no:]
        res = fn(*[r[...] for r in in_refs], *[r[...] for r in p_refs])
        if not isinstance(res, (tuple, list)):
            res = (res,)
        assert len(res) == no + nr, (name, len(res))
        for r, v in zip(o_refs, res[:no]):
            r[...] = v.astype(r.dtype)
        i = pl.program_id(0)
        for r, v in zip(r_refs, res[no:]):
            @pl.when(i == 0)
            def _(r=r, v=v):
                r[...] = v

            @pl.when(i > 0)
            def _(r=r, v=v):
                r[...] += v

    res = _pallas(
        body, name=name, grid=(S // tm,),
        in_specs=[pl.BlockSpec((tm, a.shape[1]), lambda i: (i, 0)) for a in ins]
        + [pl.BlockSpec(p.shape, lambda i: (0, 0)) for p in params],
        out_specs=[pl.BlockSpec((tm, o.shape[1]), lambda i: (i, 0)) for o in outs]
        + [pl.BlockSpec(r.shape, lambda i: (0, 0)) for r in reds],
        out_shape=list(outs) + list(reds),
        compiler_params=_params(("arbitrary",)),
    )(*ins, *params)
    return res


def _colsum(v):
    return jnp.sum(v, axis=0, keepdims=True)


def _sigmoid(v):
    return 1.0 / (1.0 + jnp.exp(-v))


def _rms_stats(x):
    r = lax.rsqrt(jnp.mean(x * x, axis=-1, keepdims=True) + EPS)
    return x * r, r


def _rms_fwd(x, g):
    xh, _ = _rms_stats(x)
    return (xh * g,)


def _with_rms(h, g):
    xh, _ = _rms_stats(h)
    return h, xh * g


def _rms_bwd(dy, x, dres, g):
    xh, r = _rms_stats(x)
    dyg = dy * g
    dx = r * (dyg - xh * jnp.mean(dyg * xh, axis=-1, keepdims=True))
    return dres + dx, _colsum(dy * xh)


def _dup(fn):
    def wrapped(*a):
        r = fn(*a)
        return (r[0], r[0]) + tuple(r[1:])
    return wrapped


def _rms_bwd_bias(dy, x, dres, g):
    dx, dg = _rms_bwd(dy, x, dres, g)
    return dx, dg, _colsum(dx)


def _glu_fwd(u):
    d = u.shape[1] // 2
    return (u[:, :d] * _sigmoid(u[:, d:]),)


def _glu_bwd(dglu, u):
    d = u.shape[1] // 2
    a, sig = u[:, :d], _sigmoid(u[:, d:])
    du = jnp.concatenate([dglu * sig, dglu * a * sig * (1.0 - sig)], axis=1)
    return du, _colsum(du)


def _ln_parts(x, g, b):
    mu = jnp.mean(x, axis=-1, keepdims=True)
    xc = x - mu
    r = lax.rsqrt(jnp.mean(xc * xc, axis=-1, keepdims=True) + EPS)
    xh = xc * r
    return xh, r, xh * g + b


def _ln_silu_fwd(x, g, b):
    _, _, y = _ln_parts(x, g, b)
    return (y * _sigmoid(y),)


def _ln_silu_bwd(dsw, x, g, b):
    xh, r, y = _ln_parts(x, g, b)
    sig = _sigmoid(y)
    dy = dsw * sig * (1.0 + y * (1.0 - sig))
    dxh = dy * g
    dx = r * (dxh - jnp.mean(dxh, axis=-1, keepdims=True) - xh * jnp.mean(dxh * xh, axis=-1, keepdims=True))
    return dx, _colsum(dy * xh), _colsum(dy)


def _relu2(z):
    zp = jnp.maximum(z, 0.0)
    return (zp * zp,)


def _ple_fwd(h, zg, pp):
    return (h + _sigmoid(zg) * pp,)


def _ple_bwd(dh, zg, pp):
    gate = _sigmoid(zg)
    return dh * pp * gate * (1.0 - gate), dh * gate


def _final_fn(h, t, g):
    xh, r = _rms_stats(h)
    err = xh * g - t
    dy = err * (1.0 / h.shape[1])
    dyg = dy * g
    dh = r * (dyg - xh * jnp.mean(dyg * xh, axis=-1, keepdims=True))
    return dh, _colsum(err * err), _colsum(dy * xh)


def _add2(a, b):
    return (a + b,)


def _adamw_fn(w, g, m, v):
    m = ADAM_B1 * m + (1.0 - ADAM_B1) * g
    v = ADAM_B2 * v + (1.0 - ADAM_B2) * (g * g)
    m_hat = m / (1.0 - ADAM_B1 ** ADAM_STEP)
    v_hat = v / (1.0 - ADAM_B2 ** ADAM_STEP)
    delta = -ADAM_LR * (m_hat / (jnp.sqrt(v_hat) + ADAM_EPS) + ADAM_WD * w)
    return g, delta, m, v


def _adamw(w, g, m, v, name):
    shape = w.shape
    cols = shape[-1] if len(shape) > 1 else shape[0]
    two = lambda t: t.reshape(-1, cols)
    o = _sds(two(w).shape, F32)
    res = _rows(_adamw_fn, [two(w), two(g), two(m), two(v)], [], [o, o, o, o], tm=512, name=name)
    return [r.reshape(shape) for r in res]


def _dwconv_fwd(u, w, b, *, tm=512, name):
    S, D = u.shape
    tm = min(tm, S)
    rc = min(128, tm)
    per = tm // CONV_PAD

    def body(prev_ref, cur_ref, w_ref, b_ref, o_ref, win):
        i = pl.program_id(0)

        @pl.when(i == 0)
        def _():
            win[0:CONV_PAD, :] = jnp.zeros((CONV_PAD, D), F32)

        @pl.when(i > 0)
        def _():
            win[0:CONV_PAD, :] = prev_ref[...]

        win[CONV_PAD:CONV_PAD + tm, :] = cur_ref[...]
        for lc in range(D // LANES):
            ls = slice(lc * LANES, (lc + 1) * LANES)
            for r0 in range(0, tm, rc):
                acc = jnp.zeros((rc, LANES), F32) + b_ref[:, ls]
                for k in range(CONV_WIDTH):
                    acc = acc + win[r0 + 2 + k:r0 + 2 + k + rc, ls] * w_ref[k:k + 1, ls]
                o_ref[r0:r0 + rc, ls] = acc

    return _pallas(
        body, name=name, grid=(S // tm,),
        in_specs=[pl.BlockSpec((CONV_PAD, D), lambda i: (jnp.maximum(i * per - 1, 0), 0)),
                  pl.BlockSpec((tm, D), lambda i: (i, 0)),
                  pl.BlockSpec((CONV_PAD, D), lambda i: (0, 0)),
                  pl.BlockSpec((1, D), lambda i: (0, 0))],
        out_specs=pl.BlockSpec((tm, D), lambda i: (i, 0)),
        out_shape=_sds((S, D), F32),
        scratch_shapes=[pltpu.VMEM((tm + CONV_PAD, D), F32)],
        compiler_params=_params(("arbitrary",)),
    )(u, u, w, b)


def _dwconv_bwd(dy, u, w, *, tm=512, name):
    S, D = u.shape
    tm = min(tm, S)
    rc = min(128, tm)
    per = tm // CONV_PAD
    n = S // tm
    nxt = S // CONV_PAD - 1

    def body(dy_ref, dyn_ref, up_ref, u_ref, w_ref, du_ref, dw_ref, db_ref, wd, wu, dwacc, dbacc):
        i = pl.program_id(0)

        @pl.when(i == 0)
        def _():
            wu[0:CONV_PAD, :] = jnp.zeros((CONV_PAD, D), F32)
            dwacc[...] = jnp.zeros(dwacc.shape, F32)
            dbacc[...] = jnp.zeros(dbacc.shape, F32)

        @pl.when(i > 0)
        def _():
            wu[0:CONV_PAD, :] = up_ref[...]

        @pl.when(i == n - 1)
        def _():
            wd[tm:tm + CONV_PAD, :] = jnp.zeros((CONV_PAD, D), F32)

        @pl.when(i < n - 1)
        def _():
            wd[tm:tm + CONV_PAD, :] = dyn_ref[...]

        wu[CONV_PAD:CONV_PAD + tm, :] = u_ref[...]
        wd[0:tm, :] = dy_ref[...]
        for lc in range(D // LANES):
            ls = slice(lc * LANES, (lc + 1) * LANES)
            for r0 in range(0, tm, rc):
                acc = jnp.zeros((rc, LANES), F32)
                for k in range(CONV_WIDTH):
                    acc = acc + wd[r0 + 30 - k:r0 + 30 - k + rc, ls] * w_ref[k:k + 1, ls]
                du_ref[r0:r0 + rc, ls] = acc
                dyc = wd[r0:r0 + rc, ls]
                dbacc[:, ls] += jnp.sum(dyc.reshape(rc // 8, 8, LANES), axis=0)
                for k in range(CONV_WIDTH):
                    prod = dyc * wu[r0 + 2 + k:r0 + 2 + k + rc, ls]
                    dwacc[8 * k:8 * k + 8, ls] += jnp.sum(prod.reshape(rc // 8, 8, LANES), axis=0)

        @pl.when(i == n - 1)
        def _():
            dw_ref[...] = jnp.zeros(dw_ref.shape, F32)
            for k in range(CONV_WIDTH):
                dw_ref[k:k + 1, :] = jnp.sum(dwacc[8 * k:8 * k + 8, :], axis=0, keepdims=True)
            db_ref[...] = jnp.sum(dbacc[...], axis=0, keepdims=True)

    return _pallas(
        body, name=name, grid=(n,),
        in_specs=[pl.BlockSpec((tm, D), lambda i: (i, 0)),
                  pl.BlockSpec((CONV_PAD, D), lambda i: (jnp.minimum((i + 1) * per, nxt), 0)),
                  pl.BlockSpec((CONV_PAD, D), lambda i: (jnp.maximum(i * per - 1, 0), 0)),
                  pl.BlockSpec((tm, D), lambda i: (i, 0)),
                  pl.BlockSpec((CONV_PAD, D), lambda i: (0, 0))],
        out_specs=[pl.BlockSpec((tm, D), lambda i: (i, 0)),
                   pl.BlockSpec((CONV_PAD, D), lambda i: (0, 0)),
                   pl.BlockSpec((1, D), lambda i: (0, 0))],
        out_shape=[_sds((S, D), F32), _sds((CONV_PAD, D), F32), _sds((1, D), F32)],
        scratch_shapes=[pltpu.VMEM((tm + CONV_PAD, D), F32), pltpu.VMEM((tm + CONV_PAD, D), F32),
                        pltpu.VMEM((8 * CONV_PAD, D), F32), pltpu.VMEM((8, D), F32)],
        compiler_params=_params(("arbitrary",)),
    )(dy, dy, u, u, w)


def _tri_dot(tri, x):
    x1 = x.astype(BF16)
    r1 = x - x1.astype(F32)
    x2 = r1.astype(BF16)
    x3 = (r1 - x2.astype(F32)).astype(BF16)
    d = lambda v: jnp.dot(tri, v, preferred_element_type=F32)
    return d(x1) + d(x2) + d(x3)


def _log_sigmoid(x):
    return jnp.minimum(x, 0.0) - jnp.log(1.0 + jnp.exp(-jnp.abs(x)))


def _gate_cumsum(fl, bf, *, tm=256, name):
    S, W = fl.shape
    tm = min(tm, S)

    def body(fl_ref, bf_ref, c_ref, carry):
        i = pl.program_id(0)

        @pl.when(i == 0)
        def _():
            carry[...] = jnp.zeros(carry.shape, F32)

        x = _log_sigmoid(fl_ref[...] + bf_ref[...])
        row = lax.broadcasted_iota(jnp.int32, (tm, tm), 0)
        col = lax.broadcasted_iota(jnp.int32, (tm, tm), 1)
        tri = jnp.where(row >= col, 1.0, 0.0).astype(BF16)
        cs = _tri_dot(tri, x) + carry[0:1, :]
        c_ref[...] = cs
        carry[...] = jnp.broadcast_to(cs[tm - 1:tm, :], carry.shape)

    return _pallas(
        body, name=name, grid=(S // tm,),
        in_specs=[pl.BlockSpec((tm, W), lambda i: (i, 0)), pl.BlockSpec((1, W), lambda i: (0, 0))],
        out_specs=pl.BlockSpec((tm, W), lambda i: (i, 0)),
        out_shape=_sds((S, W), F32),
        scratch_shapes=[pltpu.VMEM((8, W), F32)],
        compiler_params=_params(("arbitrary",)),
    )(fl, bf)


def _gate_cumsum_bwd(sums, fl, bf, *, tm=256, name):
    S, W = fl.shape
    tm = min(tm, S)
    n = S // tm
    ns = len(sums)
    assert ns % 2 == 0

    def body(*refs):
        sum_refs = refs[:ns]
        fl_ref, bf_ref, o_ref, s_ref, carry = refs[ns:]
        i = pl.program_id(0)

        @pl.when(i == 0)
        def _():
            carry[...] = jnp.zeros(carry.shape, F32)
            s_ref[...] = jnp.zeros(s_ref.shape, F32)

        dc = sum_refs[0][...] - sum_refs[1][...]
        for a in range(2, ns, 2):
            dc = dc + (sum_refs[a][...] - sum_refs[a + 1][...])
        row = lax.broadcasted_iota(jnp.int32, (tm, tm), 0)
        col = lax.broadcasted_iota(jnp.int32, (tm, tm), 1)
        tri = jnp.where(col >= row, 1.0, 0.0).astype(BF16)
        rs = _tri_dot(tri, dc) + carry[0:1, :]
        carry[...] = jnp.broadcast_to(rs[0:1, :], carry.shape)
        dfl = rs * _sigmoid(-(fl_ref[...] + bf_ref[...]))
        o_ref[...] = dfl
        s_ref[...] += _colsum(dfl)

    rev = lambda i: (n - 1 - i, 0)
    return _pallas(
        body, name=name, grid=(n,),
        in_specs=[pl.BlockSpec((tm, W), rev)] * (ns + 1) + [pl.BlockSpec((1, W), lambda i: (0, 0))],
        out_specs=[pl.BlockSpec((tm, W), rev), pl.BlockSpec((1, W), lambda i: (0, 0))],
        out_shape=[_sds((S, W), F32), _sds((1, W), F32)],
        scratch_shapes=[pltpu.VMEM((8, W), F32)],
        compiler_params=_params(("arbitrary",)),
    )(*sums, fl, bf)


def _tri_tables(nq, qc, by_query):
    if by_query:
        pairs = [(i, j) for i in range(nq) for j in range(qc * (i + 1))]
    else:
        pairs = [(i, j) for j in range(qc * nq) for i in range(j // qc, nq)]
    ii, jj = zip(*pairs)
    return jnp.asarray(np.array(ii, np.int32)), jnp.asarray(np.array(jj, np.int32))


def _chunk_kinds(qc, dd):
    if dd < 0:
        return ("full",) * qc
    return tuple("full" if r > dd else "diag" if r == dd else "skip" for r in range(qc))


def _rep(v, t):
    return jnp.tile(v, (1, t // LANES))


def _attn_fwd(q, kv, ck, *, tb=512, row_chunks=2, name):
    S, D = q.shape
    HP = D // LANES
    T = min(tb, S)
    QC = row_chunks if S >= row_chunks * T else 1
    TQ = QC * T
    it, jt = _tri_tables(S // TQ, QC, True)

    def body(it_ref, jt_ref, q_ref, k_ref, v_ref, ck_ref, o_ref, lse_ref, st):
        s_id = pl.program_id(1)
        i, j = it_ref[s_id], jt_ref[s_id]
        dd = j - QC * i
        lane = lax.broadcasted_iota(jnp.int32, (TQ, LANES), 1)
        head0 = lane < HEAD_DIM
        h0 = head0[:T]
        hms = (h0, jnp.logical_not(h0))

        @pl.when(j == 0)
        def _():
            st[0:2] = jnp.full((2, TQ, LANES), NEG_BIG, F32)
            st[2:4] = jnp.zeros((2, TQ, LANES), F32)

        def step(kinds):
            kvv, vv = k_ref[...], v_ref[...]
            one = jnp.ones_like(vv)
            vaug = [jnp.where(hms[h], vv, one) for h in range(2)]
            old = st[...]
            rows = lambda r: slice(r * T, (r + 1) * T)
            live = [r for r in range(QC) if kinds[r] != "skip"]
            chains = [(r, h) for r in live for h in range(2)]
            ss = {}
            for r, h in chains:
                qv = q_ref[rows(r), :]
                ss[r, h] = lax.dot_general(jnp.where(hms[h], qv, jnp.zeros_like(qv)), kvv, _DIMS["nt"], preferred_element_type=F32)
            ps, alphas, m_new = {}, {}, {}
            for r, h in chains:
                s = ss[r, h] - ck_ref[h:h + 1, :]
                if kinds[r] == "diag":
                    row = lax.broadcasted_iota(jnp.int32, (T, T), 0)
                    col = lax.broadcasted_iota(jnp.int32, (T, T), 1)
                    s = jnp.where(row >= col, s, NEG_BIG)
                m_prev = old[h, rows(r), :]
                m_new[r, h] = jnp.maximum(m_prev, jnp.max(s, axis=1, keepdims=True))
                ps[r, h] = jnp.exp(s - _rep(m_new[r, h], T)).astype(BF16)
                alphas[r, h] = jnp.exp(m_prev - m_new[r, h])
            new = [[], [], [], []]
            for r in range(QC):
                if kinds[r] == "skip":
                    for a in range(4):
                        new[a].append(old[a, rows(r), :])
                    continue
                pv = [jnp.dot(ps[r, h], vaug[h], preferred_element_type=F32) for h in range(2)]
                a0, a1 = alphas[r, 0], alphas[r, 1]
                new[0].append(m_new[r, 0])
                new[1].append(m_new[r, 1])
                new[2].append(jnp.where(h0, a0, a1) * old[2, rows(r), :] + jnp.where(h0, pv[0], pv[1]))
                new[3].append(jnp.where(h0, a1, a0) * old[3, rows(r), :] + jnp.where(h0, pv[1], pv[0]))
            res = jnp.stack([jnp.concatenate(n, axis=0) for n in new], axis=0)
            st[...] = res
            return res

        @pl.when(dd < 0)
        def _():
            step(_chunk_kinds(QC, -1))

        for d in range(QC):
            @pl.when(dd == d)
            def _(d=d):
                res = step(_chunk_kinds(QC, d))
                if d == QC - 1:
                    lr = pltpu.roll(res[3], HEAD_DIM, axis=1)
                    o_ref[...] = res[2] / lr
                    lse_ref[0] = res[0] + jnp.log(jnp.where(head0, lr, res[3]))
                    lse_ref[1] = res[1] + jnp.log(jnp.where(head0, res[3], lr))

    grid_spec = pltpu.PrefetchScalarGridSpec(
        num_scalar_prefetch=2, grid=(HP, it.shape[0]),
        in_specs=[pl.BlockSpec((TQ, LANES), lambda h, s, it, jt: (it[s], h)),
                  pl.BlockSpec((T, LANES), lambda h, s, it, jt: (jt[s], h)),
                  pl.BlockSpec((T, LANES), lambda h, s, it, jt: (jt[s], HP + h)),
                  pl.BlockSpec((None, 8, T), lambda h, s, it, jt: (h, 0, jt[s]))],
        out_specs=[pl.BlockSpec((TQ, LANES), lambda h, s, it, jt: (it[s], h)),
                   pl.BlockSpec((2, TQ, LANES), lambda h, s, it, jt: (h, it[s], 0))],
        scratch_shapes=[pltpu.VMEM((4, TQ, LANES), F32)],
    )
    return _pallas(
        body, name=name, grid_spec=grid_spec,
        out_shape=[_sds((S, D), F32), _sds((2 * HP, S, LANES), F32)],
        compiler_params=_params(("parallel", "arbitrary")),
    )(it, jt, q, kv, kv, ck)


def _attn_bwd(q, kv, o, do, lse, ck, *, tb=512, row_chunks=2, name):
    S, D = q.shape
    HP = D // LANES
    T = min(tb, S)
    QC = row_chunks if S >= row_chunks * T else 1
    TQ = QC * T
    it, jt = _tri_tables(S // TQ, QC, False)

    def body(it_ref, jt_ref, q_ref, k_ref, v_ref, o_ref, do_ref, lse_ref, ck_ref, dq_ref, drs_ref, dk_ref, dv_ref, dcs_ref):
        s_id = pl.program_id(1)
        i, j = it_ref[s_id], jt_ref[s_id]
        dd = j - QC * i
        lane = lax.broadcasted_iota(jnp.int32, (T, LANES), 1)
        head0 = lane < HEAD_DIM
        hms = (head0, jnp.logical_not(head0))

        @pl.when(s_id == 0)
        def _():
            dq_ref[...] = jnp.zeros(dq_ref.shape, F32)
            drs_ref[...] = jnp.zeros(drs_ref.shape, F32)

        @pl.when(dd >= 0)
        def _():
            dk_ref[...] = jnp.zeros(dk_ref.shape, F32)
            dv_ref[...] = jnp.zeros(dv_ref.shape, F32)
            dcs_ref[...] = jnp.zeros(dcs_ref.shape, F32)

        def step(kinds):
            kvv, vv = k_ref[...], v_ref[...]
            one = jnp.ones_like(kvv)
            zero = jnp.zeros_like(kvv)
            rows = lambda r: slice(r * T, (r + 1) * T)
            live = [r for r in range(QC) if kinds[r] != "skip"]
            chains = [(r, h) for r in live for h in range(2)]
            qv = {r: q_ref[rows(r), :] for r in live}
            dob = {r: do_ref[rows(r), :].astype(BF16) for r in live}
            ss = {(r, h): lax.dot_general(jnp.where(hms[h], qv[r], zero), kvv, _DIMS["nt"], preferred_element_type=F32)
                  for r, h in chains}
            dps = {(r, h): lax.dot_general(jnp.where(hms[h], dob[r], zero), vv, _DIMS["nt"], preferred_element_type=F32)
                   for r, h in chains}
            pbs, dsbs = {}, {}
            for r, h in chains:
                s = ss[r, h] - ck_ref[h:h + 1, :]
                if kinds[r] == "diag":
                    row = lax.broadcasted_iota(jnp.int32, (T, T), 0)
                    col = lax.broadcasted_iota(jnp.int32, (T, T), 1)
                    s = jnp.where(row >= col, s, NEG_BIG)
                p = jnp.exp(s - _rep(lse_ref[h, rows(r), :], T))
                prod = dob[r].astype(F32) * o_ref[rows(r), :]
                delta = jnp.sum(jnp.where(hms[h], prod, 0.0), axis=1, keepdims=True)
                pbs[r, h] = p.astype(BF16)
                dsbs[r, h] = (p * (dps[r, h] - delta)).astype(BF16)
            dvs, dks = [None, None], [None, None]
            for r in live:
                dqs = []
                for h in range(2):
                    dqs.append(jnp.dot(dsbs[r, h], jnp.where(hms[h], kvv, one), preferred_element_type=F32))
                    dv = jnp.dot(pbs[r, h].T, dob[r], preferred_element_type=F32)
                    dk = jnp.dot(dsbs[r, h].T, jnp.where(hms[h], qv[r], one), preferred_element_type=F32)
                    dvs[h] = dv if dvs[h] is None else dvs[h] + dv
                    dks[h] = dk if dks[h] is None else dks[h] + dk
                qrows = pl.ds(pl.multiple_of(i * TQ + r * T, T), T)
                dq_ref[qrows, :] += jnp.where(head0, dqs[0], dqs[1])
                drs_ref[qrows, :] += jnp.where(head0, dqs[1], dqs[0])
            dv_ref[...] += jnp.where(head0, dvs[0], dvs[1])
            dk_ref[...] += jnp.where(head0, dks[0], dks[1])
            dcs_ref[...] += jnp.where(head0, dks[1], dks[0])

        @pl.when(dd < 0)
        def _():
            step(_chunk_kinds(QC, -1))

        for d in range(QC):
            @pl.when(dd == d)
            def _(d=d):
                step(_chunk_kinds(QC, d))

    by_q = lambda h, s, it, jt: (it[s], h)
    by_k = lambda h, s, it, jt: (jt[s], h)
    whole = lambda h, s, it, jt: (0, h)
    grid_spec = pltpu.PrefetchScalarGridSpec(
        num_scalar_prefetch=2, grid=(HP, it.shape[0]),
        in_specs=[pl.BlockSpec((TQ, LANES), by_q),
                  pl.BlockSpec((T, LANES), by_k),
                  pl.BlockSpec((T, LANES), lambda h, s, it, jt: (jt[s], HP + h)),
                  pl.BlockSpec((TQ, LANES), by_q),
                  pl.BlockSpec((TQ, LANES), by_q),
                  pl.BlockSpec((2, TQ, LANES), lambda h, s, it, jt: (h, it[s], 0)),
                  pl.BlockSpec((None, 8, T), lambda h, s, it, jt: (h, 0, jt[s]))],
        out_specs=[pl.BlockSpec((S, LANES), whole), pl.BlockSpec((S, LANES), whole),
                   pl.BlockSpec((T, LANES), by_k), pl.BlockSpec((T, LANES), by_k), pl.BlockSpec((T, LANES), by_k)],
        scratch_shapes=[],
    )
    return _pallas(
        body, name=name, grid_spec=grid_spec,
        out_shape=[_sds((S, D), F32)] * 5,
        compiler_params=_params(("parallel", "arbitrary")),
    )(it, jt, q, kv, kv, o, do, lse, ck)


ANY = pl.BlockSpec(memory_space=pl.ANY)


def _coords():
    x, y, c = lax.axis_index("x"), lax.axis_index("y"), lax.axis_index("c")
    return x, y, c


def _remote(src, dst, send_sems, recv_sems, k, to):
    return pltpu.make_async_remote_copy(src_ref=src, dst_ref=dst, send_sem=send_sems.at[k], recv_sem=recv_sems.at[k],
                                        device_id=to, device_id_type=MESH)


def _all_gather_shards(pack, *, name):
    R, C = pack.shape
    assert R % 4 == 0
    H, Q = R // 2, R // 4

    def body(in_ref, out_ref, send_sems, recv_sems):
        x, y, c = _coords()
        me, sib = (x, y, c), (x, y, 1 - c)
        xn, yn = (1 - x, y, c), (x, 1 - y, c)
        s, sx, sy, sd = 2 * x + y, 2 * (1 - x) + y, 2 * x + 1 - y, 2 * (1 - x) + 1 - y
        half = pl.ds(c * H, H)
        other = pl.ds((1 - c) * H, H)
        q0 = pl.ds(c * H, Q)
        q1 = pl.ds(c * H + Q, Q)
        rc = functools.partial(_remote, send_sems=send_sems, recv_sems=recv_sems)

        sends = [rc(in_ref.at[half], out_ref.at[s, half], k=0, to=xn),
                 rc(in_ref.at[half], out_ref.at[s, half], k=1, to=yn),
                 rc(in_ref, out_ref.at[s], k=7, to=sib)]
        for cp in sends:
            cp.start()
        rc(in_ref.at[half], out_ref.at[sx, half], k=0, to=me).wait_recv()
        sends.append(rc(out_ref.at[sx, q0], out_ref.at[sx, q0], k=2, to=yn))
        sends[-1].start()
        sends.append(rc(out_ref.at[sx, half], out_ref.at[sx, half], k=4, to=sib))
        sends[-1].start()
        rc(in_ref.at[half], out_ref.at[sy, half], k=1, to=me).wait_recv()
        sends.append(rc(out_ref.at[sy, q1], out_ref.at[sy, q1], k=3, to=xn))
        sends[-1].start()
        sends.append(rc(out_ref.at[sy, half], out_ref.at[sy, half], k=5, to=sib))
        sends[-1].start()
        rc(out_ref.at[sd, q0], out_ref.at[sd, q0], k=2, to=me).wait_recv()
        rc(out_ref.at[sd, q1], out_ref.at[sd, q1], k=3, to=me).wait_recv()
        sends.append(rc(out_ref.at[sd, half], out_ref.at[sd, half], k=6, to=sib))
        sends[-1].start()
        for k, sh in ((4, sx), (5, sy), (6, sd)):
            rc(out_ref.at[sh, other], out_ref.at[sh, other], k=k, to=me).wait_recv()
        rc(in_ref, out_ref.at[s], k=7, to=me).wait_recv()
        for cp in sends:
            cp.wait_send()

    return _pallas(
        body, name=name, in_specs=[ANY], out_specs=ANY,
        out_shape=_sds((N_CHIPS, R, C), pack.dtype),
        scratch_shapes=[pltpu.SemaphoreType.DMA((8,)), pltpu.SemaphoreType.DMA((8,))],
    )(pack)


def _rs_pair(g, *, name):
    n, R, C = g.shape
    H = R // 2

    def body(g_ref, land_ref, send_sems, recv_sems):
        x, y, c = _coords()
        other = pl.ds((1 - c) * H, H)
        cps = [_remote(g_ref.at[sh, other], land_ref.at[sh], send_sems, recv_sems, sh, (x, y, 1 - c)) for sh in range(n)]
        for cp in cps:
            cp.start()
        for cp in cps:
            cp.wait_recv()
        for cp in cps:
            cp.wait_send()

    return _pallas(
        body, name=name, in_specs=[ANY], out_specs=ANY, out_shape=_sds((n, H, C), g.dtype),
        scratch_shapes=[pltpu.SemaphoreType.DMA((n,)), pltpu.SemaphoreType.DMA((n,))],
    )(g)


def _rs_quarters(p, *, name):
    n, H, C = p.shape
    Q = H // 2

    def body(p_ref, la_ref, lb_ref, send_sems, recv_sems):
        x, y, c = _coords()
        sd = 2 * (1 - x) + 1 - y
        a = _remote(p_ref.at[sd, pl.ds(0, Q)], la_ref, send_sems, recv_sems, 0, (x, 1 - y, c))
        b = _remote(p_ref.at[sd, pl.ds(Q, Q)], lb_ref, send_sems, recv_sems, 1, (1 - x, y, c))
        a.start()
        b.start()
        a.wait_recv()
        b.wait_recv()
        a.wait_send()
        b.wait_send()

    return _pallas(
        body, name=name, in_specs=[ANY], out_specs=[ANY, ANY],
        out_shape=[_sds((Q, C), p.dtype), _sds((Q, C), p.dtype)],
        scratch_shapes=[pltpu.SemaphoreType.DMA((2,)), pltpu.SemaphoreType.DMA((2,))],
    )(p)


def _rs_halves(p, ax, ay, *, name):
    n, H, C = p.shape
    Q = H // 2

    def body(p_ref, ax_ref, ay_ref, la_ref, lb_ref, send_sems, recv_sems):
        x, y, c = _coords()
        sx, sy = 2 * (1 - x) + y, 2 * x + 1 - y
        xn, yn = (1 - x, y, c), (x, 1 - y, c)
        lo, hi = pl.ds(0, Q), pl.ds(Q, Q)
        cps = [_remote(ax_ref, la_ref.at[lo], send_sems, recv_sems, 0, xn),
               _remote(p_ref.at[sx, hi], la_ref.at[hi], send_sems, recv_sems, 1, xn),
               _remote(p_ref.at[sy, lo], lb_ref.at[lo], send_sems, recv_sems, 2, yn),
               _remote(ay_ref, lb_ref.at[hi], send_sems, recv_sems, 3, yn)]
        for cp in cps:
            cp.start()
        for cp in cps:
            cp.wait_recv()
        for cp in cps:
            cp.wait_send()

    return _pallas(
        body, name=name, in_specs=[ANY, ANY, ANY], out_specs=[ANY, ANY],
        out_shape=[_sds((H, C), p.dtype), _sds((H, C), p.dtype)],
        scratch_shapes=[pltpu.SemaphoreType.DMA((4,)), pltpu.SemaphoreType.DMA((4,))],
    )(p, ax, ay)


def _rs_join(buf, *, name):
    R, C = buf.shape
    H = R // 2

    def body(in_ref, out_ref, send_sems, recv_sems):
        x, y, c = _coords()
        half = pl.ds(c * H, H)
        other = pl.ds((1 - c) * H, H)
        cp = _remote(in_ref.at[half], out_ref.at[half], send_sems, recv_sems, 0, (x, y, 1 - c))
        cp.start()
        _remote(in_ref.at[other], out_ref.at[other], send_sems, recv_sems, 0, (x, y, c)).wait_recv()
        cp.wait_send()

    return _pallas(
        body, name=name, in_specs=[ANY], out_specs=ANY, out_shape=_sds((R, C), buf.dtype),
        input_output_aliases={0: 0},
        scratch_shapes=[pltpu.SemaphoreType.DMA((1,)), pltpu.SemaphoreType.DMA((1,))],
    )(buf)


def _tile_add(ins_specs, arrays, n_steps, out_spec, out_shape, scalars, *, name):
    grid_spec = pltpu.PrefetchScalarGridSpec(
        num_scalar_prefetch=1, grid=(n_steps,), in_specs=ins_specs, out_specs=out_spec, scratch_shapes=[])

    def body(sc_ref, *refs):
        acc = refs[0][...].astype(F32)
        for r in refs[1:-1]:
            acc = acc + r[...].astype(F32)
        refs[-1][...] = acc.astype(refs[-1].dtype)

    return _pallas(body, name=name, grid_spec=grid_spec, out_shape=out_shape,
                   compiler_params=_params(("arbitrary",)))(scalars, *arrays)


def _reduce_scatter(g, *, name):
    n, R, C = g.shape
    H, Q = R // 2, R // 4
    tm = RS_ROW_MULT // 4
    assert Q % tm == 0, (R, tm)
    x, y, c = _coords()
    sx, sy, s = 2 * (1 - x) + y, 2 * x + 1 - y, 2 * x + y
    sc = jnp.stack([c, sx, sy, s]).astype(jnp.int32)
    hb, qb = H // tm, Q // tm
    blk = lambda f: pl.BlockSpec((None, tm, C), f)
    flat = lambda f: pl.BlockSpec((tm, C), f)

    land = _rs_pair(g, name=name + "_pair")
    p = _tile_add([blk(lambda i, sc: (i // hb, sc[0] * hb + i % hb, 0)), blk(lambda i, sc: (i // hb, i % hb, 0))],
                  [g, land], n * hb, blk(lambda i, sc: (i // hb, i % hb, 0)), _sds((n, H, C), g.dtype), sc, name=name + "_add0")
    la, lb = _rs_quarters(p, name=name + "_quarters")
    ax = _tile_add([blk(lambda i, sc: (sc[1], i, 0)), flat(lambda i, sc: (i, 0))], [p, la], qb,
                   flat(lambda i, sc: (i, 0)), _sds((Q, C), g.dtype), sc, name=name + "_add1x")
    ay = _tile_add([blk(lambda i, sc: (sc[2], qb + i, 0)), flat(lambda i, sc: (i, 0))], [p, lb], qb,
                   flat(lambda i, sc: (i, 0)), _sds((Q, C), g.dtype), sc, name=name + "_add1y")
    fa, fb = _rs_halves(p, ax, ay, name=name + "_halves")
    buf = _tile_add([blk(lambda i, sc: (sc[3], i, 0)), flat(lambda i, sc: (i, 0)), flat(lambda i, sc: (i, 0))],
                    [p, fa, fb], hb, flat(lambda i, sc: (sc[0] * hb + i, 0)), _sds((R, C), F32), sc, name=name + "_add2")
    return _rs_join(buf, name=name + "_join")


def _all_reduce_small(v, *, name):
    M, N = v.shape

    def body(x_ref, out_ref, send_sems, recv_sems, local_sem):
        x, y, c = _coords()
        me, sibling = (x, y, c), (x, y, 1 - c)
        chips = [(1 - x, y), (x, 1 - y), (1 - x, 1 - y)]

        def rows(px, py, pc):
            return out_ref.at[pl.ds((4 * px + 2 * py + pc) * M, M), :]

        def copy(k, block, to, src=None):
            return pltpu.make_async_remote_copy(
                src_ref=rows(*block) if src is None else src, dst_ref=rows(*block),
                send_sem=send_sems.at[k], recv_sem=recv_sems.at[k], device_id=to, device_id_type=MESH)

        mine = pltpu.make_async_copy(x_ref, rows(*me), local_sem)
        mine.start()
        first = [copy(0, me, sibling, src=x_ref)]
        first += [copy(1 + j, me, (*chip, c), src=x_ref) for j, chip in enumerate(chips)]
        for cp in first:
            cp.start()
        passed = [copy(4 + j, (*chip, c), sibling) for j, chip in enumerate(chips)]
        for j, chip in enumerate(chips):
            copy(1 + j, (*chip, c), me).wait_recv()
            passed[j].start()
        copy(0, sibling, me).wait_recv()
        for j, chip in enumerate(chips):
            copy(4 + j, (*chip, 1 - c), me).wait_recv()
        for cp in first + passed:
            cp.wait_send()
        mine.wait()

    gathered = _pallas(
        body, name=name + "_gather",
        out_shape=_sds((N_DEV * M, N), F32),
        in_specs=[pl.BlockSpec(memory_space=pltpu.VMEM)],
        out_specs=pl.BlockSpec(memory_space=pltpu.VMEM),
        scratch_shapes=[pltpu.SemaphoreType.DMA((7,)), pltpu.SemaphoreType.DMA((7,)), pltpu.SemaphoreType.DMA],
    )(v)

    def sum_body(g_ref, o_ref):
        acc = g_ref[0:M, :]
        for d in range(1, N_DEV):
            acc = acc + g_ref[d * M:(d + 1) * M, :]
        o_ref[...] = acc

    return _pallas(sum_body, name=name + "_sum", out_shape=_sds((M, N), F32))(gathered)


MATS = [("conv_w_pw1", 2), ("conv_w_pw2", 1), ("attn_w_q", 1), ("attn_w_o", 1), ("ffn_w1", 2), ("ffn_w2", 1),
        ("ple_w_gate", 1), ("ple_w_proj", 2), ("w_kvf", 1)]
VECS = [("conv_b_pw1", 1), ("conv_w_dw", 2), ("conv_b_dw", 1), ("conv_ln_g", 1), ("conv_ln_b", 1), ("conv_b_pw2", 1)]
REPL = ["mix_norm", "ffn_norm", "ple_norm", "kv_norm", "final_norm", "b_f"]
WEIGHTS = ["mix_norm", "conv_w_pw1", "conv_b_pw1", "conv_w_dw", "conv_b_dw", "conv_ln_g", "conv_ln_b", "conv_w_pw2",
           "conv_b_pw2", "kv_norm", "w_kvf", "b_f", "attn_w_q", "attn_w_o", "ffn_norm", "ffn_w1", "ffn_w2", "ple_norm",
           "ple_w_gate", "ple_w_proj", "final_norm"]


def _round_up(n, m):
    return -(-n // m) * m


def _to_rows(t, C, mult):
    flat = t.reshape(-1)
    rows = _round_up(_round_up(flat.shape[0], C) // C, mult)
    flat = jnp.pad(flat, (0, rows * C - flat.shape[0]))
    return flat.reshape(rows, C)


def _pack(tensors, C, mult, total_mult):
    parts = [_to_rows(t, C, mult) for t in tensors]
    rows = sum(p.shape[0] for p in parts)
    pad = _round_up(rows, total_mult) - rows
    if pad:
        parts.append(jnp.zeros((pad, C), parts[0].dtype))
    return jnp.concatenate(parts, axis=0)


def _row_counts(shapes, C, mult):
    return [_round_up(_round_up(int(np.prod(s)), C) // C, mult) for s in shapes]


def _unpack(packed, shapes, C, mult):
    outs, r0 = [], 0
    lead = packed.shape[:-2]
    for shp, nr in zip(shapes, _row_counts(shapes, C, mult)):
        n = int(np.prod(shp))
        seg = packed[..., r0:r0 + nr, :].reshape(lead + (nr * C,))[..., :n]
        outs.append(seg.reshape(lead + tuple(shp)))
        r0 += nr
    return outs


def _unshard(t, axis):
    return jnp.concatenate([t[s] for s in range(N_CHIPS)], axis=axis)


def _shards(t, axis):
    return jnp.split(t, N_CHIPS, axis=axis)


def _local_step(x, p, tgt, w):
    S, D = x.shape
    L = p.shape[0]
    NA = w["conv_w_pw1"].shape[0]
    H = w["b_f"].shape[0]
    HP = D // LANES
    row = lambda v: v.reshape(1, -1)
    act = lambda dt, n=D: _sds((S, n), dt)
    g = {}

    def rms(hh, gain, name):
        return _rows(_rms_fwd, [hh], [row(gain)], [act(BF16)], name=name)[0]

    saved = []
    h = x
    kv = ck = fl = nkv = h_kv = None
    bfp = jnp.pad(w["b_f"], (0, LANES - H)).reshape(1, LANES)
    wk = w["w_kvf"][:, :D]
    wv = w["w_kvf"][:, D:2 * D]
    wkv = w["w_kvf"][:, :2 * D]
    wf = jnp.pad(w["w_kvf"][:, 2 * D:], ((0, 0), (0, LANES - H)))
    res_rms = lambda acc, r, gn: _with_rms(acc + r, gn)
    hn = rms(h, w["mix_norm"][0], "rms_mix")
    for i in range(L):
        sv = {"h0": h, "hn": hn}
        g_ffn, g_ple = row(w["ffn_norm"][i]), row(w["ple_norm"][i])
        if i < NA:
            u = _mm(hn, w["conv_w_pw1"][i], extras=[row(w["conv_b_pw1"][i])], epi=lambda acc, b: acc + b, name="mm_pw1")
            glu = _rows(_glu_fwd, [u], [], [act(F32)], name="glu_fwd")[0]
            wdw = jnp.pad(w["conv_w_dw"][i], ((0, CONV_PAD - CONV_WIDTH), (0, 0)))
            cv = _dwconv_fwd(glu, wdw, row(w["conv_b_dw"][i]), name="dwconv_fwd")
            sw = _rows(_ln_silu_fwd, [cv], [row(w["conv_ln_g"][i]), row(w["conv_ln_b"][i])], [act(BF16)], name="ln_silu_fwd")[0]
            h1, hn2 = _mm(sw, w["conv_w_pw2"][i], extras=[row(w["conv_b_pw2"][i]), h, g_ffn],
                          epi=lambda acc, b, r, gn: _with_rms(acc + b + r, gn), out_dtype=[F32, BF16], tm=512, tn=D, name="mm_pw2")
            sv.update(u=u, glu=glu, cv=cv, sw=sw, wdw=wdw)
        else:
            if i == NA:
                h_kv = h
                nkv = rms(h, w["kv_norm"], "rms_kv")
                kv = _mm(nkv, wkv, out_dtype=BF16, name="mm_kv")
                fl = _mm(nkv, wf, name="mm_f")
                c = _gate_cumsum(fl, bfp, name="gate_cumsum")
                ck = jnp.pad(c[:, :H].T.reshape(HP, 2, S), ((0, 0), (0, 6), (0, 0)))
            j = i - NA
            q = _mm(hn, w["attn_w_q"][j], epi=lambda acc: acc * (HEAD_DIM ** -0.5), out_dtype=BF16, name="mm_q")
            o, lse = _attn_fwd(q, kv, ck, name="attn_fwd")
            h1, hn2 = _mm(o, w["attn_w_o"][j], extras=[h, g_ffn], epi=res_rms, out_dtype=[F32, BF16], tm=512, tn=D, name="mm_o")
            sv.update(q=q, o=o, lse=lse)
        zb, f = _mm(hn2, w["ffn_w1"][i], epi=lambda acc: (acc, jnp.square(jnp.maximum(acc, 0.0))), out_dtype=[BF16, BF16],
                    name="mm_ffn1")
        h2, n3 = _mm(f, w["ffn_w2"][i], extras=[h1, g_ple], epi=res_rms, out_dtype=[F32, BF16], tm=512, tn=D, tk=4 * D, name="mm_ffn2")
        zg = _mm(n3, w["ple_w_gate"][i], name="mm_gate")
        ple = lambda acc, r, zz: r + _sigmoid(zz) * acc
        if i + 1 < L:
            h, hn = _mm(p[i], w["ple_w_proj"][i], extras=[h2, zg, row(w["mix_norm"][i + 1])],
                        epi=lambda acc, r, zz, gn: _with_rms(ple(acc, r, zz), gn), out_dtype=[F32, BF16], tm=512, tn=D, name="mm_proj")
        else:
            h = _mm(p[i], w["ple_w_proj"][i], extras=[h2, zg], epi=ple, tm=512, tn=D, name="mm_proj_last")
        sv.update(h1=h1, hn2=hn2, zb=zb, f=f, h2=h2, n3=n3, zg=zg)
        saved.append(sv)

    dh, err2, g_final = _rows(_final_fn, [h, tgt], [row(w["final_norm"])], [act(F32)], [_sds((1, D), F32), _sds((1, D), F32)],
                              name="final")
    loss = 0.5 * jnp.sum(err2) / D
    g["final_norm"] = g_final.reshape(-1)

    red = _sds((1, D), F32)
    stack = {k: [None] * n for k, n in (("mix_norm", L), ("ffn_norm", L), ("ple_norm", L), ("ffn_w1", L), ("ffn_w2", L),
                                        ("ple_w_gate", L), ("ple_w_proj", L), ("conv_w_pw1", NA), ("conv_b_pw1", NA),
                                        ("conv_w_dw", NA), ("conv_b_dw", NA), ("conv_ln_g", NA), ("conv_ln_b", NA),
                                        ("conv_w_pw2", NA), ("conv_b_pw2", NA), ("attn_w_q", L - NA), ("attn_w_o", L - NA))}
    dk_sum = dv_sum = None
    dcks = []
    for i in reversed(range(L)):
        sv = saved[i]
        dzg, dpp = _mm(p[i], w["ple_w_proj"][i], extras=[dh, sv["zg"]], epi=lambda acc, d, zz: _ple_bwd(d, zz, acc),
                       out_dtype=[BF16, BF16], name="mm_ple_bwd")
        stack["ple_w_proj"][i] = _mm(p[i], dpp, mode="tn", tk=2048, out_dtype=BF16, name="mm_dproj")
        stack["ple_w_gate"][i] = _mm(sv["n3"], dzg, mode="tn", tk=2048, out_dtype=BF16, name="mm_dgate")
        dh, dhb, dgain = _mm(dzg, w["ple_w_gate"][i], mode="nt", extras=[sv["h2"], dh, row(w["ple_norm"][i])], epi=_dup(_rms_bwd),
                             out_dtype=[F32, BF16], reds=1, tm=512, tn=D, name="mm_dn3")
        stack["ple_norm"][i] = dgain.reshape(-1)
        dz = _mm(dhb, w["ffn_w2"][i], mode="nt", extras=[sv["zb"]], epi=lambda acc, zz: acc * (2.0 * jnp.maximum(zz, 0.0).astype(F32)),
                 out_dtype=BF16, name="mm_dz")
        stack["ffn_w2"][i] = _mm(sv["f"], dhb, mode="tn", tk=2048, out_dtype=BF16, name="mm_dffn2")
        stack["ffn_w1"][i] = _mm(sv["hn2"], dz, mode="tn", tk=2048, out_dtype=BF16, name="mm_dffn1")
        if i < NA:
            dh, dhb, dgain, dbias = _mm(dz, w["ffn_w1"][i], mode="nt", extras=[sv["h1"], dh, row(w["ffn_norm"][i])],
                                        epi=_dup(_rms_bwd_bias), out_dtype=[F32, BF16], reds=2, tm=512, tn=D, tk=4 * D, name="mm_dhn2_bias")
            stack["conv_b_pw2"][i] = dbias.reshape(-1)
        else:
            dh, dhb, dgain = _mm(dz, w["ffn_w1"][i], mode="nt", extras=[sv["h1"], dh, row(w["ffn_norm"][i])], epi=_dup(_rms_bwd),
                                 out_dtype=[F32, BF16], reds=1, tm=512, tn=D, tk=4 * D, name="mm_dhn2")
        stack["ffn_norm"][i] = dgain.reshape(-1)
        if i < NA:
            dsw = _mm(dhb, w["conv_w_pw2"][i], mode="nt", name="mm_dsw")
            stack["conv_w_pw2"][i] = _mm(sv["sw"], dhb, mode="tn", tk=2048, out_dtype=BF16, name="mm_dpw2")
            dcv, dlg, dlb = _rows(_ln_silu_bwd, [dsw, sv["cv"]], [row(w["conv_ln_g"][i]), row(w["conv_ln_b"][i])], [act(F32)],
                                  [red, red], name="ln_silu_bwd")
            stack["conv_ln_g"][i], stack["conv_ln_b"][i] = dlg.reshape(-1), dlb.reshape(-1)
            dglu, dwdw, dbdw = _dwconv_bwd(dcv, sv["glu"], sv["wdw"], name="dwconv_bwd")
            stack["conv_w_dw"][i], stack["conv_b_dw"][i] = dwdw[:CONV_WIDTH], dbdw.reshape(-1)
            du, dbu = _rows(_glu_bwd, [dglu, sv["u"]], [], [act(BF16, 2 * D)], [_sds((1, 2 * D), F32)], name="glu_bwd")
            stack["conv_b_pw1"][i] = dbu.reshape(-1)
            stack["conv_w_pw1"][i] = _mm(sv["hn"], du, mode="tn", tk=2048, out_dtype=BF16, name="mm_dpw1")
            dh, dgain = _mm(du, w["conv_w_pw1"][i], mode="nt", extras=[sv["h0"], dh, row(w["mix_norm"][i])], epi=_rms_bwd,
                            reds=1, tm=512, tn=D, name="mm_dhn_a")
        else:
            j = i - NA
            do = _mm(dhb, w["attn_w_o"][j], mode="nt", out_dtype=BF16, name="mm_do")
            stack["attn_w_o"][j] = _mm(sv["o"], dhb, mode="tn", tk=2048, out_dtype=BF16, name="mm_dwo")
            dq, drs, dk, dv, dcs = _attn_bwd(sv["q"], kv, sv["o"], do, sv["lse"], ck, name="attn_bwd")
            scale = lambda acc: acc * (HEAD_DIM ** -0.5)
            stack["attn_w_q"][j] = _mm(sv["hn"], dq, mode="tn", tk=2048, out_dtype=BF16, epi=scale, name="mm_dwq")
            dh_in = dh
            dh, dgain = _mm(dq, w["attn_w_q"][j], mode="nt", extras=[sv["h0"], dh_in, row(w["mix_norm"][i])],
                            epi=lambda acc, xx, dr, gn: _rms_bwd(scale(acc), xx, dr, gn), reds=1, tm=512, tn=D, name="mm_dhn_b")
            pick = lambda t: jnp.pad(t.reshape(S, HP, 2, HEAD_DIM)[:, :, ::-1, 0].reshape(S, H), ((0, 0), (0, LANES - H)))
            dcks += [pick(drs), pick(dcs)]
            if dk_sum is None:
                dk_sum, dv_sum = dk, dv
            else:
                dk_sum = _rows(_add2, [dk_sum, dk], [], [act(F32)], name="add_dk")[0]
                dv_sum = _rows(_add2, [dv_sum, dv], [], [act(F32)], name="add_dk")[0]
        stack["mix_norm"][i] = dgain.reshape(-1)
        if i == NA:
            dfl, dbf = _gate_cumsum_bwd(dcks, fl, bfp, name="gate_cumsum_bwd")
            g["b_f"] = dbf[0, :H]
            gk = _mm(nkv, dk_sum, mode="tn", tk=2048, out_dtype=BF16, name="mm_dwk")
            gv = _mm(nkv, dv_sum, mode="tn", tk=2048, out_dtype=BF16, name="mm_dwk")
            gf = _mm(nkv, dfl, mode="tn", tk=2048, out_dtype=BF16, name="mm_dwf")
            g["w_kvf"] = jnp.concatenate([gk, gv, gf[:, :H]], axis=1)
            dn = _mm(dk_sum, wk, mode="nt", name="mm_dnk")
            dn = _mm(dv_sum, wv, mode="nt", extras=[dn], epi=lambda acc, r: acc + r, name="mm_dnv")
            dh, dgain = _mm(dfl, wf, mode="nt", extras=[dn, h_kv, dh, row(w["kv_norm"])],
                            epi=lambda acc, r, xx, dr, gn: _rms_bwd(acc + r, xx, dr, gn), reds=1, tm=512, tn=D, name="mm_dnf")
            g["kv_norm"] = dgain.reshape(-1)
    for k, v in stack.items():
        g[k] = jnp.stack(v, axis=0)
    return loss, dh, g


def kernel(x, p, mix_norm, conv_w_pw1, conv_b_pw1, conv_w_dw, conv_b_dw, conv_ln_g, conv_ln_b, conv_w_pw2, conv_b_pw2, kv_norm, w_kvf, b_f, attn_w_q, attn_w_o, ffn_norm, ffn_w1, ffn_w2, ple_norm, ple_w_gate, ple_w_proj, final_norm, loss_target, m_mix_norm, m_conv_w_pw1, m_conv_b_pw1, m_conv_w_dw, m_conv_b_dw, m_conv_ln_g, m_conv_ln_b, m_conv_w_pw2, m_conv_b_pw2, m_kv_norm, m_w_kvf, m_b_f, m_attn_w_q, m_attn_w_o, m_ffn_norm, m_ffn_w1, m_ffn_w2, m_ple_norm, m_ple_w_gate, m_ple_w_proj, m_final_norm, v_mix_norm, v_conv_w_pw1, v_conv_b_pw1, v_conv_w_dw, v_conv_b_dw, v_conv_ln_g, v_conv_ln_b, v_conv_w_pw2, v_conv_b_pw2, v_kv_norm, v_w_kvf, v_b_f, v_attn_w_q, v_attn_w_o, v_ffn_norm, v_ffn_w1, v_ffn_w2, v_ple_norm, v_ple_w_gate, v_ple_w_proj, v_final_norm):
    args = dict(locals())
    wl = {n: args[n] for n in WEIGHTS}
    ml = {n: args["m_" + n] for n in WEIGHTS}
    vl = {n: args["v_" + n] for n in WEIGHTS}
    S, D = x.shape[1], x.shape[2]
    C = D

    mat_shapes = [wl[n].shape for n, _ in MATS]
    vec_shapes = [wl[n].shape for n, _ in VECS]
    mats = _all_gather_shards(_pack([wl[n].astype(BF16) for n, _ in MATS], C, 16, 64), name="ag_mats")
    vecs = _all_gather_shards(_pack([wl[n] for n, _ in VECS], C, 1, 32), name="ag_vecs")
    full = {n: wl[n] for n in REPL}
    for (n, ax), t in zip(MATS, _unpack(mats, mat_shapes, C, 16)):
        full[n] = _unshard(t, ax)
    for (n, ax), t in zip(VECS, _unpack(vecs, vec_shapes, C, 1)):
        full[n] = _unshard(t, ax)

    loss, dx, g = _local_step(x[0], p[:, 0], loss_target[0], full)
    loss = lax.psum(loss, ("x", "y", "c"))

    names = [n for n, _ in MATS] + [n for n, _ in VECS]
    axes = dict(MATS + VECS)
    shard_shapes = [wl[n].shape for n in names]
    per_shard = [[] for _ in range(N_CHIPS)]
    for n in names:
        for s, piece in enumerate(_shards(g[n].astype(BF16), axes[n])):
            per_shard[s].append(piece)
    gpack = jnp.stack([_pack(ts, C, 16, RS_ROW_MULT) for ts in per_shard], axis=0)
    gred = _reduce_scatter(gpack, name="rs")
    gl = dict(zip(names, _unpack(gred, shard_shapes, C, 16)))

    rep_shapes = [wl[n].shape for n in REPL]
    rpack = _pack([g[n] for n in REPL], C, 1, 8)
    for n, t in zip(REPL, _unpack(_all_reduce_small(rpack, name="ar"), rep_shapes, C, 1)):
        gl[n] = t

    grads, deltas, new_m, new_v = {}, {}, {}, {}
    rep_w, rep_m, rep_v = (_pack([d[n] for n in REPL], C, 1, 8) for d in (wl, ml, vl))
    rep_out = _adamw(rep_w, rpack_like(gl, rep_shapes, C), rep_m, rep_v, "adamw_rep")
    for dst, packed in zip((grads, deltas, new_m, new_v), rep_out):
        for n, t in zip(REPL, _unpack(packed, rep_shapes, C, 1)):
            dst[n] = t
    for n in names:
        res = _adamw(wl[n], gl[n], ml[n], vl[n], "adamw_" + n)
        for dst, t in zip((grads, deltas, new_m, new_v), res):
            dst[n] = t
    out = [loss, dx[None]]
    for d in (grads, deltas, new_m, new_v):
        out += [d[n] for n in WEIGHTS]
    return tuple(out)


def rpack_like(gl, rep_shapes, C):
    return _pack([gl[n] for n in REPL], C, 1, 8)
```

```python
import functools

import jax
import jax.numpy as jnp
import numpy as np
from jax import lax
from jax.experimental import pallas as pl
from jax.experimental.pallas import tpu as pltpu

F32 = jnp.float32
BF16 = jnp.bfloat16
MESH = pl.DeviceIdType.MESH

N_CHIPS = 4
N_DEV = 8
HEAD_DIM = 64
LANES = 128
CONV_WIDTH = 31
CONV_PAD = 32
EPS = 1e-6
NEG_BIG = -1e30
VMEM_LIMIT = 56 * 1024 * 1024

RS_ROW_MULT = 1024

ADAM_LR, ADAM_B1, ADAM_B2, ADAM_EPS, ADAM_WD, ADAM_STEP = 0.001, 0.9, 0.999, 1e-08, 0.01, 10


def _pallas(body, **kw):
    return pl.pallas_call(body, **kw)


def _params(sem=None):
    return pltpu.CompilerParams(dimension_semantics=sem, vmem_limit_bytes=VMEM_LIMIT)


def _sds(shape, dtype):
    return jax.ShapeDtypeStruct(tuple(shape), dtype)


_DIMS = {"nn": (((1,), (0,)), ((), ())), "nt": (((1,), (1,)), ((), ())), "tn": (((0,), (0,)), ((), ()))}


def _mm(a, b, *, mode="nn", extras=(), epi=None, out_dtype=F32, reds=0, tm=1024, tn=1024, tk=1024, name):
    if mode == "nn":
        (M, K), (K2, N) = a.shape, b.shape
    elif mode == "nt":
        (M, K), (N, K2) = a.shape, b.shape
    else:
        (K, M), (K2, N) = a.shape, b.shape
    assert K == K2, (name, a.shape, b.shape)
    tm, tn, tk = min(tm, M), min(tn, N), min(tk, K)
    assert M % tm == 0 and N % tn == 0 and K % tk == 0, (name, a.shape, b.shape)
    nk = K // tk
    if mode == "tn":
        a_spec = pl.BlockSpec((tk, tm), lambda i, j, k: (k, i))
    else:
        a_spec = pl.BlockSpec((tm, tk), lambda i, j, k: (i, k))
    if mode == "nt":
        b_spec = pl.BlockSpec((tn, tk), lambda i, j, k: (j, k))
    else:
        b_spec = pl.BlockSpec((tk, tn), lambda i, j, k: (k, j))
    ex_specs = []
    for e in extras:
        if e.shape[0] == 1:
            ex_specs.append(pl.BlockSpec((1, tn), lambda i, j, k: (0, j)))
        else:
            assert e.shape == (M, N), (name, e.shape)
            ex_specs.append(pl.BlockSpec((tm, tn), lambda i, j, k: (i, j)))
    ne = len(extras)
    dims = _DIMS[mode]
    many = isinstance(out_dtype, (list, tuple))
    out_dtypes = list(out_dtype) if many else [out_dtype]
    no = len(out_dtypes)
    assert not reds or tn == N, name

    def body(a_ref, b_ref, *rest):
        ex_refs, o_refs, r_refs = rest[:ne], rest[ne:ne + no], rest[ne + no:ne + no + reds]
        part = lax.dot_general(a_ref[...].astype(BF16), b_ref[...].astype(BF16), dims, preferred_element_type=F32)
        i = pl.program_id(0)

        def finish(acc):
            res = epi(acc, *[r[...] for r in ex_refs]) if epi is not None else acc
            if not isinstance(res, (tuple, list)):
                res = (res,)
            assert len(res) == no + reds, (name, len(res))
            for r, v in zip(o_refs, res[:no]):
                r[...] = v.astype(r.dtype)
            for r, v in zip(r_refs, res[no:]):
                @pl.when(i == 0)
                def _(r=r, v=v):
                    r[...] = v

                @pl.when(i > 0)
                def _(r=r, v=v):
                    r[...] += v

        if nk == 1:
            finish(part)
        else:
            acc_ref = rest[ne + no + reds]
            k = pl.program_id(2)

            @pl.when(k == 0)
            def _():
                acc_ref[...] = part

            @pl.when(k > 0)
            def _():
                acc_ref[...] += part

            @pl.when(k == nk - 1)
            def _():
                finish(acc_ref[...])

    res = _pallas(
        body, name=name, grid=(M // tm, N // tn, nk),
        in_specs=[a_spec, b_spec] + ex_specs,
        out_specs=[pl.BlockSpec((tm, tn), lambda i, j, k: (i, j))] * no + [pl.BlockSpec((1, tn), lambda i, j, k: (0, j))] * reds,
        out_shape=[_sds((M, N), dt) for dt in out_dtypes] + [_sds((1, N), F32)] * reds,
        scratch_shapes=[pltpu.VMEM((tm, tn), F32)] if nk > 1 else [],
        compiler_params=_params(("arbitrary",) * 3 if reds else ("parallel", "parallel", "arbitrary")),
    )(a, b, *extras)
    return res if (many or reds) else res[0]


def _rows(fn, ins, params, outs, reds=(), *, tm=256, name):
    S = ins[0].shape[0]
    tm = min(tm, S)
    assert S % tm == 0, (name, S, tm)
    ni, npar, no, nr = len(ins), len(params), len(outs), len(reds)

    def body(*refs):
        in_refs, p_refs = refs[:ni], refs[ni:ni + npar]
        o_refs, r_refs = refs[ni + npar:ni + npar + no], refs[ni + npar + no:]
        res = fn(*[r[...] for r in in_refs], *[r[...] for r in p_refs])
        if not isinstance(res, (tuple, list)):
            res = (res,)
        assert len(res) == no + nr, (name, len(res))
        for r, v in zip(o_refs, res[:no]):
            r[...] = v.astype(r.dtype)
        i = pl.program_id(0)
        for r, v in zip(r_refs, res[no:]):
            @pl.when(i == 0)
            def _(r=r, v=v):
                r[...] = v

            @pl.when(i > 0)
            def _(r=r, v=v):
                r[...] += v

    res = _pallas(
        body, name=name, grid=(S // tm,),
        in_specs=[pl.BlockSpec((tm, a.shape[1]), lambda i: (i, 0)) for a in ins]
        + [pl.BlockSpec(p.shape, lambda i: (0, 0)) for p in params],
        out_specs=[pl.BlockSpec((tm, o.shape[1]), lambda i: (i, 0)) for o in outs]
        + [pl.BlockSpec(r.shape, lambda i: (0, 0)) for r in reds],
        out_shape=list(outs) + list(reds),
        compiler_params=_params(("arbitrary",)),
    )(*ins, *params)
    return res


def _colsum(v):
    return jnp.sum(v, axis=0, keepdims=True)


def _sigmoid(v):
    return 1.0 / (1.0 + jnp.exp(-v))


def _rms_stats(x):
    r = lax.rsqrt(jnp.mean(x * x, axis=-1, keepdims=True) + EPS)
    return x * r, r


def _rms_fwd(x, g):
    xh, _ = _rms_stats(x)
    return (xh * g,)


def _with_rms(h, g):
    xh, _ = _rms_stats(h)
    return h, xh * g


def _rms_bwd(dy, x, dres, g):
    xh, r = _rms_stats(x)
    dyg = dy * g
    dx = r * (dyg - xh * jnp.mean(dyg * xh, axis=-1, keepdims=True))
    return dres + dx, _colsum(dy * xh)


def _dup(fn):
    def wrapped(*a):
        r = fn(*a)
        return (r[0], r[0]) + tuple(r[1:])
    return wrapped


def _rms_bwd_bias(dy, x, dres, g):
    dx, dg = _rms_bwd(dy, x, dres, g)
    return dx, dg, _colsum(dx)


def _glu_fwd(u):
    d = u.shape[1] // 2
    return (u[:, :d] * _sigmoid(u[:, d:]),)


def _glu_bwd(dglu, u):
    d = u.shape[1] // 2
    a, sig = u[:, :d], _sigmoid(u[:, d:])
    du = jnp.concatenate([dglu * sig, dglu * a * sig * (1.0 - sig)], axis=1)
    return du, _colsum(du)


def _ln_parts(x, g, b):
    mu = jnp.mean(x, axis=-1, keepdims=True)
    xc = x - mu
    r = lax.rsqrt(jnp.mean(xc * xc, axis=-1, keepdims=True) + EPS)
    xh = xc * r
    return xh, r, xh * g + b


def _ln_silu_fwd(x, g, b):
    _, _, y = _ln_parts(x, g, b)
    return (y * _sigmoid(y),)


def _ln_silu_bwd(dsw, x, g, b):
    xh, r, y = _ln_parts(x, g, b)
    sig = _sigmoid(y)
    dy = dsw * sig * (1.0 + y * (1.0 - sig))
    dxh = dy * g
    dx = r * (dxh - jnp.mean(dxh, axis=-1, keepdims=True) - xh * jnp.mean(dxh * xh, axis=-1, keepdims=True))
    return dx, _colsum(dy * xh), _colsum(dy)


def _relu2(z):
    zp = jnp.maximum(z, 0.0)
    return (zp * zp,)


def _ple_fwd(h, zg, pp):
    return (h + _sigmoid(zg) * pp,)


def _ple_bwd(dh, zg, pp):
    gate = _sigmoid(zg)
    return dh * pp * gate * (1.0 - gate), dh * gate


def _final_fn(h, t, g):
    xh, r = _rms_stats(h)
    err = xh * g - t
    dy = err * (1.0 / h.shape[1])
    dyg = dy * g
    dh = r * (dyg - xh * jnp.mean(dyg * xh, axis=-1, keepdims=True))
    return dh, _colsum(err * err), _colsum(dy * xh)


def _add2(a, b):
    return (a + b,)


def _adamw_fn(w, g, m, v):
    m = ADAM_B1 * m + (1.0 - ADAM_B1) * g
    v = ADAM_B2 * v + (1.0 - ADAM_B2) * (g * g)
    m_hat = m / (1.0 - ADAM_B1 ** ADAM_STEP)
    v_hat = v / (1.0 - ADAM_B2 ** ADAM_STEP)
    delta = -ADAM_LR * (m_hat / (jnp.sqrt(v_hat) + ADAM_EPS) + ADAM_WD * w)
    return g, delta, m, v


def _adamw(w, g, m, v, name):
    shape = w.shape
    cols = shape[-1] if len(shape) > 1 else shape[0]
    two = lambda t: t.reshape(-1, cols)
    o = _sds(two(w).shape, F32)
    res = _rows(_adamw_fn, [two(w), two(g), two(m), two(v)], [], [o, o, o, o], tm=512, name=name)
    return [r.reshape(shape) for r in res]


def _sublane_shifts(win):
    n = win.shape[0]
    return [win] + [pltpu.roll(win, n - b, axis=0) for b in range(1, 8)]


def _tap(shifted, offset, rows):
    a, b = divmod(offset, 8)
    return shifted[b][8 * a:8 * a + rows]


def _dwconv_fwd(u, w, b, *, tm=512, name):
    S, D = u.shape
    tm = min(tm, S)
    rc = min(128, tm)
    per = tm // CONV_PAD

    def body(prev_ref, cur_ref, w_ref, b_ref, o_ref, win):
        i = pl.program_id(0)

        @pl.when(i == 0)
        def _():
            win[0:CONV_PAD, :] = jnp.zeros((CONV_PAD, D), F32)

        @pl.when(i > 0)
        def _():
            win[0:CONV_PAD, :] = prev_ref[...]

        win[CONV_PAD:CONV_PAD + tm, :] = cur_ref[...]
        for lc in range(D // LANES):
            ls = slice(lc * LANES, (lc + 1) * LANES)
            for r0 in range(0, tm, rc):
                shifted = _sublane_shifts(win[r0:r0 + rc + CONV_PAD, ls])
                acc = jnp.zeros((rc, LANES), F32) + b_ref[:, ls]
                for k in range(CONV_WIDTH):
                    acc = acc + _tap(shifted, 2 + k, rc) * w_ref[k:k + 1, ls]
                o_ref[r0:r0 + rc, ls] = acc

    return _pallas(
        body, name=name, grid=(S // tm,),
        in_specs=[pl.BlockSpec((CONV_PAD, D), lambda i: (jnp.maximum(i * per - 1, 0), 0)),
                  pl.BlockSpec((tm, D), lambda i: (i, 0)),
                  pl.BlockSpec((CONV_PAD, D), lambda i: (0, 0)),
                  pl.BlockSpec((1, D), lambda i: (0, 0))],
        out_specs=pl.BlockSpec((tm, D), lambda i: (i, 0)),
        out_shape=_sds((S, D), F32),
        scratch_shapes=[pltpu.VMEM((tm + CONV_PAD, D), F32)],
        compiler_params=_params(("arbitrary",)),
    )(u, u, w, b)


def _dwconv_bwd(dy, u, w, *, tm=512, name):
    S, D = u.shape
    tm = min(tm, S)
    rc = min(128, tm)
    per = tm // CONV_PAD
    n = S // tm
    nxt = S // CONV_PAD - 1

    def body(dy_ref, dyn_ref, up_ref, u_ref, w_ref, du_ref, dw_ref, db_ref, wd, wu, dwacc, dbacc):
        i = pl.program_id(0)

        @pl.when(i == 0)
        def _():
            wu[0:CONV_PAD, :] = jnp.zeros((CONV_PAD, D), F32)
            dwacc[...] = jnp.zeros(dwacc.shape, F32)
            dbacc[...] = jnp.zeros(dbacc.shape, F32)

        @pl.when(i > 0)
        def _():
            wu[0:CONV_PAD, :] = up_ref[...]

        @pl.when(i == n - 1)
        def _():
            wd[tm:tm + CONV_PAD, :] = jnp.zeros((CONV_PAD, D), F32)

        @pl.when(i < n - 1)
        def _():
            wd[tm:tm + CONV_PAD, :] = dyn_ref[...]

        wu[CONV_PAD:CONV_PAD + tm, :] = u_ref[...]
        wd[0:tm, :] = dy_ref[...]
        for lc in range(D // LANES):
            ls = slice(lc * LANES, (lc + 1) * LANES)
            for r0 in range(0, tm, rc):
                sd = _sublane_shifts(wd[r0:r0 + rc + CONV_PAD, ls])
                acc = jnp.zeros((rc, LANES), F32)
                for k in range(CONV_WIDTH):
                    acc = acc + _tap(sd, 30 - k, rc) * w_ref[k:k + 1, ls]
                du_ref[r0:r0 + rc, ls] = acc
                dyc = wd[r0:r0 + rc, ls]
                dbacc[:, ls] += jnp.sum(dyc.reshape(rc // 8, 8, LANES), axis=0)
                for k in range(CONV_WIDTH):
                    prod = dyc * wu[r0 + 2 + k:r0 + 2 + k + rc, ls]
                    dwacc[8 * k:8 * k + 8, ls] += jnp.sum(prod.reshape(rc // 8, 8, LANES), axis=0)

        @pl.when(i == n - 1)
        def _():
            dw_ref[...] = jnp.zeros(dw_ref.shape, F32)
            for k in range(CONV_WIDTH):
                dw_ref[k:k + 1, :] = jnp.sum(dwacc[8 * k:8 * k + 8, :], axis=0, keepdims=True)
            db_ref[...] = jnp.sum(dbacc[...], axis=0, keepdims=True)

    return _pallas(
        body, name=name, grid=(n,),
        in_specs=[pl.BlockSpec((tm, D), lambda i: (i, 0)),
                  pl.BlockSpec((CONV_PAD, D), lambda i: (jnp.minimum((i + 1) * per, nxt), 0)),
                  pl.BlockSpec((CONV_PAD, D), lambda i: (jnp.maximum(i * per - 1, 0), 0)),
                  pl.BlockSpec((tm, D), lambda i: (i, 0)),
                  pl.BlockSpec((CONV_PAD, D), lambda i: (0, 0))],
        out_specs=[pl.BlockSpec((tm, D), lambda i: (i, 0)),
                   pl.BlockSpec((CONV_PAD, D), lambda i: (0, 0)),
                   pl.BlockSpec((1, D), lambda i: (0, 0))],
        out_shape=[_sds((S, D), F32), _sds((CONV_PAD, D), F32), _sds((1, D), F32)],
        scratch_shapes=[pltpu.VMEM((tm + CONV_PAD, D), F32), pltpu.VMEM((tm + CONV_PAD, D), F32),
                        pltpu.VMEM((8 * CONV_PAD, D), F32), pltpu.VMEM((8, D), F32)],
        compiler_params=_params(("arbitrary",)),
    )(dy, dy, u, u, w)


def _tri_dot(tri, x):
    x1 = x.astype(BF16)
    r1 = x - x1.astype(F32)
    x2 = r1.astype(BF16)
    x3 = (r1 - x2.astype(F32)).astype(BF16)
    d = lambda v: jnp.dot(tri, v, preferred_element_type=F32)
    return d(x1) + d(x2) + d(x3)


def _log_sigmoid(x):
    return jnp.minimum(x, 0.0) - jnp.log(1.0 + jnp.exp(-jnp.abs(x)))


def _gate_cumsum(fl, bf, *, tm=256, name):
    S, W = fl.shape
    tm = min(tm, S)

    def body(fl_ref, bf_ref, c_ref, carry):
        i = pl.program_id(0)

        @pl.when(i == 0)
        def _():
            carry[...] = jnp.zeros(carry.shape, F32)

        x = _log_sigmoid(fl_ref[...] + bf_ref[...])
        row = lax.broadcasted_iota(jnp.int32, (tm, tm), 0)
        col = lax.broadcasted_iota(jnp.int32, (tm, tm), 1)
        tri = jnp.where(row >= col, 1.0, 0.0).astype(BF16)
        cs = _tri_dot(tri, x) + carry[0:1, :]
        c_ref[...] = cs
        carry[...] = jnp.broadcast_to(cs[tm - 1:tm, :], carry.shape)

    return _pallas(
        body, name=name, grid=(S // tm,),
        in_specs=[pl.BlockSpec((tm, W), lambda i: (i, 0)), pl.BlockSpec((1, W), lambda i: (0, 0))],
        out_specs=pl.BlockSpec((tm, W), lambda i: (i, 0)),
        out_shape=_sds((S, W), F32),
        scratch_shapes=[pltpu.VMEM((8, W), F32)],
        compiler_params=_params(("arbitrary",)),
    )(fl, bf)


def _gate_cumsum_bwd(sums, fl, bf, *, tm=256, name):
    S, W = fl.shape
    tm = min(tm, S)
    n = S // tm
    ns = len(sums)
    assert ns % 2 == 0

    def body(*refs):
        sum_refs = refs[:ns]
        fl_ref, bf_ref, o_ref, s_ref, carry = refs[ns:]
        i = pl.program_id(0)

        @pl.when(i == 0)
        def _():
            carry[...] = jnp.zeros(carry.shape, F32)
            s_ref[...] = jnp.zeros(s_ref.shape, F32)

        dc = sum_refs[0][...] - sum_refs[1][...]
        for a in range(2, ns, 2):
            dc = dc + (sum_refs[a][...] - sum_refs[a + 1][...])
        row = lax.broadcasted_iota(jnp.int32, (tm, tm), 0)
        col = lax.broadcasted_iota(jnp.int32, (tm, tm), 1)
        tri = jnp.where(col >= row, 1.0, 0.0).astype(BF16)
        rs = _tri_dot(tri, dc) + carry[0:1, :]
        carry[...] = jnp.broadcast_to(rs[0:1, :], carry.shape)
        dfl = rs * _sigmoid(-(fl_ref[...] + bf_ref[...]))
        o_ref[...] = dfl
        s_ref[...] += _colsum(dfl)

    rev = lambda i: (n - 1 - i, 0)
    return _pallas(
        body, name=name, grid=(n,),
        in_specs=[pl.BlockSpec((tm, W), rev)] * (ns + 1) + [pl.BlockSpec((1, W), lambda i: (0, 0))],
        out_specs=[pl.BlockSpec((tm, W), rev), pl.BlockSpec((1, W), lambda i: (0, 0))],
        out_shape=[_sds((S, W), F32), _sds((1, W), F32)],
        scratch_shapes=[pltpu.VMEM((8, W), F32)],
        compiler_params=_params(("arbitrary",)),
    )(*sums, fl, bf)


def _tri_tables(nq, qc, by_query):
    if by_query:
        pairs = [(i, j) for i in range(nq) for j in range(qc * (i + 1))]
    else:
        pairs = [(i, j) for j in range(qc * nq) for i in range(j // qc, nq)]
    ii, jj = zip(*pairs)
    return jnp.asarray(np.array(ii, np.int32)), jnp.asarray(np.array(jj, np.int32))


def _chunk_kinds(qc, dd):
    if dd < 0:
        return ("full",) * qc
    return tuple("full" if r > dd else "diag" if r == dd else "skip" for r in range(qc))


def _rep(v, t):
    return jnp.tile(v, (1, t // LANES))


def _attn_fwd(q, kv, ck, *, tb=512, row_chunks=4, name):
    S, D = q.shape
    HP = D // LANES
    T = min(tb, S)
    QC = row_chunks if S >= row_chunks * T else 1
    TQ = QC * T
    it, jt = _tri_tables(S // TQ, QC, True)

    def body(it_ref, jt_ref, q_ref, k_ref, v_ref, ck_ref, o_ref, lse_ref, st):
        s_id = pl.program_id(1)
        i, j = it_ref[s_id], jt_ref[s_id]
        dd = j - QC * i
        lane = lax.broadcasted_iota(jnp.int32, (TQ, LANES), 1)
        head0 = lane < HEAD_DIM
        h0 = head0[:T]
        hms = (h0, jnp.logical_not(h0))

        @pl.when(j == 0)
        def _():
            st[0:2] = jnp.full((2, TQ, LANES), NEG_BIG, F32)
            st[2:4] = jnp.zeros((2, TQ, LANES), F32)

        def step(kinds):
            kvv, vv = k_ref[...], v_ref[...]
            one = jnp.ones_like(vv)
            vaug = [jnp.where(hms[h], vv, one) for h in range(2)]
            old = st[...]
            rows = lambda r: slice(r * T, (r + 1) * T)
            live = [r for r in range(QC) if kinds[r] != "skip"]
            chains = [(r, h) for r in live for h in range(2)]
            ss = {}
            for r, h in chains:
                qv = q_ref[rows(r), :]
                ss[r, h] = lax.dot_general(jnp.where(hms[h], qv, jnp.zeros_like(qv)), kvv, _DIMS["nt"], preferred_element_type=F32)
            ps, alphas, m_new = {}, {}, {}
            for r, h in chains:
                s = ss[r, h] - ck_ref[h:h + 1, :]
                if kinds[r] == "diag":
                    row = lax.broadcasted_iota(jnp.int32, (T, T), 0)
                    col = lax.broadcasted_iota(jnp.int32, (T, T), 1)
                    s = jnp.where(row >= col, s, NEG_BIG)
                m_prev = old[h, rows(r), :]
                m_new[r, h] = jnp.maximum(m_prev, jnp.max(s, axis=1, keepdims=True))
                ps[r, h] = jnp.exp(s - _rep(m_new[r, h], T)).astype(BF16)
                alphas[r, h] = jnp.exp(m_prev - m_new[r, h])
            new = [[], [], [], []]
            for r in range(QC):
                if kinds[r] == "skip":
                    for a in range(4):
                        new[a].append(old[a, rows(r), :])
                    continue
                pv = [jnp.dot(ps[r, h], vaug[h], preferred_element_type=F32) for h in range(2)]
                a0, a1 = alphas[r, 0], alphas[r, 1]
                new[0].append(m_new[r, 0])
                new[1].append(m_new[r, 1])
                new[2].append(jnp.where(h0, a0, a1) * old[2, rows(r), :] + jnp.where(h0, pv[0], pv[1]))
                new[3].append(jnp.where(h0, a1, a0) * old[3, rows(r), :] + jnp.where(h0, pv[1], pv[0]))
            res = jnp.stack([jnp.concatenate(n, axis=0) for n in new], axis=0)
            st[...] = res
            return res

        @pl.when(dd < 0)
        def _():
            step(_chunk_kinds(QC, -1))

        for d in range(QC):
            @pl.when(dd == d)
            def _(d=d):
                res = step(_chunk_kinds(QC, d))
                if d == QC - 1:
                    lr = pltpu.roll(res[3], HEAD_DIM, axis=1)
                    o_ref[...] = res[2] / lr
                    lse_ref[0] = res[0] + jnp.log(jnp.where(head0, lr, res[3]))
                    lse_ref[1] = res[1] + jnp.log(jnp.where(head0, res[3], lr))

    grid_spec = pltpu.PrefetchScalarGridSpec(
        num_scalar_prefetch=2, grid=(HP, it.shape[0]),
        in_specs=[pl.BlockSpec((TQ, LANES), lambda h, s, it, jt: (it[s], h)),
                  pl.BlockSpec((T, LANES), lambda h, s, it, jt: (jt[s], h)),
                  pl.BlockSpec((T, LANES), lambda h, s, it, jt: (jt[s], HP + h)),
                  pl.BlockSpec((None, 8, T), lambda h, s, it, jt: (h, 0, jt[s]))],
        out_specs=[pl.BlockSpec((TQ, LANES), lambda h, s, it, jt: (it[s], h)),
                   pl.BlockSpec((2, TQ, LANES), lambda h, s, it, jt: (h, it[s], 0))],
        scratch_shapes=[pltpu.VMEM((4, TQ, LANES), F32)],
    )
    return _pallas(
        body, name=name, grid_spec=grid_spec,
        out_shape=[_sds((S, D), F32), _sds((2 * HP, S, LANES), F32)],
        compiler_params=_params(("parallel", "arbitrary")),
    )(it, jt, q, kv, kv, ck)


def _attn_bwd(q, kv, o, do, lse, ck, *, tb=512, row_chunks=4, name):
    S, D = q.shape
    HP = D // LANES
    T = min(tb, S)
    QC = row_chunks if S >= row_chunks * T else 1
    TQ = QC * T
    it, jt = _tri_tables(S // TQ, QC, False)

    def body(it_ref, jt_ref, q_ref, k_ref, v_ref, o_ref, do_ref, lse_ref, ck_ref, dq_ref, drs_ref, dk_ref, dv_ref, dcs_ref):
        s_id = pl.program_id(1)
        i, j = it_ref[s_id], jt_ref[s_id]
        dd = j - QC * i
        lane = lax.broadcasted_iota(jnp.int32, (T, LANES), 1)
        head0 = lane < HEAD_DIM
        hms = (head0, jnp.logical_not(head0))

        @pl.when(s_id == 0)
        def _():
            dq_ref[...] = jnp.zeros(dq_ref.shape, F32)
            drs_ref[...] = jnp.zeros(drs_ref.shape, F32)

        @pl.when(dd >= 0)
        def _():
            dk_ref[...] = jnp.zeros(dk_ref.shape, F32)
            dv_ref[...] = jnp.zeros(dv_ref.shape, F32)
            dcs_ref[...] = jnp.zeros(dcs_ref.shape, F32)

        def step(kinds):
            kvv, vv = k_ref[...], v_ref[...]
            one = jnp.ones_like(kvv)
            zero = jnp.zeros_like(kvv)
            rows = lambda r: slice(r * T, (r + 1) * T)
            live = [r for r in range(QC) if kinds[r] != "skip"]
            chains = [(r, h) for r in live for h in range(2)]
            qv = {r: q_ref[rows(r), :] for r in live}
            dob = {r: do_ref[rows(r), :].astype(BF16) for r in live}
            ss = {(r, h): lax.dot_general(jnp.where(hms[h], qv[r], zero), kvv, _DIMS["nt"], preferred_element_type=F32)
                  for r, h in chains}
            dps = {(r, h): lax.dot_general(jnp.where(hms[h], dob[r], zero), vv, _DIMS["nt"], preferred_element_type=F32)
                   for r, h in chains}
            pbs, dsbs = {}, {}
            for r, h in chains:
                s = ss[r, h] - ck_ref[h:h + 1, :]
                if kinds[r] == "diag":
                    row = lax.broadcasted_iota(jnp.int32, (T, T), 0)
                    col = lax.broadcasted_iota(jnp.int32, (T, T), 1)
                    s = jnp.where(row >= col, s, NEG_BIG)
                p = jnp.exp(s - _rep(lse_ref[h, rows(r), :], T))
                prod = dob[r].astype(F32) * o_ref[rows(r), :]
                delta = jnp.sum(jnp.where(hms[h], prod, 0.0), axis=1, keepdims=True)
                pbs[r, h] = p.astype(BF16)
                dsbs[r, h] = (p * (dps[r, h] - delta)).astype(BF16)
            dvs, dks = [None, None], [None, None]
            for r in live:
                dqs = []
                for h in range(2):
                    dqs.append(jnp.dot(dsbs[r, h], jnp.where(hms[h], kvv, one), preferred_element_type=F32))
                    dv = jnp.dot(pbs[r, h].T, dob[r], preferred_element_type=F32)
                    dk = jnp.dot(dsbs[r, h].T, jnp.where(hms[h], qv[r], one), preferred_element_type=F32)
                    dvs[h] = dv if dvs[h] is None else dvs[h] + dv
                    dks[h] = dk if dks[h] is None else dks[h] + dk
                qrows = pl.ds(pl.multiple_of(i * TQ + r * T, T), T)
                dq_ref[qrows, :] += jnp.where(head0, dqs[0], dqs[1])
                drs_ref[qrows, :] += jnp.where(head0, dqs[1], dqs[0])
            dv_ref[...] += jnp.where(head0, dvs[0], dvs[1])
            dk_ref[...] += jnp.where(head0, dks[0], dks[1])
            dcs_ref[...] += jnp.where(head0, dks[1], dks[0])

        @pl.when(dd < 0)
        def _():
            step(_chunk_kinds(QC, -1))

        for d in range(QC):
            @pl.when(dd == d)
            def _(d=d):
                step(_chunk_kinds(QC, d))

    by_q = lambda h, s, it, jt: (it[s], h)
    by_k = lambda h, s, it, jt: (jt[s], h)
    whole = lambda h, s, it, jt: (0, h)
    grid_spec = pltpu.PrefetchScalarGridSpec(
        num_scalar_prefetch=2, grid=(HP, it.shape[0]),
        in_specs=[pl.BlockSpec((TQ, LANES), by_q),
                  pl.BlockSpec((T, LANES), by_k),
                  pl.BlockSpec((T, LANES), lambda h, s, it, jt: (jt[s], HP + h)),
                  pl.BlockSpec((TQ, LANES), by_q),
                  pl.BlockSpec((TQ, LANES), by_q),
                  pl.BlockSpec((2, TQ, LANES), lambda h, s, it, jt: (h, it[s], 0)),
                  pl.BlockSpec((None, 8, T), lambda h, s, it, jt: (h, 0, jt[s]))],
        out_specs=[pl.BlockSpec((S, LANES), whole), pl.BlockSpec((S, LANES), whole),
                   pl.BlockSpec((T, LANES), by_k), pl.BlockSpec((T, LANES), by_k), pl.BlockSpec((T, LANES), by_k)],
        scratch_shapes=[],
    )
    return _pallas(
        body, name=name, grid_spec=grid_spec,
        out_shape=[_sds((S, D), F32)] * 5,
        compiler_params=_params(("parallel", "arbitrary")),
    )(it, jt, q, kv, kv, o, do, lse, ck)


ANY = pl.BlockSpec(memory_space=pl.ANY)


def _coords():
    x, y, c = lax.axis_index("x"), lax.axis_index("y"), lax.axis_index("c")
    return x, y, c


def _remote(src, dst, send_sems, recv_sems, k, to):
    return pltpu.make_async_remote_copy(src_ref=src, dst_ref=dst, send_sem=send_sems.at[k], recv_sem=recv_sems.at[k],
                                        device_id=to, device_id_type=MESH)


def _all_gather_shards(pack, *, name):
    R, C = pack.shape
    assert R % 4 == 0
    H, Q = R // 2, R // 4

    def body(in_ref, out_ref, send_sems, recv_sems):
        x, y, c = _coords()
        me, sib = (x, y, c), (x, y, 1 - c)
        xn, yn = (1 - x, y, c), (x, 1 - y, c)
        s, sx, sy, sd = 2 * x + y, 2 * (1 - x) + y, 2 * x + 1 - y, 2 * (1 - x) + 1 - y
        half = pl.ds(c * H, H)
        other = pl.ds((1 - c) * H, H)
        q0 = pl.ds(c * H, Q)
        q1 = pl.ds(c * H + Q, Q)
        rc = functools.partial(_remote, send_sems=send_sems, recv_sems=recv_sems)

        sends = [rc(in_ref.at[half], out_ref.at[s, half], k=0, to=xn),
                 rc(in_ref.at[half], out_ref.at[s, half], k=1, to=yn),
                 rc(in_ref, out_ref.at[s], k=7, to=sib)]
        for cp in sends:
            cp.start()
        rc(in_ref.at[half], out_ref.at[sx, half], k=0, to=me).wait_recv()
        sends.append(rc(out_ref.at[sx, q0], out_ref.at[sx, q0], k=2, to=yn))
        sends[-1].start()
        sends.append(rc(out_ref.at[sx, half], out_ref.at[sx, half], k=4, to=sib))
        sends[-1].start()
        rc(in_ref.at[half], out_ref.at[sy, half], k=1, to=me).wait_recv()
        sends.append(rc(out_ref.at[sy, q1], out_ref.at[sy, q1], k=3, to=xn))
        sends[-1].start()
        sends.append(rc(out_ref.at[sy, half], out_ref.at[sy, half], k=5, to=sib))
        sends[-1].start()
        rc(out_ref.at[sd, q0], out_ref.at[sd, q0], k=2, to=me).wait_recv()
        rc(out_ref.at[sd, q1], out_ref.at[sd, q1], k=3, to=me).wait_recv()
        sends.append(rc(out_ref.at[sd, half], out_ref.at[sd, half], k=6, to=sib))
        sends[-1].start()
        for k, sh in ((4, sx), (5, sy), (6, sd)):
            rc(out_ref.at[sh, other], out_ref.at[sh, other], k=k, to=me).wait_recv()
        rc(in_ref, out_ref.at[s], k=7, to=me).wait_recv()
        for cp in sends:
            cp.wait_send()

    return _pallas(
        body, name=name, in_specs=[ANY], out_specs=ANY,
        out_shape=_sds((N_CHIPS, R, C), pack.dtype),
        scratch_shapes=[pltpu.SemaphoreType.DMA((8,)), pltpu.SemaphoreType.DMA((8,))],
    )(pack)


def _rs_pair(g, *, name):
    n, R, C = g.shape
    H = R // 2

    def body(g_ref, land_ref, send_sems, recv_sems):
        x, y, c = _coords()
        other = pl.ds((1 - c) * H, H)
        cps = [_remote(g_ref.at[sh, other], land_ref.at[sh], send_sems, recv_sems, sh, (x, y, 1 - c)) for sh in range(n)]
        for cp in cps:
            cp.start()
        for cp in cps:
            cp.wait_recv()
        for cp in cps:
            cp.wait_send()

    return _pallas(
        body, name=name, in_specs=[ANY], out_specs=ANY, out_shape=_sds((n, H, C), g.dtype),
        scratch_shapes=[pltpu.SemaphoreType.DMA((n,)), pltpu.SemaphoreType.DMA((n,))],
    )(g)


def _rs_quarters(p, *, name):
    n, H, C = p.shape
    Q = H // 2

    def body(p_ref, la_ref, lb_ref, send_sems, recv_sems):
        x, y, c = _coords()
        sd = 2 * (1 - x) + 1 - y
        a = _remote(p_ref.at[sd, pl.ds(0, Q)], la_ref, send_sems, recv_sems, 0, (x, 1 - y, c))
        b = _remote(p_ref.at[sd, pl.ds(Q, Q)], lb_ref, send_sems, recv_sems, 1, (1 - x, y, c))
        a.start()
        b.start()
        a.wait_recv()
        b.wait_recv()
        a.wait_send()
        b.wait_send()

    return _pallas(
        body, name=name, in_specs=[ANY], out_specs=[ANY, ANY],
        out_shape=[_sds((Q, C), p.dtype), _sds((Q, C), p.dtype)],
        scratch_shapes=[pltpu.SemaphoreType.DMA((2,)), pltpu.SemaphoreType.DMA((2,))],
    )(p)


def _rs_halves(p, ax, ay, *, name):
    n, H, C = p.shape
    Q = H // 2

    def body(p_ref, ax_ref, ay_ref, la_ref, lb_ref, send_sems, recv_sems):
        x, y, c = _coords()
        sx, sy = 2 * (1 - x) + y, 2 * x + 1 - y
        xn, yn = (1 - x, y, c), (x, 1 - y, c)
        lo, hi = pl.ds(0, Q), pl.ds(Q, Q)
        cps = [_remote(ax_ref, la_ref.at[lo], send_sems, recv_sems, 0, xn),
               _remote(p_ref.at[sx, hi], la_ref.at[hi], send_sems, recv_sems, 1, xn),
               _remote(p_ref.at[sy, lo], lb_ref.at[lo], send_sems, recv_sems, 2, yn),
               _remote(ay_ref, lb_ref.at[hi], send_sems, recv_sems, 3, yn)]
        for cp in cps:
            cp.start()
        for cp in cps:
            cp.wait_recv()
        for cp in cps:
            cp.wait_send()

    return _pallas(
        body, name=name, in_specs=[ANY, ANY, ANY], out_specs=[ANY, ANY],
        out_shape=[_sds((H, C), p.dtype), _sds((H, C), p.dtype)],
        scratch_shapes=[pltpu.SemaphoreType.DMA((4,)), pltpu.SemaphoreType.DMA((4,))],
    )(p, ax, ay)


def _rs_join(buf, *, name):
    R, C = buf.shape
    H = R // 2

    def body(in_ref, out_ref, send_sems, recv_sems):
        x, y, c = _coords()
        half = pl.ds(c * H, H)
        other = pl.ds((1 - c) * H, H)
        cp = _remote(in_ref.at[half], out_ref.at[half], send_sems, recv_sems, 0, (x, y, 1 - c))
        cp.start()
        _remote(in_ref.at[other], out_ref.at[other], send_sems, recv_sems, 0, (x, y, c)).wait_recv()
        cp.wait_send()

    return _pallas(
        body, name=name, in_specs=[ANY], out_specs=ANY, out_shape=_sds((R, C), buf.dtype),
        input_output_aliases={0: 0},
        scratch_shapes=[pltpu.SemaphoreType.DMA((1,)), pltpu.SemaphoreType.DMA((1,))],
    )(buf)


def _tile_add(ins_specs, arrays, n_steps, out_spec, out_shape, scalars, *, name):
    grid_spec = pltpu.PrefetchScalarGridSpec(
        num_scalar_prefetch=1, grid=(n_steps,), in_specs=ins_specs, out_specs=out_spec, scratch_shapes=[])

    def body(sc_ref, *refs):
        acc = refs[0][...].astype(F32)
        for r in refs[1:-1]:
            acc = acc + r[...].astype(F32)
        refs[-1][...] = acc.astype(refs[-1].dtype)

    return _pallas(body, name=name, grid_spec=grid_spec, out_shape=out_shape,
                   compiler_params=_params(("arbitrary",)))(scalars, *arrays)


def _reduce_scatter(g, *, name):
    n, R, C = g.shape
    H, Q = R // 2, R // 4
    tm = RS_ROW_MULT // 4
    assert Q % tm == 0, (R, tm)
    x, y, c = _coords()
    sx, sy, s = 2 * (1 - x) + y, 2 * x + 1 - y, 2 * x + y
    sc = jnp.stack([c, sx, sy, s]).astype(jnp.int32)
    hb, qb = H // tm, Q // tm
    blk = lambda f: pl.BlockSpec((None, tm, C), f)
    flat = lambda f: pl.BlockSpec((tm, C), f)

    land = _rs_pair(g, name=name + "_pair")
    p = _tile_add([blk(lambda i, sc: (i // hb, sc[0] * hb + i % hb, 0)), blk(lambda i, sc: (i // hb, i % hb, 0))],
                  [g, land], n * hb, blk(lambda i, sc: (i // hb, i % hb, 0)), _sds((n, H, C), g.dtype), sc, name=name + "_add0")
    la, lb = _rs_quarters(p, name=name + "_quarters")
    ax = _tile_add([blk(lambda i, sc: (sc[1], i, 0)), flat(lambda i, sc: (i, 0))], [p, la], qb,
                   flat(lambda i, sc: (i, 0)), _sds((Q, C), g.dtype), sc, name=name + "_add1x")
    ay = _tile_add([blk(lambda i, sc: (sc[2], qb + i, 0)), flat(lambda i, sc: (i, 0))], [p, lb], qb,
                   flat(lambda i, sc: (i, 0)), _sds((Q, C), g.dtype), sc, name=name + "_add1y")
    fa, fb = _rs_halves(p, ax, ay, name=name + "_halves")
    buf = _tile_add([blk(lambda i, sc: (sc[3], i, 0)), flat(lambda i, sc: (i, 0)), flat(lambda i, sc: (i, 0))],
                    [p, fa, fb], hb, flat(lambda i, sc: (sc[0] * hb + i, 0)), _sds((R, C), F32), sc, name=name + "_add2")
    return _rs_join(buf, name=name + "_join")


def _all_reduce_small(v, *, name):
    M, N = v.shape

    def body(x_ref, out_ref, send_sems, recv_sems, local_sem):
        x, y, c = _coords()
        me, sibling = (x, y, c), (x, y, 1 - c)
        chips = [(1 - x, y), (x, 1 - y), (1 - x, 1 - y)]

        def rows(px, py, pc):
            return out_ref.at[pl.ds((4 * px + 2 * py + pc) * M, M), :]

        def copy(k, block, to, src=None):
            return pltpu.make_async_remote_copy(
                src_ref=rows(*block) if src is None else src, dst_ref=rows(*block),
                send_sem=send_sems.at[k], recv_sem=recv_sems.at[k], device_id=to, device_id_type=MESH)

        mine = pltpu.make_async_copy(x_ref, rows(*me), local_sem)
        mine.start()
        first = [copy(0, me, sibling, src=x_ref)]
        first += [copy(1 + j, me, (*chip, c), src=x_ref) for j, chip in enumerate(chips)]
        for cp in first:
            cp.start()
        passed = [copy(4 + j, (*chip, c), sibling) for j, chip in enumerate(chips)]
        for j, chip in enumerate(chips):
            copy(1 + j, (*chip, c), me).wait_recv()
            passed[j].start()
        copy(0, sibling, me).wait_recv()
        for j, chip in enumerate(chips):
            copy(4 + j, (*chip, 1 - c), me).wait_recv()
        for cp in first + passed:
            cp.wait_send()
        mine.wait()

    gathered = _pallas(
        body, name=name + "_gather",
        out_shape=_sds((N_DEV * M, N), F32),
        in_specs=[pl.BlockSpec(memory_space=pltpu.VMEM)],
        out_specs=pl.BlockSpec(memory_space=pltpu.VMEM),
        scratch_shapes=[pltpu.SemaphoreType.DMA((7,)), pltpu.SemaphoreType.DMA((7,)), pltpu.SemaphoreType.DMA],
    )(v)

    def sum_body(g_ref, o_ref):
        acc = g_ref[0:M, :]
        for d in range(1, N_DEV):
            acc = acc + g_ref[d * M:(d + 1) * M, :]
        o_ref[...] = acc

    return _pallas(sum_body, name=name + "_sum", out_shape=_sds((M, N), F32))(gathered)


MATS = [("conv_w_pw1", 2), ("conv_w_pw2", 1), ("attn_w_q", 1), ("attn_w_o", 1), ("ffn_w1", 2), ("ffn_w2", 1),
        ("ple_w_gate", 1), ("ple_w_proj", 2), ("w_kvf", 1)]
VECS = [("conv_b_pw1", 1), ("conv_w_dw", 2), ("conv_b_dw", 1), ("conv_ln_g", 1), ("conv_ln_b", 1), ("conv_b_pw2", 1)]
REPL = ["mix_norm", "ffn_norm", "ple_norm", "kv_norm", "final_norm", "b_f"]
WEIGHTS = ["mix_norm", "conv_w_pw1", "conv_b_pw1", "conv_w_dw", "conv_b_dw", "conv_ln_g", "conv_ln_b", "conv_w_pw2",
           "conv_b_pw2", "kv_norm", "w_kvf", "b_f", "attn_w_q", "attn_w_o", "ffn_norm", "ffn_w1", "ffn_w2", "ple_norm",
           "ple_w_gate", "ple_w_proj", "final_norm"]


def _round_up(n, m):
    return -(-n // m) * m


def _to_rows(t, C, mult):
    flat = t.reshape(-1)
    rows = _round_up(_round_up(flat.shape[0], C) // C, mult)
    flat = jnp.pad(flat, (0, rows * C - flat.shape[0]))
    return flat.reshape(rows, C)


def _pack(tensors, C, mult, total_mult):
    parts = [_to_rows(t, C, mult) for t in tensors]
    rows = sum(p.shape[0] for p in parts)
    pad = _round_up(rows, total_mult) - rows
    if pad:
        parts.append(jnp.zeros((pad, C), parts[0].dtype))
    return jnp.concatenate(parts, axis=0)


def _row_counts(shapes, C, mult):
    return [_round_up(_round_up(int(np.prod(s)), C) // C, mult) for s in shapes]


def _unpack(packed, shapes, C, mult):
    outs, r0 = [], 0
    lead = packed.shape[:-2]
    for shp, nr in zip(shapes, _row_counts(shapes, C, mult)):
        n = int(np.prod(shp))
        seg = packed[..., r0:r0 + nr, :].reshape(lead + (nr * C,))[..., :n]
        outs.append(seg.reshape(lead + tuple(shp)))
        r0 += nr
    return outs


def _unshard(t, axis):
    return jnp.concatenate([t[s] for s in range(N_CHIPS)], axis=axis)


def _shards(t, axis):
    return jnp.split(t, N_CHIPS, axis=axis)


def _local_step(x, p, tgt, w):
    S, D = x.shape
    L = p.shape[0]
    NA = w["conv_w_pw1"].shape[0]
    H = w["b_f"].shape[0]
    HP = D // LANES
    row = lambda v: v.reshape(1, -1)
    act = lambda dt, n=D: _sds((S, n), dt)
    g = {}

    def rms(hh, gain, name):
        return _rows(_rms_fwd, [hh], [row(gain)], [act(BF16)], name=name)[0]

    saved = []
    h = x
    kv = ck = fl = nkv = h_kv = None
    bfp = jnp.pad(w["b_f"], (0, LANES - H)).reshape(1, LANES)
    wk = w["w_kvf"][:, :D]
    wv = w["w_kvf"][:, D:2 * D]
    wkv = w["w_kvf"][:, :2 * D]
    wf = jnp.pad(w["w_kvf"][:, 2 * D:], ((0, 0), (0, LANES - H)))
    res_rms = lambda acc, r, gn: _with_rms(acc + r, gn)
    hn = rms(h, w["mix_norm"][0], "rms_mix")
    for i in range(L):
        sv = {"h0": h, "hn": hn}
        g_ffn, g_ple = row(w["ffn_norm"][i]), row(w["ple_norm"][i])
        if i < NA:
            u = _mm(hn, w["conv_w_pw1"][i], extras=[row(w["conv_b_pw1"][i])], epi=lambda acc, b: acc + b, name="mm_pw1")
            glu = _rows(_glu_fwd, [u], [], [act(F32)], name="glu_fwd")[0]
            wdw = jnp.pad(w["conv_w_dw"][i], ((0, CONV_PAD - CONV_WIDTH), (0, 0)))
            cv = _dwconv_fwd(glu, wdw, row(w["conv_b_dw"][i]), name="dwconv_fwd")
            sw = _rows(_ln_silu_fwd, [cv], [row(w["conv_ln_g"][i]), row(w["conv_ln_b"][i])], [act(BF16)], name="ln_silu_fwd")[0]
            h1, hn2 = _mm(sw, w["conv_w_pw2"][i], extras=[row(w["conv_b_pw2"][i]), h, g_ffn],
                          epi=lambda acc, b, r, gn: _with_rms(acc + b + r, gn), out_dtype=[F32, BF16], tm=512, tn=D, name="mm_pw2")
            sv.update(u=u, glu=glu, cv=cv, sw=sw, wdw=wdw)
        else:
            if i == NA:
                h_kv = h
                nkv = rms(h, w["kv_norm"], "rms_kv")
                kv = _mm(nkv, wkv, out_dtype=BF16, name="mm_kv")
                fl = _mm(nkv, wf, name="mm_f")
                c = _gate_cumsum(fl, bfp, name="gate_cumsum")
                ck = jnp.pad(c[:, :H].T.reshape(HP, 2, S), ((0, 0), (0, 6), (0, 0)))
            j = i - NA
            q = _mm(hn, w["attn_w_q"][j], epi=lambda acc: acc * (HEAD_DIM ** -0.5), out_dtype=BF16, name="mm_q")
            o, lse = _attn_fwd(q, kv, ck, name="attn_fwd")
            h1, hn2 = _mm(o, w["attn_w_o"][j], extras=[h, g_ffn], epi=res_rms, out_dtype=[F32, BF16], tm=512, tn=D, name="mm_o")
            sv.update(q=q, o=o, lse=lse)
        zb, f = _mm(hn2, w["ffn_w1"][i], epi=lambda acc: (acc, jnp.square(jnp.maximum(acc, 0.0))), out_dtype=[BF16, BF16],
                    name="mm_ffn1")
        h2, n3 = _mm(f, w["ffn_w2"][i], extras=[h1, g_ple], epi=res_rms, out_dtype=[F32, BF16], tm=512, tn=D, tk=4 * D, name="mm_ffn2")
        zg = _mm(n3, w["ple_w_gate"][i], name="mm_gate")
        ple = lambda acc, r, zz: r + _sigmoid(zz) * acc
        if i + 1 < L:
            h, hn = _mm(p[i], w["ple_w_proj"][i], extras=[h2, zg, row(w["mix_norm"][i + 1])],
                        epi=lambda acc, r, zz, gn: _with_rms(ple(acc, r, zz), gn), out_dtype=[F32, BF16], tm=512, tn=D, name="mm_proj")
        else:
            h = _mm(p[i], w["ple_w_proj"][i], extras=[h2, zg], epi=ple, tm=512, tn=D, name="mm_proj_last")
        sv.update(h1=h1, hn2=hn2, zb=zb, f=f, h2=h2, n3=n3, zg=zg)
        saved.append(sv)

    dh, err2, g_final = _rows(_final_fn, [h, tgt], [row(w["final_norm"])], [act(F32)], [_sds((1, D), F32), _sds((1, D), F32)],
                              name="final")
    loss = 0.5 * jnp.sum(err2) / D
    g["final_norm"] = g_final.reshape(-1)

    red = _sds((1, D), F32)
    stack = {k: [None] * n for k, n in (("mix_norm", L), ("ffn_norm", L), ("ple_norm", L), ("ffn_w1", L), ("ffn_w2", L),
                                        ("ple_w_gate", L), ("ple_w_proj", L), ("conv_w_pw1", NA), ("conv_b_pw1", NA),
                                        ("conv_w_dw", NA), ("conv_b_dw", NA), ("conv_ln_g", NA), ("conv_ln_b", NA),
                                        ("conv_w_pw2", NA), ("conv_b_pw2", NA), ("attn_w_q", L - NA), ("attn_w_o", L - NA))}
    dk_sum = dv_sum = None
    dcks = []
    for i in reversed(range(L)):
        sv = saved[i]
        dzg, dpp = _mm(p[i], w["ple_w_proj"][i], extras=[dh, sv["zg"]], epi=lambda acc, d, zz: _ple_bwd(d, zz, acc),
                       out_dtype=[BF16, BF16], name="mm_ple_bwd")
        stack["ple_w_proj"][i] = _mm(p[i], dpp, mode="tn", tk=2048, out_dtype=BF16, name="mm_dproj")
        stack["ple_w_gate"][i] = _mm(sv["n3"], dzg, mode="tn", tk=2048, out_dtype=BF16, name="mm_dgate")
        dh, dhb, dgain = _mm(dzg, w["ple_w_gate"][i], mode="nt", extras=[sv["h2"], dh, row(w["ple_norm"][i])], epi=_dup(_rms_bwd),
                             out_dtype=[F32, BF16], reds=1, tm=512, tn=D, name="mm_dn3")
        stack["ple_norm"][i] = dgain.reshape(-1)
        dz = _mm(dhb, w["ffn_w2"][i], mode="nt", extras=[sv["zb"]], epi=lambda acc, zz: acc * (2.0 * jnp.maximum(zz, 0.0).astype(F32)),
                 out_dtype=BF16, name="mm_dz")
        stack["ffn_w2"][i] = _mm(sv["f"], dhb, mode="tn", tk=2048, out_dtype=BF16, name="mm_dffn2")
        stack["ffn_w1"][i] = _mm(sv["hn2"], dz, mode="tn", tk=2048, out_dtype=BF16, name="mm_dffn1")
        if i < NA:
            dh, dhb, dgain, dbias = _mm(dz, w["ffn_w1"][i], mode="nt", extras=[sv["h1"], dh, row(w["ffn_norm"][i])],
                                        epi=_dup(_rms_bwd_bias), out_dtype=[F32, BF16], reds=2, tm=512, tn=D, tk=4 * D, name="mm_dhn2_bias")
            stack["conv_b_pw2"][i] = dbias.reshape(-1)
        else:
            dh, dhb, dgain = _mm(dz, w["ffn_w1"][i], mode="nt", extras=[sv["h1"], dh, row(w["ffn_norm"][i])], epi=_dup(_rms_bwd),
                                 out_dtype=[F32, BF16], reds=1, tm=512, tn=D, tk=4 * D, name="mm_dhn2")
        stack["ffn_norm"][i] = dgain.reshape(-1)
        if i < NA:
            dsw = _mm(dhb, w["conv_w_pw2"][i], mode="nt", name="mm_dsw")
            stack["conv_w_pw2"][i] = _mm(sv["sw"], dhb, mode="tn", tk=2048, out_dtype=BF16, name="mm_dpw2")
            dcv, dlg, dlb = _rows(_ln_silu_bwd, [dsw, sv["cv"]], [row(w["conv_ln_g"][i]), row(w["conv_ln_b"][i])], [act(F32)],
                                  [red, red], name="ln_silu_bwd")
            stack["conv_ln_g"][i], stack["conv_ln_b"][i] = dlg.reshape(-1), dlb.reshape(-1)
            dglu, dwdw, dbdw = _dwconv_bwd(dcv, sv["glu"], sv["wdw"], name="dwconv_bwd")
            stack["conv_w_dw"][i], stack["conv_b_dw"][i] = dwdw[:CONV_WIDTH], dbdw.reshape(-1)
            du, dbu = _rows(_glu_bwd, [dglu, sv["u"]], [], [act(BF16, 2 * D)], [_sds((1, 2 * D), F32)], name="glu_bwd")
            stack["conv_b_pw1"][i] = dbu.reshape(-1)
            stack["conv_w_pw1"][i] = _mm(sv["hn"], du, mode="tn", tk=2048, out_dtype=BF16, name="mm_dpw1")
            dh, dgain = _mm(du, w["conv_w_pw1"][i], mode="nt", extras=[sv["h0"], dh, row(w["mix_norm"][i])], epi=_rms_bwd,
                            reds=1, tm=512, tn=D, name="mm_dhn_a")
        else:
            j = i - NA
            do = _mm(dhb, w["attn_w_o"][j], mode="nt", out_dtype=BF16, name="mm_do")
            stack["attn_w_o"][j] = _mm(sv["o"], dhb, mode="tn", tk=2048, out_dtype=BF16, name="mm_dwo")
            dq, drs, dk, dv, dcs = _attn_bwd(sv["q"], kv, sv["o"], do, sv["lse"], ck, name="attn_bwd")
            scale = lambda acc: acc * (HEAD_DIM ** -0.5)
            stack["attn_w_q"][j] = _mm(sv["hn"], dq, mode="tn", tk=2048, out_dtype=BF16, epi=scale, name="mm_dwq")
            dh_in = dh
            dh, dgain = _mm(dq, w["attn_w_q"][j], mode="nt", extras=[sv["h0"], dh_in, row(w["mix_norm"][i])],
                            epi=lambda acc, xx, dr, gn: _rms_bwd(scale(acc), xx, dr, gn), reds=1, tm=512, tn=D, name="mm_dhn_b")
            pick = lambda t: jnp.pad(t.reshape(S, HP, 2, HEAD_DIM)[:, :, ::-1, 0].reshape(S, H), ((0, 0), (0, LANES - H)))
            dcks += [pick(drs), pick(dcs)]
            if dk_sum is None:
                dk_sum, dv_sum = dk, dv
            else:
                dk_sum = _rows(_add2, [dk_sum, dk], [], [act(F32)], name="add_dk")[0]
                dv_sum = _rows(_add2, [dv_sum, dv], [], [act(F32)], name="add_dk")[0]
        stack["mix_norm"][i] = dgain.reshape(-1)
        if i == NA:
            dfl, dbf = _gate_cumsum_bwd(dcks, fl, bfp, name="gate_cumsum_bwd")
            g["b_f"] = dbf[0, :H]
            gk = _mm(nkv, dk_sum, mode="tn", tk=2048, out_dtype=BF16, name="mm_dwk")
            gv = _mm(nkv, dv_sum, mode="tn", tk=2048, out_dtype=BF16, name="mm_dwk")
            gf = _mm(nkv, dfl, mode="tn", tk=2048, out_dtype=BF16, name="mm_dwf")
            g["w_kvf"] = jnp.concatenate([gk, gv, gf[:, :H]], axis=1)
            dn = _mm(dk_sum, wk, mode="nt", name="mm_dnk")
            dn = _mm(dv_sum, wv, mode="nt", extras=[dn], epi=lambda acc, r: acc + r, name="mm_dnv")
            dh, dgain = _mm(dfl, wf, mode="nt", extras=[dn, h_kv, dh, row(w["kv_norm"])],
                            epi=lambda acc, r, xx, dr, gn: _rms_bwd(acc + r, xx, dr, gn), reds=1, tm=512, tn=D, name="mm_dnf")
            g["kv_norm"] = dgain.reshape(-1)
    for k, v in stack.items():
        g[k] = jnp.stack(v, axis=0)
    return loss, dh, g


def kernel(x, p, mix_norm, conv_w_pw1, conv_b_pw1, conv_w_dw, conv_b_dw, conv_ln_g, conv_ln_b, conv_w_pw2, conv_b_pw2, kv_norm, w_kvf, b_f, attn_w_q, attn_w_o, ffn_norm, ffn_w1, ffn_w2, ple_norm, ple_w_gate, ple_w_proj, final_norm, loss_target, m_mix_norm, m_conv_w_pw1, m_conv_b_pw1, m_conv_w_dw, m_conv_b_dw, m_conv_ln_g, m_conv_ln_b, m_conv_w_pw2, m_conv_b_pw2, m_kv_norm, m_w_kvf, m_b_f, m_attn_w_q, m_attn_w_o, m_ffn_norm, m_ffn_w1, m_ffn_w2, m_ple_norm, m_ple_w_gate, m_ple_w_proj, m_final_norm, v_mix_norm, v_conv_w_pw1, v_conv_b_pw1, v_conv_w_dw, v_conv_b_dw, v_conv_ln_g, v_conv_ln_b, v_conv_w_pw2, v_conv_b_pw2, v_kv_norm, v_w_kvf, v_b_f, v_attn_w_q, v_attn_w_o, v_ffn_norm, v_ffn_w1, v_ffn_w2, v_ple_norm, v_ple_w_gate, v_ple_w_proj, v_final_norm):
    args = dict(locals())
    wl = {n: args[n] for n in WEIGHTS}
    ml = {n: args["m_" + n] for n in WEIGHTS}
    vl = {n: args["v_" + n] for n in WEIGHTS}
    S, D = x.shape[1], x.shape[2]
    C = D

    mat_shapes = [wl[n].shape for n, _ in MATS]
    vec_shapes = [wl[n].shape for n, _ in VECS]
    mats = _all_gather_shards(_pack([wl[n].astype(BF16) for n, _ in MATS], C, 16, 64), name="ag_mats")
    vecs = _all_gather_shards(_pack([wl[n] for n, _ in VECS], C, 1, 32), name="ag_vecs")
    full = {n: wl[n] for n in REPL}
    for (n, ax), t in zip(MATS, _unpack(mats, mat_shapes, C, 16)):
        full[n] = _unshard(t, ax)
    for (n, ax), t in zip(VECS, _unpack(vecs, vec_shapes, C, 1)):
        full[n] = _unshard(t, ax)

    loss, dx, g = _local_step(x[0], p[:, 0], loss_target[0], full)
    loss = lax.psum(loss, ("x", "y", "c"))

    names = [n for n, _ in MATS] + [n for n, _ in VECS]
    axes = dict(MATS + VECS)
    shard_shapes = [wl[n].shape for n in names]
    per_shard = [[] for _ in range(N_CHIPS)]
    for n in names:
        for s, piece in enumerate(_shards(g[n].astype(BF16), axes[n])):
            per_shard[s].append(piece)
    gpack = jnp.stack([_pack(ts, C, 16, RS_ROW_MULT) for ts in per_shard], axis=0)
    gred = _reduce_scatter(gpack, name="rs")
    gl = dict(zip(names, _unpack(gred, shard_shapes, C, 16)))

    rep_shapes = [wl[n].shape for n in REPL]
    rpack = _pack([g[n] for n in REPL], C, 1, 8)
    for n, t in zip(REPL, _unpack(_all_reduce_small(rpack, name="ar"), rep_shapes, C, 1)):
        gl[n] = t

    grads, deltas, new_m, new_v = {}, {}, {}, {}
    rep_w, rep_m, rep_v = (_pack([d[n] for n in REPL], C, 1, 8) for d in (wl, ml, vl))
    rep_out = _adamw(rep_w, rpack_like(gl, rep_shapes, C), rep_m, rep_v, "adamw_rep")
    for dst, packed in zip((grads, deltas, new_m, new_v), rep_out):
        for n, t in zip(REPL, _unpack(packed, rep_shapes, C, 1)):
            dst[n] = t
    for n in names:
        res = _adamw(wl[n], gl[n], ml[n], vl[n], "adamw_" + n)
        for dst, t in zip((grads, deltas, new_m, new_v), res):
            dst[n] = t
    out = [loss, dx[None]]
    for d in (grads, deltas, new_m, new_v):
        out += [d[n] for n in WEIGHTS]
    return tuple(out)


def rpack_like(gl, rep_shapes, C):
    return _pack([gl[n] for n in REPL], C, 1, 8)
```

```python
import functools

import jax
import jax.numpy as jnp
import numpy as np
from jax import lax
from jax.experimental import pallas as pl
from jax.experimental.pallas import tpu as pltpu

F32 = jnp.float32
BF16 = jnp.bfloat16
MESH = pl.DeviceIdType.MESH

N_CHIPS = 4
N_DEV = 8
HEAD_DIM = 64
LANES = 128
CONV_WIDTH = 31
CONV_PAD = 32
EPS = 1e-6
NEG_BIG = -1e30
VMEM_LIMIT = 56 * 1024 * 1024

RS_ROW_MULT = 1024

ADAM_LR, ADAM_B1, ADAM_B2, ADAM_EPS, ADAM_WD, ADAM_STEP = 0.001, 0.9, 0.999, 1e-08, 0.01, 10


def _pallas(body, **kw):
    return pl.pallas_call(body, **kw)


def _params(sem=None):
    return pltpu.CompilerParams(dimension_semantics=sem, vmem_limit_bytes=VMEM_LIMIT)


def _sds(shape, dtype):
    return jax.ShapeDtypeStruct(tuple(shape), dtype)


_DIMS = {"nn": (((1,), (0,)), ((), ())), "nt": (((1,), (1,)), ((), ())), "tn": (((0,), (0,)), ((), ()))}


def _mm(a, b, *, mode="nn", extras=(), epi=None, out_dtype=F32, reds=0, tm=1024, tn=1024, tk=1024, into=None, out_map=None,
        name):
    if mode == "nn":
        (M, K), (K2, N) = a.shape, b.shape
    elif mode == "nt":
        (M, K), (N, K2) = a.shape, b.shape
    else:
        (K, M), (K2, N) = a.shape, b.shape
    assert K == K2, (name, a.shape, b.shape)
    tm, tn, tk = min(tm, M), min(tn, N), min(tk, K)
    assert M % tm == 0 and N % tn == 0 and K % tk == 0, (name, a.shape, b.shape)
    nk = K // tk
    if mode == "tn":
        a_spec = pl.BlockSpec((tk, tm), lambda i, j, k: (k, i))
    else:
        a_spec = pl.BlockSpec((tm, tk), lambda i, j, k: (i, k))
    if mode == "nt":
        b_spec = pl.BlockSpec((tn, tk), lambda i, j, k: (j, k))
    else:
        b_spec = pl.BlockSpec((tk, tn), lambda i, j, k: (k, j))
    ex_specs = []
    for e in extras:
        if e.shape[0] == 1:
            ex_specs.append(pl.BlockSpec((1, tn), lambda i, j, k: (0, j)))
        else:
            assert e.shape == (M, N), (name, e.shape)
            ex_specs.append(pl.BlockSpec((tm, tn), lambda i, j, k: (i, j)))
    ne = len(extras)
    dims = _DIMS[mode]
    many = isinstance(out_dtype, (list, tuple))
    out_dtypes = list(out_dtype) if many else [out_dtype]
    no = len(out_dtypes)
    assert not reds or tn == N, name

    def body(a_ref, b_ref, *rest):
        ex_refs, o_refs, r_refs = rest[:ne], rest[ne:ne + no], rest[ne + no:ne + no + reds]
        part = lax.dot_general(a_ref[...].astype(BF16), b_ref[...].astype(BF16), dims, preferred_element_type=F32)
        i = pl.program_id(0)

        def finish(acc):
            res = epi(acc, *[r[...] for r in ex_refs]) if epi is not None else acc
            if not isinstance(res, (tuple, list)):
                res = (res,)
            assert len(res) == no + reds, (name, len(res))
            for r, v in zip(o_refs, res[:no]):
                r[...] = v.astype(r.dtype)
            for r, v in zip(r_refs, res[no:]):
                @pl.when(i == 0)
                def _(r=r, v=v):
                    r[...] = v

                @pl.when(i > 0)
                def _(r=r, v=v):
                    r[...] += v

        if nk == 1:
            finish(part)
        else:
            acc_ref = rest[ne + no + reds]
            k = pl.program_id(2)

            @pl.when(k == 0)
            def _():
                acc_ref[...] = part

            @pl.when(k > 0)
            def _():
                acc_ref[...] += part

            @pl.when(k == nk - 1)
            def _():
                finish(acc_ref[...])

    scratch = [pltpu.VMEM((tm, tn), F32)] if nk > 1 else []
    if into is not None:
        assert not many and not reds and not extras and into.dtype == out_dtype, name

        def body_into(a_ref, b_ref, into_ref, *rest):
            body(a_ref, b_ref, *rest)

        return _pallas(
            body_into, name=name, grid=(M // tm, N // tn, nk),
            in_specs=[a_spec, b_spec, pl.BlockSpec(memory_space=pl.ANY)],
            out_specs=pl.BlockSpec((tm, tn), out_map),
            out_shape=_sds(into.shape, into.dtype),
            input_output_aliases={2: 0},
            scratch_shapes=scratch,
            compiler_params=_params(("parallel", "parallel", "arbitrary")),
        )(a, b, into)
    res = _pallas(
        body, name=name, grid=(M // tm, N // tn, nk),
        in_specs=[a_spec, b_spec] + ex_specs,
        out_specs=[pl.BlockSpec((tm, tn), lambda i, j, k: (i, j))] * no + [pl.BlockSpec((1, tn), lambda i, j, k: (0, j))] * reds,
        out_shape=[_sds((M, N), dt) for dt in out_dtypes] + [_sds((1, N), F32)] * reds,
        scratch_shapes=scratch,
        compiler_params=_params(("arbitrary",) * 3 if reds else ("parallel", "parallel", "arbitrary")),
    )(a, b, *extras)
    return res if (many or reds) else res[0]


def _rows(fn, ins, params, outs, reds=(), *, tm=256, name):
    S = ins[0].shape[0]
    tm = min(tm, S)
    assert S % tm == 0, (name, S, tm)
    ni, npar, no, nr = len(ins), len(params), len(outs), len(reds)

    def body(*refs):
        in_refs, p_refs = refs[:ni], refs[ni:ni + npar]
        o_refs, r_refs = refs[ni + npar:ni + npar + no], refs[ni + npar + no:]
        res = fn(*[r[...] for r in in_refs], *[r[...] for r in p_refs])
        if not isinstance(res, (tuple, list)):
            res = (res,)
        assert len(res) == no + nr, (name, len(res))
        for r, v in zip(o_refs, res[:no]):
            r[...] = v.astype(r.dtype)
        i = pl.program_id(0)
        for r, v in zip(r_refs, res[no:]):
            @pl.when(i == 0)
            def _(r=r, v=v):
                r[...] = v

            @pl.when(i > 0)
            def _(r=r, v=v):
                r[...] += v

    res = _pallas(
        body, name=name, grid=(S // tm,),
        in_specs=[pl.BlockSpec((tm, a.shape[1]), lambda i: (i, 0)) for a in ins]
        + [pl.BlockSpec(p.shape, lambda i: (0, 0)) for p in params],
        out_specs=[pl.BlockSpec((tm, o.shape[1]), lambda i: (i, 0)) for o in outs]
        + [pl.BlockSpec(r.shape, lambda i: (0, 0)) for r in reds],
        out_shape=list(outs) + list(reds),
        compiler_params=_params(("arbitrary",)),
    )(*ins, *params)
    return res


def _colsum(v):
    return jnp.sum(v, axis=0, keepdims=True)


def _sigmoid(v):
    return 1.0 / (1.0 + jnp.exp(-v))


def _rms_stats(x):
    r = lax.rsqrt(jnp.mean(x * x, axis=-1, keepdims=True) + EPS)
    return x * r, r


def _rms_fwd(x, g):
    xh, _ = _rms_stats(x)
    return (xh * g,)


def _with_rms(h, g):
    xh, _ = _rms_stats(h)
    return h, xh * g


def _rms_bwd(dy, x, dres, g):
    xh, r = _rms_stats(x)
    dyg = dy * g
    dx = r * (dyg - xh * jnp.mean(dyg * xh, axis=-1, keepdims=True))
    return dres + dx, _colsum(dy * xh)


def _dup(fn):
    def wrapped(*a):
        r = fn(*a)
        return (r[0], r[0]) + tuple(r[1:])
    return wrapped


def _rms_bwd_bias(dy, x, dres, g):
    dx, dg = _rms_bwd(dy, x, dres, g)
    return dx, dg, _colsum(dx)


def _glu_fwd(u):
    d = u.shape[1] // 2
    return (u[:, :d] * _sigmoid(u[:, d:]),)


def _glu_bwd(dglu, u):
    d = u.shape[1] // 2
    a, sig = u[:, :d], _sigmoid(u[:, d:])
    du = jnp.concatenate([dglu * sig, dglu * a * sig * (1.0 - sig)], axis=1)
    return du, _colsum(du)


def _ln_parts(x, g, b):
    mu = jnp.mean(x, axis=-1, keepdims=True)
    xc = x - mu
    r = lax.rsqrt(jnp.mean(xc * xc, axis=-1, keepdims=True) + EPS)
    xh = xc * r
    return xh, r, xh * g + b


def _ln_silu_fwd(x, g, b):
    _, _, y = _ln_parts(x, g, b)
    return (y * _sigmoid(y),)


def _ln_silu_bwd(dsw, x, g, b):
    xh, r, y = _ln_parts(x, g, b)
    sig = _sigmoid(y)
    dy = dsw * sig * (1.0 + y * (1.0 - sig))
    dxh = dy * g
    dx = r * (dxh - jnp.mean(dxh, axis=-1, keepdims=True) - xh * jnp.mean(dxh * xh, axis=-1, keepdims=True))
    return dx, _colsum(dy * xh), _colsum(dy)


def _relu2(z):
    zp = jnp.maximum(z, 0.0)
    return (zp * zp,)


def _ple_fwd(h, zg, pp):
    return (h + _sigmoid(zg) * pp,)


def _ple_bwd(dh, zg, pp):
    gate = _sigmoid(zg)
    return dh * pp * gate * (1.0 - gate), dh * gate


def _final_fn(h, t, g):
    xh, r = _rms_stats(h)
    err = xh * g - t
    dy = err * (1.0 / h.shape[1])
    dyg = dy * g
    dh = r * (dyg - xh * jnp.mean(dyg * xh, axis=-1, keepdims=True))
    return dh, _colsum(err * err), _colsum(dy * xh)


def _add2(a, b):
    return (a + b,)


def _adamw_fn(w, g, m, v):
    m = ADAM_B1 * m + (1.0 - ADAM_B1) * g
    v = ADAM_B2 * v + (1.0 - ADAM_B2) * (g * g)
    m_hat = m / (1.0 - ADAM_B1 ** ADAM_STEP)
    v_hat = v / (1.0 - ADAM_B2 ** ADAM_STEP)
    delta = -ADAM_LR * (m_hat / (jnp.sqrt(v_hat) + ADAM_EPS) + ADAM_WD * w)
    return g, delta, m, v


def _adamw(w, g, m, v, name):
    shape = w.shape
    cols = shape[-1] if len(shape) > 1 else shape[0]
    two = lambda t: t.reshape(-1, cols)
    o = _sds(two(w).shape, F32)
    res = _rows(_adamw_fn, [two(w), two(g), two(m), two(v)], [], [o, o, o, o], tm=512, name=name)
    return [r.reshape(shape) for r in res]


def _sublane_shifts(win):
    n = win.shape[0]
    return [win] + [pltpu.roll(win, n - b, axis=0) for b in range(1, 8)]


def _tap(shifted, offset, rows):
    a, b = divmod(offset, 8)
    return shifted[b][8 * a:8 * a + rows]


def _dwconv_fwd(u, w, b, *, tm=512, name):
    S, D = u.shape
    tm = min(tm, S)
    rc = min(128, tm)
    per = tm // CONV_PAD

    def body(prev_ref, cur_ref, w_ref, b_ref, o_ref, win):
        i = pl.program_id(0)

        @pl.when(i == 0)
        def _():
            win[0:CONV_PAD, :] = jnp.zeros((CONV_PAD, D), F32)

        @pl.when(i > 0)
        def _():
            win[0:CONV_PAD, :] = prev_ref[...]

        win[CONV_PAD:CONV_PAD + tm, :] = cur_ref[...]
        for lc in range(D // LANES):
            ls = slice(lc * LANES, (lc + 1) * LANES)
            for r0 in range(0, tm, rc):
                shifted = _sublane_shifts(win[r0:r0 + rc + CONV_PAD, ls])
                acc = jnp.zeros((rc, LANES), F32) + b_ref[:, ls]
                for k in range(CONV_WIDTH):
                    acc = acc + _tap(shifted, 2 + k, rc) * w_ref[k:k + 1, ls]
                o_ref[r0:r0 + rc, ls] = acc

    return _pallas(
        body, name=name, grid=(S // tm,),
        in_specs=[pl.BlockSpec((CONV_PAD, D), lambda i: (jnp.maximum(i * per - 1, 0), 0)),
                  pl.BlockSpec((tm, D), lambda i: (i, 0)),
                  pl.BlockSpec((CONV_PAD, D), lambda i: (0, 0)),
                  pl.BlockSpec((1, D), lambda i: (0, 0))],
        out_specs=pl.BlockSpec((tm, D), lambda i: (i, 0)),
        out_shape=_sds((S, D), F32),
        scratch_shapes=[pltpu.VMEM((tm + CONV_PAD, D), F32)],
        compiler_params=_params(("arbitrary",)),
    )(u, u, w, b)


def _dwconv_bwd(dy, u, w, *, tm=512, name):
    S, D = u.shape
    tm = min(tm, S)
    rc = min(128, tm)
    per = tm // CONV_PAD
    n = S // tm
    nxt = S // CONV_PAD - 1

    def body(dy_ref, dyn_ref, up_ref, u_ref, w_ref, du_ref, dw_ref, db_ref, wd, wu, dwacc, dbacc):
        i = pl.program_id(0)

        @pl.when(i == 0)
        def _():
            wu[0:CONV_PAD, :] = jnp.zeros((CONV_PAD, D), F32)
            dwacc[...] = jnp.zeros(dwacc.shape, F32)
            dbacc[...] = jnp.zeros(dbacc.shape, F32)

        @pl.when(i > 0)
        def _():
            wu[0:CONV_PAD, :] = up_ref[...]

        @pl.when(i == n - 1)
        def _():
            wd[tm:tm + CONV_PAD, :] = jnp.zeros((CONV_PAD, D), F32)

        @pl.when(i < n - 1)
        def _():
            wd[tm:tm + CONV_PAD, :] = dyn_ref[...]

        wu[CONV_PAD:CONV_PAD + tm, :] = u_ref[...]
        wd[0:tm, :] = dy_ref[...]
        for lc in range(D // LANES):
            ls = slice(lc * LANES, (lc + 1) * LANES)
            for r0 in range(0, tm, rc):
                sd = _sublane_shifts(wd[r0:r0 + rc + CONV_PAD, ls])
                acc = jnp.zeros((rc, LANES), F32)
                for k in range(CONV_WIDTH):
                    acc = acc + _tap(sd, 30 - k, rc) * w_ref[k:k + 1, ls]
                du_ref[r0:r0 + rc, ls] = acc
                dyc = wd[r0:r0 + rc, ls]
                dbacc[:, ls] += jnp.sum(dyc.reshape(rc // 8, 8, LANES), axis=0)
                for k in range(CONV_WIDTH):
                    prod = dyc * wu[r0 + 2 + k:r0 + 2 + k + rc, ls]
                    dwacc[8 * k:8 * k + 8, ls] += jnp.sum(prod.reshape(rc // 8, 8, LANES), axis=0)

        @pl.when(i == n - 1)
        def _():
            dw_ref[...] = jnp.zeros(dw_ref.shape, F32)
            for k in range(CONV_WIDTH):
                dw_ref[k:k + 1, :] = jnp.sum(dwacc[8 * k:8 * k + 8, :], axis=0, keepdims=True)
            db_ref[...] = jnp.sum(dbacc[...], axis=0, keepdims=True)

    return _pallas(
        body, name=name, grid=(n,),
        in_specs=[pl.BlockSpec((tm, D), lambda i: (i, 0)),
                  pl.BlockSpec((CONV_PAD, D), lambda i: (jnp.minimum((i + 1) * per, nxt), 0)),
                  pl.BlockSpec((CONV_PAD, D), lambda i: (jnp.maximum(i * per - 1, 0), 0)),
                  pl.BlockSpec((tm, D), lambda i: (i, 0)),
                  pl.BlockSpec((CONV_PAD, D), lambda i: (0, 0))],
        out_specs=[pl.BlockSpec((tm, D), lambda i: (i, 0)),
                   pl.BlockSpec((CONV_PAD, D), lambda i: (0, 0)),
                   pl.BlockSpec((1, D), lambda i: (0, 0))],
        out_shape=[_sds((S, D), F32), _sds((CONV_PAD, D), F32), _sds((1, D), F32)],
        scratch_shapes=[pltpu.VMEM((tm + CONV_PAD, D), F32), pltpu.VMEM((tm + CONV_PAD, D), F32),
                        pltpu.VMEM((8 * CONV_PAD, D), F32), pltpu.VMEM((8, D), F32)],
        compiler_params=_params(("arbitrary",)),
    )(dy, dy, u, u, w)


def _tri_dot(tri, x):
    x1 = x.astype(BF16)
    r1 = x - x1.astype(F32)
    x2 = r1.astype(BF16)
    x3 = (r1 - x2.astype(F32)).astype(BF16)
    d = lambda v: jnp.dot(tri, v, preferred_element_type=F32)
    return d(x1) + d(x2) + d(x3)


def _log_sigmoid(x):
    return jnp.minimum(x, 0.0) - jnp.log(1.0 + jnp.exp(-jnp.abs(x)))


def _gate_cumsum(fl, bf, *, tm=256, name):
    S, W = fl.shape
    tm = min(tm, S)

    def body(fl_ref, bf_ref, c_ref, carry):
        i = pl.program_id(0)

        @pl.when(i == 0)
        def _():
            carry[...] = jnp.zeros(carry.shape, F32)

        x = _log_sigmoid(fl_ref[...] + bf_ref[...])
        row = lax.broadcasted_iota(jnp.int32, (tm, tm), 0)
        col = lax.broadcasted_iota(jnp.int32, (tm, tm), 1)
        tri = jnp.where(row >= col, 1.0, 0.0).astype(BF16)
        cs = _tri_dot(tri, x) + carry[0:1, :]
        c_ref[...] = cs
        carry[...] = jnp.broadcast_to(cs[tm - 1:tm, :], carry.shape)

    return _pallas(
        body, name=name, grid=(S // tm,),
        in_specs=[pl.BlockSpec((tm, W), lambda i: (i, 0)), pl.BlockSpec((1, W), lambda i: (0, 0))],
        out_specs=pl.BlockSpec((tm, W), lambda i: (i, 0)),
        out_shape=_sds((S, W), F32),
        scratch_shapes=[pltpu.VMEM((8, W), F32)],
        compiler_params=_params(("arbitrary",)),
    )(fl, bf)


def _gate_cumsum_bwd(sums, fl, bf, *, tm=256, name):
    S, W = fl.shape
    tm = min(tm, S)
    n = S // tm
    ns = len(sums)
    assert ns % 2 == 0

    def body(*refs):
        sum_refs = refs[:ns]
        fl_ref, bf_ref, o_ref, s_ref, carry = refs[ns:]
        i = pl.program_id(0)

        @pl.when(i == 0)
        def _():
            carry[...] = jnp.zeros(carry.shape, F32)
            s_ref[...] = jnp.zeros(s_ref.shape, F32)

        dc = sum_refs[0][...] - sum_refs[1][...]
        for a in range(2, ns, 2):
            dc = dc + (sum_refs[a][...] - sum_refs[a + 1][...])
        row = lax.broadcasted_iota(jnp.int32, (tm, tm), 0)
        col = lax.broadcasted_iota(jnp.int32, (tm, tm), 1)
        tri = jnp.where(col >= row, 1.0, 0.0).astype(BF16)
        rs = _tri_dot(tri, dc) + carry[0:1, :]
        carry[...] = jnp.broadcast_to(rs[0:1, :], carry.shape)
        dfl = rs * _sigmoid(-(fl_ref[...] + bf_ref[...]))
        o_ref[...] = dfl
        s_ref[...] += _colsum(dfl)

    rev = lambda i: (n - 1 - i, 0)
    return _pallas(
        body, name=name, grid=(n,),
        in_specs=[pl.BlockSpec((tm, W), rev)] * (ns + 1) + [pl.BlockSpec((1, W), lambda i: (0, 0))],
        out_specs=[pl.BlockSpec((tm, W), rev), pl.BlockSpec((1, W), lambda i: (0, 0))],
        out_shape=[_sds((S, W), F32), _sds((1, W), F32)],
        scratch_shapes=[pltpu.VMEM((8, W), F32)],
        compiler_params=_params(("arbitrary",)),
    )(*sums, fl, bf)


def _tri_tables(nq, qc, by_query):
    if by_query:
        pairs = [(i, j) for i in range(nq) for j in range(qc * (i + 1))]
    else:
        pairs = [(i, j) for j in range(qc * nq) for i in range(j // qc, nq)]
    ii, jj = zip(*pairs)
    return jnp.asarray(np.array(ii, np.int32)), jnp.asarray(np.array(jj, np.int32))


def _chunk_kinds(qc, dd):
    if dd < 0:
        return ("full",) * qc
    return tuple("full" if r > dd else "diag" if r == dd else "skip" for r in range(qc))


def _rep(v, t):
    return jnp.tile(v, (1, t // LANES))


def _attn_fwd(q, kv, ck, *, tb=512, row_chunks=4, name):
    S, D = q.shape
    HP = D // LANES
    T = min(tb, S)
    QC = row_chunks if S >= row_chunks * T else 1
    TQ = QC * T
    it, jt = _tri_tables(S // TQ, QC, True)

    def body(it_ref, jt_ref, q_ref, k_ref, v_ref, ck_ref, o_ref, lse_ref, st):
        s_id = pl.program_id(1)
        i, j = it_ref[s_id], jt_ref[s_id]
        dd = j - QC * i
        lane = lax.broadcasted_iota(jnp.int32, (TQ, LANES), 1)
        head0 = lane < HEAD_DIM
        h0 = head0[:T]
        hms = (h0, jnp.logical_not(h0))

        @pl.when(j == 0)
        def _():
            st[0:2] = jnp.full((2, TQ, LANES), NEG_BIG, F32)
            st[2:4] = jnp.zeros((2, TQ, LANES), F32)

        def step(kinds):
            kvv, vv = k_ref[...], v_ref[...]
            one = jnp.ones_like(vv)
            vaug = [jnp.where(hms[h], vv, one) for h in range(2)]
            old = st[...]
            rows = lambda r: slice(r * T, (r + 1) * T)
            live = [r for r in range(QC) if kinds[r] != "skip"]
            chains = [(r, h) for r in live for h in range(2)]
            ss = {}
            for r, h in chains:
                qv = q_ref[rows(r), :]
                ss[r, h] = lax.dot_general(jnp.where(hms[h], qv, jnp.zeros_like(qv)), kvv, _DIMS["nt"], preferred_element_type=F32)
            ps, alphas, m_new = {}, {}, {}
            for r, h in chains:
                s = ss[r, h] - ck_ref[h:h + 1, :]
                if kinds[r] == "diag":
                    row = lax.broadcasted_iota(jnp.int32, (T, T), 0)
                    col = lax.broadcasted_iota(jnp.int32, (T, T), 1)
                    s = jnp.where(row >= col, s, NEG_BIG)
                m_prev = old[h, rows(r), :]
                m_new[r, h] = jnp.maximum(m_prev, jnp.max(s, axis=1, keepdims=True))
                ps[r, h] = jnp.exp(s - _rep(m_new[r, h], T)).astype(BF16)
                alphas[r, h] = jnp.exp(m_prev - m_new[r, h])
            new = [[], [], [], []]
            for r in range(QC):
                if kinds[r] == "skip":
                    for a in range(4):
                        new[a].append(old[a, rows(r), :])
                    continue
                pv = [jnp.dot(ps[r, h], vaug[h], preferred_element_type=F32) for h in range(2)]
                a0, a1 = alphas[r, 0], alphas[r, 1]
                new[0].append(m_new[r, 0])
                new[1].append(m_new[r, 1])
                new[2].append(jnp.where(h0, a0, a1) * old[2, rows(r), :] + jnp.where(h0, pv[0], pv[1]))
                new[3].append(jnp.where(h0, a1, a0) * old[3, rows(r), :] + jnp.where(h0, pv[1], pv[0]))
            res = jnp.stack([jnp.concatenate(n, axis=0) for n in new], axis=0)
            st[...] = res
            return res

        @pl.when(dd < 0)
        def _():
            step(_chunk_kinds(QC, -1))

        for d in range(QC):
            @pl.when(dd == d)
            def _(d=d):
                res = step(_chunk_kinds(QC, d))
                if d == QC - 1:
                    lr = pltpu.roll(res[3], HEAD_DIM, axis=1)
                    o_ref[...] = res[2] / lr
                    lse_ref[0] = res[0] + jnp.log(jnp.where(head0, lr, res[3]))
                    lse_ref[1] = res[1] + jnp.log(jnp.where(head0, res[3], lr))

    grid_spec = pltpu.PrefetchScalarGridSpec(
        num_scalar_prefetch=2, grid=(HP, it.shape[0]),
        in_specs=[pl.BlockSpec((TQ, LANES), lambda h, s, it, jt: (it[s], h)),
                  pl.BlockSpec((T, LANES), lambda h, s, it, jt: (jt[s], h)),
                  pl.BlockSpec((T, LANES), lambda h, s, it, jt: (jt[s], HP + h)),
                  pl.BlockSpec((None, 8, T), lambda h, s, it, jt: (h, 0, jt[s]))],
        out_specs=[pl.BlockSpec((TQ, LANES), lambda h, s, it, jt: (it[s], h)),
                   pl.BlockSpec((2, TQ, LANES), lambda h, s, it, jt: (h, it[s], 0))],
        scratch_shapes=[pltpu.VMEM((4, TQ, LANES), F32)],
    )
    return _pallas(
        body, name=name, grid_spec=grid_spec,
        out_shape=[_sds((S, D), F32), _sds((2 * HP, S, LANES), F32)],
        compiler_params=_params(("parallel", "arbitrary")),
    )(it, jt, q, kv, kv, ck)


def _attn_bwd(q, kv, o, do, lse, ck, *, tb=512, row_chunks=4, name):
    S, D = q.shape
    HP = D // LANES
    T = min(tb, S)
    QC = row_chunks if S >= row_chunks * T else 1
    TQ = QC * T
    it, jt = _tri_tables(S // TQ, QC, False)

    def body(it_ref, jt_ref, q_ref, k_ref, v_ref, o_ref, do_ref, lse_ref, ck_ref, dq_ref, drs_ref, dk_ref, dv_ref, dcs_ref):
        s_id = pl.program_id(1)
        i, j = it_ref[s_id], jt_ref[s_id]
        dd = j - QC * i
        lane = lax.broadcasted_iota(jnp.int32, (T, LANES), 1)
        head0 = lane < HEAD_DIM
        hms = (head0, jnp.logical_not(head0))

        @pl.when(s_id == 0)
        def _():
            dq_ref[...] = jnp.zeros(dq_ref.shape, F32)
            drs_ref[...] = jnp.zeros(drs_ref.shape, F32)

        @pl.when(dd >= 0)
        def _():
            dk_ref[...] = jnp.zeros(dk_ref.shape, F32)
            dv_ref[...] = jnp.zeros(dv_ref.shape, F32)
            dcs_ref[...] = jnp.zeros(dcs_ref.shape, F32)

        def step(kinds):
            kvv, vv = k_ref[...], v_ref[...]
            one = jnp.ones_like(kvv)
            zero = jnp.zeros_like(kvv)
            rows = lambda r: slice(r * T, (r + 1) * T)
            live = [r for r in range(QC) if kinds[r] != "skip"]
            chains = [(r, h) for r in live for h in range(2)]
            qv = {r: q_ref[rows(r), :] for r in live}
            dob = {r: do_ref[rows(r), :].astype(BF16) for r in live}
            ss = {(r, h): lax.dot_general(jnp.where(hms[h], qv[r], zero), kvv, _DIMS["nt"], preferred_element_type=F32)
                  for r, h in chains}
            dps = {(r, h): lax.dot_general(jnp.where(hms[h], dob[r], zero), vv, _DIMS["nt"], preferred_element_type=F32)
                   for r, h in chains}
            pbs, dsbs = {}, {}
            for r, h in chains:
                s = ss[r, h] - ck_ref[h:h + 1, :]
                if kinds[r] == "diag":
                    row = lax.broadcasted_iota(jnp.int32, (T, T), 0)
                    col = lax.broadcasted_iota(jnp.int32, (T, T), 1)
                    s = jnp.where(row >= col, s, NEG_BIG)
                p = jnp.exp(s - _rep(lse_ref[h, rows(r), :], T))
                prod = dob[r].astype(F32) * o_ref[rows(r), :]
                delta = jnp.sum(jnp.where(hms[h], prod, 0.0), axis=1, keepdims=True)
                pbs[r, h] = p.astype(BF16)
                dsbs[r, h] = (p * (dps[r, h] - delta)).astype(BF16)
            dvs, dks = [None, None], [None, None]
            for r in live:
                dqs = []
                for h in range(2):
                    dqs.append(jnp.dot(dsbs[r, h], jnp.where(hms[h], kvv, one), preferred_element_type=F32))
                    dv = jnp.dot(pbs[r, h].T, dob[r], preferred_element_type=F32)
                    dk = jnp.dot(dsbs[r, h].T, jnp.where(hms[h], qv[r], one), preferred_element_type=F32)
                    dvs[h] = dv if dvs[h] is None else dvs[h] + dv
                    dks[h] = dk if dks[h] is None else dks[h] + dk
                qrows = pl.ds(pl.multiple_of(i * TQ + r * T, T), T)
                dq_ref[qrows, :] += jnp.where(head0, dqs[0], dqs[1])
                drs_ref[qrows, :] += jnp.where(head0, dqs[1], dqs[0])
            dv_ref[...] += jnp.where(head0, dvs[0], dvs[1])
            dk_ref[...] += jnp.where(head0, dks[0], dks[1])
            dcs_ref[...] += jnp.where(head0, dks[1], dks[0])

        @pl.when(dd < 0)
        def _():
            step(_chunk_kinds(QC, -1))

        for d in range(QC):
            @pl.when(dd == d)
            def _(d=d):
                step(_chunk_kinds(QC, d))

    by_q = lambda h, s, it, jt: (it[s], h)
    by_k = lambda h, s, it, jt: (jt[s], h)
    whole = lambda h, s, it, jt: (0, h)
    grid_spec = pltpu.PrefetchScalarGridSpec(
        num_scalar_prefetch=2, grid=(HP, it.shape[0]),
        in_specs=[pl.BlockSpec((TQ, LANES), by_q),
                  pl.BlockSpec((T, LANES), by_k),
                  pl.BlockSpec((T, LANES), lambda h, s, it, jt: (jt[s], HP + h)),
                  pl.BlockSpec((TQ, LANES), by_q),
                  pl.BlockSpec((TQ, LANES), by_q),
                  pl.BlockSpec((2, TQ, LANES), lambda h, s, it, jt: (h, it[s], 0)),
                  pl.BlockSpec((None, 8, T), lambda h, s, it, jt: (h, 0, jt[s]))],
        out_specs=[pl.BlockSpec((S, LANES), whole), pl.BlockSpec((S, LANES), whole),
                   pl.BlockSpec((T, LANES), by_k), pl.BlockSpec((T, LANES), by_k), pl.BlockSpec((T, LANES), by_k)],
        scratch_shapes=[],
    )
    return _pallas(
        body, name=name, grid_spec=grid_spec,
        out_shape=[_sds((S, D), F32)] * 5,
        compiler_params=_params(("parallel", "arbitrary")),
    )(it, jt, q, kv, kv, o, do, lse, ck)


ANY = pl.BlockSpec(memory_space=pl.ANY)


def _coords():
    x, y, c = lax.axis_index("x"), lax.axis_index("y"), lax.axis_index("c")
    return x, y, c


def _remote(src, dst, send_sems, recv_sems, k, to):
    return pltpu.make_async_remote_copy(src_ref=src, dst_ref=dst, send_sem=send_sems.at[k], recv_sem=recv_sems.at[k],
                                        device_id=to, device_id_type=MESH)


def _all_gather_shards(pack, *, name):
    R, C = pack.shape
    assert R % 4 == 0
    H, Q = R // 2, R // 4

    def body(in_ref, out_ref, send_sems, recv_sems):
        x, y, c = _coords()
        me, sib = (x, y, c), (x, y, 1 - c)
        xn, yn = (1 - x, y, c), (x, 1 - y, c)
        s, sx, sy, sd = 2 * x + y, 2 * (1 - x) + y, 2 * x + 1 - y, 2 * (1 - x) + 1 - y
        half = pl.ds(c * H, H)
        other = pl.ds((1 - c) * H, H)
        q0 = pl.ds(c * H, Q)
        q1 = pl.ds(c * H + Q, Q)
        rc = functools.partial(_remote, send_sems=send_sems, recv_sems=recv_sems)

        sends = [rc(in_ref.at[half], out_ref.at[s, half], k=0, to=xn),
                 rc(in_ref.at[half], out_ref.at[s, half], k=1, to=yn),
                 rc(in_ref, out_ref.at[s], k=7, to=sib)]
        for cp in sends:
            cp.start()
        rc(in_ref.at[half], out_ref.at[sx, half], k=0, to=me).wait_recv()
        sends.append(rc(out_ref.at[sx, q0], out_ref.at[sx, q0], k=2, to=yn))
        sends[-1].start()
        sends.append(rc(out_ref.at[sx, half], out_ref.at[sx, half], k=4, to=sib))
        sends[-1].start()
        rc(in_ref.at[half], out_ref.at[sy, half], k=1, to=me).wait_recv()
        sends.append(rc(out_ref.at[sy, q1], out_ref.at[sy, q1], k=3, to=xn))
        sends[-1].start()
        sends.append(rc(out_ref.at[sy, half], out_ref.at[sy, half], k=5, to=sib))
        sends[-1].start()
        rc(out_ref.at[sd, q0], out_ref.at[sd, q0], k=2, to=me).wait_recv()
        rc(out_ref.at[sd, q1], out_ref.at[sd, q1], k=3, to=me).wait_recv()
        sends.append(rc(out_ref.at[sd, half], out_ref.at[sd, half], k=6, to=sib))
        sends[-1].start()
        for k, sh in ((4, sx), (5, sy), (6, sd)):
            rc(out_ref.at[sh, other], out_ref.at[sh, other], k=k, to=me).wait_recv()
        rc(in_ref, out_ref.at[s], k=7, to=me).wait_recv()
        for cp in sends:
            cp.wait_send()

    return _pallas(
        body, name=name, in_specs=[ANY], out_specs=ANY,
        out_shape=_sds((N_CHIPS, R, C), pack.dtype),
        scratch_shapes=[pltpu.SemaphoreType.DMA((8,)), pltpu.SemaphoreType.DMA((8,))],
    )(pack)


def _rs_pair(g, *, name):
    n, R, C = g.shape
    H = R // 2

    def body(g_ref, land_ref, send_sems, recv_sems):
        x, y, c = _coords()
        other = pl.ds((1 - c) * H, H)
        cps = [_remote(g_ref.at[sh, other], land_ref.at[sh], send_sems, recv_sems, sh, (x, y, 1 - c)) for sh in range(n)]
        for cp in cps:
            cp.start()
        for cp in cps:
            cp.wait_recv()
        for cp in cps:
            cp.wait_send()

    return _pallas(
        body, name=name, in_specs=[ANY], out_specs=ANY, out_shape=_sds((n, H, C), g.dtype),
        scratch_shapes=[pltpu.SemaphoreType.DMA((n,)), pltpu.SemaphoreType.DMA((n,))],
    )(g)


def _rs_quarters(p, *, name):
    n, H, C = p.shape
    Q = H // 2

    def body(p_ref, la_ref, lb_ref, send_sems, recv_sems):
        x, y, c = _coords()
        sd = 2 * (1 - x) + 1 - y
        a = _remote(p_ref.at[sd, pl.ds(0, Q)], la_ref, send_sems, recv_sems, 0, (x, 1 - y, c))
        b = _remote(p_ref.at[sd, pl.ds(Q, Q)], lb_ref, send_sems, recv_sems, 1, (1 - x, y, c))
        a.start()
        b.start()
        a.wait_recv()
        b.wait_recv()
        a.wait_send()
        b.wait_send()

    return _pallas(
        body, name=name, in_specs=[ANY], out_specs=[ANY, ANY],
        out_shape=[_sds((Q, C), p.dtype), _sds((Q, C), p.dtype)],
        scratch_shapes=[pltpu.SemaphoreType.DMA((2,)), pltpu.SemaphoreType.DMA((2,))],
    )(p)


def _rs_halves(p, ax, ay, *, name):
    n, H, C = p.shape
    Q = H // 2

    def body(p_ref, ax_ref, ay_ref, la_ref, lb_ref, send_sems, recv_sems):
        x, y, c = _coords()
        sx, sy = 2 * (1 - x) + y, 2 * x + 1 - y
        xn, yn = (1 - x, y, c), (x, 1 - y, c)
        lo, hi = pl.ds(0, Q), pl.ds(Q, Q)
        cps = [_remote(ax_ref, la_ref.at[lo], send_sems, recv_sems, 0, xn),
               _remote(p_ref.at[sx, hi], la_ref.at[hi], send_sems, recv_sems, 1, xn),
               _remote(p_ref.at[sy, lo], lb_ref.at[lo], send_sems, recv_sems, 2, yn),
               _remote(ay_ref, lb_ref.at[hi], send_sems, recv_sems, 3, yn)]
        for cp in cps:
            cp.start()
        for cp in cps:
            cp.wait_recv()
        for cp in cps:
            cp.wait_send()

    return _pallas(
        body, name=name, in_specs=[ANY, ANY, ANY], out_specs=[ANY, ANY],
        out_shape=[_sds((H, C), p.dtype), _sds((H, C), p.dtype)],
        scratch_shapes=[pltpu.SemaphoreType.DMA((4,)), pltpu.SemaphoreType.DMA((4,))],
    )(p, ax, ay)


def _rs_join(buf, *, name):
    R, C = buf.shape
    H = R // 2

    def body(in_ref, out_ref, send_sems, recv_sems):
        x, y, c = _coords()
        half = pl.ds(c * H, H)
        other = pl.ds((1 - c) * H, H)
        cp = _remote(in_ref.at[half], out_ref.at[half], send_sems, recv_sems, 0, (x, y, 1 - c))
        cp.start()
        _remote(in_ref.at[other], out_ref.at[other], send_sems, recv_sems, 0, (x, y, c)).wait_recv()
        cp.wait_send()

    return _pallas(
        body, name=name, in_specs=[ANY], out_specs=ANY, out_shape=_sds((R, C), buf.dtype),
        input_output_aliases={0: 0},
        scratch_shapes=[pltpu.SemaphoreType.DMA((1,)), pltpu.SemaphoreType.DMA((1,))],
    )(buf)


def _tile_add(ins_specs, arrays, n_steps, out_spec, out_shape, scalars, *, name):
    grid_spec = pltpu.PrefetchScalarGridSpec(
        num_scalar_prefetch=1, grid=(n_steps,), in_specs=ins_specs, out_specs=out_spec, scratch_shapes=[])

    def body(sc_ref, *refs):
        acc = refs[0][...].astype(F32)
        for r in refs[1:-1]:
            acc = acc + r[...].astype(F32)
        refs[-1][...] = acc.astype(refs[-1].dtype)

    return _pallas(body, name=name, grid_spec=grid_spec, out_shape=out_shape,
                   compiler_params=_params(("arbitrary",)))(scalars, *arrays)


def _reduce_scatter(g, *, name):
    n, R, C = g.shape
    H, Q = R // 2, R // 4
    tm = RS_ROW_MULT // 4
    assert Q % tm == 0, (R, tm)
    x, y, c = _coords()
    sx, sy, s = 2 * (1 - x) + y, 2 * x + 1 - y, 2 * x + y
    sc = jnp.stack([c, sx, sy, s]).astype(jnp.int32)
    hb, qb = H // tm, Q // tm
    blk = lambda f: pl.BlockSpec((None, tm, C), f)
    flat = lambda f: pl.BlockSpec((tm, C), f)

    land = _rs_pair(g, name=name + "_pair")
    p = _tile_add([blk(lambda i, sc: (i // hb, sc[0] * hb + i % hb, 0)), blk(lambda i, sc: (i // hb, i % hb, 0))],
                  [g, land], n * hb, blk(lambda i, sc: (i // hb, i % hb, 0)), _sds((n, H, C), g.dtype), sc, name=name + "_add0")
    la, lb = _rs_quarters(p, name=name + "_quarters")
    ax = _tile_add([blk(lambda i, sc: (sc[1], i, 0)), flat(lambda i, sc: (i, 0))], [p, la], qb,
                   flat(lambda i, sc: (i, 0)), _sds((Q, C), g.dtype), sc, name=name + "_add1x")
    ay = _tile_add([blk(lambda i, sc: (sc[2], qb + i, 0)), flat(lambda i, sc: (i, 0))], [p, lb], qb,
                   flat(lambda i, sc: (i, 0)), _sds((Q, C), g.dtype), sc, name=name + "_add1y")
    fa, fb = _rs_halves(p, ax, ay, name=name + "_halves")
    buf = _tile_add([blk(lambda i, sc: (sc[3], i, 0)), flat(lambda i, sc: (i, 0)), flat(lambda i, sc: (i, 0))],
                    [p, fa, fb], hb, flat(lambda i, sc: (sc[0] * hb + i, 0)), _sds((R, C), F32), sc, name=name + "_add2")
    return _rs_join(buf, name=name + "_join")


def _all_reduce_small(v, *, name):
    M, N = v.shape

    def body(x_ref, out_ref, send_sems, recv_sems, local_sem):
        x, y, c = _coords()
        me, sibling = (x, y, c), (x, y, 1 - c)
        chips = [(1 - x, y), (x, 1 - y), (1 - x, 1 - y)]

        def rows(px, py, pc):
            return out_ref.at[pl.ds((4 * px + 2 * py + pc) * M, M), :]

        def copy(k, block, to, src=None):
            return pltpu.make_async_remote_copy(
                src_ref=rows(*block) if src is None else src, dst_ref=rows(*block),
                send_sem=send_sems.at[k], recv_sem=recv_sems.at[k], device_id=to, device_id_type=MESH)

        mine = pltpu.make_async_copy(x_ref, rows(*me), local_sem)
        mine.start()
        first = [copy(0, me, sibling, src=x_ref)]
        first += [copy(1 + j, me, (*chip, c), src=x_ref) for j, chip in enumerate(chips)]
        for cp in first:
            cp.start()
        passed = [copy(4 + j, (*chip, c), sibling) for j, chip in enumerate(chips)]
        for j, chip in enumerate(chips):
            copy(1 + j, (*chip, c), me).wait_recv()
            passed[j].start()
        copy(0, sibling, me).wait_recv()
        for j, chip in enumerate(chips):
            copy(4 + j, (*chip, 1 - c), me).wait_recv()
        for cp in first + passed:
            cp.wait_send()
        mine.wait()

    gathered = _pallas(
        body, name=name + "_gather",
        out_shape=_sds((N_DEV * M, N), F32),
        in_specs=[pl.BlockSpec(memory_space=pltpu.VMEM)],
        out_specs=pl.BlockSpec(memory_space=pltpu.VMEM),
        scratch_shapes=[pltpu.SemaphoreType.DMA((7,)), pltpu.SemaphoreType.DMA((7,)), pltpu.SemaphoreType.DMA],
    )(v)

    def sum_body(g_ref, o_ref):
        acc = g_ref[0:M, :]
        for d in range(1, N_DEV):
            acc = acc + g_ref[d * M:(d + 1) * M, :]
        o_ref[...] = acc

    return _pallas(sum_body, name=name + "_sum", out_shape=_sds((M, N), F32))(gathered)


MATS = [("conv_w_pw1", 2), ("conv_w_pw2", 1), ("attn_w_q", 1), ("attn_w_o", 1), ("ffn_w1", 2), ("ffn_w2", 1),
        ("ple_w_gate", 1), ("ple_w_proj", 2), ("w_kvf", 1)]
VECS = [("conv_b_pw1", 1), ("conv_w_dw", 2), ("conv_b_dw", 1), ("conv_ln_g", 1), ("conv_ln_b", 1), ("conv_b_pw2", 1)]
REPL = ["mix_norm", "ffn_norm", "ple_norm", "kv_norm", "final_norm", "b_f"]
WEIGHTS = ["mix_norm", "conv_w_pw1", "conv_b_pw1", "conv_w_dw", "conv_b_dw", "conv_ln_g", "conv_ln_b", "conv_w_pw2",
           "conv_b_pw2", "kv_norm", "w_kvf", "b_f", "attn_w_q", "attn_w_o", "ffn_norm", "ffn_w1", "ffn_w2", "ple_norm",
           "ple_w_gate", "ple_w_proj", "final_norm"]


def _round_up(n, m):
    return -(-n // m) * m


def _to_rows(t, C, mult):
    flat = t.reshape(-1)
    rows = _round_up(_round_up(flat.shape[0], C) // C, mult)
    flat = jnp.pad(flat, (0, rows * C - flat.shape[0]))
    return flat.reshape(rows, C)


def _pack(tensors, C, mult, total_mult):
    parts = [_to_rows(t, C, mult) for t in tensors]
    rows = sum(p.shape[0] for p in parts)
    pad = _round_up(rows, total_mult) - rows
    if pad:
        parts.append(jnp.zeros((pad, C), parts[0].dtype))
    return jnp.concatenate(parts, axis=0)


def _row_counts(shapes, C, mult):
    return [_round_up(_round_up(int(np.prod(s)), C) // C, mult) for s in shapes]


def _unpack(packed, shapes, C, mult):
    outs, r0 = [], 0
    lead = packed.shape[:-2]
    for shp, nr in zip(shapes, _row_counts(shapes, C, mult)):
        n = int(np.prod(shp))
        seg = packed[..., r0:r0 + nr, :].reshape(lead + (nr * C,))[..., :n]
        outs.append(seg.reshape(lead + tuple(shp)))
        r0 += nr
    return outs


def _unshard(t, axis):
    return jnp.concatenate([t[s] for s in range(N_CHIPS)], axis=axis)


def _shards(t, axis):
    return jnp.split(t, N_CHIPS, axis=axis)


class _GradPack:
    DIRECT = {"conv_w_pw2": "rows", "attn_w_q": "rows", "attn_w_o": "rows", "ffn_w2": "rows", "ple_w_gate": "rows",
              "ffn_w1": "cols"}

    def __init__(self, names, shard_shapes, C):
        self.C, self.names, self.shapes, self.off = C, names, dict(zip(names, shard_shapes)), {}
        r = 0
        for n, cnt in zip(names, _row_counts(shard_shapes, C, 16)):
            self.off[n] = r
            r += cnt
        self.R = _round_up(r, RS_ROW_MULT)
        self.buf = jnp.zeros((N_CHIPS * self.R, C), BF16)

    def matmul(self, wname, layer, a, b, **kw):
        rows_s, cols_s = self.shapes[wname][-2:]
        base = self.off[wname] + layer * (rows_s * cols_s // self.C)
        tm = next(t for t in (512, 256, 128, 64, 32, 16) if rows_s % t == 0 and base % t == 0 and self.R % t == 0)
        per, shard_blocks, first = rows_s // tm, self.R // tm, base // tm
        if self.DIRECT[wname] == "rows":
            assert cols_s == self.C
            out_map = lambda i, j, k: ((i // per) * shard_blocks + first + i % per, 0)
        else:
            assert cols_s == self.C and a.shape[1] == rows_s
            out_map = lambda i, j, k: (j * shard_blocks + first + i, 0)
        self.buf = _mm(a, b, mode="tn", tk=2048, tm=tm, tn=self.C, out_dtype=BF16, into=self.buf, out_map=out_map, **kw)

    def insert(self, grads, axes):
        run = []
        for n in self.names + [None]:
            if n is not None and n not in self.DIRECT:
                run.append(n)
                continue
            if run:
                for s in range(N_CHIPS):
                    rows = jnp.concatenate([_to_rows(_shards(grads[m].astype(BF16), axes[m])[s], self.C, 16) for m in run], axis=0)
                    self.buf = lax.dynamic_update_slice(self.buf, rows, (s * self.R + self.off[run[0]], 0))
                run = []


def _local_step(x, p, tgt, w, gp=None):
    S, D = x.shape
    L = p.shape[0]
    NA = w["conv_w_pw1"].shape[0]
    H = w["b_f"].shape[0]
    HP = D // LANES
    row = lambda v: v.reshape(1, -1)
    act = lambda dt, n=D: _sds((S, n), dt)
    g = {}

    def dw(wname, layer, a, b, **kw):
        if gp is not None and wname in gp.DIRECT:
            gp.matmul(wname, layer, a, b, **kw)
            return None
        return _mm(a, b, mode="tn", tk=2048, out_dtype=BF16, **kw)

    def rms(hh, gain, name):
        return _rows(_rms_fwd, [hh], [row(gain)], [act(BF16)], name=name)[0]

    saved = []
    h = x
    kv = ck = fl = nkv = h_kv = None
    bfp = jnp.pad(w["b_f"], (0, LANES - H)).reshape(1, LANES)
    wk = w["w_kvf"][:, :D]
    wv = w["w_kvf"][:, D:2 * D]
    wkv = w["w_kvf"][:, :2 * D]
    wf = jnp.pad(w["w_kvf"][:, 2 * D:], ((0, 0), (0, LANES - H)))
    res_rms = lambda acc, r, gn: _with_rms(acc + r, gn)
    hn = rms(h, w["mix_norm"][0], "rms_mix")
    for i in range(L):
        sv = {"h0": h, "hn": hn}
        g_ffn, g_ple = row(w["ffn_norm"][i]), row(w["ple_norm"][i])
        if i < NA:
            u = _mm(hn, w["conv_w_pw1"][i], extras=[row(w["conv_b_pw1"][i])], epi=lambda acc, b: acc + b, name="mm_pw1")
            glu = _rows(_glu_fwd, [u], [], [act(F32)], name="glu_fwd")[0]
            wdw = jnp.pad(w["conv_w_dw"][i], ((0, CONV_PAD - CONV_WIDTH), (0, 0)))
            cv = _dwconv_fwd(glu, wdw, row(w["conv_b_dw"][i]), name="dwconv_fwd")
            sw = _rows(_ln_silu_fwd, [cv], [row(w["conv_ln_g"][i]), row(w["conv_ln_b"][i])], [act(BF16)], name="ln_silu_fwd")[0]
            h1, hn2 = _mm(sw, w["conv_w_pw2"][i], extras=[row(w["conv_b_pw2"][i]), h, g_ffn],
                          epi=lambda acc, b, r, gn: _with_rms(acc + b + r, gn), out_dtype=[F32, BF16], tm=512, tn=D, name="mm_pw2")
            sv.update(u=u, glu=glu, cv=cv, sw=sw, wdw=wdw)
        else:
            if i == NA:
                h_kv = h
                nkv = rms(h, w["kv_norm"], "rms_kv")
                kv = _mm(nkv, wkv, out_dtype=BF16, name="mm_kv")
                fl = _mm(nkv, wf, name="mm_f")
                c = _gate_cumsum(fl, bfp, name="gate_cumsum")
                ck = jnp.pad(c[:, :H].T.reshape(HP, 2, S), ((0, 0), (0, 6), (0, 0)))
            j = i - NA
            q = _mm(hn, w["attn_w_q"][j], epi=lambda acc: acc * (HEAD_DIM ** -0.5), out_dtype=BF16, name="mm_q")
            o, lse = _attn_fwd(q, kv, ck, name="attn_fwd")
            h1, hn2 = _mm(o, w["attn_w_o"][j], extras=[h, g_ffn], epi=res_rms, out_dtype=[F32, BF16], tm=512, tn=D, name="mm_o")
            sv.update(q=q, o=o, lse=lse)
        zb, f = _mm(hn2, w["ffn_w1"][i], epi=lambda acc: (acc, jnp.square(jnp.maximum(acc, 0.0))), out_dtype=[BF16, BF16],
                    name="mm_ffn1")
        h2, n3 = _mm(f, w["ffn_w2"][i], extras=[h1, g_ple], epi=res_rms, out_dtype=[F32, BF16], tm=512, tn=D, tk=4 * D, name="mm_ffn2")
        zg = _mm(n3, w["ple_w_gate"][i], name="mm_gate")
        ple = lambda acc, r, zz: r + _sigmoid(zz) * acc
        if i + 1 < L:
            h, hn = _mm(p[i], w["ple_w_proj"][i], extras=[h2, zg, row(w["mix_norm"][i + 1])],
                        epi=lambda acc, r, zz, gn: _with_rms(ple(acc, r, zz), gn), out_dtype=[F32, BF16], tm=512, tn=D, name="mm_proj")
        else:
            h = _mm(p[i], w["ple_w_proj"][i], extras=[h2, zg], epi=ple, tm=512, tn=D, name="mm_proj_last")
        sv.update(h1=h1, hn2=hn2, zb=zb, f=f, h2=h2, n3=n3, zg=zg)
        saved.append(sv)

    dh, err2, g_final = _rows(_final_fn, [h, tgt], [row(w["final_norm"])], [act(F32)], [_sds((1, D), F32), _sds((1, D), F32)],
                              name="final")
    loss = 0.5 * jnp.sum(err2) / D
    g["final_norm"] = g_final.reshape(-1)

    red = _sds((1, D), F32)
    stack = {k: [None] * n for k, n in (("mix_norm", L), ("ffn_norm", L), ("ple_norm", L), ("ffn_w1", L), ("ffn_w2", L),
                                        ("ple_w_gate", L), ("ple_w_proj", L), ("conv_w_pw1", NA), ("conv_b_pw1", NA),
                                        ("conv_w_dw", NA), ("conv_b_dw", NA), ("conv_ln_g", NA), ("conv_ln_b", NA),
                                        ("conv_w_pw2", NA), ("conv_b_pw2", NA), ("attn_w_q", L - NA), ("attn_w_o", L - NA))}
    dk_sum = dv_sum = None
    dcks = []
    for i in reversed(range(L)):
        sv = saved[i]
        dzg, dpp = _mm(p[i], w["ple_w_proj"][i], extras=[dh, sv["zg"]], epi=lambda acc, d, zz: _ple_bwd(d, zz, acc),
                       out_dtype=[BF16, BF16], name="mm_ple_bwd")
        stack["ple_w_proj"][i] = _mm(p[i], dpp, mode="tn", tk=2048, out_dtype=BF16, name="mm_dproj")
        stack["ple_w_gate"][i] = dw("ple_w_gate", i, sv["n3"], dzg, name="mm_dgate")
        dh, dhb, dgain = _mm(dzg, w["ple_w_gate"][i], mode="nt", extras=[sv["h2"], dh, row(w["ple_norm"][i])], epi=_dup(_rms_bwd),
                             out_dtype=[F32, BF16], reds=1, tm=512, tn=D, name="mm_dn3")
        stack["ple_norm"][i] = dgain.reshape(-1)
        dz = _mm(dhb, w["ffn_w2"][i], mode="nt", extras=[sv["zb"]], epi=lambda acc, zz: acc * (2.0 * jnp.maximum(zz, 0.0).astype(F32)),
                 out_dtype=BF16, name="mm_dz")
        stack["ffn_w2"][i] = dw("ffn_w2", i, sv["f"], dhb, name="mm_dffn2")
        stack["ffn_w1"][i] = dw("ffn_w1", i, sv["hn2"], dz, name="mm_dffn1")
        if i < NA:
            dh, dhb, dgain, dbias = _mm(dz, w["ffn_w1"][i], mode="nt", extras=[sv["h1"], dh, row(w["ffn_norm"][i])],
                                        epi=_dup(_rms_bwd_bias), out_dtype=[F32, BF16], reds=2, tm=512, tn=D, tk=4 * D, name="mm_dhn2_bias")
            stack["conv_b_pw2"][i] = dbias.reshape(-1)
        else:
            dh, dhb, dgain = _mm(dz, w["ffn_w1"][i], mode="nt", extras=[sv["h1"], dh, row(w["ffn_norm"][i])], epi=_dup(_rms_bwd),
                                 out_dtype=[F32, BF16], reds=1, tm=512, tn=D, tk=4 * D, name="mm_dhn2")
        stack["ffn_norm"][i] = dgain.reshape(-1)
        if i < NA:
            dsw = _mm(dhb, w["conv_w_pw2"][i], mode="nt", name="mm_dsw")
            stack["conv_w_pw2"][i] = dw("conv_w_pw2", i, sv["sw"], dhb, name="mm_dpw2")
            dcv, dlg, dlb = _rows(_ln_silu_bwd, [dsw, sv["cv"]], [row(w["conv_ln_g"][i]), row(w["conv_ln_b"][i])], [act(F32)],
                                  [red, red], name="ln_silu_bwd")
            stack["conv_ln_g"][i], stack["conv_ln_b"][i] = dlg.reshape(-1), dlb.reshape(-1)
            dglu, dwdw, dbdw = _dwconv_bwd(dcv, sv["glu"], sv["wdw"], name="dwconv_bwd")
            stack["conv_w_dw"][i], stack["conv_b_dw"][i] = dwdw[:CONV_WIDTH], dbdw.reshape(-1)
            du, dbu = _rows(_glu_bwd, [dglu, sv["u"]], [], [act(BF16, 2 * D)], [_sds((1, 2 * D), F32)], name="glu_bwd")
            stack["conv_b_pw1"][i] = dbu.reshape(-1)
            stack["conv_w_pw1"][i] = _mm(sv["hn"], du, mode="tn", tk=2048, out_dtype=BF16, name="mm_dpw1")
            dh, dgain = _mm(du, w["conv_w_pw1"][i], mode="nt", extras=[sv["h0"], dh, row(w["mix_norm"][i])], epi=_rms_bwd,
                            reds=1, tm=512, tn=D, name="mm_dhn_a")
        else:
            j = i - NA
            do = _mm(dhb, w["attn_w_o"][j], mode="nt", out_dtype=BF16, name="mm_do")
            stack["attn_w_o"][j] = dw("attn_w_o", j, sv["o"], dhb, name="mm_dwo")
            dq, drs, dk, dv, dcs = _attn_bwd(sv["q"], kv, sv["o"], do, sv["lse"], ck, name="attn_bwd")
            scale = lambda acc: acc * (HEAD_DIM ** -0.5)
            stack["attn_w_q"][j] = dw("attn_w_q", j, sv["hn"], dq, epi=scale, name="mm_dwq")
            dh_in = dh
            dh, dgain = _mm(dq, w["attn_w_q"][j], mode="nt", extras=[sv["h0"], dh_in, row(w["mix_norm"][i])],
                            epi=lambda acc, xx, dr, gn: _rms_bwd(scale(acc), xx, dr, gn), reds=1, tm=512, tn=D, name="mm_dhn_b")
            pick = lambda t: jnp.pad(t.reshape(S, HP, 2, HEAD_DIM)[:, :, ::-1, 0].reshape(S, H), ((0, 0), (0, LANES - H)))
            dcks += [pick(drs), pick(dcs)]
            if dk_sum is None:
                dk_sum, dv_sum = dk, dv
            else:
                dk_sum = _rows(_add2, [dk_sum, dk], [], [act(F32)], name="add_dk")[0]
                dv_sum = _rows(_add2, [dv_sum, dv], [], [act(F32)], name="add_dk")[0]
        stack["mix_norm"][i] = dgain.reshape(-1)
        if i == NA:
            dfl, dbf = _gate_cumsum_bwd(dcks, fl, bfp, name="gate_cumsum_bwd")
            g["b_f"] = dbf[0, :H]
            gk = _mm(nkv, dk_sum, mode="tn", tk=2048, out_dtype=BF16, name="mm_dwk")
            gv = _mm(nkv, dv_sum, mode="tn", tk=2048, out_dtype=BF16, name="mm_dwk")
            gf = _mm(nkv, dfl, mode="tn", tk=2048, out_dtype=BF16, name="mm_dwf")
            g["w_kvf"] = jnp.concatenate([gk, gv, gf[:, :H]], axis=1)
            dn = _mm(dk_sum, wk, mode="nt", name="mm_dnk")
            dn = _mm(dv_sum, wv, mode="nt", extras=[dn], epi=lambda acc, r: acc + r, name="mm_dnv")
            dh, dgain = _mm(dfl, wf, mode="nt", extras=[dn, h_kv, dh, row(w["kv_norm"])],
                            epi=lambda acc, r, xx, dr, gn: _rms_bwd(acc + r, xx, dr, gn), reds=1, tm=512, tn=D, name="mm_dnf")
            g["kv_norm"] = dgain.reshape(-1)
    for k, v in stack.items():
        if v[0] is not None:
            g[k] = jnp.stack(v, axis=0)
    return loss, dh, g


def kernel(x, p, mix_norm, conv_w_pw1, conv_b_pw1, conv_w_dw, conv_b_dw, conv_ln_g, conv_ln_b, conv_w_pw2, conv_b_pw2, kv_norm, w_kvf, b_f, attn_w_q, attn_w_o, ffn_norm, ffn_w1, ffn_w2, ple_norm, ple_w_gate, ple_w_proj, final_norm, loss_target, m_mix_norm, m_conv_w_pw1, m_conv_b_pw1, m_conv_w_dw, m_conv_b_dw, m_conv_ln_g, m_conv_ln_b, m_conv_w_pw2, m_conv_b_pw2, m_kv_norm, m_w_kvf, m_b_f, m_attn_w_q, m_attn_w_o, m_ffn_norm, m_ffn_w1, m_ffn_w2, m_ple_norm, m_ple_w_gate, m_ple_w_proj, m_final_norm, v_mix_norm, v_conv_w_pw1, v_conv_b_pw1, v_conv_w_dw, v_conv_b_dw, v_conv_ln_g, v_conv_ln_b, v_conv_w_pw2, v_conv_b_pw2, v_kv_norm, v_w_kvf, v_b_f, v_attn_w_q, v_attn_w_o, v_ffn_norm, v_ffn_w1, v_ffn_w2, v_ple_norm, v_ple_w_gate, v_ple_w_proj, v_final_norm):
    args = dict(locals())
    wl = {n: args[n] for n in WEIGHTS}
    ml = {n: args["m_" + n] for n in WEIGHTS}
    vl = {n: args["v_" + n] for n in WEIGHTS}
    S, D = x.shape[1], x.shape[2]
    C = D

    mat_shapes = [wl[n].shape for n, _ in MATS]
    vec_shapes = [wl[n].shape for n, _ in VECS]
    mats = _all_gather_shards(_pack([wl[n].astype(BF16) for n, _ in MATS], C, 16, 64), name="ag_mats")
    vecs = _all_gather_shards(_pack([wl[n] for n, _ in VECS], C, 1, 32), name="ag_vecs")
    full = {n: wl[n] for n in REPL}
    for (n, ax), t in zip(MATS, _unpack(mats, mat_shapes, C, 16)):
        full[n] = _unshard(t, ax)
    for (n, ax), t in zip(VECS, _unpack(vecs, vec_shapes, C, 1)):
        full[n] = _unshard(t, ax)

    names = [n for n, _ in MATS] + [n for n, _ in VECS]
    axes = dict(MATS + VECS)
    shard_shapes = [wl[n].shape for n in names]
    gp = _GradPack(names, shard_shapes, C)
    loss, dx, g = _local_step(x[0], p[:, 0], loss_target[0], full, gp)
    loss = lax.psum(loss, ("x", "y", "c"))

    gp.insert(g, axes)
    gred = _reduce_scatter(gp.buf.reshape(N_CHIPS, gp.R, C), name="rs")
    gl = dict(zip(names, _unpack(gred, shard_shapes, C, 16)))

    rep_shapes = [wl[n].shape for n in REPL]
    rpack = _pack([g[n] for n in REPL], C, 1, 8)
    for n, t in zip(REPL, _unpack(_all_reduce_small(rpack, name="ar"), rep_shapes, C, 1)):
        gl[n] = t

    grads, deltas, new_m, new_v = {}, {}, {}, {}
    rep_w, rep_m, rep_v = (_pack([d[n] for n in REPL], C, 1, 8) for d in (wl, ml, vl))
    rep_out = _adamw(rep_w, rpack_like(gl, rep_shapes, C), rep_m, rep_v, "adamw_rep")
    for dst, packed in zip((grads, deltas, new_m, new_v), rep_out):
        for n, t in zip(REPL, _unpack(packed, rep_shapes, C, 1)):
            dst[n] = t
    for n in names:
        res = _adamw(wl[n], gl[n], ml[n], vl[n], "adamw_" + n)
        for dst, t in zip((grads, deltas, new_m, new_v), res):
            dst[n] = t
    out = [loss, dx[None]]
    for d in (grads, deltas, new_m, new_v):
        out += [d[n] for n in WEIGHTS]
    return tuple(out)


def rpack_like(gl, rep_shapes, C):
    return _pack([gl[n] for n in REPL], C, 1, 8)
```

```python
import functools

import jax
import jax.numpy as jnp
import numpy as np
from jax import lax
from jax.experimental import pallas as pl
from jax.experimental.pallas import tpu as pltpu

F32 = jnp.float32
BF16 = jnp.bfloat16
MESH = pl.DeviceIdType.MESH

N_CHIPS = 4
N_DEV = 8
HEAD_DIM = 64
LANES = 128
CONV_WIDTH = 31
CONV_PAD = 32
EPS = 1e-6
NEG_BIG = -1e30
VMEM_LIMIT = 56 * 1024 * 1024

RS_ROW_MULT = 1024

ADAM_LR, ADAM_B1, ADAM_B2, ADAM_EPS, ADAM_WD, ADAM_STEP = 0.001, 0.9, 0.999, 1e-08, 0.01, 10


def _pallas(body, **kw):
    return pl.pallas_call(body, **kw)


def _params(sem=None):
    return pltpu.CompilerParams(dimension_semantics=sem, vmem_limit_bytes=VMEM_LIMIT)


def _sds(shape, dtype):
    return jax.ShapeDtypeStruct(tuple(shape), dtype)


_DIMS = {"nn": (((1,), (0,)), ((), ())), "nt": (((1,), (1,)), ((), ())), "tn": (((0,), (0,)), ((), ()))}


def _mm(a, b, *, mode="nn", extras=(), epi=None, out_dtype=F32, reds=0, tm=1024, tn=1024, tk=1024, into=None, out_block=None,
        out_map=None, name):
    if mode == "nn":
        (M, K), (K2, N) = a.shape, b.shape
    elif mode == "nt":
        (M, K), (N, K2) = a.shape, b.shape
    else:
        (K, M), (K2, N) = a.shape, b.shape
    assert K == K2, (name, a.shape, b.shape)
    tm, tn, tk = min(tm, M), min(tn, N), min(tk, K)
    assert M % tm == 0 and N % tn == 0 and K % tk == 0, (name, a.shape, b.shape)
    nk = K // tk
    if mode == "tn":
        a_spec = pl.BlockSpec((tk, tm), lambda i, j, k: (k, i))
    else:
        a_spec = pl.BlockSpec((tm, tk), lambda i, j, k: (i, k))
    if mode == "nt":
        b_spec = pl.BlockSpec((tn, tk), lambda i, j, k: (j, k))
    else:
        b_spec = pl.BlockSpec((tk, tn), lambda i, j, k: (k, j))
    ex_specs = []
    for e in extras:
        if e.shape[0] == 1:
            ex_specs.append(pl.BlockSpec((1, tn), lambda i, j, k: (0, j)))
        else:
            assert e.shape == (M, N), (name, e.shape)
            ex_specs.append(pl.BlockSpec((tm, tn), lambda i, j, k: (i, j)))
    ne = len(extras)
    dims = _DIMS[mode]
    many = isinstance(out_dtype, (list, tuple))
    out_dtypes = list(out_dtype) if many else [out_dtype]
    no = len(out_dtypes)
    assert not reds or tn == N, name

    def body(a_ref, b_ref, *rest):
        ex_refs, o_refs, r_refs = rest[:ne], rest[ne:ne + no], rest[ne + no:ne + no + reds]
        part = lax.dot_general(a_ref[...].astype(BF16), b_ref[...].astype(BF16), dims, preferred_element_type=F32)
        i = pl.program_id(0)

        def finish(acc):
            res = epi(acc, *[r[...] for r in ex_refs]) if epi is not None else acc
            if not isinstance(res, (tuple, list)):
                res = (res,)
            assert len(res) == no + reds, (name, len(res))
            for r, v in zip(o_refs, res[:no]):
                r[...] = v.astype(r.dtype).reshape(r.shape)
            for r, v in zip(r_refs, res[no:]):
                @pl.when(i == 0)
                def _(r=r, v=v):
                    r[...] = v

                @pl.when(i > 0)
                def _(r=r, v=v):
                    r[...] += v

        if nk == 1:
            finish(part)
        else:
            acc_ref = rest[ne + no + reds]
            k = pl.program_id(2)

            @pl.when(k == 0)
            def _():
                acc_ref[...] = part

            @pl.when(k > 0)
            def _():
                acc_ref[...] += part

            @pl.when(k == nk - 1)
            def _():
                finish(acc_ref[...])

    scratch = [pltpu.VMEM((tm, tn), F32)] if nk > 1 else []
    if into is not None:
        assert not many and not reds and not extras and into.dtype == out_dtype, name

        def body_into(a_ref, b_ref, into_ref, *rest):
            body(a_ref, b_ref, *rest)

        return _pallas(
            body_into, name=name, grid=(M // tm, N // tn, nk),
            in_specs=[a_spec, b_spec, pl.BlockSpec(memory_space=pl.ANY)],
            out_specs=pl.BlockSpec(out_block, out_map),
            out_shape=_sds(into.shape, into.dtype),
            input_output_aliases={2: 0},
            scratch_shapes=scratch,
            compiler_params=_params(("parallel", "parallel", "arbitrary")),
        )(a, b, into)
    res = _pallas(
        body, name=name, grid=(M // tm, N // tn, nk),
        in_specs=[a_spec, b_spec] + ex_specs,
        out_specs=[pl.BlockSpec((tm, tn), lambda i, j, k: (i, j))] * no + [pl.BlockSpec((1, tn), lambda i, j, k: (0, j))] * reds,
        out_shape=[_sds((M, N), dt) for dt in out_dtypes] + [_sds((1, N), F32)] * reds,
        scratch_shapes=scratch,
        compiler_params=_params(("arbitrary",) * 3 if reds else ("parallel", "parallel", "arbitrary")),
    )(a, b, *extras)
    return res if (many or reds) else res[0]


def _rows(fn, ins, params, outs, reds=(), *, tm=256, name):
    S = ins[0].shape[0]
    tm = min(tm, S)
    assert S % tm == 0, (name, S, tm)
    ni, npar, no, nr = len(ins), len(params), len(outs), len(reds)

    def body(*refs):
        in_refs, p_refs = refs[:ni], refs[ni:ni + npar]
        o_refs, r_refs = refs[ni + npar:ni + npar + no], refs[ni + npar + no:]
        res = fn(*[r[...] for r in in_refs], *[r[...] for r in p_refs])
        if not isinstance(res, (tuple, list)):
            res = (res,)
        assert len(res) == no + nr, (name, len(res))
        for r, v in zip(o_refs, res[:no]):
            r[...] = v.astype(r.dtype)
        i = pl.program_id(0)
        for r, v in zip(r_refs, res[no:]):
            @pl.when(i == 0)
            def _(r=r, v=v):
                r[...] = v

            @pl.when(i > 0)
            def _(r=r, v=v):
                r[...] += v

    res = _pallas(
        body, name=name, grid=(S // tm,),
        in_specs=[pl.BlockSpec((tm, a.shape[1]), lambda i: (i, 0)) for a in ins]
        + [pl.BlockSpec(p.shape, lambda i: (0, 0)) for p in params],
        out_specs=[pl.BlockSpec((tm, o.shape[1]), lambda i: (i, 0)) for o in outs]
        + [pl.BlockSpec(r.shape, lambda i: (0, 0)) for r in reds],
        out_shape=list(outs) + list(reds),
        compiler_params=_params(("arbitrary",)),
    )(*ins, *params)
    return res


def _colsum(v):
    return jnp.sum(v, axis=0, keepdims=True)


def _sigmoid(v):
    return 1.0 / (1.0 + jnp.exp(-v))


def _rms_stats(x):
    r = lax.rsqrt(jnp.mean(x * x, axis=-1, keepdims=True) + EPS)
    return x * r, r


def _rms_fwd(x, g):
    xh, _ = _rms_stats(x)
    return (xh * g,)


def _with_rms(h, g):
    xh, _ = _rms_stats(h)
    return h, xh * g


def _rms_bwd(dy, x, dres, g):
    xh, r = _rms_stats(x)
    dyg = dy * g
    dx = r * (dyg - xh * jnp.mean(dyg * xh, axis=-1, keepdims=True))
    return dres + dx, _colsum(dy * xh)


def _dup(fn):
    def wrapped(*a):
        r = fn(*a)
        return (r[0], r[0]) + tuple(r[1:])
    return wrapped


def _rms_bwd_bias(dy, x, dres, g):
    dx, dg = _rms_bwd(dy, x, dres, g)
    return dx, dg, _colsum(dx)


def _glu_fwd(u):
    d = u.shape[1] // 2
    return (u[:, :d] * _sigmoid(u[:, d:]),)


def _glu_bwd(dglu, u):
    d = u.shape[1] // 2
    a, sig = u[:, :d], _sigmoid(u[:, d:])
    du = jnp.concatenate([dglu * sig, dglu * a * sig * (1.0 - sig)], axis=1)
    return du, _colsum(du)


def _ln_parts(x, g, b):
    mu = jnp.mean(x, axis=-1, keepdims=True)
    xc = x - mu
    r = lax.rsqrt(jnp.mean(xc * xc, axis=-1, keepdims=True) + EPS)
    xh = xc * r
    return xh, r, xh * g + b


def _ln_silu_fwd(x, g, b):
    _, _, y = _ln_parts(x, g, b)
    return (y * _sigmoid(y),)


def _ln_silu_bwd(dsw, x, g, b):
    xh, r, y = _ln_parts(x, g, b)
    sig = _sigmoid(y)
    dy = dsw * sig * (1.0 + y * (1.0 - sig))
    dxh = dy * g
    dx = r * (dxh - jnp.mean(dxh, axis=-1, keepdims=True) - xh * jnp.mean(dxh * xh, axis=-1, keepdims=True))
    return dx, _colsum(dy * xh), _colsum(dy)


def _relu2(z):
    zp = jnp.maximum(z, 0.0)
    return (zp * zp,)


def _ple_fwd(h, zg, pp):
    return (h + _sigmoid(zg) * pp,)


def _ple_bwd(dh, zg, pp):
    gate = _sigmoid(zg)
    return dh * pp * gate * (1.0 - gate), dh * gate


def _final_fn(h, t, g):
    xh, r = _rms_stats(h)
    err = xh * g - t
    dy = err * (1.0 / h.shape[1])
    dyg = dy * g
    dh = r * (dyg - xh * jnp.mean(dyg * xh, axis=-1, keepdims=True))
    return dh, _colsum(err * err), _colsum(dy * xh)


def _add2(a, b):
    return (a + b,)


def _adamw_fn(w, g, m, v):
    m = ADAM_B1 * m + (1.0 - ADAM_B1) * g
    v = ADAM_B2 * v + (1.0 - ADAM_B2) * (g * g)
    m_hat = m / (1.0 - ADAM_B1 ** ADAM_STEP)
    v_hat = v / (1.0 - ADAM_B2 ** ADAM_STEP)
    delta = -ADAM_LR * (m_hat / (jnp.sqrt(v_hat) + ADAM_EPS) + ADAM_WD * w)
    return g, delta, m, v


def _adamw(w, g, m, v, name):
    shape = w.shape
    cols = shape[-1] if len(shape) > 1 else shape[0]
    two = lambda t: t.reshape(-1, cols)
    o = _sds(two(w).shape, F32)
    res = _rows(_adamw_fn, [two(w), two(g), two(m), two(v)], [], [o, o, o, o], tm=512, name=name)
    return [r.reshape(shape) for r in res]


def _sublane_shifts(win):
    n = win.shape[0]
    return [win] + [pltpu.roll(win, n - b, axis=0) for b in range(1, 8)]


def _tap(shifted, offset, rows):
    a, b = divmod(offset, 8)
    return shifted[b][8 * a:8 * a + rows]


def _dwconv_fwd(u, w, b, *, tm=512, name):
    S, D = u.shape
    tm = min(tm, S)
    rc = min(128, tm)
    per = tm // CONV_PAD

    def body(prev_ref, cur_ref, w_ref, b_ref, o_ref, win):
        i = pl.program_id(0)

        @pl.when(i == 0)
        def _():
            win[0:CONV_PAD, :] = jnp.zeros((CONV_PAD, D), F32)

        @pl.when(i > 0)
        def _():
            win[0:CONV_PAD, :] = prev_ref[...]

        win[CONV_PAD:CONV_PAD + tm, :] = cur_ref[...]
        for lc in range(D // LANES):
            ls = slice(lc * LANES, (lc + 1) * LANES)
            for r0 in range(0, tm, rc):
                shifted = _sublane_shifts(win[r0:r0 + rc + CONV_PAD, ls])
                acc = jnp.zeros((rc, LANES), F32) + b_ref[:, ls]
                for k in range(CONV_WIDTH):
                    acc = acc + _tap(shifted, 2 + k, rc) * w_ref[k:k + 1, ls]
                o_ref[r0:r0 + rc, ls] = acc

    return _pallas(
        body, name=name, grid=(S // tm,),
        in_specs=[pl.BlockSpec((CONV_PAD, D), lambda i: (jnp.maximum(i * per - 1, 0), 0)),
                  pl.BlockSpec((tm, D), lambda i: (i, 0)),
                  pl.BlockSpec((CONV_PAD, D), lambda i: (0, 0)),
                  pl.BlockSpec((1, D), lambda i: (0, 0))],
        out_specs=pl.BlockSpec((tm, D), lambda i: (i, 0)),
        out_shape=_sds((S, D), F32),
        scratch_shapes=[pltpu.VMEM((tm + CONV_PAD, D), F32)],
        compiler_params=_params(("arbitrary",)),
    )(u, u, w, b)


def _dwconv_bwd(dy, u, w, *, tm=512, name):
    S, D = u.shape
    tm = min(tm, S)
    rc = min(128, tm)
    per = tm // CONV_PAD
    n = S // tm
    nxt = S // CONV_PAD - 1

    def body(dy_ref, dyn_ref, up_ref, u_ref, w_ref, du_ref, dw_ref, db_ref, wd, wu, dwacc, dbacc):
        i = pl.program_id(0)

        @pl.when(i == 0)
        def _():
            wu[0:CONV_PAD, :] = jnp.zeros((CONV_PAD, D), F32)
            dwacc[...] = jnp.zeros(dwacc.shape, F32)
            dbacc[...] = jnp.zeros(dbacc.shape, F32)

        @pl.when(i > 0)
        def _():
            wu[0:CONV_PAD, :] = up_ref[...]

        @pl.when(i == n - 1)
        def _():
            wd[tm:tm + CONV_PAD, :] = jnp.zeros((CONV_PAD, D), F32)

        @pl.when(i < n - 1)
        def _():
            wd[tm:tm + CONV_PAD, :] = dyn_ref[...]

        wu[CONV_PAD:CONV_PAD + tm, :] = u_ref[...]
        wd[0:tm, :] = dy_ref[...]
        for lc in range(D // LANES):
            ls = slice(lc * LANES, (lc + 1) * LANES)
            for r0 in range(0, tm, rc):
                sd = _sublane_shifts(wd[r0:r0 + rc + CONV_PAD, ls])
                acc = jnp.zeros((rc, LANES), F32)
                for k in range(CONV_WIDTH):
                    acc = acc + _tap(sd, 30 - k, rc) * w_ref[k:k + 1, ls]
                du_ref[r0:r0 + rc, ls] = acc
                dyc = wd[r0:r0 + rc, ls]
                dbacc[:, ls] += jnp.sum(dyc.reshape(rc // 8, 8, LANES), axis=0)
                for k in range(CONV_WIDTH):
                    prod = dyc * wu[r0 + 2 + k:r0 + 2 + k + rc, ls]
                    dwacc[8 * k:8 * k + 8, ls] += jnp.sum(prod.reshape(rc // 8, 8, LANES), axis=0)

        @pl.when(i == n - 1)
        def _():
            dw_ref[...] = jnp.zeros(dw_ref.shape, F32)
            for k in range(CONV_WIDTH):
                dw_ref[k:k + 1, :] = jnp.sum(dwacc[8 * k:8 * k + 8, :], axis=0, keepdims=True)
            db_ref[...] = jnp.sum(dbacc[...], axis=0, keepdims=True)

    return _pallas(
        body, name=name, grid=(n,),
        in_specs=[pl.BlockSpec((tm, D), lambda i: (i, 0)),
                  pl.BlockSpec((CONV_PAD, D), lambda i: (jnp.minimum((i + 1) * per, nxt), 0)),
                  pl.BlockSpec((CONV_PAD, D), lambda i: (jnp.maximum(i * per - 1, 0), 0)),
                  pl.BlockSpec((tm, D), lambda i: (i, 0)),
                  pl.BlockSpec((CONV_PAD, D), lambda i: (0, 0))],
        out_specs=[pl.BlockSpec((tm, D), lambda i: (i, 0)),
                   pl.BlockSpec((CONV_PAD, D), lambda i: (0, 0)),
                   pl.BlockSpec((1, D), lambda i: (0, 0))],
        out_shape=[_sds((S, D), F32), _sds((CONV_PAD, D), F32), _sds((1, D), F32)],
        scratch_shapes=[pltpu.VMEM((tm + CONV_PAD, D), F32), pltpu.VMEM((tm + CONV_PAD, D), F32),
                        pltpu.VMEM((8 * CONV_PAD, D), F32), pltpu.VMEM((8, D), F32)],
        compiler_params=_params(("arbitrary",)),
    )(dy, dy, u, u, w)


def _tri_dot(tri, x):
    x1 = x.astype(BF16)
    r1 = x - x1.astype(F32)
    x2 = r1.astype(BF16)
    x3 = (r1 - x2.astype(F32)).astype(BF16)
    d = lambda v: jnp.dot(tri, v, preferred_element_type=F32)
    return d(x1) + d(x2) + d(x3)


def _log_sigmoid(x):
    return jnp.minimum(x, 0.0) - jnp.log(1.0 + jnp.exp(-jnp.abs(x)))


def _gate_cumsum(fl, bf, *, tm=256, name):
    S, W = fl.shape
    tm = min(tm, S)

    def body(fl_ref, bf_ref, c_ref, carry):
        i = pl.program_id(0)

        @pl.when(i == 0)
        def _():
            carry[...] = jnp.zeros(carry.shape, F32)

        x = _log_sigmoid(fl_ref[...] + bf_ref[...])
        row = lax.broadcasted_iota(jnp.int32, (tm, tm), 0)
        col = lax.broadcasted_iota(jnp.int32, (tm, tm), 1)
        tri = jnp.where(row >= col, 1.0, 0.0).astype(BF16)
        cs = _tri_dot(tri, x) + carry[0:1, :]
        c_ref[...] = cs
        carry[...] = jnp.broadcast_to(cs[tm - 1:tm, :], carry.shape)

    return _pallas(
        body, name=name, grid=(S // tm,),
        in_specs=[pl.BlockSpec((tm, W), lambda i: (i, 0)), pl.BlockSpec((1, W), lambda i: (0, 0))],
        out_specs=pl.BlockSpec((tm, W), lambda i: (i, 0)),
        out_shape=_sds((S, W), F32),
        scratch_shapes=[pltpu.VMEM((8, W), F32)],
        compiler_params=_params(("arbitrary",)),
    )(fl, bf)


def _gate_cumsum_bwd(sums, fl, bf, *, tm=256, name):
    S, W = fl.shape
    tm = min(tm, S)
    n = S // tm
    ns = len(sums)
    assert ns % 2 == 0

    def body(*refs):
        sum_refs = refs[:ns]
        fl_ref, bf_ref, o_ref, s_ref, carry = refs[ns:]
        i = pl.program_id(0)

        @pl.when(i == 0)
        def _():
            carry[...] = jnp.zeros(carry.shape, F32)
            s_ref[...] = jnp.zeros(s_ref.shape, F32)

        dc = sum_refs[0][...] - sum_refs[1][...]
        for a in range(2, ns, 2):
            dc = dc + (sum_refs[a][...] - sum_refs[a + 1][...])
        row = lax.broadcasted_iota(jnp.int32, (tm, tm), 0)
        col = lax.broadcasted_iota(jnp.int32, (tm, tm), 1)
        tri = jnp.where(col >= row, 1.0, 0.0).astype(BF16)
        rs = _tri_dot(tri, dc) + carry[0:1, :]
        carry[...] = jnp.broadcast_to(rs[0:1, :], carry.shape)
        dfl = rs * _sigmoid(-(fl_ref[...] + bf_ref[...]))
        o_ref[...] = dfl
        s_ref[...] += _colsum(dfl)

    rev = lambda i: (n - 1 - i, 0)
    return _pallas(
        body, name=name, grid=(n,),
        in_specs=[pl.BlockSpec((tm, W), rev)] * (ns + 1) + [pl.BlockSpec((1, W), lambda i: (0, 0))],
        out_specs=[pl.BlockSpec((tm, W), rev), pl.BlockSpec((1, W), lambda i: (0, 0))],
        out_shape=[_sds((S, W), F32), _sds((1, W), F32)],
        scratch_shapes=[pltpu.VMEM((8, W), F32)],
        compiler_params=_params(("arbitrary",)),
    )(*sums, fl, bf)


def _tri_tables(nq, qc, by_query):
    if by_query:
        pairs = [(i, j) for i in range(nq) for j in range(qc * (i + 1))]
    else:
        pairs = [(i, j) for j in range(qc * nq) for i in range(j // qc, nq)]
    ii, jj = zip(*pairs)
    return jnp.asarray(np.array(ii, np.int32)), jnp.asarray(np.array(jj, np.int32))


def _chunk_kinds(qc, dd):
    if dd < 0:
        return ("full",) * qc
    return tuple("full" if r > dd else "diag" if r == dd else "skip" for r in range(qc))


def _rep(v, t):
    return jnp.tile(v, (1, t // LANES))


def _attn_fwd(q, kv, ck, *, tb=512, row_chunks=4, name):
    S, D = q.shape
    HP = D // LANES
    T = min(tb, S)
    QC = row_chunks if S >= row_chunks * T else 1
    TQ = QC * T
    it, jt = _tri_tables(S // TQ, QC, True)

    def body(it_ref, jt_ref, q_ref, k_ref, v_ref, ck_ref, o_ref, lse_ref, st):
        s_id = pl.program_id(1)
        i, j = it_ref[s_id], jt_ref[s_id]
        dd = j - QC * i
        lane = lax.broadcasted_iota(jnp.int32, (TQ, LANES), 1)
        head0 = lane < HEAD_DIM
        h0 = head0[:T]
        hms = (h0, jnp.logical_not(h0))

        @pl.when(j == 0)
        def _():
            st[0:2] = jnp.full((2, TQ, LANES), NEG_BIG, F32)
            st[2:4] = jnp.zeros((2, TQ, LANES), F32)

        def step(kinds):
            kvv, vv = k_ref[...], v_ref[...]
            one = jnp.ones_like(vv)
            vaug = [jnp.where(hms[h], vv, one) for h in range(2)]
            old = st[...]
            rows = lambda r: slice(r * T, (r + 1) * T)
            live = [r for r in range(QC) if kinds[r] != "skip"]
            chains = [(r, h) for r in live for h in range(2)]
            ss = {}
            for r, h in chains:
                qv = q_ref[rows(r), :]
                ss[r, h] = lax.dot_general(jnp.where(hms[h], qv, jnp.zeros_like(qv)), kvv, _DIMS["nt"], preferred_element_type=F32)
            ps, alphas, m_new = {}, {}, {}
            for r, h in chains:
                s = ss[r, h] - ck_ref[h:h + 1, :]
                if kinds[r] == "diag":
                    row = lax.broadcasted_iota(jnp.int32, (T, T), 0)
                    col = lax.broadcasted_iota(jnp.int32, (T, T), 1)
                    s = jnp.where(row >= col, s, NEG_BIG)
                m_prev = old[h, rows(r), :]
                m_new[r, h] = jnp.maximum(m_prev, jnp.max(s, axis=1, keepdims=True))
                ps[r, h] = jnp.exp(s - _rep(m_new[r, h], T)).astype(BF16)
                alphas[r, h] = jnp.exp(m_prev - m_new[r, h])
            new = [[], [], [], []]
            for r in range(QC):
                if kinds[r] == "skip":
                    for a in range(4):
                        new[a].append(old[a, rows(r), :])
                    continue
                pv = [jnp.dot(ps[r, h], vaug[h], preferred_element_type=F32) for h in range(2)]
                a0, a1 = alphas[r, 0], alphas[r, 1]
                new[0].append(m_new[r, 0])
                new[1].append(m_new[r, 1])
                new[2].append(jnp.where(h0, a0, a1) * old[2, rows(r), :] + jnp.where(h0, pv[0], pv[1]))
                new[3].append(jnp.where(h0, a1, a0) * old[3, rows(r), :] + jnp.where(h0, pv[1], pv[0]))
            res = jnp.stack([jnp.concatenate(n, axis=0) for n in new], axis=0)
            st[...] = res
            return res

        @pl.when(dd < 0)
        def _():
            step(_chunk_kinds(QC, -1))

        for d in range(QC):
            @pl.when(dd == d)
            def _(d=d):
                res = step(_chunk_kinds(QC, d))
                if d == QC - 1:
                    lr = pltpu.roll(res[3], HEAD_DIM, axis=1)
                    o_ref[...] = res[2] / lr
                    lse_ref[0] = res[0] + jnp.log(jnp.where(head0, lr, res[3]))
                    lse_ref[1] = res[1] + jnp.log(jnp.where(head0, res[3], lr))

    grid_spec = pltpu.PrefetchScalarGridSpec(
        num_scalar_prefetch=2, grid=(HP, it.shape[0]),
        in_specs=[pl.BlockSpec((TQ, LANES), lambda h, s, it, jt: (it[s], h)),
                  pl.BlockSpec((T, LANES), lambda h, s, it, jt: (jt[s], h)),
                  pl.BlockSpec((T, LANES), lambda h, s, it, jt: (jt[s], HP + h)),
                  pl.BlockSpec((None, 8, T), lambda h, s, it, jt: (h, 0, jt[s]))],
        out_specs=[pl.BlockSpec((TQ, LANES), lambda h, s, it, jt: (it[s], h)),
                   pl.BlockSpec((2, TQ, LANES), lambda h, s, it, jt: (h, it[s], 0))],
        scratch_shapes=[pltpu.VMEM((4, TQ, LANES), F32)],
    )
    return _pallas(
        body, name=name, grid_spec=grid_spec,
        out_shape=[_sds((S, D), F32), _sds((2 * HP, S, LANES), F32)],
        compiler_params=_params(("parallel", "arbitrary")),
    )(it, jt, q, kv, kv, ck)


def _attn_bwd(q, kv, o, do, lse, ck, *, tb=512, row_chunks=4, name):
    S, D = q.shape
    HP = D // LANES
    T = min(tb, S)
    QC = row_chunks if S >= row_chunks * T else 1
    TQ = QC * T
    it, jt = _tri_tables(S // TQ, QC, False)

    def body(it_ref, jt_ref, q_ref, k_ref, v_ref, o_ref, do_ref, lse_ref, ck_ref, dq_ref, drs_ref, dk_ref, dv_ref, dcs_ref):
        s_id = pl.program_id(1)
        i, j = it_ref[s_id], jt_ref[s_id]
        dd = j - QC * i
        lane = lax.broadcasted_iota(jnp.int32, (T, LANES), 1)
        head0 = lane < HEAD_DIM
        hms = (head0, jnp.logical_not(head0))

        @pl.when(s_id == 0)
        def _():
            dq_ref[...] = jnp.zeros(dq_ref.shape, F32)
            drs_ref[...] = jnp.zeros(drs_ref.shape, F32)

        @pl.when(dd >= 0)
        def _():
            dk_ref[...] = jnp.zeros(dk_ref.shape, F32)
            dv_ref[...] = jnp.zeros(dv_ref.shape, F32)
            dcs_ref[...] = jnp.zeros(dcs_ref.shape, F32)

        def step(kinds):
            kvv, vv = k_ref[...], v_ref[...]
            one = jnp.ones_like(kvv)
            zero = jnp.zeros_like(kvv)
            rows = lambda r: slice(r * T, (r + 1) * T)
            live = [r for r in range(QC) if kinds[r] != "skip"]
            chains = [(r, h) for r in live for h in range(2)]
            qv = {r: q_ref[rows(r), :] for r in live}
            dob = {r: do_ref[rows(r), :].astype(BF16) for r in live}
            ss = {(r, h): lax.dot_general(jnp.where(hms[h], qv[r], zero), kvv, _DIMS["nt"], preferred_element_type=F32)
                  for r, h in chains}
            dps = {(r, h): lax.dot_general(jnp.where(hms[h], dob[r], zero), vv, _DIMS["nt"], preferred_element_type=F32)
                   for r, h in chains}
            pbs, dsbs = {}, {}
            for r, h in chains:
                s = ss[r, h] - ck_ref[h:h + 1, :]
                if kinds[r] == "diag":
                    row = lax.broadcasted_iota(jnp.int32, (T, T), 0)
                    col = lax.broadcasted_iota(jnp.int32, (T, T), 1)
                    s = jnp.where(row >= col, s, NEG_BIG)
                p = jnp.exp(s - _rep(lse_ref[h, rows(r), :], T))
                prod = dob[r].astype(F32) * o_ref[rows(r), :]
                delta = jnp.sum(jnp.where(hms[h], prod, 0.0), axis=1, keepdims=True)
                pbs[r, h] = p.astype(BF16)
                dsbs[r, h] = (p * (dps[r, h] - delta)).astype(BF16)
            dvs, dks = [None, None], [None, None]
            for r in live:
                dqs = []
                for h in range(2):
                    dqs.append(jnp.dot(dsbs[r, h], jnp.where(hms[h], kvv, one), preferred_element_type=F32))
                    dv = jnp.dot(pbs[r, h].T, dob[r], preferred_element_type=F32)
                    dk = jnp.dot(dsbs[r, h].T, jnp.where(hms[h], qv[r], one), preferred_element_type=F32)
                    dvs[h] = dv if dvs[h] is None else dvs[h] + dv
                    dks[h] = dk if dks[h] is None else dks[h] + dk
                qrows = pl.ds(pl.multiple_of(i * TQ + r * T, T), T)
                dq_ref[qrows, :] += jnp.where(head0, dqs[0], dqs[1])
                drs_ref[qrows, :] += jnp.where(head0, dqs[1], dqs[0])
            dv_ref[...] += jnp.where(head0, dvs[0], dvs[1])
            dk_ref[...] += jnp.where(head0, dks[0], dks[1])
            dcs_ref[...] += jnp.where(head0, dks[1], dks[0])

        @pl.when(dd < 0)
        def _():
            step(_chunk_kinds(QC, -1))

        for d in range(QC):
            @pl.when(dd == d)
            def _(d=d):
                step(_chunk_kinds(QC, d))

    by_q = lambda h, s, it, jt: (it[s], h)
    by_k = lambda h, s, it, jt: (jt[s], h)
    whole = lambda h, s, it, jt: (0, h)
    grid_spec = pltpu.PrefetchScalarGridSpec(
        num_scalar_prefetch=2, grid=(HP, it.shape[0]),
        in_specs=[pl.BlockSpec((TQ, LANES), by_q),
                  pl.BlockSpec((T, LANES), by_k),
                  pl.BlockSpec((T, LANES), lambda h, s, it, jt: (jt[s], HP + h)),
                  pl.BlockSpec((TQ, LANES), by_q),
                  pl.BlockSpec((TQ, LANES), by_q),
                  pl.BlockSpec((2, TQ, LANES), lambda h, s, it, jt: (h, it[s], 0)),
                  pl.BlockSpec((None, 8, T), lambda h, s, it, jt: (h, 0, jt[s]))],
        out_specs=[pl.BlockSpec((S, LANES), whole), pl.BlockSpec((S, LANES), whole),
                   pl.BlockSpec((T, LANES), by_k), pl.BlockSpec((T, LANES), by_k), pl.BlockSpec((T, LANES), by_k)],
        scratch_shapes=[],
    )
    return _pallas(
        body, name=name, grid_spec=grid_spec,
        out_shape=[_sds((S, D), F32)] * 5,
        compiler_params=_params(("parallel", "arbitrary")),
    )(it, jt, q, kv, kv, o, do, lse, ck)


ANY = pl.BlockSpec(memory_space=pl.ANY)


def _coords():
    x, y, c = lax.axis_index("x"), lax.axis_index("y"), lax.axis_index("c")
    return x, y, c


def _remote(src, dst, send_sems, recv_sems, k, to):
    return pltpu.make_async_remote_copy(src_ref=src, dst_ref=dst, send_sem=send_sems.at[k], recv_sem=recv_sems.at[k],
                                        device_id=to, device_id_type=MESH)


def _all_gather_shards(pack, *, name):
    R, C = pack.shape
    assert R % 4 == 0
    H, Q = R // 2, R // 4

    def body(in_ref, out_ref, send_sems, recv_sems):
        x, y, c = _coords()
        me, sib = (x, y, c), (x, y, 1 - c)
        xn, yn = (1 - x, y, c), (x, 1 - y, c)
        s, sx, sy, sd = 2 * x + y, 2 * (1 - x) + y, 2 * x + 1 - y, 2 * (1 - x) + 1 - y
        half = pl.ds(c * H, H)
        other = pl.ds((1 - c) * H, H)
        q0 = pl.ds(c * H, Q)
        q1 = pl.ds(c * H + Q, Q)
        rc = functools.partial(_remote, send_sems=send_sems, recv_sems=recv_sems)

        sends = [rc(in_ref.at[half], out_ref.at[s, half], k=0, to=xn),
                 rc(in_ref.at[half], out_ref.at[s, half], k=1, to=yn),
                 rc(in_ref, out_ref.at[s], k=7, to=sib)]
        for cp in sends:
            cp.start()
        rc(in_ref.at[half], out_ref.at[sx, half], k=0, to=me).wait_recv()
        sends.append(rc(out_ref.at[sx, q0], out_ref.at[sx, q0], k=2, to=yn))
        sends[-1].start()
        sends.append(rc(out_ref.at[sx, half], out_ref.at[sx, half], k=4, to=sib))
        sends[-1].start()
        rc(in_ref.at[half], out_ref.at[sy, half], k=1, to=me).wait_recv()
        sends.append(rc(out_ref.at[sy, q1], out_ref.at[sy, q1], k=3, to=xn))
        sends[-1].start()
        sends.append(rc(out_ref.at[sy, half], out_ref.at[sy, half], k=5, to=sib))
        sends[-1].start()
        rc(out_ref.at[sd, q0], out_ref.at[sd, q0], k=2, to=me).wait_recv()
        rc(out_ref.at[sd, q1], out_ref.at[sd, q1], k=3, to=me).wait_recv()
        sends.append(rc(out_ref.at[sd, half], out_ref.at[sd, half], k=6, to=sib))
        sends[-1].start()
        for k, sh in ((4, sx), (5, sy), (6, sd)):
            rc(out_ref.at[sh, other], out_ref.at[sh, other], k=k, to=me).wait_recv()
        rc(in_ref, out_ref.at[s], k=7, to=me).wait_recv()
        for cp in sends:
            cp.wait_send()

    return _pallas(
        body, name=name, in_specs=[ANY], out_specs=ANY,
        out_shape=_sds((N_CHIPS, R, C), pack.dtype),
        scratch_shapes=[pltpu.SemaphoreType.DMA((8,)), pltpu.SemaphoreType.DMA((8,))],
    )(pack)


def _rs_pair(g, *, name):
    n, R, C = g.shape
    H = R // 2

    def body(g_ref, land_ref, send_sems, recv_sems):
        x, y, c = _coords()
        other = pl.ds((1 - c) * H, H)
        cps = [_remote(g_ref.at[sh, other], land_ref.at[sh], send_sems, recv_sems, sh, (x, y, 1 - c)) for sh in range(n)]
        for cp in cps:
            cp.start()
        for cp in cps:
            cp.wait_recv()
        for cp in cps:
            cp.wait_send()

    return _pallas(
        body, name=name, in_specs=[ANY], out_specs=ANY, out_shape=_sds((n, H, C), g.dtype),
        scratch_shapes=[pltpu.SemaphoreType.DMA((n,)), pltpu.SemaphoreType.DMA((n,))],
    )(g)


def _rs_quarters(p, *, name):
    n, H, C = p.shape
    Q = H // 2

    def body(p_ref, la_ref, lb_ref, send_sems, recv_sems):
        x, y, c = _coords()
        sd = 2 * (1 - x) + 1 - y
        a = _remote(p_ref.at[sd, pl.ds(0, Q)], la_ref, send_sems, recv_sems, 0, (x, 1 - y, c))
        b = _remote(p_ref.at[sd, pl.ds(Q, Q)], lb_ref, send_sems, recv_sems, 1, (1 - x, y, c))
        a.start()
        b.start()
        a.wait_recv()
        b.wait_recv()
        a.wait_send()
        b.wait_send()

    return _pallas(
        body, name=name, in_specs=[ANY], out_specs=[ANY, ANY],
        out_shape=[_sds((Q, C), p.dtype), _sds((Q, C), p.dtype)],
        scratch_shapes=[pltpu.SemaphoreType.DMA((2,)), pltpu.SemaphoreType.DMA((2,))],
    )(p)


def _rs_halves(p, ax, ay, *, name):
    n, H, C = p.shape
    Q = H // 2

    def body(p_ref, ax_ref, ay_ref, la_ref, lb_ref, send_sems, recv_sems):
        x, y, c = _coords()
        sx, sy = 2 * (1 - x) + y, 2 * x + 1 - y
        xn, yn = (1 - x, y, c), (x, 1 - y, c)
        lo, hi = pl.ds(0, Q), pl.ds(Q, Q)
        cps = [_remote(ax_ref, la_ref.at[lo], send_sems, recv_sems, 0, xn),
               _remote(p_ref.at[sx, hi], la_ref.at[hi], send_sems, recv_sems, 1, xn),
               _remote(p_ref.at[sy, lo], lb_ref.at[lo], send_sems, recv_sems, 2, yn),
               _remote(ay_ref, lb_ref.at[hi], send_sems, recv_sems, 3, yn)]
        for cp in cps:
            cp.start()
        for cp in cps:
            cp.wait_recv()
        for cp in cps:
            cp.wait_send()

    return _pallas(
        body, name=name, in_specs=[ANY, ANY, ANY], out_specs=[ANY, ANY],
        out_shape=[_sds((H, C), p.dtype), _sds((H, C), p.dtype)],
        scratch_shapes=[pltpu.SemaphoreType.DMA((4,)), pltpu.SemaphoreType.DMA((4,))],
    )(p, ax, ay)


def _rs_join(buf, *, name):
    R, C = buf.shape
    H = R // 2

    def body(in_ref, out_ref, send_sems, recv_sems):
        x, y, c = _coords()
        half = pl.ds(c * H, H)
        other = pl.ds((1 - c) * H, H)
        cp = _remote(in_ref.at[half], out_ref.at[half], send_sems, recv_sems, 0, (x, y, 1 - c))
        cp.start()
        _remote(in_ref.at[other], out_ref.at[other], send_sems, recv_sems, 0, (x, y, c)).wait_recv()
        cp.wait_send()

    return _pallas(
        body, name=name, in_specs=[ANY], out_specs=ANY, out_shape=_sds((R, C), buf.dtype),
        input_output_aliases={0: 0},
        scratch_shapes=[pltpu.SemaphoreType.DMA((1,)), pltpu.SemaphoreType.DMA((1,))],
    )(buf)


def _tile_add(ins_specs, arrays, n_steps, out_spec, out_shape, scalars, *, name):
    grid_spec = pltpu.PrefetchScalarGridSpec(
        num_scalar_prefetch=1, grid=(n_steps,), in_specs=ins_specs, out_specs=out_spec, scratch_shapes=[])

    def body(sc_ref, *refs):
        acc = refs[0][...].astype(F32)
        for r in refs[1:-1]:
            acc = acc + r[...].astype(F32)
        refs[-1][...] = acc.astype(refs[-1].dtype)

    return _pallas(body, name=name, grid_spec=grid_spec, out_shape=out_shape,
                   compiler_params=_params(("arbitrary",)))(scalars, *arrays)


def _reduce_scatter(g, *, name):
    n, R, C = g.shape
    H, Q = R // 2, R // 4
    tm = RS_ROW_MULT // 4
    assert Q % tm == 0, (R, tm)
    x, y, c = _coords()
    sx, sy, s = 2 * (1 - x) + y, 2 * x + 1 - y, 2 * x + y
    sc = jnp.stack([c, sx, sy, s]).astype(jnp.int32)
    hb, qb = H // tm, Q // tm
    blk = lambda f: pl.BlockSpec((None, tm, C), f)
    flat = lambda f: pl.BlockSpec((tm, C), f)

    land = _rs_pair(g, name=name + "_pair")
    p = _tile_add([blk(lambda i, sc: (i // hb, sc[0] * hb + i % hb, 0)), blk(lambda i, sc: (i // hb, i % hb, 0))],
                  [g, land], n * hb, blk(lambda i, sc: (i // hb, i % hb, 0)), _sds((n, H, C), g.dtype), sc, name=name + "_add0")
    la, lb = _rs_quarters(p, name=name + "_quarters")
    ax = _tile_add([blk(lambda i, sc: (sc[1], i, 0)), flat(lambda i, sc: (i, 0))], [p, la], qb,
                   flat(lambda i, sc: (i, 0)), _sds((Q, C), g.dtype), sc, name=name + "_add1x")
    ay = _tile_add([blk(lambda i, sc: (sc[2], qb + i, 0)), flat(lambda i, sc: (i, 0))], [p, lb], qb,
                   flat(lambda i, sc: (i, 0)), _sds((Q, C), g.dtype), sc, name=name + "_add1y")
    fa, fb = _rs_halves(p, ax, ay, name=name + "_halves")
    buf = _tile_add([blk(lambda i, sc: (sc[3], i, 0)), flat(lambda i, sc: (i, 0)), flat(lambda i, sc: (i, 0))],
                    [p, fa, fb], hb, flat(lambda i, sc: (sc[0] * hb + i, 0)), _sds((R, C), F32), sc, name=name + "_add2")
    return _rs_join(buf, name=name + "_join")


def _all_reduce_small(v, *, name):
    M, N = v.shape

    def body(x_ref, out_ref, send_sems, recv_sems, local_sem):
        x, y, c = _coords()
        me, sibling = (x, y, c), (x, y, 1 - c)
        chips = [(1 - x, y), (x, 1 - y), (1 - x, 1 - y)]

        def rows(px, py, pc):
            return out_ref.at[pl.ds((4 * px + 2 * py + pc) * M, M), :]

        def copy(k, block, to, src=None):
            return pltpu.make_async_remote_copy(
                src_ref=rows(*block) if src is None else src, dst_ref=rows(*block),
                send_sem=send_sems.at[k], recv_sem=recv_sems.at[k], device_id=to, device_id_type=MESH)

        mine = pltpu.make_async_copy(x_ref, rows(*me), local_sem)
        mine.start()
        first = [copy(0, me, sibling, src=x_ref)]
        first += [copy(1 + j, me, (*chip, c), src=x_ref) for j, chip in enumerate(chips)]
        for cp in first:
            cp.start()
        passed = [copy(4 + j, (*chip, c), sibling) for j, chip in enumerate(chips)]
        for j, chip in enumerate(chips):
            copy(1 + j, (*chip, c), me).wait_recv()
            passed[j].start()
        copy(0, sibling, me).wait_recv()
        for j, chip in enumerate(chips):
            copy(4 + j, (*chip, 1 - c), me).wait_recv()
        for cp in first + passed:
            cp.wait_send()
        mine.wait()

    gathered = _pallas(
        body, name=name + "_gather",
        out_shape=_sds((N_DEV * M, N), F32),
        in_specs=[pl.BlockSpec(memory_space=pltpu.VMEM)],
        out_specs=pl.BlockSpec(memory_space=pltpu.VMEM),
        scratch_shapes=[pltpu.SemaphoreType.DMA((7,)), pltpu.SemaphoreType.DMA((7,)), pltpu.SemaphoreType.DMA],
    )(v)

    def sum_body(g_ref, o_ref):
        acc = g_ref[0:M, :]
        for d in range(1, N_DEV):
            acc = acc + g_ref[d * M:(d + 1) * M, :]
        o_ref[...] = acc

    return _pallas(sum_body, name=name + "_sum", out_shape=_sds((M, N), F32))(gathered)


MATS = [("ffn_w1", 2), ("ffn_w2", 1), ("ple_w_gate", 1), ("conv_w_pw1", 2), ("conv_w_pw2", 1), ("attn_w_q", 1), ("attn_w_o", 1),
        ("ple_w_proj", 2), ("w_kvf", 1)]
VECS = [("conv_b_pw1", 1), ("conv_w_dw", 2), ("conv_b_dw", 1), ("conv_ln_g", 1), ("conv_ln_b", 1), ("conv_b_pw2", 1)]
REPL = ["mix_norm", "ffn_norm", "ple_norm", "kv_norm", "final_norm", "b_f"]
WEIGHTS = ["mix_norm", "conv_w_pw1", "conv_b_pw1", "conv_w_dw", "conv_b_dw", "conv_ln_g", "conv_ln_b", "conv_w_pw2",
           "conv_b_pw2", "kv_norm", "w_kvf", "b_f", "attn_w_q", "attn_w_o", "ffn_norm", "ffn_w1", "ffn_w2", "ple_norm",
           "ple_w_gate", "ple_w_proj", "final_norm"]


def _round_up(n, m):
    return -(-n // m) * m


def _to_rows(t, C, mult):
    flat = t.reshape(-1)
    rows = _round_up(_round_up(flat.shape[0], C) // C, mult)
    flat = jnp.pad(flat, (0, rows * C - flat.shape[0]))
    return flat.reshape(rows, C)


def _pack(tensors, C, mult, total_mult):
    parts = [_to_rows(t, C, mult) for t in tensors]
    rows = sum(p.shape[0] for p in parts)
    pad = _round_up(rows, total_mult) - rows
    if pad:
        parts.append(jnp.zeros((pad, C), parts[0].dtype))
    return jnp.concatenate(parts, axis=0)


def _row_counts(shapes, C, mult):
    return [_round_up(_round_up(int(np.prod(s)), C) // C, mult) for s in shapes]


def _unpack(packed, shapes, C, mult):
    outs, r0 = [], 0
    lead = packed.shape[:-2]
    for shp, nr in zip(shapes, _row_counts(shapes, C, mult)):
        n = int(np.prod(shp))
        seg = packed[..., r0:r0 + nr, :].reshape(lead + (nr * C,))[..., :n]
        outs.append(seg.reshape(lead + tuple(shp)))
        r0 += nr
    return outs


def _unshard(t, axis):
    return jnp.concatenate([t[s] for s in range(N_CHIPS)], axis=axis)


def _shards(t, axis):
    return jnp.split(t, N_CHIPS, axis=axis)


class _GradPack:
    DIRECT = {"conv_w_pw2": "rows", "attn_w_q": "rows", "attn_w_o": "rows", "ffn_w2": "rows", "ple_w_gate": "rows",
              "ffn_w1": "cols"}

    def __init__(self, names, shard_shapes, C):
        self.C, self.names, self.shapes, self.off = C, names, dict(zip(names, shard_shapes)), {}
        r = 0
        for n, cnt in zip(names, _row_counts(shard_shapes, C, 16)):
            self.off[n] = r
            r += cnt
        self.R = _round_up(r, RS_ROW_MULT)
        self.buf = jnp.zeros((N_CHIPS, self.R, C), BF16)

    def matmul(self, wname, layer, a, b, **kw):
        rows_s, cols_s = self.shapes[wname][-2:]
        assert cols_s == self.C
        base = self.off[wname] + layer * rows_s
        if self.DIRECT[wname] == "rows" and N_CHIPS * rows_s <= 1024:
            assert base % rows_s == 0
            tm, block = N_CHIPS * rows_s, (N_CHIPS, rows_s, self.C)
            out_map = lambda i, j, k: (0, base // rows_s, 0)
        else:
            tm = next(t for t in (1024, 512, 256, 128, 64, 32, 16) if rows_s % t == 0 and base % t == 0)
            per, first, block = rows_s // tm, base // tm, (None, tm, self.C)
            if self.DIRECT[wname] == "rows":
                out_map = lambda i, j, k: (i // per, first + i % per, 0)
            else:
                assert a.shape[1] == rows_s
                out_map = lambda i, j, k: (j, first + i, 0)
        self.buf = _mm(a, b, mode="tn", tk=2048, tm=tm, tn=self.C, out_dtype=BF16, into=self.buf, out_block=block,
                       out_map=out_map, **kw)

    def insert(self, grads, axes):
        run = []
        for n in self.names + [None]:
            if n is not None and n not in self.DIRECT:
                run.append(n)
                continue
            if run:
                for s in range(N_CHIPS):
                    rows = jnp.concatenate([_to_rows(_shards(grads[m].astype(BF16), axes[m])[s], self.C, 16) for m in run], axis=0)
                    self.buf = lax.dynamic_update_slice(self.buf, rows[None], (s, self.off[run[0]], 0))
                run = []


def _local_step(x, p, tgt, w, gp=None):
    S, D = x.shape
    L = p.shape[0]
    NA = w["conv_w_pw1"].shape[0]
    H = w["b_f"].shape[0]
    HP = D // LANES
    row = lambda v: v.reshape(1, -1)
    act = lambda dt, n=D: _sds((S, n), dt)
    g = {}

    def dw(wname, layer, a, b, **kw):
        if gp is not None and wname in gp.DIRECT:
            gp.matmul(wname, layer, a, b, **kw)
            return None
        return _mm(a, b, mode="tn", tk=2048, out_dtype=BF16, **kw)

    def rms(hh, gain, name):
        return _rows(_rms_fwd, [hh], [row(gain)], [act(BF16)], name=name)[0]

    saved = []
    h = x
    kv = ck = fl = nkv = h_kv = None
    bfp = jnp.pad(w["b_f"], (0, LANES - H)).reshape(1, LANES)
    wk = w["w_kvf"][:, :D]
    wv = w["w_kvf"][:, D:2 * D]
    wkv = w["w_kvf"][:, :2 * D]
    wf = jnp.pad(w["w_kvf"][:, 2 * D:], ((0, 0), (0, LANES - H)))
    res_rms = lambda acc, r, gn: _with_rms(acc + r, gn)
    hn = rms(h, w["mix_norm"][0], "rms_mix")
    for i in range(L):
        sv = {"h0": h, "hn": hn}
        g_ffn, g_ple = row(w["ffn_norm"][i]), row(w["ple_norm"][i])
        if i < NA:
            u = _mm(hn, w["conv_w_pw1"][i], extras=[row(w["conv_b_pw1"][i])], epi=lambda acc, b: acc + b, name="mm_pw1")
            glu = _rows(_glu_fwd, [u], [], [act(F32)], name="glu_fwd")[0]
            wdw = jnp.pad(w["conv_w_dw"][i], ((0, CONV_PAD - CONV_WIDTH), (0, 0)))
            cv = _dwconv_fwd(glu, wdw, row(w["conv_b_dw"][i]), name="dwconv_fwd")
            sw = _rows(_ln_silu_fwd, [cv], [row(w["conv_ln_g"][i]), row(w["conv_ln_b"][i])], [act(BF16)], name="ln_silu_fwd")[0]
            h1, hn2 = _mm(sw, w["conv_w_pw2"][i], extras=[row(w["conv_b_pw2"][i]), h, g_ffn],
                          epi=lambda acc, b, r, gn: _with_rms(acc + b + r, gn), out_dtype=[F32, BF16], tm=512, tn=D, name="mm_pw2")
            sv.update(u=u, glu=glu, cv=cv, sw=sw, wdw=wdw)
        else:
            if i == NA:
                h_kv = h
                nkv = rms(h, w["kv_norm"], "rms_kv")
                kv = _mm(nkv, wkv, out_dtype=BF16, name="mm_kv")
                fl = _mm(nkv, wf, name="mm_f")
                c = _gate_cumsum(fl, bfp, name="gate_cumsum")
                ck = jnp.pad(c[:, :H].T.reshape(HP, 2, S), ((0, 0), (0, 6), (0, 0)))
            j = i - NA
            q = _mm(hn, w["attn_w_q"][j], epi=lambda acc: acc * (HEAD_DIM ** -0.5), out_dtype=BF16, name="mm_q")
            o, lse = _attn_fwd(q, kv, ck, name="attn_fwd")
            h1, hn2 = _mm(o, w["attn_w_o"][j], extras=[h, g_ffn], epi=res_rms, out_dtype=[F32, BF16], tm=512, tn=D, name="mm_o")
            sv.update(q=q, o=o, lse=lse)
        zb, f = _mm(hn2, w["ffn_w1"][i], epi=lambda acc: (acc, jnp.square(jnp.maximum(acc, 0.0))), out_dtype=[BF16, BF16],
                    name="mm_ffn1")
        h2, n3 = _mm(f, w["ffn_w2"][i], extras=[h1, g_ple], epi=res_rms, out_dtype=[F32, BF16], tm=512, tn=D, tk=4 * D, name="mm_ffn2")
        zg = _mm(n3, w["ple_w_gate"][i], name="mm_gate")
        ple = lambda acc, r, zz: r + _sigmoid(zz) * acc
        if i + 1 < L:
            h, hn = _mm(p[i], w["ple_w_proj"][i], extras=[h2, zg, row(w["mix_norm"][i + 1])],
                        epi=lambda acc, r, zz, gn: _with_rms(ple(acc, r, zz), gn), out_dtype=[F32, BF16], tm=512, tn=D, name="mm_proj")
        else:
            h = _mm(p[i], w["ple_w_proj"][i], extras=[h2, zg], epi=ple, tm=512, tn=D, name="mm_proj_last")
        sv.update(h1=h1, hn2=hn2, zb=zb, f=f, h2=h2, n3=n3, zg=zg)
        saved.append(sv)

    dh, err2, g_final = _rows(_final_fn, [h, tgt], [row(w["final_norm"])], [act(F32)], [_sds((1, D), F32), _sds((1, D), F32)],
                              name="final")
    loss = 0.5 * jnp.sum(err2) / D
    g["final_norm"] = g_final.reshape(-1)

    red = _sds((1, D), F32)
    stack = {k: [None] * n for k, n in (("mix_norm", L), ("ffn_norm", L), ("ple_norm", L), ("ffn_w1", L), ("ffn_w2", L),
                                        ("ple_w_gate", L), ("ple_w_proj", L), ("conv_w_pw1", NA), ("conv_b_pw1", NA),
                                        ("conv_w_dw", NA), ("conv_b_dw", NA), ("conv_ln_g", NA), ("conv_ln_b", NA),
                                        ("conv_w_pw2", NA), ("conv_b_pw2", NA), ("attn_w_q", L - NA), ("attn_w_o", L - NA))}
    dk_sum = dv_sum = None
    dcks = []
    for i in reversed(range(L)):
        sv = saved[i]
        dzg, dpp = _mm(p[i], w["ple_w_proj"][i], extras=[dh, sv["zg"]], epi=lambda acc, d, zz: _ple_bwd(d, zz, acc),
                       out_dtype=[BF16, BF16], name="mm_ple_bwd")
        stack["ple_w_proj"][i] = _mm(p[i], dpp, mode="tn", tk=2048, out_dtype=BF16, name="mm_dproj")
        stack["ple_w_gate"][i] = dw("ple_w_gate", i, sv["n3"], dzg, name="mm_dgate")
        dh, dhb, dgain = _mm(dzg, w["ple_w_gate"][i], mode="nt", extras=[sv["h2"], dh, row(w["ple_norm"][i])], epi=_dup(_rms_bwd),
                             out_dtype=[F32, BF16], reds=1, tm=512, tn=D, name="mm_dn3")
        stack["ple_norm"][i] = dgain.reshape(-1)
        dz = _mm(dhb, w["ffn_w2"][i], mode="nt", extras=[sv["zb"]], epi=lambda acc, zz: acc * (2.0 * jnp.maximum(zz, 0.0).astype(F32)),
                 out_dtype=BF16, name="mm_dz")
        stack["ffn_w2"][i] = dw("ffn_w2", i, sv["f"], dhb, name="mm_dffn2")
        stack["ffn_w1"][i] = dw("ffn_w1", i, sv["hn2"], dz, name="mm_dffn1")
        if i < NA:
            dh, dhb, dgain, dbias = _mm(dz, w["ffn_w1"][i], mode="nt", extras=[sv["h1"], dh, row(w["ffn_norm"][i])],
                                        epi=_dup(_rms_bwd_bias), out_dtype=[F32, BF16], reds=2, tm=512, tn=D, tk=4 * D, name="mm_dhn2_bias")
            stack["conv_b_pw2"][i] = dbias.reshape(-1)
        else:
            dh, dhb, dgain = _mm(dz, w["ffn_w1"][i], mode="nt", extras=[sv["h1"], dh, row(w["ffn_norm"][i])], epi=_dup(_rms_bwd),
                                 out_dtype=[F32, BF16], reds=1, tm=512, tn=D, tk=4 * D, name="mm_dhn2")
        stack["ffn_norm"][i] = dgain.reshape(-1)
        if i < NA:
            dsw = _mm(dhb, w["conv_w_pw2"][i], mode="nt", name="mm_dsw")
            stack["conv_w_pw2"][i] = dw("conv_w_pw2", i, sv["sw"], dhb, name="mm_dpw2")
            dcv, dlg, dlb = _rows(_ln_silu_bwd, [dsw, sv["cv"]], [row(w["conv_ln_g"][i]), row(w["conv_ln_b"][i])], [act(F32)],
                                  [red, red], name="ln_silu_bwd")
            stack["conv_ln_g"][i], stack["conv_ln_b"][i] = dlg.reshape(-1), dlb.reshape(-1)
            dglu, dwdw, dbdw = _dwconv_bwd(dcv, sv["glu"], sv["wdw"], name="dwconv_bwd")
            stack["conv_w_dw"][i], stack["conv_b_dw"][i] = dwdw[:CONV_WIDTH], dbdw.reshape(-1)
            du, dbu = _rows(_glu_bwd, [dglu, sv["u"]], [], [act(BF16, 2 * D)], [_sds((1, 2 * D), F32)], name="glu_bwd")
            stack["conv_b_pw1"][i] = dbu.reshape(-1)
            stack["conv_w_pw1"][i] = _mm(sv["hn"], du, mode="tn", tk=2048, out_dtype=BF16, name="mm_dpw1")
            dh, dgain = _mm(du, w["conv_w_pw1"][i], mode="nt", extras=[sv["h0"], dh, row(w["mix_norm"][i])], epi=_rms_bwd,
                            reds=1, tm=512, tn=D, name="mm_dhn_a")
        else:
            j = i - NA
            do = _mm(dhb, w["attn_w_o"][j], mode="nt", out_dtype=BF16, name="mm_do")
            stack["attn_w_o"][j] = dw("attn_w_o", j, sv["o"], dhb, name="mm_dwo")
            dq, drs, dk, dv, dcs = _attn_bwd(sv["q"], kv, sv["o"], do, sv["lse"], ck, name="attn_bwd")
            scale = lambda acc: acc * (HEAD_DIM ** -0.5)
            stack["attn_w_q"][j] = dw("attn_w_q", j, sv["hn"], dq, epi=scale, name="mm_dwq")
            dh_in = dh
            dh, dgain = _mm(dq, w["attn_w_q"][j], mode="nt", extras=[sv["h0"], dh_in, row(w["mix_norm"][i])],
                            epi=lambda acc, xx, dr, gn: _rms_bwd(scale(acc), xx, dr, gn), reds=1, tm=512, tn=D, name="mm_dhn_b")
            pick = lambda t: jnp.pad(t.reshape(S, HP, 2, HEAD_DIM)[:, :, ::-1, 0].reshape(S, H), ((0, 0), (0, LANES - H)))
            dcks += [pick(drs), pick(dcs)]
            if dk_sum is None:
                dk_sum, dv_sum = dk, dv
            else:
                dk_sum = _rows(_add2, [dk_sum, dk], [], [act(F32)], name="add_dk")[0]
                dv_sum = _rows(_add2, [dv_sum, dv], [], [act(F32)], name="add_dk")[0]
        stack["mix_norm"][i] = dgain.reshape(-1)
        if i == NA:
            dfl, dbf = _gate_cumsum_bwd(dcks, fl, bfp, name="gate_cumsum_bwd")
            g["b_f"] = dbf[0, :H]
            gk = _mm(nkv, dk_sum, mode="tn", tk=2048, out_dtype=BF16, name="mm_dwk")
            gv = _mm(nkv, dv_sum, mode="tn", tk=2048, out_dtype=BF16, name="mm_dwk")
            gf = _mm(nkv, dfl, mode="tn", tk=2048, out_dtype=BF16, name="mm_dwf")
            g["w_kvf"] = jnp.concatenate([gk, gv, gf[:, :H]], axis=1)
            dn = _mm(dk_sum, wk, mode="nt", name="mm_dnk")
            dn = _mm(dv_sum, wv, mode="nt", extras=[dn], epi=lambda acc, r: acc + r, name="mm_dnv")
            dh, dgain = _mm(dfl, wf, mode="nt", extras=[dn, h_kv, dh, row(w["kv_norm"])],
                            epi=lambda acc, r, xx, dr, gn: _rms_bwd(acc + r, xx, dr, gn), reds=1, tm=512, tn=D, name="mm_dnf")
            g["kv_norm"] = dgain.reshape(-1)
    for k, v in stack.items():
        if v[0] is not None:
            g[k] = jnp.stack(v, axis=0)
    return loss, dh, g


def kernel(x, p, mix_norm, conv_w_pw1, conv_b_pw1, conv_w_dw, conv_b_dw, conv_ln_g, conv_ln_b, conv_w_pw2, conv_b_pw2, kv_norm, w_kvf, b_f, attn_w_q, attn_w_o, ffn_norm, ffn_w1, ffn_w2, ple_norm, ple_w_gate, ple_w_proj, final_norm, loss_target, m_mix_norm, m_conv_w_pw1, m_conv_b_pw1, m_conv_w_dw, m_conv_b_dw, m_conv_ln_g, m_conv_ln_b, m_conv_w_pw2, m_conv_b_pw2, m_kv_norm, m_w_kvf, m_b_f, m_attn_w_q, m_attn_w_o, m_ffn_norm, m_ffn_w1, m_ffn_w2, m_ple_norm, m_ple_w_gate, m_ple_w_proj, m_final_norm, v_mix_norm, v_conv_w_pw1, v_conv_b_pw1, v_conv_w_dw, v_conv_b_dw, v_conv_ln_g, v_conv_ln_b, v_conv_w_pw2, v_conv_b_pw2, v_kv_norm, v_w_kvf, v_b_f, v_attn_w_q, v_attn_w_o, v_ffn_norm, v_ffn_w1, v_ffn_w2, v_ple_norm, v_ple_w_gate, v_ple_w_proj, v_final_norm):
    args = dict(locals())
    wl = {n: args[n] for n in WEIGHTS}
    ml = {n: args["m_" + n] for n in WEIGHTS}
    vl = {n: args["v_" + n] for n in WEIGHTS}
    S, D = x.shape[1], x.shape[2]
    C = D

    mat_shapes = [wl[n].shape for n, _ in MATS]
    vec_shapes = [wl[n].shape for n, _ in VECS]
    mats = _all_gather_shards(_pack([wl[n].astype(BF16) for n, _ in MATS], C, 16, 64), name="ag_mats")
    vecs = _all_gather_shards(_pack([wl[n] for n, _ in VECS], C, 1, 32), name="ag_vecs")
    full = {n: wl[n] for n in REPL}
    for (n, ax), t in zip(MATS, _unpack(mats, mat_shapes, C, 16)):
        full[n] = _unshard(t, ax)
    for (n, ax), t in zip(VECS, _unpack(vecs, vec_shapes, C, 1)):
        full[n] = _unshard(t, ax)

    names = [n for n, _ in MATS] + [n for n, _ in VECS]
    axes = dict(MATS + VECS)
    shard_shapes = [wl[n].shape for n in names]
    gp = _GradPack(names, shard_shapes, C)
    loss, dx, g = _local_step(x[0], p[:, 0], loss_target[0], full, gp)
    loss = lax.psum(loss, ("x", "y", "c"))

    gp.insert(g, axes)
    gred = _reduce_scatter(gp.buf, name="rs")
    gl = dict(zip(names, _unpack(gred, shard_shapes, C, 16)))

    rep_shapes = [wl[n].shape for n in REPL]
    rpack = _pack([g[n] for n in REPL], C, 1, 8)
    for n, t in zip(REPL, _unpack(_all_reduce_small(rpack, name="ar"), rep_shapes, C, 1)):
        gl[n] = t

    grads, deltas, new_m, new_v = {}, {}, {}, {}
    rep_w, rep_m, rep_v = (_pack([d[n] for n in REPL], C, 1, 8) for d in (wl, ml, vl))
    rep_out = _adamw(rep_w, rpack_like(gl, rep_shapes, C), rep_m, rep_v, "adamw_rep")
    for dst, packed in zip((grads, deltas, new_m, new_v), rep_out):
        for n, t in zip(REPL, _unpack(packed, rep_shapes, C, 1)):
            dst[n] = t
    for n in names:
        res = _adamw(wl[n], gl[n], ml[n], vl[n], "adamw_" + n)
        for dst, t in zip((grads, deltas, new_m, new_v), res):
            dst[n] = t
    out = [loss, dx[None]]
    for d in (grads, deltas, new_m, new_v):
        out += [d[n] for n in WEIGHTS]
    return tuple(out)


def rpack_like(gl, rep_shapes, C):
    return _pack([gl[n] for n in REPL], C, 1, 8)
```

```python
import functools

import jax
import jax.numpy as jnp
import numpy as np
from jax import lax
from jax.experimental import pallas as pl
from jax.experimental.pallas import tpu as pltpu

F32 = jnp.float32
BF16 = jnp.bfloat16
MESH = pl.DeviceIdType.MESH

N_CHIPS = 4
N_DEV = 8
HEAD_DIM = 64
LANES = 128
CONV_WIDTH = 31
CONV_PAD = 32
EPS = 1e-6
NEG_BIG = -1e30
VMEM_LIMIT = 56 * 1024 * 1024

RS_ROW_MULT = 1024

ADAM_LR, ADAM_B1, ADAM_B2, ADAM_EPS, ADAM_WD, ADAM_STEP = 0.001, 0.9, 0.999, 1e-08, 0.01, 10


def _pallas(body, **kw):
    return pl.pallas_call(body, **kw)


def _params(sem=None):
    return pltpu.CompilerParams(dimension_semantics=sem, vmem_limit_bytes=VMEM_LIMIT)


def _sds(shape, dtype):
    return jax.ShapeDtypeStruct(tuple(shape), dtype)


_DIMS = {"nn": (((1,), (0,)), ((), ())), "nt": (((1,), (1,)), ((), ())), "tn": (((0,), (0,)), ((), ()))}


def _mm(a, b, *, mode="nn", extras=(), epi=None, out_dtype=F32, reds=0, tm=1024, tn=1024, tk=1024, into=None, out_block=None,
        out_map=None, name):
    if mode == "nn":
        (M, K), (K2, N) = a.shape, b.shape
    elif mode == "nt":
        (M, K), (N, K2) = a.shape, b.shape
    else:
        (K, M), (K2, N) = a.shape, b.shape
    assert K == K2, (name, a.shape, b.shape)
    tm, tn, tk = min(tm, M), min(tn, N), min(tk, K)
    assert M % tm == 0 and N % tn == 0 and K % tk == 0, (name, a.shape, b.shape)
    nk = K // tk
    if mode == "tn":
        a_spec = pl.BlockSpec((tk, tm), lambda i, j, k: (k, i))
    else:
        a_spec = pl.BlockSpec((tm, tk), lambda i, j, k: (i, k))
    if mode == "nt":
        b_spec = pl.BlockSpec((tn, tk), lambda i, j, k: (j, k))
    else:
        b_spec = pl.BlockSpec((tk, tn), lambda i, j, k: (k, j))
    ex_specs = []
    for e in extras:
        if e.shape[0] == 1:
            ex_specs.append(pl.BlockSpec((1, tn), lambda i, j, k: (0, j)))
        else:
            assert e.shape == (M, N), (name, e.shape)
            ex_specs.append(pl.BlockSpec((tm, tn), lambda i, j, k: (i, j)))
    ne = len(extras)
    dims = _DIMS[mode]
    many = isinstance(out_dtype, (list, tuple))
    out_dtypes = list(out_dtype) if many else [out_dtype]
    no = len(out_dtypes)
    assert not reds or tn == N, name

    def body(a_ref, b_ref, *rest):
        ex_refs, o_refs, r_refs = rest[:ne], rest[ne:ne + no], rest[ne + no:ne + no + reds]
        part = lax.dot_general(a_ref[...].astype(BF16), b_ref[...].astype(BF16), dims, preferred_element_type=F32)
        i = pl.program_id(0)

        def finish(acc):
            res = epi(acc, *[r[...] for r in ex_refs]) if epi is not None else acc
            if not isinstance(res, (tuple, list)):
                res = (res,)
            assert len(res) == no + reds, (name, len(res))
            for r, v in zip(o_refs, res[:no]):
                r[...] = v.astype(r.dtype).reshape(r.shape)
            for r, v in zip(r_refs, res[no:]):
                @pl.when(i == 0)
                def _(r=r, v=v):
                    r[...] = v

                @pl.when(i > 0)
                def _(r=r, v=v):
                    r[...] += v

        if nk == 1:
            finish(part)
        else:
            acc_ref = rest[ne + no + reds]
            k = pl.program_id(2)

            @pl.when(k == 0)
            def _():
                acc_ref[...] = part

            @pl.when(k > 0)
            def _():
                acc_ref[...] += part

            @pl.when(k == nk - 1)
            def _():
                finish(acc_ref[...])

    scratch = [pltpu.VMEM((tm, tn), F32)] if nk > 1 else []
    if into is not None:
        assert not many and not reds and not extras and into.dtype == out_dtype, name

        def body_into(a_ref, b_ref, into_ref, *rest):
            body(a_ref, b_ref, *rest)

        return _pallas(
            body_into, name=name, grid=(M // tm, N // tn, nk),
            in_specs=[a_spec, b_spec, pl.BlockSpec(memory_space=pl.ANY)],
            out_specs=pl.BlockSpec(out_block, out_map),
            out_shape=_sds(into.shape, into.dtype),
            input_output_aliases={2: 0},
            scratch_shapes=scratch,
            compiler_params=_params(("parallel", "parallel", "arbitrary")),
        )(a, b, into)
    res = _pallas(
        body, name=name, grid=(M // tm, N // tn, nk),
        in_specs=[a_spec, b_spec] + ex_specs,
        out_specs=[pl.BlockSpec((tm, tn), lambda i, j, k: (i, j))] * no + [pl.BlockSpec((1, tn), lambda i, j, k: (0, j))] * reds,
        out_shape=[_sds((M, N), dt) for dt in out_dtypes] + [_sds((1, N), F32)] * reds,
        scratch_shapes=scratch,
        compiler_params=_params(("arbitrary",) * 3 if reds else ("parallel", "parallel", "arbitrary")),
    )(a, b, *extras)
    return res if (many or reds) else res[0]


def _rows(fn, ins, params, outs, reds=(), *, tm=256, row_offsets=None, name):
    S = outs[0].shape[0] if outs else ins[0].shape[0]
    tm = min(tm, S)
    assert S % tm == 0, (name, S, tm)
    ni, npar, no, nr = len(ins), len(params), len(outs), len(reds)
    offs = list(row_offsets) if row_offsets is not None else [0] * ni
    assert all(o % tm == 0 for o in offs), (name, offs, tm)

    def body(*refs):
        in_refs, p_refs = refs[:ni], refs[ni:ni + npar]
        o_refs, r_refs = refs[ni + npar:ni + npar + no], refs[ni + npar + no:]
        res = fn(*[r[...] for r in in_refs], *[r[...] for r in p_refs])
        if not isinstance(res, (tuple, list)):
            res = (res,)
        assert len(res) == no + nr, (name, len(res))
        for r, v in zip(o_refs, res[:no]):
            r[...] = v.astype(r.dtype)
        i = pl.program_id(0)
        for r, v in zip(r_refs, res[no:]):
            @pl.when(i == 0)
            def _(r=r, v=v):
                r[...] = v

            @pl.when(i > 0)
            def _(r=r, v=v):
                r[...] += v

    res = _pallas(
        body, name=name, grid=(S // tm,),
        in_specs=[pl.BlockSpec((tm, a.shape[1]), lambda i, o=o: (o // tm + i, 0)) for a, o in zip(ins, offs)]
        + [pl.BlockSpec(p.shape, lambda i: (0, 0)) for p in params],
        out_specs=[pl.BlockSpec((tm, o.shape[1]), lambda i: (i, 0)) for o in outs]
        + [pl.BlockSpec(r.shape, lambda i: (0, 0)) for r in reds],
        out_shape=list(outs) + list(reds),
        compiler_params=_params(("arbitrary",)),
    )(*ins, *params)
    return res


def _colsum(v):
    return jnp.sum(v, axis=0, keepdims=True)


def _sigmoid(v):
    return 1.0 / (1.0 + jnp.exp(-v))


def _rms_stats(x):
    r = lax.rsqrt(jnp.mean(x * x, axis=-1, keepdims=True) + EPS)
    return x * r, r


def _rms_fwd(x, g):
    xh, _ = _rms_stats(x)
    return (xh * g,)


def _with_rms(h, g):
    xh, _ = _rms_stats(h)
    return h, xh * g


def _rms_bwd(dy, x, dres, g):
    xh, r = _rms_stats(x)
    dyg = dy * g
    dx = r * (dyg - xh * jnp.mean(dyg * xh, axis=-1, keepdims=True))
    return dres + dx, _colsum(dy * xh)


def _dup(fn):
    def wrapped(*a):
        r = fn(*a)
        return (r[0], r[0]) + tuple(r[1:])
    return wrapped


def _rms_bwd_bias(dy, x, dres, g):
    dx, dg = _rms_bwd(dy, x, dres, g)
    return dx, dg, _colsum(dx)


def _glu_fwd(u):
    d = u.shape[1] // 2
    return (u[:, :d] * _sigmoid(u[:, d:]),)


def _glu_bwd(dglu, u):
    d = u.shape[1] // 2
    a, sig = u[:, :d], _sigmoid(u[:, d:])
    du = jnp.concatenate([dglu * sig, dglu * a * sig * (1.0 - sig)], axis=1)
    return du, _colsum(du)


def _ln_parts(x, g, b):
    mu = jnp.mean(x, axis=-1, keepdims=True)
    xc = x - mu
    r = lax.rsqrt(jnp.mean(xc * xc, axis=-1, keepdims=True) + EPS)
    xh = xc * r
    return xh, r, xh * g + b


def _ln_silu_fwd(x, g, b):
    _, _, y = _ln_parts(x, g, b)
    return (y * _sigmoid(y),)


def _ln_silu_bwd(dsw, x, g, b):
    xh, r, y = _ln_parts(x, g, b)
    sig = _sigmoid(y)
    dy = dsw * sig * (1.0 + y * (1.0 - sig))
    dxh = dy * g
    dx = r * (dxh - jnp.mean(dxh, axis=-1, keepdims=True) - xh * jnp.mean(dxh * xh, axis=-1, keepdims=True))
    return dx, _colsum(dy * xh), _colsum(dy)


def _ple_bwd(dh, zg, pp):
    gate = _sigmoid(zg)
    return dh * pp * gate * (1.0 - gate), dh * gate


def _final_fn(h, t, g):
    xh, r = _rms_stats(h)
    err = xh * g - t
    dy = err * (1.0 / h.shape[1])
    dyg = dy * g
    dh = r * (dyg - xh * jnp.mean(dyg * xh, axis=-1, keepdims=True))
    return dh, _colsum(err * err), _colsum(dy * xh)


def _adamw_fn(w, g, m, v):
    m = ADAM_B1 * m + (1.0 - ADAM_B1) * g
    v = ADAM_B2 * v + (1.0 - ADAM_B2) * (g * g)
    m_hat = m / (1.0 - ADAM_B1 ** ADAM_STEP)
    v_hat = v / (1.0 - ADAM_B2 ** ADAM_STEP)
    delta = -ADAM_LR * (m_hat / (jnp.sqrt(v_hat) + ADAM_EPS) + ADAM_WD * w)
    return g, delta, m, v


def _adamw(w, g, m, v, name, g_row=None):
    shape = w.shape
    cols = shape[-1] if len(shape) > 1 else shape[0]
    two = lambda t: t.reshape(-1, cols)
    o = _sds(two(w).shape, F32)
    tm = min(512, o.shape[0])
    assert g_row is None or g.shape[1] == cols, name
    res = _rows(_adamw_fn, [two(w), g if g_row is not None else two(g), two(m), two(v)], [], [o, o, o, o], tm=tm,
                row_offsets=None if g_row is None else [0, g_row, 0, 0], name=name)
    return [r.reshape(shape) for r in res]


def _sublane_shifts(win):
    n = win.shape[0]
    return [win] + [pltpu.roll(win, n - b, axis=0) for b in range(1, 8)]


def _tap(shifted, offset, rows):
    a, b = divmod(offset, 8)
    return shifted[b][8 * a:8 * a + rows]


def _dwconv_fwd(u, w, b, *, tm=512, name):
    S, D = u.shape
    tm = min(tm, S)
    rc = min(128, tm)
    per = tm // CONV_PAD

    def body(prev_ref, cur_ref, w_ref, b_ref, o_ref, win):
        i = pl.program_id(0)

        @pl.when(i == 0)
        def _():
            win[0:CONV_PAD, :] = jnp.zeros((CONV_PAD, D), F32)

        @pl.when(i > 0)
        def _():
            win[0:CONV_PAD, :] = prev_ref[...]

        win[CONV_PAD:CONV_PAD + tm, :] = cur_ref[...]
        for lc in range(D // LANES):
            ls = slice(lc * LANES, (lc + 1) * LANES)
            for r0 in range(0, tm, rc):
                shifted = _sublane_shifts(win[r0:r0 + rc + CONV_PAD, ls])
                acc = jnp.zeros((rc, LANES), F32) + b_ref[:, ls]
                for k in range(CONV_WIDTH):
                    acc = acc + _tap(shifted, 2 + k, rc) * w_ref[k:k + 1, ls]
                o_ref[r0:r0 + rc, ls] = acc

    return _pallas(
        body, name=name, grid=(S // tm,),
        in_specs=[pl.BlockSpec((CONV_PAD, D), lambda i: (jnp.maximum(i * per - 1, 0), 0)),
                  pl.BlockSpec((tm, D), lambda i: (i, 0)),
                  pl.BlockSpec((CONV_PAD, D), lambda i: (0, 0)),
                  pl.BlockSpec((1, D), lambda i: (0, 0))],
        out_specs=pl.BlockSpec((tm, D), lambda i: (i, 0)),
        out_shape=_sds((S, D), F32),
        scratch_shapes=[pltpu.VMEM((tm + CONV_PAD, D), F32)],
        compiler_params=_params(("arbitrary",)),
    )(u, u, w, b)


def _dwconv_bwd(dy, u, w, *, tm=512, name):
    S, D = u.shape
    tm = min(tm, S)
    rc = min(128, tm)
    per = tm // CONV_PAD
    n = S // tm
    nxt = S // CONV_PAD - 1

    def body(dy_ref, dyn_ref, up_ref, u_ref, w_ref, du_ref, dw_ref, db_ref, wd, wu, dwacc, dbacc):
        i = pl.program_id(0)

        @pl.when(i == 0)
        def _():
            wu[0:CONV_PAD, :] = jnp.zeros((CONV_PAD, D), F32)
            dwacc[...] = jnp.zeros(dwacc.shape, F32)
            dbacc[...] = jnp.zeros(dbacc.shape, F32)

        @pl.when(i > 0)
        def _():
            wu[0:CONV_PAD, :] = up_ref[...]

        @pl.when(i == n - 1)
        def _():
            wd[tm:tm + CONV_PAD, :] = jnp.zeros((CONV_PAD, D), F32)

        @pl.when(i < n - 1)
        def _():
            wd[tm:tm + CONV_PAD, :] = dyn_ref[...]

        wu[CONV_PAD:CONV_PAD + tm, :] = u_ref[...]
        wd[0:tm, :] = dy_ref[...]
        for lc in range(D // LANES):
            ls = slice(lc * LANES, (lc + 1) * LANES)
            for r0 in range(0, tm, rc):
                sd = _sublane_shifts(wd[r0:r0 + rc + CONV_PAD, ls])
                acc = jnp.zeros((rc, LANES), F32)
                for k in range(CONV_WIDTH):
                    acc = acc + _tap(sd, 30 - k, rc) * w_ref[k:k + 1, ls]
                du_ref[r0:r0 + rc, ls] = acc
                dyc = wd[r0:r0 + rc, ls]
                dbacc[:, ls] += jnp.sum(dyc.reshape(rc // 8, 8, LANES), axis=0)
                for k in range(CONV_WIDTH):
                    prod = dyc * wu[r0 + 2 + k:r0 + 2 + k + rc, ls]
                    dwacc[8 * k:8 * k + 8, ls] += jnp.sum(prod.reshape(rc // 8, 8, LANES), axis=0)

        @pl.when(i == n - 1)
        def _():
            dw_ref[...] = jnp.zeros(dw_ref.shape, F32)
            for k in range(CONV_WIDTH):
                dw_ref[k:k + 1, :] = jnp.sum(dwacc[8 * k:8 * k + 8, :], axis=0, keepdims=True)
            db_ref[...] = jnp.sum(dbacc[...], axis=0, keepdims=True)

    return _pallas(
        body, name=name, grid=(n,),
        in_specs=[pl.BlockSpec((tm, D), lambda i: (i, 0)),
                  pl.BlockSpec((CONV_PAD, D), lambda i: (jnp.minimum((i + 1) * per, nxt), 0)),
                  pl.BlockSpec((CONV_PAD, D), lambda i: (jnp.maximum(i * per - 1, 0), 0)),
                  pl.BlockSpec((tm, D), lambda i: (i, 0)),
                  pl.BlockSpec((CONV_PAD, D), lambda i: (0, 0))],
        out_specs=[pl.BlockSpec((tm, D), lambda i: (i, 0)),
                   pl.BlockSpec((CONV_PAD, D), lambda i: (0, 0)),
                   pl.BlockSpec((1, D), lambda i: (0, 0))],
        out_shape=[_sds((S, D), F32), _sds((CONV_PAD, D), F32), _sds((1, D), F32)],
        scratch_shapes=[pltpu.VMEM((tm + CONV_PAD, D), F32), pltpu.VMEM((tm + CONV_PAD, D), F32),
                        pltpu.VMEM((8 * CONV_PAD, D), F32), pltpu.VMEM((8, D), F32)],
        compiler_params=_params(("arbitrary",)),
    )(dy, dy, u, u, w)


def _tri_dot(tri, x):
    x1 = x.astype(BF16)
    r1 = x - x1.astype(F32)
    x2 = r1.astype(BF16)
    x3 = (r1 - x2.astype(F32)).astype(BF16)
    d = lambda v: jnp.dot(tri, v, preferred_element_type=F32)
    return d(x1) + d(x2) + d(x3)


def _log_sigmoid(x):
    return jnp.minimum(x, 0.0) - jnp.log(1.0 + jnp.exp(-jnp.abs(x)))


def _gate_cumsum(fl, bf, *, tm=256, name):
    S, W = fl.shape
    tm = min(tm, S)

    def body(fl_ref, bf_ref, c_ref, carry):
        i = pl.program_id(0)

        @pl.when(i == 0)
        def _():
            carry[...] = jnp.zeros(carry.shape, F32)

        x = _log_sigmoid(fl_ref[...] + bf_ref[...])
        row = lax.broadcasted_iota(jnp.int32, (tm, tm), 0)
        col = lax.broadcasted_iota(jnp.int32, (tm, tm), 1)
        tri = jnp.where(row >= col, 1.0, 0.0).astype(BF16)
        cs = _tri_dot(tri, x) + carry[0:1, :]
        c_ref[...] = cs
        carry[...] = jnp.broadcast_to(cs[tm - 1:tm, :], carry.shape)

    return _pallas(
        body, name=name, grid=(S // tm,),
        in_specs=[pl.BlockSpec((tm, W), lambda i: (i, 0)), pl.BlockSpec((1, W), lambda i: (0, 0))],
        out_specs=pl.BlockSpec((tm, W), lambda i: (i, 0)),
        out_shape=_sds((S, W), F32),
        scratch_shapes=[pltpu.VMEM((8, W), F32)],
        compiler_params=_params(("arbitrary",)),
    )(fl, bf)


def _gate_cumsum_bwd(sums, fl, bf, *, tm=256, name):
    S, W = fl.shape
    tm = min(tm, S)
    n = S // tm
    ns = len(sums)
    assert ns % 2 == 0

    def body(*refs):
        sum_refs = refs[:ns]
        fl_ref, bf_ref, o_ref, s_ref, carry = refs[ns:]
        i = pl.program_id(0)

        @pl.when(i == 0)
        def _():
            carry[...] = jnp.zeros(carry.shape, F32)
            s_ref[...] = jnp.zeros(s_ref.shape, F32)

        dc = sum_refs[0][...] - sum_refs[1][...]
        for a in range(2, ns, 2):
            dc = dc + (sum_refs[a][...] - sum_refs[a + 1][...])
        row = lax.broadcasted_iota(jnp.int32, (tm, tm), 0)
        col = lax.broadcasted_iota(jnp.int32, (tm, tm), 1)
        tri = jnp.where(col >= row, 1.0, 0.0).astype(BF16)
        rs = _tri_dot(tri, dc) + carry[0:1, :]
        carry[...] = jnp.broadcast_to(rs[0:1, :], carry.shape)
        dfl = rs * _sigmoid(-(fl_ref[...] + bf_ref[...]))
        o_ref[...] = dfl
        s_ref[...] += _colsum(dfl)

    rev = lambda i: (n - 1 - i, 0)
    return _pallas(
        body, name=name, grid=(n,),
        in_specs=[pl.BlockSpec((tm, W), rev)] * (ns + 1) + [pl.BlockSpec((1, W), lambda i: (0, 0))],
        out_specs=[pl.BlockSpec((tm, W), rev), pl.BlockSpec((1, W), lambda i: (0, 0))],
        out_shape=[_sds((S, W), F32), _sds((1, W), F32)],
        scratch_shapes=[pltpu.VMEM((8, W), F32)],
        compiler_params=_params(("arbitrary",)),
    )(*sums, fl, bf)


def _tri_tables(nq, qc, by_query):
    if by_query:
        pairs = [(i, j) for i in range(nq) for j in range(qc * (i + 1))]
    else:
        pairs = [(i, j) for j in range(qc * nq) for i in range(j // qc, nq)]
    ii, jj = zip(*pairs)
    return jnp.asarray(np.array(ii, np.int32)), jnp.asarray(np.array(jj, np.int32))


def _chunk_kinds(qc, dd):
    if dd < 0:
        return ("full",) * qc
    return tuple("full" if r > dd else "diag" if r == dd else "skip" for r in range(qc))


def _rep(v, t):
    return jnp.tile(v, (1, t // LANES))


def _attn_fwd(q, kv, ck, *, tb=512, row_chunks=4, name):
    S, D = q.shape
    HP = D // LANES
    T = min(tb, S)
    QC = row_chunks if S >= row_chunks * T else 1
    TQ = QC * T
    it, jt = _tri_tables(S // TQ, QC, True)

    def body(it_ref, jt_ref, q_ref, k_ref, v_ref, ck_ref, o_ref, lse_ref, st):
        s_id = pl.program_id(1)
        i, j = it_ref[s_id], jt_ref[s_id]
        dd = j - QC * i
        lane = lax.broadcasted_iota(jnp.int32, (TQ, LANES), 1)
        head0 = lane < HEAD_DIM
        h0 = head0[:T]
        hms = (h0, jnp.logical_not(h0))

        @pl.when(j == 0)
        def _():
            st[0:2] = jnp.full((2, TQ, LANES), NEG_BIG, F32)
            st[2:4] = jnp.zeros((2, TQ, LANES), F32)

        def step(kinds):
            kvv, vv = k_ref[...], v_ref[...]
            one = jnp.ones_like(vv)
            vaug = [jnp.where(hms[h], vv, one) for h in range(2)]
            old = st[...]
            rows = lambda r: slice(r * T, (r + 1) * T)
            live = [r for r in range(QC) if kinds[r] != "skip"]
            chains = [(r, h) for r in live for h in range(2)]
            ss = {}
            for r, h in chains:
                qv = q_ref[rows(r), :]
                ss[r, h] = lax.dot_general(jnp.where(hms[h], qv, jnp.zeros_like(qv)), kvv, _DIMS["nt"], preferred_element_type=F32)
            ps, alphas, m_new = {}, {}, {}
            for r, h in chains:
                s = ss[r, h] - ck_ref[h:h + 1, :]
                if kinds[r] == "diag":
                    row = lax.broadcasted_iota(jnp.int32, (T, T), 0)
                    col = lax.broadcasted_iota(jnp.int32, (T, T), 1)
                    s = jnp.where(row >= col, s, NEG_BIG)
                m_prev = old[h, rows(r), :]
                m_new[r, h] = jnp.maximum(m_prev, jnp.max(s, axis=1, keepdims=True))
                ps[r, h] = jnp.exp(s - _rep(m_new[r, h], T)).astype(BF16)
                alphas[r, h] = jnp.exp(m_prev - m_new[r, h])
            new = [[], [], [], []]
            for r in range(QC):
                if kinds[r] == "skip":
                    for a in range(4):
                        new[a].append(old[a, rows(r), :])
                    continue
                pv = [jnp.dot(ps[r, h], vaug[h], preferred_element_type=F32) for h in range(2)]
                a0, a1 = alphas[r, 0], alphas[r, 1]
                new[0].append(m_new[r, 0])
                new[1].append(m_new[r, 1])
                new[2].append(jnp.where(h0, a0, a1) * old[2, rows(r), :] + jnp.where(h0, pv[0], pv[1]))
                new[3].append(jnp.where(h0, a1, a0) * old[3, rows(r), :] + jnp.where(h0, pv[1], pv[0]))
            res = jnp.stack([jnp.concatenate(n, axis=0) for n in new], axis=0)
            st[...] = res
            return res

        @pl.when(dd < 0)
        def _():
            step(_chunk_kinds(QC, -1))

        for d in range(QC):
            @pl.when(dd == d)
            def _(d=d):
                res = step(_chunk_kinds(QC, d))
                if d == QC - 1:
                    lr = pltpu.roll(res[3], HEAD_DIM, axis=1)
                    o_ref[...] = res[2] / lr
                    lse_ref[0] = res[0] + jnp.log(jnp.where(head0, lr, res[3]))
                    lse_ref[1] = res[1] + jnp.log(jnp.where(head0, res[3], lr))

    grid_spec = pltpu.PrefetchScalarGridSpec(
        num_scalar_prefetch=2, grid=(HP, it.shape[0]),
        in_specs=[pl.BlockSpec((TQ, LANES), lambda h, s, it, jt: (it[s], h)),
                  pl.BlockSpec((T, LANES), lambda h, s, it, jt: (jt[s], h)),
                  pl.BlockSpec((T, LANES), lambda h, s, it, jt: (jt[s], HP + h)),
                  pl.BlockSpec((None, 8, T), lambda h, s, it, jt: (h, 0, jt[s]))],
        out_specs=[pl.BlockSpec((TQ, LANES), lambda h, s, it, jt: (it[s], h)),
                   pl.BlockSpec((2, TQ, LANES), lambda h, s, it, jt: (h, it[s], 0))],
        scratch_shapes=[pltpu.VMEM((4, TQ, LANES), F32)],
    )
    return _pallas(
        body, name=name, grid_spec=grid_spec,
        out_shape=[_sds((S, D), F32), _sds((2 * HP, S, LANES), F32)],
        compiler_params=_params(("parallel", "arbitrary")),
    )(it, jt, q, kv, kv, ck)


def _attn_bwd(q, kv, o, do, lse, ck, dkv0=None, *, tb=512, row_chunks=4, name):
    S, D = q.shape
    HP = D // LANES
    T = min(tb, S)
    QC = row_chunks if S >= row_chunks * T else 1
    TQ = QC * T
    it, jt = _tri_tables(S // TQ, QC, False)
    n0 = 0 if dkv0 is None else 2

    def body(it_ref, jt_ref, q_ref, k_ref, v_ref, o_ref, do_ref, lse_ref, ck_ref, *rest):
        dq_ref, drs_ref, dk_ref, dv_ref, dcs_ref = rest[n0:]
        s_id = pl.program_id(1)
        i, j = it_ref[s_id], jt_ref[s_id]
        dd = j - QC * i
        lane = lax.broadcasted_iota(jnp.int32, (T, LANES), 1)
        head0 = lane < HEAD_DIM
        hms = (head0, jnp.logical_not(head0))

        @pl.when(s_id == 0)
        def _():
            dq_ref[...] = jnp.zeros(dq_ref.shape, F32)
            drs_ref[...] = jnp.zeros(drs_ref.shape, F32)

        @pl.when(dd >= 0)
        def _():
            dk_ref[...] = jnp.zeros(dk_ref.shape, F32) if dkv0 is None else rest[0][...]
            dv_ref[...] = jnp.zeros(dv_ref.shape, F32) if dkv0 is None else rest[1][...]
            dcs_ref[...] = jnp.zeros(dcs_ref.shape, F32)

        def step(kinds):
            kvv, vv = k_ref[...], v_ref[...]
            one = jnp.ones_like(kvv)
            zero = jnp.zeros_like(kvv)
            rows = lambda r: slice(r * T, (r + 1) * T)
            live = [r for r in range(QC) if kinds[r] != "skip"]
            chains = [(r, h) for r in live for h in range(2)]
            qv = {r: q_ref[rows(r), :] for r in live}
            dob = {r: do_ref[rows(r), :].astype(BF16) for r in live}
            ss = {(r, h): lax.dot_general(jnp.where(hms[h], qv[r], zero), kvv, _DIMS["nt"], preferred_element_type=F32)
                  for r, h in chains}
            dps = {(r, h): lax.dot_general(jnp.where(hms[h], dob[r], zero), vv, _DIMS["nt"], preferred_element_type=F32)
                   for r, h in chains}
            pbs, dsbs = {}, {}
            for r, h in chains:
                s = ss[r, h] - ck_ref[h:h + 1, :]
                if kinds[r] == "diag":
                    row = lax.broadcasted_iota(jnp.int32, (T, T), 0)
                    col = lax.broadcasted_iota(jnp.int32, (T, T), 1)
                    s = jnp.where(row >= col, s, NEG_BIG)
                p = jnp.exp(s - _rep(lse_ref[h, rows(r), :], T))
                prod = dob[r].astype(F32) * o_ref[rows(r), :]
                delta = jnp.sum(jnp.where(hms[h], prod, 0.0), axis=1, keepdims=True)
                pbs[r, h] = p.astype(BF16)
                dsbs[r, h] = (p * (dps[r, h] - delta)).astype(BF16)
            dvs, dks = [None, None], [None, None]
            for r in live:
                dqs = []
                for h in range(2):
                    dqs.append(jnp.dot(dsbs[r, h], jnp.where(hms[h], kvv, one), preferred_element_type=F32))
                    dv = jnp.dot(pbs[r, h].T, dob[r], preferred_element_type=F32)
                    dk = jnp.dot(dsbs[r, h].T, jnp.where(hms[h], qv[r], one), preferred_element_type=F32)
                    dvs[h] = dv if dvs[h] is None else dvs[h] + dv
                    dks[h] = dk if dks[h] is None else dks[h] + dk
                qrows = pl.ds(pl.multiple_of(i * TQ + r * T, T), T)
                dq_ref[qrows, :] += jnp.where(head0, dqs[0], dqs[1])
                drs_ref[qrows, :] += jnp.where(head0, dqs[1], dqs[0])
            dv_ref[...] += jnp.where(head0, dvs[0], dvs[1])
            dk_ref[...] += jnp.where(head0, dks[0], dks[1])
            dcs_ref[...] += jnp.where(head0, dks[1], dks[0])

        @pl.when(dd < 0)
        def _():
            step(_chunk_kinds(QC, -1))

        for d in range(QC):
            @pl.when(dd == d)
            def _(d=d):
                step(_chunk_kinds(QC, d))

    by_q = lambda h, s, it, jt: (it[s], h)
    by_k = lambda h, s, it, jt: (jt[s], h)
    whole = lambda h, s, it, jt: (0, h)
    grid_spec = pltpu.PrefetchScalarGridSpec(
        num_scalar_prefetch=2, grid=(HP, it.shape[0]),
        in_specs=[pl.BlockSpec((TQ, LANES), by_q),
                  pl.BlockSpec((T, LANES), by_k),
                  pl.BlockSpec((T, LANES), lambda h, s, it, jt: (jt[s], HP + h)),
                  pl.BlockSpec((TQ, LANES), by_q),
                  pl.BlockSpec((TQ, LANES), by_q),
                  pl.BlockSpec((2, TQ, LANES), lambda h, s, it, jt: (h, it[s], 0)),
                  pl.BlockSpec((None, 8, T), lambda h, s, it, jt: (h, 0, jt[s]))]
        + [pl.BlockSpec((T, LANES), by_k)] * n0,
        out_specs=[pl.BlockSpec((S, LANES), whole), pl.BlockSpec((S, LANES), whole),
                   pl.BlockSpec((T, LANES), by_k), pl.BlockSpec((T, LANES), by_k), pl.BlockSpec((T, LANES), by_k)],
        scratch_shapes=[],
    )
    return _pallas(
        body, name=name, grid_spec=grid_spec,
        out_shape=[_sds((S, D), F32)] * 5,
        compiler_params=_params(("parallel", "arbitrary")),
    )(it, jt, q, kv, kv, o, do, lse, ck, *(dkv0 or ()))


ANY = pl.BlockSpec(memory_space=pl.ANY)


def _coords():
    x, y, c = lax.axis_index("x"), lax.axis_index("y"), lax.axis_index("c")
    return x, y, c


def _remote(src, dst, send_sems, recv_sems, k, to):
    return pltpu.make_async_remote_copy(src_ref=src, dst_ref=dst, send_sem=send_sems.at[k], recv_sem=recv_sems.at[k],
                                        device_id=to, device_id_type=MESH)


def _all_gather_shards(pack, *, name):
    R, C = pack.shape
    assert R % 4 == 0
    H, Q = R // 2, R // 4

    def body(in_ref, out_ref, send_sems, recv_sems):
        x, y, c = _coords()
        me, sib = (x, y, c), (x, y, 1 - c)
        xn, yn = (1 - x, y, c), (x, 1 - y, c)
        s, sx, sy, sd = 2 * x + y, 2 * (1 - x) + y, 2 * x + 1 - y, 2 * (1 - x) + 1 - y
        half = pl.ds(c * H, H)
        other = pl.ds((1 - c) * H, H)
        q0 = pl.ds(c * H, Q)
        q1 = pl.ds(c * H + Q, Q)
        rc = functools.partial(_remote, send_sems=send_sems, recv_sems=recv_sems)

        sends = [rc(in_ref.at[half], out_ref.at[s, half], k=0, to=xn),
                 rc(in_ref.at[half], out_ref.at[s, half], k=1, to=yn),
                 rc(in_ref, out_ref.at[s], k=7, to=sib)]
        for cp in sends:
            cp.start()
        rc(in_ref.at[half], out_ref.at[sx, half], k=0, to=me).wait_recv()
        sends.append(rc(out_ref.at[sx, q0], out_ref.at[sx, q0], k=2, to=yn))
        sends[-1].start()
        sends.append(rc(out_ref.at[sx, half], out_ref.at[sx, half], k=4, to=sib))
        sends[-1].start()
        rc(in_ref.at[half], out_ref.at[sy, half], k=1, to=me).wait_recv()
        sends.append(rc(out_ref.at[sy, q1], out_ref.at[sy, q1], k=3, to=xn))
        sends[-1].start()
        sends.append(rc(out_ref.at[sy, half], out_ref.at[sy, half], k=5, to=sib))
        sends[-1].start()
        rc(out_ref.at[sd, q0], out_ref.at[sd, q0], k=2, to=me).wait_recv()
        rc(out_ref.at[sd, q1], out_ref.at[sd, q1], k=3, to=me).wait_recv()
        sends.append(rc(out_ref.at[sd, half], out_ref.at[sd, half], k=6, to=sib))
        sends[-1].start()
        for k, sh in ((4, sx), (5, sy), (6, sd)):
            rc(out_ref.at[sh, other], out_ref.at[sh, other], k=k, to=me).wait_recv()
        rc(in_ref, out_ref.at[s], k=7, to=me).wait_recv()
        for cp in sends:
            cp.wait_send()

    return _pallas(
        body, name=name, in_specs=[ANY], out_specs=ANY,
        out_shape=_sds((N_CHIPS, R, C), pack.dtype),
        scratch_shapes=[pltpu.SemaphoreType.DMA((8,)), pltpu.SemaphoreType.DMA((8,))],
    )(pack)


def _rs_pair(g, *, name):
    n, R, C = g.shape
    H = R // 2

    def body(g_ref, land_ref, send_sems, recv_sems):
        x, y, c = _coords()
        other = pl.ds((1 - c) * H, H)
        cps = [_remote(g_ref.at[sh, other], land_ref.at[sh], send_sems, recv_sems, sh, (x, y, 1 - c)) for sh in range(n)]
        for cp in cps:
            cp.start()
        for cp in cps:
            cp.wait_recv()
        for cp in cps:
            cp.wait_send()

    return _pallas(
        body, name=name, in_specs=[ANY], out_specs=ANY, out_shape=_sds((n, H, C), g.dtype),
        scratch_shapes=[pltpu.SemaphoreType.DMA((n,)), pltpu.SemaphoreType.DMA((n,))],
    )(g)


def _rs_quarters(p, *, name):
    n, H, C = p.shape
    Q = H // 2

    def body(p_ref, la_ref, lb_ref, send_sems, recv_sems):
        x, y, c = _coords()
        sd = 2 * (1 - x) + 1 - y
        a = _remote(p_ref.at[sd, pl.ds(0, Q)], la_ref, send_sems, recv_sems, 0, (x, 1 - y, c))
        b = _remote(p_ref.at[sd, pl.ds(Q, Q)], lb_ref, send_sems, recv_sems, 1, (1 - x, y, c))
        a.start()
        b.start()
        a.wait_recv()
        b.wait_recv()
        a.wait_send()
        b.wait_send()

    return _pallas(
        body, name=name, in_specs=[ANY], out_specs=[ANY, ANY],
        out_shape=[_sds((Q, C), p.dtype), _sds((Q, C), p.dtype)],
        scratch_shapes=[pltpu.SemaphoreType.DMA((2,)), pltpu.SemaphoreType.DMA((2,))],
    )(p)


def _rs_halves(p, ax, ay, *, name):
    n, H, C = p.shape
    Q = H // 2

    def body(p_ref, ax_ref, ay_ref, la_ref, lb_ref, send_sems, recv_sems):
        x, y, c = _coords()
        sx, sy = 2 * (1 - x) + y, 2 * x + 1 - y
        xn, yn = (1 - x, y, c), (x, 1 - y, c)
        lo, hi = pl.ds(0, Q), pl.ds(Q, Q)
        cps = [_remote(ax_ref, la_ref.at[lo], send_sems, recv_sems, 0, xn),
               _remote(p_ref.at[sx, hi], la_ref.at[hi], send_sems, recv_sems, 1, xn),
               _remote(p_ref.at[sy, lo], lb_ref.at[lo], send_sems, recv_sems, 2, yn),
               _remote(ay_ref, lb_ref.at[hi], send_sems, recv_sems, 3, yn)]
        for cp in cps:
            cp.start()
        for cp in cps:
            cp.wait_recv()
        for cp in cps:
            cp.wait_send()

    return _pallas(
        body, name=name, in_specs=[ANY, ANY, ANY], out_specs=[ANY, ANY],
        out_shape=[_sds((H, C), p.dtype), _sds((H, C), p.dtype)],
        scratch_shapes=[pltpu.SemaphoreType.DMA((4,)), pltpu.SemaphoreType.DMA((4,))],
    )(p, ax, ay)


def _rs_join(buf, *, name):
    R, C = buf.shape
    H = R // 2

    def body(in_ref, out_ref, send_sems, recv_sems):
        x, y, c = _coords()
        half = pl.ds(c * H, H)
        other = pl.ds((1 - c) * H, H)
        cp = _remote(in_ref.at[half], out_ref.at[half], send_sems, recv_sems, 0, (x, y, 1 - c))
        cp.start()
        _remote(in_ref.at[other], out_ref.at[other], send_sems, recv_sems, 0, (x, y, c)).wait_recv()
        cp.wait_send()

    return _pallas(
        body, name=name, in_specs=[ANY], out_specs=ANY, out_shape=_sds((R, C), buf.dtype),
        input_output_aliases={0: 0},
        scratch_shapes=[pltpu.SemaphoreType.DMA((1,)), pltpu.SemaphoreType.DMA((1,))],
    )(buf)


def _tile_add(ins_specs, arrays, n_steps, out_spec, out_shape, scalars, *, name):
    grid_spec = pltpu.PrefetchScalarGridSpec(
        num_scalar_prefetch=1, grid=(n_steps,), in_specs=ins_specs, out_specs=out_spec, scratch_shapes=[])

    def body(sc_ref, *refs):
        acc = refs[0][...].astype(F32)
        for r in refs[1:-1]:
            acc = acc + r[...].astype(F32)
        refs[-1][...] = acc.astype(refs[-1].dtype)

    return _pallas(body, name=name, grid_spec=grid_spec, out_shape=out_shape,
                   compiler_params=_params(("arbitrary",)))(scalars, *arrays)


def _reduce_scatter(g, *, name):
    n, R, C = g.shape
    H, Q = R // 2, R // 4
    tm = RS_ROW_MULT // 4
    assert Q % tm == 0, (R, tm)
    x, y, c = _coords()
    sx, sy, s = 2 * (1 - x) + y, 2 * x + 1 - y, 2 * x + y
    sc = jnp.stack([c, sx, sy, s]).astype(jnp.int32)
    hb, qb = H // tm, Q // tm
    blk = lambda f: pl.BlockSpec((None, tm, C), f)
    flat = lambda f: pl.BlockSpec((tm, C), f)

    land = _rs_pair(g, name=name + "_pair")
    p = _tile_add([blk(lambda i, sc: (i // hb, sc[0] * hb + i % hb, 0)), blk(lambda i, sc: (i // hb, i % hb, 0))],
                  [g, land], n * hb, blk(lambda i, sc: (i // hb, i % hb, 0)), _sds((n, H, C), g.dtype), sc, name=name + "_add0")
    la, lb = _rs_quarters(p, name=name + "_quarters")
    ax = _tile_add([blk(lambda i, sc: (sc[1], i, 0)), flat(lambda i, sc: (i, 0))], [p, la], qb,
                   flat(lambda i, sc: (i, 0)), _sds((Q, C), g.dtype), sc, name=name + "_add1x")
    ay = _tile_add([blk(lambda i, sc: (sc[2], qb + i, 0)), flat(lambda i, sc: (i, 0))], [p, lb], qb,
                   flat(lambda i, sc: (i, 0)), _sds((Q, C), g.dtype), sc, name=name + "_add1y")
    fa, fb = _rs_halves(p, ax, ay, name=name + "_halves")
    buf = _tile_add([blk(lambda i, sc: (sc[3], i, 0)), flat(lambda i, sc: (i, 0)), flat(lambda i, sc: (i, 0))],
                    [p, fa, fb], hb, flat(lambda i, sc: (sc[0] * hb + i, 0)), _sds((R, C), F32), sc, name=name + "_add2")
    return _rs_join(buf, name=name + "_join")


def _all_reduce_small(v, *, name):
    M, N = v.shape

    def body(x_ref, out_ref, send_sems, recv_sems, local_sem):
        x, y, c = _coords()
        me, sibling = (x, y, c), (x, y, 1 - c)
        chips = [(1 - x, y), (x, 1 - y), (1 - x, 1 - y)]

        def rows(px, py, pc):
            return out_ref.at[pl.ds((4 * px + 2 * py + pc) * M, M), :]

        def copy(k, block, to, src=None):
            return pltpu.make_async_remote_copy(
                src_ref=rows(*block) if src is None else src, dst_ref=rows(*block),
                send_sem=send_sems.at[k], recv_sem=recv_sems.at[k], device_id=to, device_id_type=MESH)

        mine = pltpu.make_async_copy(x_ref, rows(*me), local_sem)
        mine.start()
        first = [copy(0, me, sibling, src=x_ref)]
        first += [copy(1 + j, me, (*chip, c), src=x_ref) for j, chip in enumerate(chips)]
        for cp in first:
            cp.start()
        passed = [copy(4 + j, (*chip, c), sibling) for j, chip in enumerate(chips)]
        for j, chip in enumerate(chips):
            copy(1 + j, (*chip, c), me).wait_recv()
            passed[j].start()
        copy(0, sibling, me).wait_recv()
        for j, chip in enumerate(chips):
            copy(4 + j, (*chip, 1 - c), me).wait_recv()
        for cp in first + passed:
            cp.wait_send()
        mine.wait()

    gathered = _pallas(
        body, name=name + "_gather",
        out_shape=_sds((N_DEV * M, N), F32),
        in_specs=[pl.BlockSpec(memory_space=pltpu.VMEM)],
        out_specs=pl.BlockSpec(memory_space=pltpu.VMEM),
        scratch_shapes=[pltpu.SemaphoreType.DMA((7,)), pltpu.SemaphoreType.DMA((7,)), pltpu.SemaphoreType.DMA],
    )(v)

    def sum_body(g_ref, o_ref):
        acc = g_ref[0:M, :]
        for d in range(1, N_DEV):
            acc = acc + g_ref[d * M:(d + 1) * M, :]
        o_ref[...] = acc

    return _pallas(sum_body, name=name + "_sum", out_shape=_sds((M, N), F32))(gathered)


MATS = [("ffn_w1", 2), ("ffn_w2", 1), ("ple_w_gate", 1), ("conv_w_pw1", 2), ("conv_w_pw2", 1), ("attn_w_q", 1), ("attn_w_o", 1),
        ("ple_w_proj", 2), ("w_kvf", 1)]
VECS = [("conv_b_pw1", 1), ("conv_w_dw", 2), ("conv_b_dw", 1), ("conv_ln_g", 1), ("conv_ln_b", 1), ("conv_b_pw2", 1)]
REPL = ["mix_norm", "ffn_norm", "ple_norm", "kv_norm", "final_norm", "b_f"]
WEIGHTS = ["mix_norm", "conv_w_pw1", "conv_b_pw1", "conv_w_dw", "conv_b_dw", "conv_ln_g", "conv_ln_b", "conv_w_pw2",
           "conv_b_pw2", "kv_norm", "w_kvf", "b_f", "attn_w_q", "attn_w_o", "ffn_norm", "ffn_w1", "ffn_w2", "ple_norm",
           "ple_w_gate", "ple_w_proj", "final_norm"]


def _round_up(n, m):
    return -(-n // m) * m


def _to_rows(t, C, mult):
    flat = t.reshape(-1)
    rows = _round_up(_round_up(flat.shape[0], C) // C, mult)
    flat = jnp.pad(flat, (0, rows * C - flat.shape[0]))
    return flat.reshape(rows, C)


def _pack(tensors, C, mult, total_mult):
    parts = [_to_rows(t, C, mult) for t in tensors]
    rows = sum(p.shape[0] for p in parts)
    pad = _round_up(rows, total_mult) - rows
    if pad:
        parts.append(jnp.zeros((pad, C), parts[0].dtype))
    return jnp.concatenate(parts, axis=0)


def _row_counts(shapes, C, mult):
    return [_round_up(_round_up(int(np.prod(s)), C) // C, mult) for s in shapes]


def _unpack(packed, shapes, C, mult):
    outs, r0 = [], 0
    lead = packed.shape[:-2]
    for shp, nr in zip(shapes, _row_counts(shapes, C, mult)):
        n = int(np.prod(shp))
        seg = packed[..., r0:r0 + nr, :].reshape(lead + (nr * C,))[..., :n]
        outs.append(seg.reshape(lead + tuple(shp)))
        r0 += nr
    return outs


def _unshard(t, axis):
    return jnp.concatenate([t[s] for s in range(N_CHIPS)], axis=axis)


def _shards(t, axis):
    return jnp.split(t, N_CHIPS, axis=axis)


class _GradPack:
    DIRECT = {"conv_w_pw2": "rows", "attn_w_q": "rows", "attn_w_o": "rows", "ffn_w2": "rows", "ple_w_gate": "rows",
              "ffn_w1": "cols"}

    def __init__(self, names, shard_shapes, C):
        self.C, self.names, self.shapes, self.off = C, names, dict(zip(names, shard_shapes)), {}
        r = 0
        for n, cnt in zip(names, _row_counts(shard_shapes, C, 16)):
            self.off[n] = r
            r += cnt
        self.R = _round_up(r, RS_ROW_MULT)
        self.buf = jnp.zeros((N_CHIPS, self.R, C), BF16)

    def matmul(self, wname, layer, a, b, **kw):
        rows_s, cols_s = self.shapes[wname][-2:]
        assert cols_s == self.C
        base = self.off[wname] + layer * rows_s
        if self.DIRECT[wname] == "rows" and N_CHIPS * rows_s <= 1024:
            assert base % rows_s == 0
            tm, block = N_CHIPS * rows_s, (N_CHIPS, rows_s, self.C)
            out_map = lambda i, j, k: (0, base // rows_s, 0)
        else:
            tm = next(t for t in (1024, 512, 256, 128, 64, 32, 16) if rows_s % t == 0 and base % t == 0)
            per, first, block = rows_s // tm, base // tm, (None, tm, self.C)
            if self.DIRECT[wname] == "rows":
                out_map = lambda i, j, k: (i // per, first + i % per, 0)
            else:
                assert a.shape[1] == rows_s
                out_map = lambda i, j, k: (j, first + i, 0)
        self.buf = _mm(a, b, mode="tn", tk=2048, tm=tm, tn=self.C, out_dtype=BF16, into=self.buf, out_block=block,
                       out_map=out_map, **kw)

    def insert(self, grads, axes):
        run = []
        for n in self.names + [None]:
            if n is not None and n not in self.DIRECT:
                run.append(n)
                continue
            if run:
                for s in range(N_CHIPS):
                    rows = jnp.concatenate([_to_rows(_shards(grads[m].astype(BF16), axes[m])[s], self.C, 16) for m in run], axis=0)
                    self.buf = lax.dynamic_update_slice(self.buf, rows[None], (s, self.off[run[0]], 0))
                run = []


def _local_step(x, p, tgt, w, gp=None):
    S, D = x.shape
    L = p.shape[0]
    NA = w["conv_w_pw1"].shape[0]
    H = w["b_f"].shape[0]
    HP = D // LANES
    row = lambda v: v.reshape(1, -1)
    act = lambda dt, n=D: _sds((S, n), dt)
    g = {}

    def dw(wname, layer, a, b, **kw):
        if gp is not None and wname in gp.DIRECT:
            gp.matmul(wname, layer, a, b, **kw)
            return None
        return _mm(a, b, mode="tn", tk=2048, out_dtype=BF16, **kw)

    def rms(hh, gain, name):
        return _rows(_rms_fwd, [hh], [row(gain)], [act(BF16)], name=name)[0]

    saved = []
    h = x
    kv = ck = fl = nkv = h_kv = None
    bfp = jnp.pad(w["b_f"], (0, LANES - H)).reshape(1, LANES)
    wk = w["w_kvf"][:, :D]
    wv = w["w_kvf"][:, D:2 * D]
    wkv = w["w_kvf"][:, :2 * D]
    wf = jnp.pad(w["w_kvf"][:, 2 * D:], ((0, 0), (0, LANES - H)))
    res_rms = lambda acc, r, gn: _with_rms(acc + r, gn)
    hn = rms(h, w["mix_norm"][0], "rms_mix")
    for i in range(L):
        sv = {"h0": h, "hn": hn}
        g_ffn, g_ple = row(w["ffn_norm"][i]), row(w["ple_norm"][i])
        if i < NA:
            u = _mm(hn, w["conv_w_pw1"][i], extras=[row(w["conv_b_pw1"][i])], epi=lambda acc, b: acc + b, name="mm_pw1")
            glu = _rows(_glu_fwd, [u], [], [act(F32)], name="glu_fwd")[0]
            wdw = jnp.pad(w["conv_w_dw"][i], ((0, CONV_PAD - CONV_WIDTH), (0, 0)))
            cv = _dwconv_fwd(glu, wdw, row(w["conv_b_dw"][i]), name="dwconv_fwd")
            sw = _rows(_ln_silu_fwd, [cv], [row(w["conv_ln_g"][i]), row(w["conv_ln_b"][i])], [act(BF16)], name="ln_silu_fwd")[0]
            h1, hn2 = _mm(sw, w["conv_w_pw2"][i], extras=[row(w["conv_b_pw2"][i]), h, g_ffn],
                          epi=lambda acc, b, r, gn: _with_rms(acc + b + r, gn), out_dtype=[F32, BF16], tm=512, tn=D, name="mm_pw2")
            sv.update(u=u, glu=glu, cv=cv, sw=sw, wdw=wdw)
        else:
            if i == NA:
                h_kv = h
                nkv = rms(h, w["kv_norm"], "rms_kv")
                kv = _mm(nkv, wkv, out_dtype=BF16, name="mm_kv")
                fl = _mm(nkv, wf, name="mm_f")
                c = _gate_cumsum(fl, bfp, name="gate_cumsum")
                ck = jnp.pad(c[:, :H].T.reshape(HP, 2, S), ((0, 0), (0, 6), (0, 0)))
            j = i - NA
            q = _mm(hn, w["attn_w_q"][j], epi=lambda acc: acc * (HEAD_DIM ** -0.5), out_dtype=BF16, name="mm_q")
            o, lse = _attn_fwd(q, kv, ck, name="attn_fwd")
            h1, hn2 = _mm(o, w["attn_w_o"][j], extras=[h, g_ffn], epi=res_rms, out_dtype=[F32, BF16], tm=512, tn=D, name="mm_o")
            sv.update(q=q, o=o, lse=lse)
        zb, f = _mm(hn2, w["ffn_w1"][i], epi=lambda acc: (acc, jnp.square(jnp.maximum(acc, 0.0))), out_dtype=[BF16, BF16],
                    name="mm_ffn1")
        h2, n3 = _mm(f, w["ffn_w2"][i], extras=[h1, g_ple], epi=res_rms, out_dtype=[F32, BF16], tm=512, tn=D, tk=4 * D, name="mm_ffn2")
        zg = _mm(n3, w["ple_w_gate"][i], name="mm_gate")
        ple = lambda acc, r, zz: r + _sigmoid(zz) * acc
        if i + 1 < L:
            h, hn = _mm(p[i], w["ple_w_proj"][i], extras=[h2, zg, row(w["mix_norm"][i + 1])],
                        epi=lambda acc, r, zz, gn: _with_rms(ple(acc, r, zz), gn), out_dtype=[F32, BF16], tm=512, tn=D, name="mm_proj")
        else:
            h = _mm(p[i], w["ple_w_proj"][i], extras=[h2, zg], epi=ple, tm=512, tn=D, name="mm_proj_last")
        sv.update(h1=h1, hn2=hn2, zb=zb, f=f, h2=h2, n3=n3, zg=zg)
        saved.append(sv)

    dh, err2, g_final = _rows(_final_fn, [h, tgt], [row(w["final_norm"])], [act(F32)], [_sds((1, D), F32), _sds((1, D), F32)],
                              name="final")
    loss = 0.5 * jnp.sum(err2) / D
    g["final_norm"] = g_final.reshape(-1)

    red = _sds((1, D), F32)
    stack = {k: [None] * n for k, n in (("mix_norm", L), ("ffn_norm", L), ("ple_norm", L), ("ffn_w1", L), ("ffn_w2", L),
                                        ("ple_w_gate", L), ("ple_w_proj", L), ("conv_w_pw1", NA), ("conv_b_pw1", NA),
                                        ("conv_w_dw", NA), ("conv_b_dw", NA), ("conv_ln_g", NA), ("conv_ln_b", NA),
                                        ("conv_w_pw2", NA), ("conv_b_pw2", NA), ("attn_w_q", L - NA), ("attn_w_o", L - NA))}
    dk_sum = dv_sum = None
    dcks = []
    for i in reversed(range(L)):
        sv = saved[i]
        dzg, dpp = _mm(p[i], w["ple_w_proj"][i], extras=[dh, sv["zg"]], epi=lambda acc, d, zz: _ple_bwd(d, zz, acc),
                       out_dtype=[BF16, BF16], name="mm_ple_bwd")
        stack["ple_w_proj"][i] = _mm(p[i], dpp, mode="tn", tk=2048, out_dtype=BF16, name="mm_dproj")
        stack["ple_w_gate"][i] = dw("ple_w_gate", i, sv["n3"], dzg, name="mm_dgate")
        dh, dhb, dgain = _mm(dzg, w["ple_w_gate"][i], mode="nt", extras=[sv["h2"], dh, row(w["ple_norm"][i])], epi=_dup(_rms_bwd),
                             out_dtype=[F32, BF16], reds=1, tm=512, tn=D, name="mm_dn3")
        stack["ple_norm"][i] = dgain.reshape(-1)
        dz = _mm(dhb, w["ffn_w2"][i], mode="nt", extras=[sv["zb"]], epi=lambda acc, zz: acc * (2.0 * jnp.maximum(zz, 0.0).astype(F32)),
                 out_dtype=BF16, name="mm_dz")
        stack["ffn_w2"][i] = dw("ffn_w2", i, sv["f"], dhb, name="mm_dffn2")
        stack["ffn_w1"][i] = dw("ffn_w1", i, sv["hn2"], dz, name="mm_dffn1")
        if i < NA:
            dh, dhb, dgain, dbias = _mm(dz, w["ffn_w1"][i], mode="nt", extras=[sv["h1"], dh, row(w["ffn_norm"][i])],
                                        epi=_dup(_rms_bwd_bias), out_dtype=[F32, BF16], reds=2, tm=512, tn=D, tk=4 * D, name="mm_dhn2_bias")
            stack["conv_b_pw2"][i] = dbias.reshape(-1)
        else:
            dh, dhb, dgain = _mm(dz, w["ffn_w1"][i], mode="nt", extras=[sv["h1"], dh, row(w["ffn_norm"][i])], epi=_dup(_rms_bwd),
                                 out_dtype=[F32, BF16], reds=1, tm=512, tn=D, tk=4 * D, name="mm_dhn2")
        stack["ffn_norm"][i] = dgain.reshape(-1)
        if i < NA:
            dsw = _mm(dhb, w["conv_w_pw2"][i], mode="nt", name="mm_dsw")
            stack["conv_w_pw2"][i] = dw("conv_w_pw2", i, sv["sw"], dhb, name="mm_dpw2")
            dcv, dlg, dlb = _rows(_ln_silu_bwd, [dsw, sv["cv"]], [row(w["conv_ln_g"][i]), row(w["conv_ln_b"][i])], [act(F32)],
                                  [red, red], name="ln_silu_bwd")
            stack["conv_ln_g"][i], stack["conv_ln_b"][i] = dlg.reshape(-1), dlb.reshape(-1)
            dglu, dwdw, dbdw = _dwconv_bwd(dcv, sv["glu"], sv["wdw"], name="dwconv_bwd")
            stack["conv_w_dw"][i], stack["conv_b_dw"][i] = dwdw[:CONV_WIDTH], dbdw.reshape(-1)
            du, dbu = _rows(_glu_bwd, [dglu, sv["u"]], [], [act(BF16, 2 * D)], [_sds((1, 2 * D), F32)], name="glu_bwd")
            stack["conv_b_pw1"][i] = dbu.reshape(-1)
            stack["conv_w_pw1"][i] = _mm(sv["hn"], du, mode="tn", tk=2048, out_dtype=BF16, name="mm_dpw1")
            dh, dgain = _mm(du, w["conv_w_pw1"][i], mode="nt", extras=[sv["h0"], dh, row(w["mix_norm"][i])], epi=_rms_bwd,
                            reds=1, tm=512, tn=D, name="mm_dhn_a")
        else:
            j = i - NA
            do = _mm(dhb, w["attn_w_o"][j], mode="nt", out_dtype=BF16, name="mm_do")
            stack["attn_w_o"][j] = dw("attn_w_o", j, sv["o"], dhb, name="mm_dwo")
            dq, drs, dk_sum, dv_sum, dcs = _attn_bwd(sv["q"], kv, sv["o"], do, sv["lse"], ck,
                                                     None if dk_sum is None else (dk_sum, dv_sum), name="attn_bwd")
            scale = lambda acc: acc * (HEAD_DIM ** -0.5)
            stack["attn_w_q"][j] = dw("attn_w_q", j, sv["hn"], dq, epi=scale, name="mm_dwq")
            dh_in = dh
            dh, dgain = _mm(dq, w["attn_w_q"][j], mode="nt", extras=[sv["h0"], dh_in, row(w["mix_norm"][i])],
                            epi=lambda acc, xx, dr, gn: _rms_bwd(scale(acc), xx, dr, gn), reds=1, tm=512, tn=D, name="mm_dhn_b")
            pick = lambda t: jnp.pad(t.reshape(S, HP, 2, HEAD_DIM)[:, :, ::-1, 0].reshape(S, H), ((0, 0), (0, LANES - H)))
            dcks += [pick(drs), pick(dcs)]
        stack["mix_norm"][i] = dgain.reshape(-1)
        if i == NA:
            dfl, dbf = _gate_cumsum_bwd(dcks, fl, bfp, name="gate_cumsum_bwd")
            g["b_f"] = dbf[0, :H]
            gk = _mm(nkv, dk_sum, mode="tn", tk=2048, out_dtype=BF16, name="mm_dwk")
            gv = _mm(nkv, dv_sum, mode="tn", tk=2048, out_dtype=BF16, name="mm_dwk")
            gf = _mm(nkv, dfl, mode="tn", tk=2048, out_dtype=BF16, name="mm_dwf")
            g["w_kvf"] = jnp.concatenate([gk, gv, gf[:, :H]], axis=1)
            dn = _mm(dk_sum, wk, mode="nt", name="mm_dnk")
            dn = _mm(dv_sum, wv, mode="nt", extras=[dn], epi=lambda acc, r: acc + r, name="mm_dnv")
            dh, dgain = _mm(dfl, wf, mode="nt", extras=[dn, h_kv, dh, row(w["kv_norm"])],
                            epi=lambda acc, r, xx, dr, gn: _rms_bwd(acc + r, xx, dr, gn), reds=1, tm=512, tn=D, name="mm_dnf")
            g["kv_norm"] = dgain.reshape(-1)
    for k, v in stack.items():
        if v[0] is not None:
            g[k] = jnp.stack(v, axis=0)
    return loss, dh, g


def kernel(x, p, mix_norm, conv_w_pw1, conv_b_pw1, conv_w_dw, conv_b_dw, conv_ln_g, conv_ln_b, conv_w_pw2, conv_b_pw2, kv_norm, w_kvf, b_f, attn_w_q, attn_w_o, ffn_norm, ffn_w1, ffn_w2, ple_norm, ple_w_gate, ple_w_proj, final_norm, loss_target, m_mix_norm, m_conv_w_pw1, m_conv_b_pw1, m_conv_w_dw, m_conv_b_dw, m_conv_ln_g, m_conv_ln_b, m_conv_w_pw2, m_conv_b_pw2, m_kv_norm, m_w_kvf, m_b_f, m_attn_w_q, m_attn_w_o, m_ffn_norm, m_ffn_w1, m_ffn_w2, m_ple_norm, m_ple_w_gate, m_ple_w_proj, m_final_norm, v_mix_norm, v_conv_w_pw1, v_conv_b_pw1, v_conv_w_dw, v_conv_b_dw, v_conv_ln_g, v_conv_ln_b, v_conv_w_pw2, v_conv_b_pw2, v_kv_norm, v_w_kvf, v_b_f, v_attn_w_q, v_attn_w_o, v_ffn_norm, v_ffn_w1, v_ffn_w2, v_ple_norm, v_ple_w_gate, v_ple_w_proj, v_final_norm):
    args = dict(locals())
    wl = {n: args[n] for n in WEIGHTS}
    ml = {n: args["m_" + n] for n in WEIGHTS}
    vl = {n: args["v_" + n] for n in WEIGHTS}
    S, D = x.shape[1], x.shape[2]
    C = D

    mat_shapes = [wl[n].shape for n, _ in MATS]
    vec_shapes = [wl[n].shape for n, _ in VECS]
    mats = _all_gather_shards(_pack([wl[n].astype(BF16) for n, _ in MATS], C, 16, 64), name="ag_mats")
    vecs = _all_gather_shards(_pack([wl[n] for n, _ in VECS], C, 1, 32), name="ag_vecs")
    full = {n: wl[n] for n in REPL}
    for (n, ax), t in zip(MATS, _unpack(mats, mat_shapes, C, 16)):
        full[n] = _unshard(t, ax)
    for (n, ax), t in zip(VECS, _unpack(vecs, vec_shapes, C, 1)):
        full[n] = _unshard(t, ax)

    names = [n for n, _ in MATS] + [n for n, _ in VECS]
    axes = dict(MATS + VECS)
    shard_shapes = [wl[n].shape for n in names]
    gp = _GradPack(names, shard_shapes, C)
    loss, dx, g = _local_step(x[0], p[:, 0], loss_target[0], full, gp)
    loss = lax.psum(loss, ("x", "y", "c"))

    gp.insert(g, axes)
    gred = _reduce_scatter(gp.buf, name="rs")
    gl = dict(zip(names, _unpack(gred, shard_shapes, C, 16)))

    rep_shapes = [wl[n].shape for n in REPL]
    rpack = _pack([g[n] for n in REPL], C, 1, 8)
    for n, t in zip(REPL, _unpack(_all_reduce_small(rpack, name="ar"), rep_shapes, C, 1)):
        gl[n] = t

    grads, deltas, new_m, new_v = {}, {}, {}, {}
    rep_w, rep_m, rep_v = (_pack([d[n] for n in REPL], C, 1, 8) for d in (wl, ml, vl))
    rep_out = _adamw(rep_w, rpack_like(gl, rep_shapes, C), rep_m, rep_v, "adamw_rep")
    for dst, packed in zip((grads, deltas, new_m, new_v), rep_out):
        for n, t in zip(REPL, _unpack(packed, rep_shapes, C, 1)):
            dst[n] = t
    for n in names:
        rows_n = int(np.prod(wl[n].shape[:-1]))
        step_rows = min(512, rows_n)
        if wl[n].shape[-1] == C and gp.off[n] % step_rows == 0 and rows_n % step_rows == 0:
            res = _adamw(wl[n], gred, ml[n], vl[n], "adamw_" + n, g_row=gp.off[n])
        else:
            res = _adamw(wl[n], gl[n], ml[n], vl[n], "adamw_" + n)
        for dst, t in zip((grads, deltas, new_m, new_v), res):
            dst[n] = t
    out = [loss, dx[None]]
    for d in (grads, deltas, new_m, new_v):
        out += [d[n] for n in WEIGHTS]
    return tuple(out)


def rpack_like(gl, rep_shapes, C):
    return _pack([gl[n] for n in REPL], C, 1, 8)
```

```python
import functools

import jax
import jax.numpy as jnp
import numpy as np
from jax import lax
from jax.experimental import pallas as pl
from jax.experimental.pallas import tpu as pltpu

F32 = jnp.float32
BF16 = jnp.bfloat16
MESH = pl.DeviceIdType.MESH

N_CHIPS = 4
N_DEV = 8
HEAD_DIM = 64
LANES = 128
CONV_WIDTH = 31
CONV_PAD = 32
EPS = 1e-6
NEG_BIG = -1e30
VMEM_LIMIT = 56 * 1024 * 1024

RS_ROW_MULT = 1024

ADAM_LR, ADAM_B1, ADAM_B2, ADAM_EPS, ADAM_WD, ADAM_STEP = 0.001, 0.9, 0.999, 1e-08, 0.01, 10


def _pallas(body, **kw):
    return pl.pallas_call(body, **kw)


def _params(sem=None):
    return pltpu.CompilerParams(dimension_semantics=sem, vmem_limit_bytes=VMEM_LIMIT)


def _sds(shape, dtype):
    return jax.ShapeDtypeStruct(tuple(shape), dtype)


_DIMS = {"nn": (((1,), (0,)), ((), ())), "nt": (((1,), (1,)), ((), ())), "tn": (((0,), (0,)), ((), ()))}


def _mm(a, b, *, mode="nn", extras=(), epi=None, out_dtype=F32, reds=0, tm=1024, tn=1024, tk=1024, into=None, out_block=None,
        out_map=None, name):
    if mode == "nn":
        (M, K), (K2, N) = a.shape, b.shape
    elif mode == "nt":
        (M, K), (N, K2) = a.shape, b.shape
    else:
        (K, M), (K2, N) = a.shape, b.shape
    assert K == K2, (name, a.shape, b.shape)
    tm, tn, tk = min(tm, M), min(tn, N), min(tk, K)
    assert M % tm == 0 and N % tn == 0 and K % tk == 0, (name, a.shape, b.shape)
    nk = K // tk
    if mode == "tn":
        a_spec = pl.BlockSpec((tk, tm), lambda i, j, k: (k, i))
    else:
        a_spec = pl.BlockSpec((tm, tk), lambda i, j, k: (i, k))
    if mode == "nt":
        b_spec = pl.BlockSpec((tn, tk), lambda i, j, k: (j, k))
    else:
        b_spec = pl.BlockSpec((tk, tn), lambda i, j, k: (k, j))
    ex_specs = []
    for e in extras:
        if e.shape[0] == 1:
            ex_specs.append(pl.BlockSpec((1, tn), lambda i, j, k: (0, j)))
        else:
            assert e.shape == (M, N), (name, e.shape)
            ex_specs.append(pl.BlockSpec((tm, tn), lambda i, j, k: (i, j)))
    ne = len(extras)
    dims = _DIMS[mode]
    many = isinstance(out_dtype, (list, tuple))
    out_dtypes = list(out_dtype) if many else [out_dtype]
    no = len(out_dtypes)
    assert not reds or tn == N, name

    def body(a_ref, b_ref, *rest):
        ex_refs, o_refs, r_refs = rest[:ne], rest[ne:ne + no], rest[ne + no:ne + no + reds]
        part = lax.dot_general(a_ref[...].astype(BF16), b_ref[...].astype(BF16), dims, preferred_element_type=F32)
        i = pl.program_id(0)

        def finish(acc):
            res = epi(acc, *[r[...] for r in ex_refs]) if epi is not None else acc
            if not isinstance(res, (tuple, list)):
                res = (res,)
            assert len(res) == no + reds, (name, len(res))
            for r, v in zip(o_refs, res[:no]):
                r[...] = v.astype(r.dtype).reshape(r.shape)
            for r, v in zip(r_refs, res[no:]):
                @pl.when(i == 0)
                def _(r=r, v=v):
                    r[...] = v

                @pl.when(i > 0)
                def _(r=r, v=v):
                    r[...] += v

        if nk == 1:
            finish(part)
        else:
            acc_ref = rest[ne + no + reds]
            k = pl.program_id(2)

            @pl.when(k == 0)
            def _():
                acc_ref[...] = part

            @pl.when(k > 0)
            def _():
                acc_ref[...] += part

            @pl.when(k == nk - 1)
            def _():
                finish(acc_ref[...])

    scratch = [pltpu.VMEM((tm, tn), F32)] if nk > 1 else []
    if into is not None:
        assert not many and not reds and not extras and into.dtype == out_dtype, name

        def body_into(a_ref, b_ref, into_ref, *rest):
            body(a_ref, b_ref, *rest)

        return _pallas(
            body_into, name=name, grid=(M // tm, N // tn, nk),
            in_specs=[a_spec, b_spec, pl.BlockSpec(memory_space=pl.ANY)],
            out_specs=pl.BlockSpec(out_block, out_map),
            out_shape=_sds(into.shape, into.dtype),
            input_output_aliases={2: 0},
            scratch_shapes=scratch,
            compiler_params=_params(("parallel", "parallel", "arbitrary")),
        )(a, b, into)
    res = _pallas(
        body, name=name, grid=(M // tm, N // tn, nk),
        in_specs=[a_spec, b_spec] + ex_specs,
        out_specs=[pl.BlockSpec((tm, tn), lambda i, j, k: (i, j))] * no + [pl.BlockSpec((1, tn), lambda i, j, k: (0, j))] * reds,
        out_shape=[_sds((M, N), dt) for dt in out_dtypes] + [_sds((1, N), F32)] * reds,
        scratch_shapes=scratch,
        compiler_params=_params(("arbitrary",) * 3 if reds else ("parallel", "parallel", "arbitrary")),
    )(a, b, *extras)
    return res if (many or reds) else res[0]


def _rows(fn, ins, params, outs, reds=(), *, tm=256, row_offsets=None, name):
    S = outs[0].shape[0] if outs else ins[0].shape[0]
    tm = min(tm, S)
    assert S % tm == 0, (name, S, tm)
    ni, npar, no, nr = len(ins), len(params), len(outs), len(reds)
    offs = list(row_offsets) if row_offsets is not None else [0] * ni
    assert all(o % tm == 0 for o in offs), (name, offs, tm)

    def body(*refs):
        in_refs, p_refs = refs[:ni], refs[ni:ni + npar]
        o_refs, r_refs = refs[ni + npar:ni + npar + no], refs[ni + npar + no:]
        res = fn(*[r[...] for r in in_refs], *[r[...] for r in p_refs])
        if not isinstance(res, (tuple, list)):
            res = (res,)
        assert len(res) == no + nr, (name, len(res))
        for r, v in zip(o_refs, res[:no]):
            r[...] = v.astype(r.dtype)
        i = pl.program_id(0)
        for r, v in zip(r_refs, res[no:]):
            @pl.when(i == 0)
            def _(r=r, v=v):
                r[...] = v

            @pl.when(i > 0)
            def _(r=r, v=v):
                r[...] += v

    res = _pallas(
        body, name=name, grid=(S // tm,),
        in_specs=[pl.BlockSpec((tm, a.shape[1]), lambda i, o=o: (o // tm + i, 0)) for a, o in zip(ins, offs)]
        + [pl.BlockSpec(p.shape, lambda i: (0, 0)) for p in params],
        out_specs=[pl.BlockSpec((tm, o.shape[1]), lambda i: (i, 0)) for o in outs]
        + [pl.BlockSpec(r.shape, lambda i: (0, 0)) for r in reds],
        out_shape=list(outs) + list(reds),
        compiler_params=_params(("arbitrary",)),
    )(*ins, *params)
    return res


def _colsum(v):
    return jnp.sum(v, axis=0, keepdims=True)


def _sigmoid(v):
    return 1.0 / (1.0 + jnp.exp(-v))


def _rms_stats(x):
    r = lax.rsqrt(jnp.mean(x * x, axis=-1, keepdims=True) + EPS)
    return x * r, r


def _rms_fwd(x, g):
    xh, _ = _rms_stats(x)
    return (xh * g,)


def _with_rms(h, g):
    xh, _ = _rms_stats(h)
    return h, xh * g


def _rms_bwd(dy, x, dres, g):
    xh, r = _rms_stats(x)
    dyg = dy * g
    dx = r * (dyg - xh * jnp.mean(dyg * xh, axis=-1, keepdims=True))
    return dres + dx, _colsum(dy * xh)


def _dup(fn):
    def wrapped(*a):
        r = fn(*a)
        return (r[0], r[0]) + tuple(r[1:])
    return wrapped


def _rms_bwd_bias(dy, x, dres, g):
    dx, dg = _rms_bwd(dy, x, dres, g)
    return dx, dg, _colsum(dx)


def _glu_fwd(u):
    d = u.shape[1] // 2
    return (u[:, :d] * _sigmoid(u[:, d:]),)


def _glu_bwd(dglu, u):
    d = u.shape[1] // 2
    a, sig = u[:, :d], _sigmoid(u[:, d:])
    du = jnp.concatenate([dglu * sig, dglu * a * sig * (1.0 - sig)], axis=1)
    return du, _colsum(du)


def _ln_parts(x, g, b):
    mu = jnp.mean(x, axis=-1, keepdims=True)
    xc = x - mu
    r = lax.rsqrt(jnp.mean(xc * xc, axis=-1, keepdims=True) + EPS)
    xh = xc * r
    return xh, r, xh * g + b


def _ln_silu_fwd(x, g, b):
    _, _, y = _ln_parts(x, g, b)
    return (y * _sigmoid(y),)


def _ln_silu_bwd(dsw, x, g, b):
    xh, r, y = _ln_parts(x, g, b)
    sig = _sigmoid(y)
    dy = dsw * sig * (1.0 + y * (1.0 - sig))
    dxh = dy * g
    dx = r * (dxh - jnp.mean(dxh, axis=-1, keepdims=True) - xh * jnp.mean(dxh * xh, axis=-1, keepdims=True))
    return dx, _colsum(dy * xh), _colsum(dy)


def _ple_bwd(dh, zg, pp):
    gate = _sigmoid(zg)
    return dh * pp * gate * (1.0 - gate), dh * gate


def _final_fn(h, t, g):
    xh, r = _rms_stats(h)
    err = xh * g - t
    dy = err * (1.0 / h.shape[1])
    dyg = dy * g
    dh = r * (dyg - xh * jnp.mean(dyg * xh, axis=-1, keepdims=True))
    return dh, _colsum(err * err), _colsum(dy * xh)


def _adamw_fn(w, g, m, v):
    m = ADAM_B1 * m + (1.0 - ADAM_B1) * g
    v = ADAM_B2 * v + (1.0 - ADAM_B2) * (g * g)
    m_hat = m / (1.0 - ADAM_B1 ** ADAM_STEP)
    v_hat = v / (1.0 - ADAM_B2 ** ADAM_STEP)
    delta = -ADAM_LR * (m_hat / (jnp.sqrt(v_hat) + ADAM_EPS) + ADAM_WD * w)
    return g, delta, m, v


def _adamw(w, g, m, v, name, g_row=None):
    shape = w.shape
    cols = shape[-1] if len(shape) > 1 else shape[0]
    two = lambda t: t.reshape(-1, cols)
    o = _sds(two(w).shape, F32)
    tm = min(512, o.shape[0])
    assert g_row is None or g.shape[1] == cols, name
    res = _rows(_adamw_fn, [two(w), g if g_row is not None else two(g), two(m), two(v)], [], [o, o, o, o], tm=tm,
                row_offsets=None if g_row is None else [0, g_row, 0, 0], name=name)
    return [r.reshape(shape) for r in res]


def _sublane_shifts(win):
    n = win.shape[0]
    return [win] + [pltpu.roll(win, n - b, axis=0) for b in range(1, 8)]


def _tap(shifted, offset, rows):
    a, b = divmod(offset, 8)
    return shifted[b][8 * a:8 * a + rows]


def _dwconv_fwd(u, w, b, *, tm=512, name):
    S, D = u.shape
    tm = min(tm, S)
    rc = min(128, tm)
    per = tm // CONV_PAD

    def body(prev_ref, cur_ref, w_ref, b_ref, o_ref, win):
        i = pl.program_id(0)

        @pl.when(i == 0)
        def _():
            win[0:CONV_PAD, :] = jnp.zeros((CONV_PAD, D), F32)

        @pl.when(i > 0)
        def _():
            win[0:CONV_PAD, :] = prev_ref[...]

        win[CONV_PAD:CONV_PAD + tm, :] = cur_ref[...]
        for lc in range(D // LANES):
            ls = slice(lc * LANES, (lc + 1) * LANES)
            for r0 in range(0, tm, rc):
                shifted = _sublane_shifts(win[r0:r0 + rc + CONV_PAD, ls])
                acc = jnp.zeros((rc, LANES), F32) + b_ref[:, ls]
                for k in range(CONV_WIDTH):
                    acc = acc + _tap(shifted, 2 + k, rc) * w_ref[k:k + 1, ls]
                o_ref[r0:r0 + rc, ls] = acc

    return _pallas(
        body, name=name, grid=(S // tm,),
        in_specs=[pl.BlockSpec((CONV_PAD, D), lambda i: (jnp.maximum(i * per - 1, 0), 0)),
                  pl.BlockSpec((tm, D), lambda i: (i, 0)),
                  pl.BlockSpec((CONV_PAD, D), lambda i: (0, 0)),
                  pl.BlockSpec((1, D), lambda i: (0, 0))],
        out_specs=pl.BlockSpec((tm, D), lambda i: (i, 0)),
        out_shape=_sds((S, D), F32),
        scratch_shapes=[pltpu.VMEM((tm + CONV_PAD, D), F32)],
        compiler_params=_params(("arbitrary",)),
    )(u, u, w, b)


def _dwconv_bwd(dy, u, w, *, tm=512, name):
    S, D = u.shape
    tm = min(tm, S)
    rc = min(128, tm)
    per = tm // CONV_PAD
    n = S // tm
    nxt = S // CONV_PAD - 1

    def body(dy_ref, dyn_ref, up_ref, u_ref, w_ref, du_ref, dw_ref, db_ref, wd, wu, dwacc, dbacc):
        i = pl.program_id(0)

        @pl.when(i == 0)
        def _():
            wu[0:CONV_PAD, :] = jnp.zeros((CONV_PAD, D), F32)
            dwacc[...] = jnp.zeros(dwacc.shape, F32)
            dbacc[...] = jnp.zeros(dbacc.shape, F32)

        @pl.when(i > 0)
        def _():
            wu[0:CONV_PAD, :] = up_ref[...]

        @pl.when(i == n - 1)
        def _():
            wd[tm:tm + CONV_PAD, :] = jnp.zeros((CONV_PAD, D), F32)

        @pl.when(i < n - 1)
        def _():
            wd[tm:tm + CONV_PAD, :] = dyn_ref[...]

        wu[CONV_PAD:CONV_PAD + tm, :] = u_ref[...]
        wd[0:tm, :] = dy_ref[...]
        for lc in range(D // LANES):
            ls = slice(lc * LANES, (lc + 1) * LANES)
            for r0 in range(0, tm, rc):
                sd = _sublane_shifts(wd[r0:r0 + rc + CONV_PAD, ls])
                acc = jnp.zeros((rc, LANES), F32)
                for k in range(CONV_WIDTH):
                    acc = acc + _tap(sd, 30 - k, rc) * w_ref[k:k + 1, ls]
                du_ref[r0:r0 + rc, ls] = acc
                dyc = wd[r0:r0 + rc, ls]
                dbacc[:, ls] += jnp.sum(dyc.reshape(rc // 8, 8, LANES), axis=0)
                for k in range(CONV_WIDTH):
                    prod = dyc * wu[r0 + 2 + k:r0 + 2 + k + rc, ls]
                    dwacc[8 * k:8 * k + 8, ls] += jnp.sum(prod.reshape(rc // 8, 8, LANES), axis=0)

        @pl.when(i == n - 1)
        def _():
            dw_ref[...] = jnp.zeros(dw_ref.shape, F32)
            for k in range(CONV_WIDTH):
                dw_ref[k:k + 1, :] = jnp.sum(dwacc[8 * k:8 * k + 8, :], axis=0, keepdims=True)
            db_ref[...] = jnp.sum(dbacc[...], axis=0, keepdims=True)

    return _pallas(
        body, name=name, grid=(n,),
        in_specs=[pl.BlockSpec((tm, D), lambda i: (i, 0)),
                  pl.BlockSpec((CONV_PAD, D), lambda i: (jnp.minimum((i + 1) * per, nxt), 0)),
                  pl.BlockSpec((CONV_PAD, D), lambda i: (jnp.maximum(i * per - 1, 0), 0)),
                  pl.BlockSpec((tm, D), lambda i: (i, 0)),
                  pl.BlockSpec((CONV_PAD, D), lambda i: (0, 0))],
        out_specs=[pl.BlockSpec((tm, D), lambda i: (i, 0)),
                   pl.BlockSpec((CONV_PAD, D), lambda i: (0, 0)),
                   pl.BlockSpec((1, D), lambda i: (0, 0))],
        out_shape=[_sds((S, D), F32), _sds((CONV_PAD, D), F32), _sds((1, D), F32)],
        scratch_shapes=[pltpu.VMEM((tm + CONV_PAD, D), F32), pltpu.VMEM((tm + CONV_PAD, D), F32),
                        pltpu.VMEM((8 * CONV_PAD, D), F32), pltpu.VMEM((8, D), F32)],
        compiler_params=_params(("arbitrary",)),
    )(dy, dy, u, u, w)


def _tri_dot(tri, x):
    x1 = x.astype(BF16)
    r1 = x - x1.astype(F32)
    x2 = r1.astype(BF16)
    x3 = (r1 - x2.astype(F32)).astype(BF16)
    d = lambda v: jnp.dot(tri, v, preferred_element_type=F32)
    return d(x1) + d(x2) + d(x3)


def _log_sigmoid(x):
    return jnp.minimum(x, 0.0) - jnp.log(1.0 + jnp.exp(-jnp.abs(x)))


def _gate_cumsum(fl, bf, *, tm=256, name):
    S, W = fl.shape
    tm = min(tm, S)

    def body(fl_ref, bf_ref, c_ref, carry):
        i = pl.program_id(0)

        @pl.when(i == 0)
        def _():
            carry[...] = jnp.zeros(carry.shape, F32)

        x = _log_sigmoid(fl_ref[...] + bf_ref[...])
        row = lax.broadcasted_iota(jnp.int32, (tm, tm), 0)
        col = lax.broadcasted_iota(jnp.int32, (tm, tm), 1)
        tri = jnp.where(row >= col, 1.0, 0.0).astype(BF16)
        cs = _tri_dot(tri, x) + carry[0:1, :]
        c_ref[...] = cs
        carry[...] = jnp.broadcast_to(cs[tm - 1:tm, :], carry.shape)

    return _pallas(
        body, name=name, grid=(S // tm,),
        in_specs=[pl.BlockSpec((tm, W), lambda i: (i, 0)), pl.BlockSpec((1, W), lambda i: (0, 0))],
        out_specs=pl.BlockSpec((tm, W), lambda i: (i, 0)),
        out_shape=_sds((S, W), F32),
        scratch_shapes=[pltpu.VMEM((8, W), F32)],
        compiler_params=_params(("arbitrary",)),
    )(fl, bf)


def _gate_cumsum_bwd(sums, fl, bf, *, tm=256, name):
    S, W = fl.shape
    tm = min(tm, S)
    n = S // tm
    ns = len(sums)
    assert ns % 2 == 0

    def body(*refs):
        sum_refs = refs[:ns]
        fl_ref, bf_ref, o_ref, s_ref, carry = refs[ns:]
        i = pl.program_id(0)

        @pl.when(i == 0)
        def _():
            carry[...] = jnp.zeros(carry.shape, F32)
            s_ref[...] = jnp.zeros(s_ref.shape, F32)

        dc = sum_refs[0][...] - sum_refs[1][...]
        for a in range(2, ns, 2):
            dc = dc + (sum_refs[a][...] - sum_refs[a + 1][...])
        row = lax.broadcasted_iota(jnp.int32, (tm, tm), 0)
        col = lax.broadcasted_iota(jnp.int32, (tm, tm), 1)
        tri = jnp.where(col >= row, 1.0, 0.0).astype(BF16)
        rs = _tri_dot(tri, dc) + carry[0:1, :]
        carry[...] = jnp.broadcast_to(rs[0:1, :], carry.shape)
        dfl = rs * _sigmoid(-(fl_ref[...] + bf_ref[...]))
        o_ref[...] = dfl
        s_ref[...] += _colsum(dfl)

    rev = lambda i: (n - 1 - i, 0)
    return _pallas(
        body, name=name, grid=(n,),
        in_specs=[pl.BlockSpec((tm, W), rev)] * (ns + 1) + [pl.BlockSpec((1, W), lambda i: (0, 0))],
        out_specs=[pl.BlockSpec((tm, W), rev), pl.BlockSpec((1, W), lambda i: (0, 0))],
        out_shape=[_sds((S, W), F32), _sds((1, W), F32)],
        scratch_shapes=[pltpu.VMEM((8, W), F32)],
        compiler_params=_params(("arbitrary",)),
    )(*sums, fl, bf)


def _tri_tables(nq, qc, by_query):
    if by_query:
        pairs = [(i, j) for i in range(nq) for j in range(qc * (i + 1))]
    else:
        pairs = [(i, j) for j in range(qc * nq) for i in range(j // qc, nq)]
    ii, jj = zip(*pairs)
    return jnp.asarray(np.array(ii, np.int32)), jnp.asarray(np.array(jj, np.int32))


def _chunk_kinds(qc, dd):
    if dd < 0:
        return ("full",) * qc
    return tuple("full" if r > dd else "diag" if r == dd else "skip" for r in range(qc))


def _rep(v, t):
    return jnp.tile(v, (1, t // LANES))


def _attn_fwd(q, kv, ck, *, tb=512, row_chunks=4, name):
    S, D = q.shape
    HP = D // LANES
    T = min(tb, S)
    QC = row_chunks if S >= row_chunks * T else 1
    TQ = QC * T
    it, jt = _tri_tables(S // TQ, QC, True)

    def body(it_ref, jt_ref, q_ref, k_ref, v_ref, ck_ref, o_ref, lse_ref, st):
        s_id = pl.program_id(1)
        i, j = it_ref[s_id], jt_ref[s_id]
        dd = j - QC * i
        lane = lax.broadcasted_iota(jnp.int32, (TQ, LANES), 1)
        head0 = lane < HEAD_DIM
        h0 = head0[:T]
        hms = (h0, jnp.logical_not(h0))

        @pl.when(j == 0)
        def _():
            st[0:2] = jnp.full((2, TQ, LANES), NEG_BIG, F32)
            st[2:4] = jnp.zeros((2, TQ, LANES), F32)

        def step(kinds):
            kvv, vv = k_ref[...], v_ref[...]
            one = jnp.ones_like(vv)
            vaug = [jnp.where(hms[h], vv, one) for h in range(2)]
            old = st[...]
            rows = lambda r: slice(r * T, (r + 1) * T)
            live = [r for r in range(QC) if kinds[r] != "skip"]
            chains = [(r, h) for r in live for h in range(2)]
            ss = {}
            for r, h in chains:
                qv = q_ref[rows(r), :]
                ss[r, h] = lax.dot_general(jnp.where(hms[h], qv, jnp.zeros_like(qv)), kvv, _DIMS["nt"], preferred_element_type=F32)
            ps, alphas, m_new = {}, {}, {}
            for r, h in chains:
                s = ss[r, h] - ck_ref[h:h + 1, :]
                if kinds[r] == "diag":
                    row = lax.broadcasted_iota(jnp.int32, (T, T), 0)
                    col = lax.broadcasted_iota(jnp.int32, (T, T), 1)
                    s = jnp.where(row >= col, s, NEG_BIG)
                m_prev = old[h, rows(r), :]
                m_new[r, h] = jnp.maximum(m_prev, jnp.max(s, axis=1, keepdims=True))
                ps[r, h] = jnp.exp(s - _rep(m_new[r, h], T)).astype(BF16)
                alphas[r, h] = jnp.exp(m_prev - m_new[r, h])
            new = [[], [], [], []]
            for r in range(QC):
                if kinds[r] == "skip":
                    for a in range(4):
                        new[a].append(old[a, rows(r), :])
                    continue
                pv = [jnp.dot(ps[r, h], vaug[h], preferred_element_type=F32) for h in range(2)]
                a0, a1 = alphas[r, 0], alphas[r, 1]
                new[0].append(m_new[r, 0])
                new[1].append(m_new[r, 1])
                new[2].append(jnp.where(h0, a0, a1) * old[2, rows(r), :] + jnp.where(h0, pv[0], pv[1]))
                new[3].append(jnp.where(h0, a1, a0) * old[3, rows(r), :] + jnp.where(h0, pv[1], pv[0]))
            res = jnp.stack([jnp.concatenate(n, axis=0) for n in new], axis=0)
            st[...] = res
            return res

        @pl.when(dd < 0)
        def _():
            step(_chunk_kinds(QC, -1))

        for d in range(QC):
            @pl.when(dd == d)
            def _(d=d):
                res = step(_chunk_kinds(QC, d))
                if d == QC - 1:
                    lr = pltpu.roll(res[3], HEAD_DIM, axis=1)
                    o_ref[...] = res[2] / lr
                    lse_ref[0] = res[0] + jnp.log(jnp.where(head0, lr, res[3]))
                    lse_ref[1] = res[1] + jnp.log(jnp.where(head0, res[3], lr))

    grid_spec = pltpu.PrefetchScalarGridSpec(
        num_scalar_prefetch=2, grid=(HP, it.shape[0]),
        in_specs=[pl.BlockSpec((TQ, LANES), lambda h, s, it, jt: (it[s], h)),
                  pl.BlockSpec((T, LANES), lambda h, s, it, jt: (jt[s], h)),
                  pl.BlockSpec((T, LANES), lambda h, s, it, jt: (jt[s], HP + h)),
                  pl.BlockSpec((None, 8, T), lambda h, s, it, jt: (h, 0, jt[s]))],
        out_specs=[pl.BlockSpec((TQ, LANES), lambda h, s, it, jt: (it[s], h)),
                   pl.BlockSpec((2, TQ, LANES), lambda h, s, it, jt: (h, it[s], 0))],
        scratch_shapes=[pltpu.VMEM((4, TQ, LANES), F32)],
    )
    return _pallas(
        body, name=name, grid_spec=grid_spec,
        out_shape=[_sds((S, D), F32), _sds((2 * HP, S, LANES), F32)],
        compiler_params=_params(("parallel", "arbitrary")),
    )(it, jt, q, kv, kv, ck)


def _attn_bwd(q, kv, o, do, lse, ck, dkv0=None, *, tb=512, row_chunks=4, name):
    S, D = q.shape
    HP = D // LANES
    T = min(tb, S)
    QC = row_chunks if S >= row_chunks * T else 1
    TQ = QC * T
    it, jt = _tri_tables(S // TQ, QC, False)
    n0 = 0 if dkv0 is None else 2

    def body(it_ref, jt_ref, q_ref, k_ref, v_ref, o_ref, do_ref, lse_ref, ck_ref, *rest):
        dq_ref, drs_ref, dk_ref, dv_ref, dcs_ref = rest[n0:]
        s_id = pl.program_id(1)
        i, j = it_ref[s_id], jt_ref[s_id]
        dd = j - QC * i
        lane = lax.broadcasted_iota(jnp.int32, (T, LANES), 1)
        head0 = lane < HEAD_DIM
        hms = (head0, jnp.logical_not(head0))

        @pl.when(s_id == 0)
        def _():
            dq_ref[...] = jnp.zeros(dq_ref.shape, F32)
            drs_ref[...] = jnp.zeros(drs_ref.shape, F32)

        @pl.when(dd >= 0)
        def _():
            dk_ref[...] = jnp.zeros(dk_ref.shape, F32) if dkv0 is None else rest[0][...]
            dv_ref[...] = jnp.zeros(dv_ref.shape, F32) if dkv0 is None else rest[1][...]
            dcs_ref[...] = jnp.zeros(dcs_ref.shape, F32)

        def step(kinds):
            kvv, vv = k_ref[...], v_ref[...]
            one = jnp.ones_like(kvv)
            zero = jnp.zeros_like(kvv)
            rows = lambda r: slice(r * T, (r + 1) * T)
            live = [r for r in range(QC) if kinds[r] != "skip"]
            chains = [(r, h) for r in live for h in range(2)]
            qv = {r: q_ref[rows(r), :] for r in live}
            dob = {r: do_ref[rows(r), :].astype(BF16) for r in live}
            ss = {(r, h): lax.dot_general(jnp.where(hms[h], qv[r], zero), kvv, _DIMS["nt"], preferred_element_type=F32)
                  for r, h in chains}
            dps = {(r, h): lax.dot_general(jnp.where(hms[h], dob[r], zero), vv, _DIMS["nt"], preferred_element_type=F32)
                   for r, h in chains}
            pbs, dsbs = {}, {}
            for r, h in chains:
                s = ss[r, h] - ck_ref[h:h + 1, :]
                if kinds[r] == "diag":
                    row = lax.broadcasted_iota(jnp.int32, (T, T), 0)
                    col = lax.broadcasted_iota(jnp.int32, (T, T), 1)
                    s = jnp.where(row >= col, s, NEG_BIG)
                p = jnp.exp(s - _rep(lse_ref[h, rows(r), :], T))
                prod = dob[r].astype(F32) * o_ref[rows(r), :]
                delta = jnp.sum(jnp.where(hms[h], prod, 0.0), axis=1, keepdims=True)
                pbs[r, h] = p.astype(BF16)
                dsbs[r, h] = (p * (dps[r, h] - delta)).astype(BF16)
            dvs, dks = [None, None], [None, None]
            for r in live:
                dqs = []
                for h in range(2):
                    dqs.append(jnp.dot(dsbs[r, h], jnp.where(hms[h], kvv, one), preferred_element_type=F32))
                    dv = jnp.dot(pbs[r, h].T, dob[r], preferred_element_type=F32)
                    dk = jnp.dot(dsbs[r, h].T, jnp.where(hms[h], qv[r], one), preferred_element_type=F32)
                    dvs[h] = dv if dvs[h] is None else dvs[h] + dv
                    dks[h] = dk if dks[h] is None else dks[h] + dk
                qrows = pl.ds(pl.multiple_of(i * TQ + r * T, T), T)
                dq_ref[qrows, :] += jnp.where(head0, dqs[0], dqs[1])
                drs_ref[qrows, :] += jnp.where(head0, dqs[1], dqs[0])
            dv_ref[...] += jnp.where(head0, dvs[0], dvs[1])
            dk_ref[...] += jnp.where(head0, dks[0], dks[1])
            dcs_ref[...] += jnp.where(head0, dks[1], dks[0])

        @pl.when(dd < 0)
        def _():
            step(_chunk_kinds(QC, -1))

        for d in range(QC):
            @pl.when(dd == d)
            def _(d=d):
                step(_chunk_kinds(QC, d))

    by_q = lambda h, s, it, jt: (it[s], h)
    by_k = lambda h, s, it, jt: (jt[s], h)
    whole = lambda h, s, it, jt: (0, h)
    grid_spec = pltpu.PrefetchScalarGridSpec(
        num_scalar_prefetch=2, grid=(HP, it.shape[0]),
        in_specs=[pl.BlockSpec((TQ, LANES), by_q),
                  pl.BlockSpec((T, LANES), by_k),
                  pl.BlockSpec((T, LANES), lambda h, s, it, jt: (jt[s], HP + h)),
                  pl.BlockSpec((TQ, LANES), by_q),
                  pl.BlockSpec((TQ, LANES), by_q),
                  pl.BlockSpec((2, TQ, LANES), lambda h, s, it, jt: (h, it[s], 0)),
                  pl.BlockSpec((None, 8, T), lambda h, s, it, jt: (h, 0, jt[s]))]
        + [pl.BlockSpec((T, LANES), by_k)] * n0,
        out_specs=[pl.BlockSpec((S, LANES), whole), pl.BlockSpec((S, LANES), whole),
                   pl.BlockSpec((T, LANES), by_k), pl.BlockSpec((T, LANES), by_k), pl.BlockSpec((T, LANES), by_k)],
        scratch_shapes=[],
    )
    return _pallas(
        body, name=name, grid_spec=grid_spec,
        out_shape=[_sds((S, D), F32)] * 5,
        compiler_params=_params(("parallel", "arbitrary")),
    )(it, jt, q, kv, kv, o, do, lse, ck, *(dkv0 or ()))


ANY = pl.BlockSpec(memory_space=pl.ANY)


def _coords():
    x, y, c = lax.axis_index("x"), lax.axis_index("y"), lax.axis_index("c")
    return x, y, c


def _remote(src, dst, send_sems, recv_sems, k, to):
    return pltpu.make_async_remote_copy(src_ref=src, dst_ref=dst, send_sem=send_sems.at[k], recv_sem=recv_sems.at[k],
                                        device_id=to, device_id_type=MESH)


def _all_gather_shards(pack, *, name):
    R, C = pack.shape
    assert R % 4 == 0
    H, Q = R // 2, R // 4

    def body(in_ref, out_ref, send_sems, recv_sems):
        x, y, c = _coords()
        me, sib = (x, y, c), (x, y, 1 - c)
        xn, yn = (1 - x, y, c), (x, 1 - y, c)
        s, sx, sy, sd = 2 * x + y, 2 * (1 - x) + y, 2 * x + 1 - y, 2 * (1 - x) + 1 - y
        half = pl.ds(c * H, H)
        other = pl.ds((1 - c) * H, H)
        q0 = pl.ds(c * H, Q)
        q1 = pl.ds(c * H + Q, Q)
        rc = functools.partial(_remote, send_sems=send_sems, recv_sems=recv_sems)

        sends = [rc(in_ref.at[half], out_ref.at[s, half], k=0, to=xn),
                 rc(in_ref.at[half], out_ref.at[s, half], k=1, to=yn),
                 rc(in_ref, out_ref.at[s], k=7, to=sib)]
        for cp in sends:
            cp.start()
        rc(in_ref.at[half], out_ref.at[sx, half], k=0, to=me).wait_recv()
        sends.append(rc(out_ref.at[sx, q0], out_ref.at[sx, q0], k=2, to=yn))
        sends[-1].start()
        sends.append(rc(out_ref.at[sx, half], out_ref.at[sx, half], k=4, to=sib))
        sends[-1].start()
        rc(in_ref.at[half], out_ref.at[sy, half], k=1, to=me).wait_recv()
        sends.append(rc(out_ref.at[sy, q1], out_ref.at[sy, q1], k=3, to=xn))
        sends[-1].start()
        sends.append(rc(out_ref.at[sy, half], out_ref.at[sy, half], k=5, to=sib))
        sends[-1].start()
        rc(out_ref.at[sd, q0], out_ref.at[sd, q0], k=2, to=me).wait_recv()
        rc(out_ref.at[sd, q1], out_ref.at[sd, q1], k=3, to=me).wait_recv()
        sends.append(rc(out_ref.at[sd, half], out_ref.at[sd, half], k=6, to=sib))
        sends[-1].start()
        for k, sh in ((4, sx), (5, sy), (6, sd)):
            rc(out_ref.at[sh, other], out_ref.at[sh, other], k=k, to=me).wait_recv()
        rc(in_ref, out_ref.at[s], k=7, to=me).wait_recv()
        for cp in sends:
            cp.wait_send()

    return _pallas(
        body, name=name, in_specs=[ANY], out_specs=ANY,
        out_shape=_sds((N_CHIPS, R, C), pack.dtype),
        scratch_shapes=[pltpu.SemaphoreType.DMA((8,)), pltpu.SemaphoreType.DMA((8,))],
    )(pack)


def _rs_pair(g, *, name):
    n, R, C = g.shape
    H = R // 2

    def body(g_ref, land_ref, send_sems, recv_sems):
        x, y, c = _coords()
        other = pl.ds((1 - c) * H, H)
        cps = [_remote(g_ref.at[sh, other], land_ref.at[sh], send_sems, recv_sems, sh, (x, y, 1 - c)) for sh in range(n)]
        for cp in cps:
            cp.start()
        for cp in cps:
            cp.wait_recv()
        for cp in cps:
            cp.wait_send()

    return _pallas(
        body, name=name, in_specs=[ANY], out_specs=ANY, out_shape=_sds((n, H, C), g.dtype),
        scratch_shapes=[pltpu.SemaphoreType.DMA((n,)), pltpu.SemaphoreType.DMA((n,))],
    )(g)


def _rs_quarters(p, *, name):
    n, H, C = p.shape
    Q = H // 2

    def body(p_ref, la_ref, lb_ref, send_sems, recv_sems):
        x, y, c = _coords()
        sd = 2 * (1 - x) + 1 - y
        a = _remote(p_ref.at[sd, pl.ds(0, Q)], la_ref, send_sems, recv_sems, 0, (x, 1 - y, c))
        b = _remote(p_ref.at[sd, pl.ds(Q, Q)], lb_ref, send_sems, recv_sems, 1, (1 - x, y, c))
        a.start()
        b.start()
        a.wait_recv()
        b.wait_recv()
        a.wait_send()
        b.wait_send()

    return _pallas(
        body, name=name, in_specs=[ANY], out_specs=[ANY, ANY],
        out_shape=[_sds((Q, C), p.dtype), _sds((Q, C), p.dtype)],
        scratch_shapes=[pltpu.SemaphoreType.DMA((2,)), pltpu.SemaphoreType.DMA((2,))],
    )(p)


def _rs_halves(p, ax, ay, *, name):
    n, H, C = p.shape
    Q = H // 2

    def body(p_ref, ax_ref, ay_ref, la_ref, lb_ref, send_sems, recv_sems):
        x, y, c = _coords()
        sx, sy = 2 * (1 - x) + y, 2 * x + 1 - y
        xn, yn = (1 - x, y, c), (x, 1 - y, c)
        lo, hi = pl.ds(0, Q), pl.ds(Q, Q)
        cps = [_remote(ax_ref, la_ref.at[lo], send_sems, recv_sems, 0, xn),
               _remote(p_ref.at[sx, hi], la_ref.at[hi], send_sems, recv_sems, 1, xn),
               _remote(p_ref.at[sy, lo], lb_ref.at[lo], send_sems, recv_sems, 2, yn),
               _remote(ay_ref, lb_ref.at[hi], send_sems, recv_sems, 3, yn)]
        for cp in cps:
            cp.start()
        for cp in cps:
            cp.wait_recv()
        for cp in cps:
            cp.wait_send()

    return _pallas(
        body, name=name, in_specs=[ANY, ANY, ANY], out_specs=[ANY, ANY],
        out_shape=[_sds((H, C), p.dtype), _sds((H, C), p.dtype)],
        scratch_shapes=[pltpu.SemaphoreType.DMA((4,)), pltpu.SemaphoreType.DMA((4,))],
    )(p, ax, ay)


def _rs_join(buf, *, name):
    R, C = buf.shape
    H = R // 2

    def body(in_ref, out_ref, send_sems, recv_sems):
        x, y, c = _coords()
        half = pl.ds(c * H, H)
        other = pl.ds((1 - c) * H, H)
        cp = _remote(in_ref.at[half], out_ref.at[half], send_sems, recv_sems, 0, (x, y, 1 - c))
        cp.start()
        _remote(in_ref.at[other], out_ref.at[other], send_sems, recv_sems, 0, (x, y, c)).wait_recv()
        cp.wait_send()

    return _pallas(
        body, name=name, in_specs=[ANY], out_specs=ANY, out_shape=_sds((R, C), buf.dtype),
        input_output_aliases={0: 0},
        scratch_shapes=[pltpu.SemaphoreType.DMA((1,)), pltpu.SemaphoreType.DMA((1,))],
    )(buf)


def _tile_add(ins_specs, arrays, n_steps, out_spec, out_shape, scalars, *, name):
    grid_spec = pltpu.PrefetchScalarGridSpec(
        num_scalar_prefetch=1, grid=(n_steps,), in_specs=ins_specs, out_specs=out_spec, scratch_shapes=[])

    def body(sc_ref, *refs):
        acc = refs[0][...].astype(F32)
        for r in refs[1:-1]:
            acc = acc + r[...].astype(F32)
        refs[-1][...] = acc.astype(refs[-1].dtype)

    return _pallas(body, name=name, grid_spec=grid_spec, out_shape=out_shape,
                   compiler_params=_params(("arbitrary",)))(scalars, *arrays)


def _reduce_scatter(g, *, name):
    n, R, C = g.shape
    H, Q = R // 2, R // 4
    tm = RS_ROW_MULT // 4
    assert Q % tm == 0, (R, tm)
    x, y, c = _coords()
    sx, sy, s = 2 * (1 - x) + y, 2 * x + 1 - y, 2 * x + y
    sc = jnp.stack([c, sx, sy, s]).astype(jnp.int32)
    hb, qb = H // tm, Q // tm
    blk = lambda f: pl.BlockSpec((None, tm, C), f)
    flat = lambda f: pl.BlockSpec((tm, C), f)

    land = _rs_pair(g, name=name + "_pair")
    p = _tile_add([blk(lambda i, sc: (i // hb, sc[0] * hb + i % hb, 0)), blk(lambda i, sc: (i // hb, i % hb, 0))],
                  [g, land], n * hb, blk(lambda i, sc: (i // hb, i % hb, 0)), _sds((n, H, C), g.dtype), sc, name=name + "_add0")
    la, lb = _rs_quarters(p, name=name + "_quarters")
    ax = _tile_add([blk(lambda i, sc: (sc[1], i, 0)), flat(lambda i, sc: (i, 0))], [p, la], qb,
                   flat(lambda i, sc: (i, 0)), _sds((Q, C), g.dtype), sc, name=name + "_add1x")
    ay = _tile_add([blk(lambda i, sc: (sc[2], qb + i, 0)), flat(lambda i, sc: (i, 0))], [p, lb], qb,
                   flat(lambda i, sc: (i, 0)), _sds((Q, C), g.dtype), sc, name=name + "_add1y")
    fa, fb = _rs_halves(p, ax, ay, name=name + "_halves")
    buf = _tile_add([blk(lambda i, sc: (sc[3], i, 0)), flat(lambda i, sc: (i, 0)), flat(lambda i, sc: (i, 0))],
                    [p, fa, fb], hb, flat(lambda i, sc: (sc[0] * hb + i, 0)), _sds((R, C), F32), sc, name=name + "_add2")
    return _rs_join(buf, name=name + "_join")


def _all_reduce_small(v, *, name):
    M, N = v.shape

    def body(x_ref, out_ref, send_sems, recv_sems, local_sem):
        x, y, c = _coords()
        me, sibling = (x, y, c), (x, y, 1 - c)
        chips = [(1 - x, y), (x, 1 - y), (1 - x, 1 - y)]

        def rows(px, py, pc):
            return out_ref.at[pl.ds((4 * px + 2 * py + pc) * M, M), :]

        def copy(k, block, to, src=None):
            return pltpu.make_async_remote_copy(
                src_ref=rows(*block) if src is None else src, dst_ref=rows(*block),
                send_sem=send_sems.at[k], recv_sem=recv_sems.at[k], device_id=to, device_id_type=MESH)

        mine = pltpu.make_async_copy(x_ref, rows(*me), local_sem)
        mine.start()
        first = [copy(0, me, sibling, src=x_ref)]
        first += [copy(1 + j, me, (*chip, c), src=x_ref) for j, chip in enumerate(chips)]
        for cp in first:
            cp.start()
        passed = [copy(4 + j, (*chip, c), sibling) for j, chip in enumerate(chips)]
        for j, chip in enumerate(chips):
            copy(1 + j, (*chip, c), me).wait_recv()
            passed[j].start()
        copy(0, sibling, me).wait_recv()
        for j, chip in enumerate(chips):
            copy(4 + j, (*chip, 1 - c), me).wait_recv()
        for cp in first + passed:
            cp.wait_send()
        mine.wait()

    gathered = _pallas(
        body, name=name + "_gather",
        out_shape=_sds((N_DEV * M, N), F32),
        in_specs=[pl.BlockSpec(memory_space=pltpu.VMEM)],
        out_specs=pl.BlockSpec(memory_space=pltpu.VMEM),
        scratch_shapes=[pltpu.SemaphoreType.DMA((7,)), pltpu.SemaphoreType.DMA((7,)), pltpu.SemaphoreType.DMA],
    )(v)

    def sum_body(g_ref, o_ref):
        acc = g_ref[0:M, :]
        for d in range(1, N_DEV):
            acc = acc + g_ref[d * M:(d + 1) * M, :]
        o_ref[...] = acc

    return _pallas(sum_body, name=name + "_sum", out_shape=_sds((M, N), F32))(gathered)


MATS = [("ffn_w1", 2), ("ffn_w2", 1), ("ple_w_gate", 1), ("conv_w_pw1", 2), ("conv_w_pw2", 1), ("attn_w_q", 1), ("attn_w_o", 1),
        ("ple_w_proj", 2), ("w_kvf", 1)]
VECS = [("conv_b_pw1", 1), ("conv_w_dw", 2), ("conv_b_dw", 1), ("conv_ln_g", 1), ("conv_ln_b", 1), ("conv_b_pw2", 1)]
REPL = ["mix_norm", "ffn_norm", "ple_norm", "kv_norm", "final_norm", "b_f"]
WEIGHTS = ["mix_norm", "conv_w_pw1", "conv_b_pw1", "conv_w_dw", "conv_b_dw", "conv_ln_g", "conv_ln_b", "conv_w_pw2",
           "conv_b_pw2", "kv_norm", "w_kvf", "b_f", "attn_w_q", "attn_w_o", "ffn_norm", "ffn_w1", "ffn_w2", "ple_norm",
           "ple_w_gate", "ple_w_proj", "final_norm"]


def _round_up(n, m):
    return -(-n // m) * m


def _to_rows(t, C, mult):
    flat = t.reshape(-1)
    rows = _round_up(_round_up(flat.shape[0], C) // C, mult)
    flat = jnp.pad(flat, (0, rows * C - flat.shape[0]))
    return flat.reshape(rows, C)


def _pack(tensors, C, mult, total_mult):
    parts = [_to_rows(t, C, mult) for t in tensors]
    rows = sum(p.shape[0] for p in parts)
    pad = _round_up(rows, total_mult) - rows
    if pad:
        parts.append(jnp.zeros((pad, C), parts[0].dtype))
    return jnp.concatenate(parts, axis=0)


def _row_counts(shapes, C, mult):
    return [_round_up(_round_up(int(np.prod(s)), C) // C, mult) for s in shapes]


def _unpack(packed, shapes, C, mult):
    outs, r0 = [], 0
    lead = packed.shape[:-2]
    for shp, nr in zip(shapes, _row_counts(shapes, C, mult)):
        n = int(np.prod(shp))
        seg = packed[..., r0:r0 + nr, :].reshape(lead + (nr * C,))[..., :n]
        outs.append(seg.reshape(lead + tuple(shp)))
        r0 += nr
    return outs


def _unshard(t, axis):
    return jnp.concatenate([t[s] for s in range(N_CHIPS)], axis=axis)


def _shards(t, axis):
    return jnp.split(t, N_CHIPS, axis=axis)


class _GradPack:
    DIRECT = {"conv_w_pw2": "rows", "attn_w_q": "rows", "attn_w_o": "rows", "ffn_w2": "rows", "ple_w_gate": "rows",
              "ffn_w1": "cols"}

    def __init__(self, names, shard_shapes, C):
        self.C, self.names, self.shapes, self.off = C, names, dict(zip(names, shard_shapes)), {}
        r = 0
        for n, cnt in zip(names, _row_counts(shard_shapes, C, 16)):
            self.off[n] = r
            r += cnt
        self.R = _round_up(r, RS_ROW_MULT)
        self.buf = jnp.zeros((N_CHIPS, self.R, C), BF16)

    def matmul(self, wname, layer, a, b, **kw):
        rows_s, cols_s = self.shapes[wname][-2:]
        assert cols_s == self.C
        base = self.off[wname] + layer * rows_s
        if self.DIRECT[wname] == "rows" and N_CHIPS * rows_s <= 1024:
            assert base % rows_s == 0
            tm, block = N_CHIPS * rows_s, (N_CHIPS, rows_s, self.C)
            out_map = lambda i, j, k: (0, base // rows_s, 0)
        else:
            tm = next(t for t in (1024, 512, 256, 128, 64, 32, 16) if rows_s % t == 0 and base % t == 0)
            per, first, block = rows_s // tm, base // tm, (None, tm, self.C)
            if self.DIRECT[wname] == "rows":
                out_map = lambda i, j, k: (i // per, first + i % per, 0)
            else:
                assert a.shape[1] == rows_s
                out_map = lambda i, j, k: (j, first + i, 0)
        self.buf = _mm(a, b, mode="tn", tk=2048, tm=tm, tn=self.C, out_dtype=BF16, into=self.buf, out_block=block,
                       out_map=out_map, **kw)

    def insert(self, grads, axes):
        run = []
        for n in self.names + [None]:
            if n is not None and n not in self.DIRECT:
                run.append(n)
                continue
            if run:
                for s in range(N_CHIPS):
                    rows = jnp.concatenate([_to_rows(_shards(grads[m].astype(BF16), axes[m])[s], self.C, 16) for m in run], axis=0)
                    self.buf = lax.dynamic_update_slice(self.buf, rows[None], (s, self.off[run[0]], 0))
                run = []


def _local_step(x, p, tgt, w, gp=None):
    S, D = x.shape
    L = p.shape[0]
    NA = w["conv_w_pw1"].shape[0]
    H = w["b_f"].shape[0]
    HP = D // LANES
    row = lambda v: v.reshape(1, -1)
    act = lambda dt, n=D: _sds((S, n), dt)
    g = {}

    def dw(wname, layer, a, b, **kw):
        if gp is not None and wname in gp.DIRECT:
            gp.matmul(wname, layer, a, b, **kw)
            return None
        return _mm(a, b, mode="tn", tk=2048, out_dtype=BF16, **kw)

    def rms(hh, gain, name):
        return _rows(_rms_fwd, [hh], [row(gain)], [act(BF16)], name=name)[0]

    saved = []
    h = x
    kv = ck = fl = nkv = h_kv = None
    bfp = jnp.pad(w["b_f"], (0, LANES - H)).reshape(1, LANES)
    wk = w["w_kvf"][:, :D]
    wv = w["w_kvf"][:, D:2 * D]
    wkv = w["w_kvf"][:, :2 * D]
    wf = jnp.pad(w["w_kvf"][:, 2 * D:], ((0, 0), (0, LANES - H)))
    res_rms = lambda acc, r, gn: _with_rms(acc + r, gn)
    hn = rms(h, w["mix_norm"][0], "rms_mix")
    for i in range(L):
        sv = {"h0": h, "hn": hn}
        g_ffn, g_ple = row(w["ffn_norm"][i]), row(w["ple_norm"][i])
        if i < NA:
            u = _mm(hn, w["conv_w_pw1"][i], extras=[row(w["conv_b_pw1"][i])], epi=lambda acc, b: acc + b, name="mm_pw1")
            glu = _rows(_glu_fwd, [u], [], [act(F32)], name="glu_fwd")[0]
            wdw = jnp.pad(w["conv_w_dw"][i], ((0, CONV_PAD - CONV_WIDTH), (0, 0)))
            cv = _dwconv_fwd(glu, wdw, row(w["conv_b_dw"][i]), name="dwconv_fwd")
            sw = _rows(_ln_silu_fwd, [cv], [row(w["conv_ln_g"][i]), row(w["conv_ln_b"][i])], [act(BF16)], name="ln_silu_fwd")[0]
            h1, hn2 = _mm(sw, w["conv_w_pw2"][i], extras=[row(w["conv_b_pw2"][i]), h, g_ffn],
                          epi=lambda acc, b, r, gn: _with_rms(acc + b + r, gn), out_dtype=[F32, BF16], tm=512, tn=D, name="mm_pw2")
            sv.update(u=u, glu=glu, cv=cv, sw=sw, wdw=wdw)
        else:
            if i == NA:
                h_kv = h
                nkv = rms(h, w["kv_norm"], "rms_kv")
                kv = _mm(nkv, wkv, out_dtype=BF16, name="mm_kv")
                fl = _mm(nkv, wf, name="mm_f")
                c = _gate_cumsum(fl, bfp, name="gate_cumsum")
                ck = jnp.pad(c[:, :H].T.reshape(HP, 2, S), ((0, 0), (0, 6), (0, 0)))
            j = i - NA
            q = _mm(hn, w["attn_w_q"][j], epi=lambda acc: acc * (HEAD_DIM ** -0.5), out_dtype=BF16, name="mm_q")
            o, lse = _attn_fwd(q, kv, ck, name="attn_fwd")
            h1, hn2 = _mm(o, w["attn_w_o"][j], extras=[h, g_ffn], epi=res_rms, out_dtype=[F32, BF16], tm=512, tn=D, name="mm_o")
            sv.update(q=q, o=o, lse=lse)
        zb, f = _mm(hn2, w["ffn_w1"][i], epi=lambda acc: (acc, jnp.square(jnp.maximum(acc, 0.0))), out_dtype=[BF16, BF16],
                    name="mm_ffn1")
        h2, n3 = _mm(f, w["ffn_w2"][i], extras=[h1, g_ple], epi=res_rms, out_dtype=[F32, BF16], tm=512, tn=D, tk=4 * D, name="mm_ffn2")
        zg = _mm(n3, w["ple_w_gate"][i], name="mm_gate")
        ple = lambda acc, r, zz: r + _sigmoid(zz) * acc
        if i + 1 < L:
            h, hn = _mm(p[i], w["ple_w_proj"][i], extras=[h2, zg, row(w["mix_norm"][i + 1])],
                        epi=lambda acc, r, zz, gn: _with_rms(ple(acc, r, zz), gn), out_dtype=[F32, BF16], tm=1024, tn=D, name="mm_proj")
        else:
            h = _mm(p[i], w["ple_w_proj"][i], extras=[h2, zg], epi=ple, tm=512, tn=D, name="mm_proj_last")
        sv.update(h1=h1, hn2=hn2, zb=zb, f=f, h2=h2, n3=n3, zg=zg)
        saved.append(sv)

    dh, err2, g_final = _rows(_final_fn, [h, tgt], [row(w["final_norm"])], [act(F32)], [_sds((1, D), F32), _sds((1, D), F32)],
                              name="final")
    loss = 0.5 * jnp.sum(err2) / D
    g["final_norm"] = g_final.reshape(-1)

    red = _sds((1, D), F32)
    stack = {k: [None] * n for k, n in (("mix_norm", L), ("ffn_norm", L), ("ple_norm", L), ("ffn_w1", L), ("ffn_w2", L),
                                        ("ple_w_gate", L), ("ple_w_proj", L), ("conv_w_pw1", NA), ("conv_b_pw1", NA),
                                        ("conv_w_dw", NA), ("conv_b_dw", NA), ("conv_ln_g", NA), ("conv_ln_b", NA),
                                        ("conv_w_pw2", NA), ("conv_b_pw2", NA), ("attn_w_q", L - NA), ("attn_w_o", L - NA))}
    dk_sum = dv_sum = None
    dcks = []
    for i in reversed(range(L)):
        sv = saved[i]
        dzg, dpp = _mm(p[i], w["ple_w_proj"][i], extras=[dh, sv["zg"]], epi=lambda acc, d, zz: _ple_bwd(d, zz, acc),
                       out_dtype=[BF16, BF16], name="mm_ple_bwd")
        stack["ple_w_proj"][i] = _mm(p[i], dpp, mode="tn", tk=2048, out_dtype=BF16, name="mm_dproj")
        stack["ple_w_gate"][i] = dw("ple_w_gate", i, sv["n3"], dzg, name="mm_dgate")
        dh, dhb, dgain = _mm(dzg, w["ple_w_gate"][i], mode="nt", extras=[sv["h2"], dh, row(w["ple_norm"][i])], epi=_dup(_rms_bwd),
                             out_dtype=[F32, BF16], reds=1, tm=1024, tn=D, name="mm_dn3")
        stack["ple_norm"][i] = dgain.reshape(-1)
        dz = _mm(dhb, w["ffn_w2"][i], mode="nt", extras=[sv["zb"]], epi=lambda acc, zz: acc * (2.0 * jnp.maximum(zz, 0.0).astype(F32)),
                 out_dtype=BF16, name="mm_dz")
        stack["ffn_w2"][i] = dw("ffn_w2", i, sv["f"], dhb, name="mm_dffn2")
        stack["ffn_w1"][i] = dw("ffn_w1", i, sv["hn2"], dz, name="mm_dffn1")
        if i < NA:
            dh, dhb, dgain, dbias = _mm(dz, w["ffn_w1"][i], mode="nt", extras=[sv["h1"], dh, row(w["ffn_norm"][i])],
                                        epi=_dup(_rms_bwd_bias), out_dtype=[F32, BF16], reds=2, tm=512, tn=D, tk=4 * D, name="mm_dhn2_bias")
            stack["conv_b_pw2"][i] = dbias.reshape(-1)
        else:
            dh, dhb, dgain = _mm(dz, w["ffn_w1"][i], mode="nt", extras=[sv["h1"], dh, row(w["ffn_norm"][i])], epi=_dup(_rms_bwd),
                                 out_dtype=[F32, BF16], reds=1, tm=512, tn=D, tk=4 * D, name="mm_dhn2")
        stack["ffn_norm"][i] = dgain.reshape(-1)
        if i < NA:
            dsw = _mm(dhb, w["conv_w_pw2"][i], mode="nt", name="mm_dsw")
            stack["conv_w_pw2"][i] = dw("conv_w_pw2", i, sv["sw"], dhb, name="mm_dpw2")
            dcv, dlg, dlb = _rows(_ln_silu_bwd, [dsw, sv["cv"]], [row(w["conv_ln_g"][i]), row(w["conv_ln_b"][i])], [act(F32)],
                                  [red, red], name="ln_silu_bwd")
            stack["conv_ln_g"][i], stack["conv_ln_b"][i] = dlg.reshape(-1), dlb.reshape(-1)
            dglu, dwdw, dbdw = _dwconv_bwd(dcv, sv["glu"], sv["wdw"], name="dwconv_bwd")
            stack["conv_w_dw"][i], stack["conv_b_dw"][i] = dwdw[:CONV_WIDTH], dbdw.reshape(-1)
            du, dbu = _rows(_glu_bwd, [dglu, sv["u"]], [], [act(BF16, 2 * D)], [_sds((1, 2 * D), F32)], name="glu_bwd")
            stack["conv_b_pw1"][i] = dbu.reshape(-1)
            stack["conv_w_pw1"][i] = _mm(sv["hn"], du, mode="tn", tk=2048, out_dtype=BF16, name="mm_dpw1")
            dh, dgain = _mm(du, w["conv_w_pw1"][i], mode="nt", extras=[sv["h0"], dh, row(w["mix_norm"][i])], epi=_rms_bwd,
                            reds=1, tm=1024, tn=D, name="mm_dhn_a")
        else:
            j = i - NA
            do = _mm(dhb, w["attn_w_o"][j], mode="nt", out_dtype=BF16, name="mm_do")
            stack["attn_w_o"][j] = dw("attn_w_o", j, sv["o"], dhb, name="mm_dwo")
            dq, drs, dk_sum, dv_sum, dcs = _attn_bwd(sv["q"], kv, sv["o"], do, sv["lse"], ck,
                                                     None if dk_sum is None else (dk_sum, dv_sum), name="attn_bwd")
            scale = lambda acc: acc * (HEAD_DIM ** -0.5)
            stack["attn_w_q"][j] = dw("attn_w_q", j, sv["hn"], dq, epi=scale, name="mm_dwq")
            dh_in = dh
            dh, dgain = _mm(dq, w["attn_w_q"][j], mode="nt", extras=[sv["h0"], dh_in, row(w["mix_norm"][i])],
                            epi=lambda acc, xx, dr, gn: _rms_bwd(scale(acc), xx, dr, gn), reds=1, tm=1024, tn=D, name="mm_dhn_b")
            pick = lambda t: jnp.pad(t.reshape(S, HP, 2, HEAD_DIM)[:, :, ::-1, 0].reshape(S, H), ((0, 0), (0, LANES - H)))
            dcks += [pick(drs), pick(dcs)]
        stack["mix_norm"][i] = dgain.reshape(-1)
        if i == NA:
            dfl, dbf = _gate_cumsum_bwd(dcks, fl, bfp, name="gate_cumsum_bwd")
            g["b_f"] = dbf[0, :H]
            gk = _mm(nkv, dk_sum, mode="tn", tk=2048, out_dtype=BF16, name="mm_dwk")
            gv = _mm(nkv, dv_sum, mode="tn", tk=2048, out_dtype=BF16, name="mm_dwk")
            gf = _mm(nkv, dfl, mode="tn", tk=2048, out_dtype=BF16, name="mm_dwf")
            g["w_kvf"] = jnp.concatenate([gk, gv, gf[:, :H]], axis=1)
            dn = _mm(dk_sum, wk, mode="nt", name="mm_dnk")
            dn = _mm(dv_sum, wv, mode="nt", extras=[dn], epi=lambda acc, r: acc + r, name="mm_dnv")
            dh, dgain = _mm(dfl, wf, mode="nt", extras=[dn, h_kv, dh, row(w["kv_norm"])],
                            epi=lambda acc, r, xx, dr, gn: _rms_bwd(acc + r, xx, dr, gn), reds=1, tm=512, tn=D, name="mm_dnf")
            g["kv_norm"] = dgain.reshape(-1)
    for k, v in stack.items():
        if v[0] is not None:
            g[k] = jnp.stack(v, axis=0)
    return loss, dh, g


def kernel(x, p, mix_norm, conv_w_pw1, conv_b_pw1, conv_w_dw, conv_b_dw, conv_ln_g, conv_ln_b, conv_w_pw2, conv_b_pw2, kv_norm, w_kvf, b_f, attn_w_q, attn_w_o, ffn_norm, ffn_w1, ffn_w2, ple_norm, ple_w_gate, ple_w_proj, final_norm, loss_target, m_mix_norm, m_conv_w_pw1, m_conv_b_pw1, m_conv_w_dw, m_conv_b_dw, m_conv_ln_g, m_conv_ln_b, m_conv_w_pw2, m_conv_b_pw2, m_kv_norm, m_w_kvf, m_b_f, m_attn_w_q, m_attn_w_o, m_ffn_norm, m_ffn_w1, m_ffn_w2, m_ple_norm, m_ple_w_gate, m_ple_w_proj, m_final_norm, v_mix_norm, v_conv_w_pw1, v_conv_b_pw1, v_conv_w_dw, v_conv_b_dw, v_conv_ln_g, v_conv_ln_b, v_conv_w_pw2, v_conv_b_pw2, v_kv_norm, v_w_kvf, v_b_f, v_attn_w_q, v_attn_w_o, v_ffn_norm, v_ffn_w1, v_ffn_w2, v_ple_norm, v_ple_w_gate, v_ple_w_proj, v_final_norm):
    args = dict(locals())
    wl = {n: args[n] for n in WEIGHTS}
    ml = {n: args["m_" + n] for n in WEIGHTS}
    vl = {n: args["v_" + n] for n in WEIGHTS}
    S, D = x.shape[1], x.shape[2]
    C = D

    mat_shapes = [wl[n].shape for n, _ in MATS]
    vec_shapes = [wl[n].shape for n, _ in VECS]
    mats = _all_gather_shards(_pack([wl[n].astype(BF16) for n, _ in MATS], C, 16, 64), name="ag_mats")
    vecs = _all_gather_shards(_pack([wl[n] for n, _ in VECS], C, 1, 32), name="ag_vecs")
    full = {n: wl[n] for n in REPL}
    for (n, ax), t in zip(MATS, _unpack(mats, mat_shapes, C, 16)):
        full[n] = _unshard(t, ax)
    for (n, ax), t in zip(VECS, _unpack(vecs, vec_shapes, C, 1)):
        full[n] = _unshard(t, ax)

    names = [n for n, _ in MATS] + [n for n, _ in VECS]
    axes = dict(MATS + VECS)
    shard_shapes = [wl[n].shape for n in names]
    gp = _GradPack(names, shard_shapes, C)
    loss, dx, g = _local_step(x[0], p[:, 0], loss_target[0], full, gp)
    loss = lax.psum(loss, ("x", "y", "c"))

    gp.insert(g, axes)
    gred = _reduce_scatter(gp.buf, name="rs")
    gl = dict(zip(names, _unpack(gred, shard_shapes, C, 16)))

    rep_shapes = [wl[n].shape for n in REPL]
    rpack = _pack([g[n] for n in REPL], C, 1, 8)
    for n, t in zip(REPL, _unpack(_all_reduce_small(rpack, name="ar"), rep_shapes, C, 1)):
        gl[n] = t

    grads, deltas, new_m, new_v = {}, {}, {}, {}
    rep_w, rep_m, rep_v = (_pack([d[n] for n in REPL], C, 1, 8) for d in (wl, ml, vl))
    rep_out = _adamw(rep_w, rpack_like(gl, rep_shapes, C), rep_m, rep_v, "adamw_rep")
    for dst, packed in zip((grads, deltas, new_m, new_v), rep_out):
        for n, t in zip(REPL, _unpack(packed, rep_shapes, C, 1)):
            dst[n] = t
    for n in names:
        rows_n = int(np.prod(wl[n].shape[:-1]))
        step_rows = min(512, rows_n)
        if wl[n].shape[-1] == C and gp.off[n] % step_rows == 0 and rows_n % step_rows == 0:
            res = _adamw(wl[n], gred, ml[n], vl[n], "adamw_" + n, g_row=gp.off[n])
        else:
            res = _adamw(wl[n], gl[n], ml[n], vl[n], "adamw_" + n)
        for dst, t in zip((grads, deltas, new_m, new_v), res):
            dst[n] = t
    out = [loss, dx[None]]
    for d in (grads, deltas, new_m, new_v):
        out += [d[n] for n in WEIGHTS]
    return tuple(out)


def rpack_like(gl, rep_shapes, C):
    return _pack([gl[n] for n in REPL], C, 1, 8)
```

```python
import functools

import jax
import jax.numpy as jnp
import numpy as np
from jax import lax
from jax.experimental import pallas as pl
from jax.experimental.pallas import tpu as pltpu

F32 = jnp.float32
BF16 = jnp.bfloat16
MESH = pl.DeviceIdType.MESH

N_CHIPS = 4
N_DEV = 8
HEAD_DIM = 64
LANES = 128
CONV_WIDTH = 31
CONV_PAD = 32
EPS = 1e-6
NEG_BIG = -1e30
VMEM_LIMIT = 56 * 1024 * 1024

RS_ROW_MULT = 1024

ADAM_LR, ADAM_B1, ADAM_B2, ADAM_EPS, ADAM_WD, ADAM_STEP = 0.001, 0.9, 0.999, 1e-08, 0.01, 10


def _pallas(body, **kw):
    return pl.pallas_call(body, **kw)


def _params(sem=None):
    return pltpu.CompilerParams(dimension_semantics=sem, vmem_limit_bytes=VMEM_LIMIT)


def _sds(shape, dtype):
    return jax.ShapeDtypeStruct(tuple(shape), dtype)


_DIMS = {"nn": (((1,), (0,)), ((), ())), "nt": (((1,), (1,)), ((), ())), "tn": (((0,), (0,)), ((), ()))}


def _mm(a, b, *, mode="nn", extras=(), epi=None, out_dtype=F32, reds=0, tm=1024, tn=1024, tk=1024, into=None, out_block=None,
        out_map=None, name):
    if mode == "nn":
        (M, K), (K2, N) = a.shape, b.shape
    elif mode == "nt":
        (M, K), (N, K2) = a.shape, b.shape
    else:
        (K, M), (K2, N) = a.shape, b.shape
    assert K == K2, (name, a.shape, b.shape)
    tm, tn, tk = min(tm, M), min(tn, N), min(tk, K)
    assert M % tm == 0 and N % tn == 0 and K % tk == 0, (name, a.shape, b.shape)
    nk = K // tk
    if mode == "tn":
        a_spec = pl.BlockSpec((tk, tm), lambda i, j, k: (k, i))
    else:
        a_spec = pl.BlockSpec((tm, tk), lambda i, j, k: (i, k))
    if mode == "nt":
        b_spec = pl.BlockSpec((tn, tk), lambda i, j, k: (j, k))
    else:
        b_spec = pl.BlockSpec((tk, tn), lambda i, j, k: (k, j))
    ex_specs = []
    for e in extras:
        if e.shape[0] == 1:
            ex_specs.append(pl.BlockSpec((1, tn), lambda i, j, k: (0, j)))
        else:
            assert e.shape == (M, N), (name, e.shape)
            ex_specs.append(pl.BlockSpec((tm, tn), lambda i, j, k: (i, j)))
    ne = len(extras)
    dims = _DIMS[mode]
    many = isinstance(out_dtype, (list, tuple))
    out_dtypes = list(out_dtype) if many else [out_dtype]
    no = len(out_dtypes)
    assert not reds or tn == N, name

    def body(a_ref, b_ref, *rest):
        ex_refs, o_refs, r_refs = rest[:ne], rest[ne:ne + no], rest[ne + no:ne + no + reds]
        part = lax.dot_general(a_ref[...].astype(BF16), b_ref[...].astype(BF16), dims, preferred_element_type=F32)
        i = pl.program_id(0)

        def finish(acc):
            res = epi(acc, *[r[...] for r in ex_refs]) if epi is not None else acc
            if not isinstance(res, (tuple, list)):
                res = (res,)
            assert len(res) == no + reds, (name, len(res))
            for r, v in zip(o_refs, res[:no]):
                r[...] = v.astype(r.dtype).reshape(r.shape)
            for r, v in zip(r_refs, res[no:]):
                @pl.when(i == 0)
                def _(r=r, v=v):
                    r[...] = v

                @pl.when(i > 0)
                def _(r=r, v=v):
                    r[...] += v

        if nk == 1:
            finish(part)
        else:
            acc_ref = rest[ne + no + reds]
            k = pl.program_id(2)

            @pl.when(k == 0)
            def _():
                acc_ref[...] = part

            @pl.when(k > 0)
            def _():
                acc_ref[...] += part

            @pl.when(k == nk - 1)
            def _():
                finish(acc_ref[...])

    scratch = [pltpu.VMEM((tm, tn), F32)] if nk > 1 else []
    if into is not None:
        assert not many and not reds and not extras and into.dtype == out_dtype, name

        def body_into(a_ref, b_ref, into_ref, *rest):
            body(a_ref, b_ref, *rest)

        return _pallas(
            body_into, name=name, grid=(M // tm, N // tn, nk),
            in_specs=[a_spec, b_spec, pl.BlockSpec(memory_space=pl.ANY)],
            out_specs=pl.BlockSpec(out_block, out_map),
            out_shape=_sds(into.shape, into.dtype),
            input_output_aliases={2: 0},
            scratch_shapes=scratch,
            compiler_params=_params(("parallel", "parallel", "arbitrary")),
        )(a, b, into)
    res = _pallas(
        body, name=name, grid=(M // tm, N // tn, nk),
        in_specs=[a_spec, b_spec] + ex_specs,
        out_specs=[pl.BlockSpec((tm, tn), lambda i, j, k: (i, j))] * no + [pl.BlockSpec((1, tn), lambda i, j, k: (0, j))] * reds,
        out_shape=[_sds((M, N), dt) for dt in out_dtypes] + [_sds((1, N), F32)] * reds,
        scratch_shapes=scratch,
        compiler_params=_params(("arbitrary",) * 3 if reds else ("parallel", "parallel", "arbitrary")),
    )(a, b, *extras)
    return res if (many or reds) else res[0]


def _rows(fn, ins, params, outs, reds=(), *, tm=256, row_offsets=None, name):
    S = outs[0].shape[0] if outs else ins[0].shape[0]
    tm = min(tm, S)
    assert S % tm == 0, (name, S, tm)
    ni, npar, no, nr = len(ins), len(params), len(outs), len(reds)
    offs = list(row_offsets) if row_offsets is not None else [0] * ni
    assert all(o % tm == 0 for o in offs), (name, offs, tm)

    def body(*refs):
        in_refs, p_refs = refs[:ni], refs[ni:ni + npar]
        o_refs, r_refs = refs[ni + npar:ni + npar + no], refs[ni + npar + no:]
        res = fn(*[r[...] for r in in_refs], *[r[...] for r in p_refs])
        if not isinstance(res, (tuple, list)):
            res = (res,)
        assert len(res) == no + nr, (name, len(res))
        for r, v in zip(o_refs, res[:no]):
            r[...] = v.astype(r.dtype)
        i = pl.program_id(0)
        for r, v in zip(r_refs, res[no:]):
            @pl.when(i == 0)
            def _(r=r, v=v):
                r[...] = v

            @pl.when(i > 0)
            def _(r=r, v=v):
                r[...] += v

    res = _pallas(
        body, name=name, grid=(S // tm,),
        in_specs=[pl.BlockSpec((tm, a.shape[1]), lambda i, o=o: (o // tm + i, 0)) for a, o in zip(ins, offs)]
        + [pl.BlockSpec(p.shape, lambda i: (0, 0)) for p in params],
        out_specs=[pl.BlockSpec((tm, o.shape[1]), lambda i: (i, 0)) for o in outs]
        + [pl.BlockSpec(r.shape, lambda i: (0, 0)) for r in reds],
        out_shape=list(outs) + list(reds),
        compiler_params=_params(("arbitrary",)),
    )(*ins, *params)
    return res


def _colsum(v):
    return jnp.sum(v, axis=0, keepdims=True)


def _sigmoid(v):
    return 1.0 / (1.0 + jnp.exp(-v))


def _rms_stats(x):
    r = lax.rsqrt(jnp.mean(x * x, axis=-1, keepdims=True) + EPS)
    return x * r, r


def _rms_fwd(x, g):
    xh, _ = _rms_stats(x)
    return (xh * g,)


def _with_rms(h, g):
    xh, _ = _rms_stats(h)
    return h, xh * g


def _rms_bwd(dy, x, dres, g):
    xh, r = _rms_stats(x)
    dyg = dy * g
    dx = r * (dyg - xh * jnp.mean(dyg * xh, axis=-1, keepdims=True))
    return dres + dx, _colsum(dy * xh)


def _dup(fn):
    def wrapped(*a):
        r = fn(*a)
        return (r[0], r[0]) + tuple(r[1:])
    return wrapped


def _rms_bwd_bias(dy, x, dres, g):
    dx, dg = _rms_bwd(dy, x, dres, g)
    return dx, dg, _colsum(dx)


def _glu_fwd(u):
    d = u.shape[1] // 2
    return (u[:, :d] * _sigmoid(u[:, d:]),)


def _glu_bwd(dglu, u):
    d = u.shape[1] // 2
    a, sig = u[:, :d], _sigmoid(u[:, d:])
    du = jnp.concatenate([dglu * sig, dglu * a * sig * (1.0 - sig)], axis=1)
    return du, _colsum(du)


def _ln_parts(x, g, b):
    mu = jnp.mean(x, axis=-1, keepdims=True)
    xc = x - mu
    r = lax.rsqrt(jnp.mean(xc * xc, axis=-1, keepdims=True) + EPS)
    xh = xc * r
    return xh, r, xh * g + b


def _ln_silu_fwd(x, g, b):
    _, _, y = _ln_parts(x, g, b)
    return (y * _sigmoid(y),)


def _ln_silu_bwd(dsw, x, g, b):
    xh, r, y = _ln_parts(x, g, b)
    sig = _sigmoid(y)
    dy = dsw * sig * (1.0 + y * (1.0 - sig))
    dxh = dy * g
    dx = r * (dxh - jnp.mean(dxh, axis=-1, keepdims=True) - xh * jnp.mean(dxh * xh, axis=-1, keepdims=True))
    return dx, _colsum(dy * xh), _colsum(dy)


def _ple_bwd(dh, zg, pp):
    gate = _sigmoid(zg)
    return dh * pp * gate * (1.0 - gate), dh * gate


def _final_fn(h, t, g):
    xh, r = _rms_stats(h)
    err = xh * g - t
    dy = err * (1.0 / h.shape[1])
    dyg = dy * g
    dh = r * (dyg - xh * jnp.mean(dyg * xh, axis=-1, keepdims=True))
    return dh, _colsum(err * err), _colsum(dy * xh)


def _adamw_fn(w, g, m, v):
    m = ADAM_B1 * m + (1.0 - ADAM_B1) * g
    v = ADAM_B2 * v + (1.0 - ADAM_B2) * (g * g)
    m_hat = m / (1.0 - ADAM_B1 ** ADAM_STEP)
    v_hat = v / (1.0 - ADAM_B2 ** ADAM_STEP)
    delta = -ADAM_LR * (m_hat / (jnp.sqrt(v_hat) + ADAM_EPS) + ADAM_WD * w)
    return g, delta, m, v


def _adamw(w, g, m, v, name, g_row=None):
    shape = w.shape
    cols = shape[-1] if len(shape) > 1 else shape[0]
    two = lambda t: t.reshape(-1, cols)
    o = _sds(two(w).shape, F32)
    tm = min(512, o.shape[0])
    assert g_row is None or g.shape[1] == cols, name
    res = _rows(_adamw_fn, [two(w), g if g_row is not None else two(g), two(m), two(v)], [], [o, o, o, o], tm=tm,
                row_offsets=None if g_row is None else [0, g_row, 0, 0], name=name)
    return [r.reshape(shape) for r in res]


def _sublane_shifts(win):
    n = win.shape[0]
    return [win] + [pltpu.roll(win, n - b, axis=0) for b in range(1, 8)]


def _tap(shifted, offset, rows):
    a, b = divmod(offset, 8)
    return shifted[b][8 * a:8 * a + rows]


def _dwconv_fwd(u, w, b, *, tm=512, name):
    S, D = u.shape
    tm = min(tm, S)
    rc = min(128, tm)
    per = tm // CONV_PAD

    def body(prev_ref, cur_ref, w_ref, b_ref, o_ref, win):
        i = pl.program_id(0)

        @pl.when(i == 0)
        def _():
            win[0:CONV_PAD, :] = jnp.zeros((CONV_PAD, D), F32)

        @pl.when(i > 0)
        def _():
            win[0:CONV_PAD, :] = prev_ref[...]

        win[CONV_PAD:CONV_PAD + tm, :] = cur_ref[...]
        for lc in range(D // LANES):
            ls = slice(lc * LANES, (lc + 1) * LANES)
            for r0 in range(0, tm, rc):
                shifted = _sublane_shifts(win[r0:r0 + rc + CONV_PAD, ls])
                acc = jnp.zeros((rc, LANES), F32) + b_ref[:, ls]
                for k in range(CONV_WIDTH):
                    acc = acc + _tap(shifted, 2 + k, rc) * w_ref[k:k + 1, ls]
                o_ref[r0:r0 + rc, ls] = acc

    return _pallas(
        body, name=name, grid=(S // tm,),
        in_specs=[pl.BlockSpec((CONV_PAD, D), lambda i: (jnp.maximum(i * per - 1, 0), 0)),
                  pl.BlockSpec((tm, D), lambda i: (i, 0)),
                  pl.BlockSpec((CONV_PAD, D), lambda i: (0, 0)),
                  pl.BlockSpec((1, D), lambda i: (0, 0))],
        out_specs=pl.BlockSpec((tm, D), lambda i: (i, 0)),
        out_shape=_sds((S, D), F32),
        scratch_shapes=[pltpu.VMEM((tm + CONV_PAD, D), F32)],
        compiler_params=_params(("arbitrary",)),
    )(u, u, w, b)


def _dwconv_bwd(dy, u, w, *, tm=512, name):
    S, D = u.shape
    tm = min(tm, S)
    rc = min(128, tm)
    per = tm // CONV_PAD
    n = S // tm
    nxt = S // CONV_PAD - 1

    def body(dy_ref, dyn_ref, up_ref, u_ref, w_ref, du_ref, dw_ref, db_ref, wd, wu, dwacc, dbacc):
        i = pl.program_id(0)

        @pl.when(i == 0)
        def _():
            wu[0:CONV_PAD, :] = jnp.zeros((CONV_PAD, D), F32)
            dwacc[...] = jnp.zeros(dwacc.shape, F32)
            dbacc[...] = jnp.zeros(dbacc.shape, F32)

        @pl.when(i > 0)
        def _():
            wu[0:CONV_PAD, :] = up_ref[...]

        @pl.when(i == n - 1)
        def _():
            wd[tm:tm + CONV_PAD, :] = jnp.zeros((CONV_PAD, D), F32)

        @pl.when(i < n - 1)
        def _():
            wd[tm:tm + CONV_PAD, :] = dyn_ref[...]

        wu[CONV_PAD:CONV_PAD + tm, :] = u_ref[...]
        wd[0:tm, :] = dy_ref[...]
        for lc in range(D // LANES):
            ls = slice(lc * LANES, (lc + 1) * LANES)
            for r0 in range(0, tm, rc):
                sd = _sublane_shifts(wd[r0:r0 + rc + CONV_PAD, ls])
                acc = jnp.zeros((rc, LANES), F32)
                for k in range(CONV_WIDTH):
                    acc = acc + _tap(sd, 30 - k, rc) * w_ref[k:k + 1, ls]
                du_ref[r0:r0 + rc, ls] = acc
                dyc = wd[r0:r0 + rc, ls]
                dbacc[:, ls] += jnp.sum(dyc.reshape(rc // 8, 8, LANES), axis=0)
                for k in range(CONV_WIDTH):
                    prod = dyc * wu[r0 + 2 + k:r0 + 2 + k + rc, ls]
                    dwacc[8 * k:8 * k + 8, ls] += jnp.sum(prod.reshape(rc // 8, 8, LANES), axis=0)

        @pl.when(i == n - 1)
        def _():
            dw_ref[...] = jnp.zeros(dw_ref.shape, F32)
            for k in range(CONV_WIDTH):
                dw_ref[k:k + 1, :] = jnp.sum(dwacc[8 * k:8 * k + 8, :], axis=0, keepdims=True)
            db_ref[...] = jnp.sum(dbacc[...], axis=0, keepdims=True)

    return _pallas(
        body, name=name, grid=(n,),
        in_specs=[pl.BlockSpec((tm, D), lambda i: (i, 0)),
                  pl.BlockSpec((CONV_PAD, D), lambda i: (jnp.minimum((i + 1) * per, nxt), 0)),
                  pl.BlockSpec((CONV_PAD, D), lambda i: (jnp.maximum(i * per - 1, 0), 0)),
                  pl.BlockSpec((tm, D), lambda i: (i, 0)),
                  pl.BlockSpec((CONV_PAD, D), lambda i: (0, 0))],
        out_specs=[pl.BlockSpec((tm, D), lambda i: (i, 0)),
                   pl.BlockSpec((CONV_PAD, D), lambda i: (0, 0)),
                   pl.BlockSpec((1, D), lambda i: (0, 0))],
        out_shape=[_sds((S, D), F32), _sds((CONV_PAD, D), F32), _sds((1, D), F32)],
        scratch_shapes=[pltpu.VMEM((tm + CONV_PAD, D), F32), pltpu.VMEM((tm + CONV_PAD, D), F32),
                        pltpu.VMEM((8 * CONV_PAD, D), F32), pltpu.VMEM((8, D), F32)],
        compiler_params=_params(("arbitrary",)),
    )(dy, dy, u, u, w)


def _tri_dot(tri, x):
    x1 = x.astype(BF16)
    r1 = x - x1.astype(F32)
    x2 = r1.astype(BF16)
    x3 = (r1 - x2.astype(F32)).astype(BF16)
    d = lambda v: jnp.dot(tri, v, preferred_element_type=F32)
    return d(x1) + d(x2) + d(x3)


def _log_sigmoid(x):
    return jnp.minimum(x, 0.0) - jnp.log(1.0 + jnp.exp(-jnp.abs(x)))


def _gate_cumsum(fl, bf, *, tm=256, name):
    S, W = fl.shape
    tm = min(tm, S)

    def body(fl_ref, bf_ref, c_ref, carry):
        i = pl.program_id(0)

        @pl.when(i == 0)
        def _():
            carry[...] = jnp.zeros(carry.shape, F32)

        x = _log_sigmoid(fl_ref[...] + bf_ref[...])
        row = lax.broadcasted_iota(jnp.int32, (tm, tm), 0)
        col = lax.broadcasted_iota(jnp.int32, (tm, tm), 1)
        tri = jnp.where(row >= col, 1.0, 0.0).astype(BF16)
        cs = _tri_dot(tri, x) + carry[0:1, :]
        c_ref[...] = cs
        carry[...] = jnp.broadcast_to(cs[tm - 1:tm, :], carry.shape)

    return _pallas(
        body, name=name, grid=(S // tm,),
        in_specs=[pl.BlockSpec((tm, W), lambda i: (i, 0)), pl.BlockSpec((1, W), lambda i: (0, 0))],
        out_specs=pl.BlockSpec((tm, W), lambda i: (i, 0)),
        out_shape=_sds((S, W), F32),
        scratch_shapes=[pltpu.VMEM((8, W), F32)],
        compiler_params=_params(("arbitrary",)),
    )(fl, bf)


def _gate_cumsum_bwd(sums, fl, bf, *, tm=256, name):
    S, W = fl.shape
    tm = min(tm, S)
    n = S // tm
    ns = len(sums)
    assert ns % 2 == 0

    def body(*refs):
        sum_refs = refs[:ns]
        fl_ref, bf_ref, o_ref, s_ref, carry = refs[ns:]
        i = pl.program_id(0)

        @pl.when(i == 0)
        def _():
            carry[...] = jnp.zeros(carry.shape, F32)
            s_ref[...] = jnp.zeros(s_ref.shape, F32)

        dc = sum_refs[0][...] - sum_refs[1][...]
        for a in range(2, ns, 2):
            dc = dc + (sum_refs[a][...] - sum_refs[a + 1][...])
        row = lax.broadcasted_iota(jnp.int32, (tm, tm), 0)
        col = lax.broadcasted_iota(jnp.int32, (tm, tm), 1)
        tri = jnp.where(col >= row, 1.0, 0.0).astype(BF16)
        rs = _tri_dot(tri, dc) + carry[0:1, :]
        carry[...] = jnp.broadcast_to(rs[0:1, :], carry.shape)
        dfl = rs * _sigmoid(-(fl_ref[...] + bf_ref[...]))
        o_ref[...] = dfl
        s_ref[...] += _colsum(dfl)

    rev = lambda i: (n - 1 - i, 0)
    return _pallas(
        body, name=name, grid=(n,),
        in_specs=[pl.BlockSpec((tm, W), rev)] * (ns + 1) + [pl.BlockSpec((1, W), lambda i: (0, 0))],
        out_specs=[pl.BlockSpec((tm, W), rev), pl.BlockSpec((1, W), lambda i: (0, 0))],
        out_shape=[_sds((S, W), F32), _sds((1, W), F32)],
        scratch_shapes=[pltpu.VMEM((8, W), F32)],
        compiler_params=_params(("arbitrary",)),
    )(*sums, fl, bf)


def _tri_tables(nq, qc, by_query):
    if by_query:
        pairs = [(i, j) for i in range(nq) for j in range(qc * (i + 1))]
    else:
        pairs = [(i, j) for j in range(qc * nq) for i in range(j // qc, nq)]
    ii, jj = zip(*pairs)
    return jnp.asarray(np.array(ii, np.int32)), jnp.asarray(np.array(jj, np.int32))


def _chunk_kinds(qc, dd):
    if dd < 0:
        return ("full",) * qc
    return tuple("full" if r > dd else "diag" if r == dd else "skip" for r in range(qc))


def _rep(v, t):
    return jnp.tile(v, (1, t // LANES))


def _attn_fwd(q, kv, ck, *, tb=512, row_chunks=8, name):
    S, D = q.shape
    HP = D // LANES
    T = min(tb, S)
    QC = row_chunks if S >= row_chunks * T else 1
    TQ = QC * T
    it, jt = _tri_tables(S // TQ, QC, True)

    def body(it_ref, jt_ref, q_ref, k_ref, v_ref, ck_ref, o_ref, lse_ref, st):
        s_id = pl.program_id(1)
        i, j = it_ref[s_id], jt_ref[s_id]
        dd = j - QC * i
        lane = lax.broadcasted_iota(jnp.int32, (TQ, LANES), 1)
        head0 = lane < HEAD_DIM
        h0 = head0[:T]
        hms = (h0, jnp.logical_not(h0))

        @pl.when(j == 0)
        def _():
            st[0:2] = jnp.full((2, TQ, LANES), NEG_BIG, F32)
            st[2:4] = jnp.zeros((2, TQ, LANES), F32)

        def step(kinds):
            kvv, vv = k_ref[...], v_ref[...]
            one = jnp.ones_like(vv)
            vaug = [jnp.where(hms[h], vv, one) for h in range(2)]
            old = st[...]
            rows = lambda r: slice(r * T, (r + 1) * T)
            live = [r for r in range(QC) if kinds[r] != "skip"]
            chains = [(r, h) for r in live for h in range(2)]
            ss = {}
            for r, h in chains:
                qv = q_ref[rows(r), :]
                ss[r, h] = lax.dot_general(jnp.where(hms[h], qv, jnp.zeros_like(qv)), kvv, _DIMS["nt"], preferred_element_type=F32)
            ps, alphas, m_new = {}, {}, {}
            for r, h in chains:
                s = ss[r, h] - ck_ref[h:h + 1, :]
                if kinds[r] == "diag":
                    row = lax.broadcasted_iota(jnp.int32, (T, T), 0)
                    col = lax.broadcasted_iota(jnp.int32, (T, T), 1)
                    s = jnp.where(row >= col, s, NEG_BIG)
                m_prev = old[h, rows(r), :]
                m_new[r, h] = jnp.maximum(m_prev, jnp.max(s, axis=1, keepdims=True))
                ps[r, h] = jnp.exp(s - _rep(m_new[r, h], T)).astype(BF16)
                alphas[r, h] = jnp.exp(m_prev - m_new[r, h])
            new = [[], [], [], []]
            for r in range(QC):
                if kinds[r] == "skip":
                    for a in range(4):
                        new[a].append(old[a, rows(r), :])
                    continue
                pv = [jnp.dot(ps[r, h], vaug[h], preferred_element_type=F32) for h in range(2)]
                a0, a1 = alphas[r, 0], alphas[r, 1]
                new[0].append(m_new[r, 0])
                new[1].append(m_new[r, 1])
                new[2].append(jnp.where(h0, a0, a1) * old[2, rows(r), :] + jnp.where(h0, pv[0], pv[1]))
                new[3].append(jnp.where(h0, a1, a0) * old[3, rows(r), :] + jnp.where(h0, pv[1], pv[0]))
            res = jnp.stack([jnp.concatenate(n, axis=0) for n in new], axis=0)
            st[...] = res
            return res

        @pl.when(dd < 0)
        def _():
            step(_chunk_kinds(QC, -1))

        for d in range(QC):
            @pl.when(dd == d)
            def _(d=d):
                res = step(_chunk_kinds(QC, d))
                if d == QC - 1:
                    lr = pltpu.roll(res[3], HEAD_DIM, axis=1)
                    o_ref[...] = res[2] / lr
                    lse_ref[0] = res[0] + jnp.log(jnp.where(head0, lr, res[3]))
                    lse_ref[1] = res[1] + jnp.log(jnp.where(head0, res[3], lr))

    grid_spec = pltpu.PrefetchScalarGridSpec(
        num_scalar_prefetch=2, grid=(HP, it.shape[0]),
        in_specs=[pl.BlockSpec((TQ, LANES), lambda h, s, it, jt: (it[s], h)),
                  pl.BlockSpec((T, LANES), lambda h, s, it, jt: (jt[s], h)),
                  pl.BlockSpec((T, LANES), lambda h, s, it, jt: (jt[s], HP + h)),
                  pl.BlockSpec((None, 8, T), lambda h, s, it, jt: (h, 0, jt[s]))],
        out_specs=[pl.BlockSpec((TQ, LANES), lambda h, s, it, jt: (it[s], h)),
                   pl.BlockSpec((2, TQ, LANES), lambda h, s, it, jt: (h, it[s], 0))],
        scratch_shapes=[pltpu.VMEM((4, TQ, LANES), F32)],
    )
    return _pallas(
        body, name=name, grid_spec=grid_spec,
        out_shape=[_sds((S, D), F32), _sds((2 * HP, S, LANES), F32)],
        compiler_params=_params(("parallel", "arbitrary")),
    )(it, jt, q, kv, kv, ck)


def _attn_bwd(q, kv, o, do, lse, ck, dkv0=None, *, tb=512, row_chunks=4, name):
    S, D = q.shape
    HP = D // LANES
    T = min(tb, S)
    QC = row_chunks if S >= row_chunks * T else 1
    TQ = QC * T
    it, jt = _tri_tables(S // TQ, QC, False)
    n0 = 0 if dkv0 is None else 2

    def body(it_ref, jt_ref, q_ref, k_ref, v_ref, o_ref, do_ref, lse_ref, ck_ref, *rest):
        dq_ref, drs_ref, dk_ref, dv_ref, dcs_ref = rest[n0:]
        s_id = pl.program_id(1)
        i, j = it_ref[s_id], jt_ref[s_id]
        dd = j - QC * i
        lane = lax.broadcasted_iota(jnp.int32, (T, LANES), 1)
        head0 = lane < HEAD_DIM
        hms = (head0, jnp.logical_not(head0))

        @pl.when(s_id == 0)
        def _():
            dq_ref[...] = jnp.zeros(dq_ref.shape, F32)
            drs_ref[...] = jnp.zeros(drs_ref.shape, F32)

        @pl.when(dd >= 0)
        def _():
            dk_ref[...] = jnp.zeros(dk_ref.shape, F32) if dkv0 is None else rest[0][...]
            dv_ref[...] = jnp.zeros(dv_ref.shape, F32) if dkv0 is None else rest[1][...]
            dcs_ref[...] = jnp.zeros(dcs_ref.shape, F32)

        def step(kinds):
            kvv, vv = k_ref[...], v_ref[...]
            one = jnp.ones_like(kvv)
            zero = jnp.zeros_like(kvv)
            rows = lambda r: slice(r * T, (r + 1) * T)
            live = [r for r in range(QC) if kinds[r] != "skip"]
            chains = [(r, h) for r in live for h in range(2)]
            qv = {r: q_ref[rows(r), :] for r in live}
            dob = {r: do_ref[rows(r), :].astype(BF16) for r in live}
            ss = {(r, h): lax.dot_general(jnp.where(hms[h], qv[r], zero), kvv, _DIMS["nt"], preferred_element_type=F32)
                  for r, h in chains}
            dps = {(r, h): lax.dot_general(jnp.where(hms[h], dob[r], zero), vv, _DIMS["nt"], preferred_element_type=F32)
                   for r, h in chains}
            pbs, dsbs = {}, {}
            for r, h in chains:
                s = ss[r, h] - ck_ref[h:h + 1, :]
                if kinds[r] == "diag":
                    row = lax.broadcasted_iota(jnp.int32, (T, T), 0)
                    col = lax.broadcasted_iota(jnp.int32, (T, T), 1)
                    s = jnp.where(row >= col, s, NEG_BIG)
                p = jnp.exp(s - _rep(lse_ref[h, rows(r), :], T))
                prod = dob[r].astype(F32) * o_ref[rows(r), :]
                delta = jnp.sum(jnp.where(hms[h], prod, 0.0), axis=1, keepdims=True)
                pbs[r, h] = p.astype(BF16)
                dsbs[r, h] = (p * (dps[r, h] - delta)).astype(BF16)
            dvs, dks = [None, None], [None, None]
            for r in live:
                dqs = []
                for h in range(2):
                    dqs.append(jnp.dot(dsbs[r, h], jnp.where(hms[h], kvv, one), preferred_element_type=F32))
                    dv = jnp.dot(pbs[r, h].T, dob[r], preferred_element_type=F32)
                    dk = jnp.dot(dsbs[r, h].T, jnp.where(hms[h], qv[r], one), preferred_element_type=F32)
                    dvs[h] = dv if dvs[h] is None else dvs[h] + dv
                    dks[h] = dk if dks[h] is None else dks[h] + dk
                qrows = pl.ds(pl.multiple_of(i * TQ + r * T, T), T)
                dq_ref[qrows, :] += jnp.where(head0, dqs[0], dqs[1])
                drs_ref[qrows, :] += jnp.where(head0, dqs[1], dqs[0])
            dv_ref[...] += jnp.where(head0, dvs[0], dvs[1])
            dk_ref[...] += jnp.where(head0, dks[0], dks[1])
            dcs_ref[...] += jnp.where(head0, dks[1], dks[0])

        @pl.when(dd < 0)
        def _():
            step(_chunk_kinds(QC, -1))

        for d in range(QC):
            @pl.when(dd == d)
            def _(d=d):
                step(_chunk_kinds(QC, d))

    by_q = lambda h, s, it, jt: (it[s], h)
    by_k = lambda h, s, it, jt: (jt[s], h)
    whole = lambda h, s, it, jt: (0, h)
    grid_spec = pltpu.PrefetchScalarGridSpec(
        num_scalar_prefetch=2, grid=(HP, it.shape[0]),
        in_specs=[pl.BlockSpec((TQ, LANES), by_q),
                  pl.BlockSpec((T, LANES), by_k),
                  pl.BlockSpec((T, LANES), lambda h, s, it, jt: (jt[s], HP + h)),
                  pl.BlockSpec((TQ, LANES), by_q),
                  pl.BlockSpec((TQ, LANES), by_q),
                  pl.BlockSpec((2, TQ, LANES), lambda h, s, it, jt: (h, it[s], 0)),
                  pl.BlockSpec((None, 8, T), lambda h, s, it, jt: (h, 0, jt[s]))]
        + [pl.BlockSpec((T, LANES), by_k)] * n0,
        out_specs=[pl.BlockSpec((S, LANES), whole), pl.BlockSpec((S, LANES), whole),
                   pl.BlockSpec((T, LANES), by_k), pl.BlockSpec((T, LANES), by_k), pl.BlockSpec((T, LANES), by_k)],
        scratch_shapes=[],
    )
    return _pallas(
        body, name=name, grid_spec=grid_spec,
        out_shape=[_sds((S, D), F32)] * 5,
        compiler_params=_params(("parallel", "arbitrary")),
    )(it, jt, q, kv, kv, o, do, lse, ck, *(dkv0 or ()))


ANY = pl.BlockSpec(memory_space=pl.ANY)


def _coords():
    x, y, c = lax.axis_index("x"), lax.axis_index("y"), lax.axis_index("c")
    return x, y, c


def _remote(src, dst, send_sems, recv_sems, k, to):
    return pltpu.make_async_remote_copy(src_ref=src, dst_ref=dst, send_sem=send_sems.at[k], recv_sem=recv_sems.at[k],
                                        device_id=to, device_id_type=MESH)


def _all_gather_shards(pack, *, name):
    R, C = pack.shape
    assert R % 4 == 0
    H, Q = R // 2, R // 4

    def body(in_ref, out_ref, send_sems, recv_sems):
        x, y, c = _coords()
        me, sib = (x, y, c), (x, y, 1 - c)
        xn, yn = (1 - x, y, c), (x, 1 - y, c)
        s, sx, sy, sd = 2 * x + y, 2 * (1 - x) + y, 2 * x + 1 - y, 2 * (1 - x) + 1 - y
        half = pl.ds(c * H, H)
        other = pl.ds((1 - c) * H, H)
        q0 = pl.ds(c * H, Q)
        q1 = pl.ds(c * H + Q, Q)
        rc = functools.partial(_remote, send_sems=send_sems, recv_sems=recv_sems)

        sends = [rc(in_ref.at[half], out_ref.at[s, half], k=0, to=xn),
                 rc(in_ref.at[half], out_ref.at[s, half], k=1, to=yn),
                 rc(in_ref, out_ref.at[s], k=7, to=sib)]
        for cp in sends:
            cp.start()
        rc(in_ref.at[half], out_ref.at[sx, half], k=0, to=me).wait_recv()
        sends.append(rc(out_ref.at[sx, q0], out_ref.at[sx, q0], k=2, to=yn))
        sends[-1].start()
        sends.append(rc(out_ref.at[sx, half], out_ref.at[sx, half], k=4, to=sib))
        sends[-1].start()
        rc(in_ref.at[half], out_ref.at[sy, half], k=1, to=me).wait_recv()
        sends.append(rc(out_ref.at[sy, q1], out_ref.at[sy, q1], k=3, to=xn))
        sends[-1].start()
        sends.append(rc(out_ref.at[sy, half], out_ref.at[sy, half], k=5, to=sib))
        sends[-1].start()
        rc(out_ref.at[sd, q0], out_ref.at[sd, q0], k=2, to=me).wait_recv()
        rc(out_ref.at[sd, q1], out_ref.at[sd, q1], k=3, to=me).wait_recv()
        sends.append(rc(out_ref.at[sd, half], out_ref.at[sd, half], k=6, to=sib))
        sends[-1].start()
        for k, sh in ((4, sx), (5, sy), (6, sd)):
            rc(out_ref.at[sh, other], out_ref.at[sh, other], k=k, to=me).wait_recv()
        rc(in_ref, out_ref.at[s], k=7, to=me).wait_recv()
        for cp in sends:
            cp.wait_send()

    return _pallas(
        body, name=name, in_specs=[ANY], out_specs=ANY,
        out_shape=_sds((N_CHIPS, R, C), pack.dtype),
        scratch_shapes=[pltpu.SemaphoreType.DMA((8,)), pltpu.SemaphoreType.DMA((8,))],
    )(pack)


def _rs_pair(g, *, name):
    n, R, C = g.shape
    H = R // 2

    def body(g_ref, land_ref, send_sems, recv_sems):
        x, y, c = _coords()
        other = pl.ds((1 - c) * H, H)
        cps = [_remote(g_ref.at[sh, other], land_ref.at[sh], send_sems, recv_sems, sh, (x, y, 1 - c)) for sh in range(n)]
        for cp in cps:
            cp.start()
        for cp in cps:
            cp.wait_recv()
        for cp in cps:
            cp.wait_send()

    return _pallas(
        body, name=name, in_specs=[ANY], out_specs=ANY, out_shape=_sds((n, H, C), g.dtype),
        scratch_shapes=[pltpu.SemaphoreType.DMA((n,)), pltpu.SemaphoreType.DMA((n,))],
    )(g)


def _rs_quarters(p, *, name):
    n, H, C = p.shape
    Q = H // 2

    def body(p_ref, la_ref, lb_ref, send_sems, recv_sems):
        x, y, c = _coords()
        sd = 2 * (1 - x) + 1 - y
        a = _remote(p_ref.at[sd, pl.ds(0, Q)], la_ref, send_sems, recv_sems, 0, (x, 1 - y, c))
        b = _remote(p_ref.at[sd, pl.ds(Q, Q)], lb_ref, send_sems, recv_sems, 1, (1 - x, y, c))
        a.start()
        b.start()
        a.wait_recv()
        b.wait_recv()
        a.wait_send()
        b.wait_send()

    return _pallas(
        body, name=name, in_specs=[ANY], out_specs=[ANY, ANY],
        out_shape=[_sds((Q, C), p.dtype), _sds((Q, C), p.dtype)],
        scratch_shapes=[pltpu.SemaphoreType.DMA((2,)), pltpu.SemaphoreType.DMA((2,))],
    )(p)


def _rs_halves(p, ax, ay, *, name):
    n, H, C = p.shape
    Q = H // 2

    def body(p_ref, ax_ref, ay_ref, la_ref, lb_ref, send_sems, recv_sems):
        x, y, c = _coords()
        sx, sy = 2 * (1 - x) + y, 2 * x + 1 - y
        xn, yn = (1 - x, y, c), (x, 1 - y, c)
        lo, hi = pl.ds(0, Q), pl.ds(Q, Q)
        cps = [_remote(ax_ref, la_ref.at[lo], send_sems, recv_sems, 0, xn),
               _remote(p_ref.at[sx, hi], la_ref.at[hi], send_sems, recv_sems, 1, xn),
               _remote(p_ref.at[sy, lo], lb_ref.at[lo], send_sems, recv_sems, 2, yn),
               _remote(ay_ref, lb_ref.at[hi], send_sems, recv_sems, 3, yn)]
        for cp in cps:
            cp.start()
        for cp in cps:
            cp.wait_recv()
        for cp in cps:
            cp.wait_send()

    return _pallas(
        body, name=name, in_specs=[ANY, ANY, ANY], out_specs=[ANY, ANY],
        out_shape=[_sds((H, C), p.dtype), _sds((H, C), p.dtype)],
        scratch_shapes=[pltpu.SemaphoreType.DMA((4,)), pltpu.SemaphoreType.DMA((4,))],
    )(p, ax, ay)


def _rs_join(buf, *, name):
    R, C = buf.shape
    H = R // 2

    def body(in_ref, out_ref, send_sems, recv_sems):
        x, y, c = _coords()
        half = pl.ds(c * H, H)
        other = pl.ds((1 - c) * H, H)
        cp = _remote(in_ref.at[half], out_ref.at[half], send_sems, recv_sems, 0, (x, y, 1 - c))
        cp.start()
        _remote(in_ref.at[other], out_ref.at[other], send_sems, recv_sems, 0, (x, y, c)).wait_recv()
        cp.wait_send()

    return _pallas(
        body, name=name, in_specs=[ANY], out_specs=ANY, out_shape=_sds((R, C), buf.dtype),
        input_output_aliases={0: 0},
        scratch_shapes=[pltpu.SemaphoreType.DMA((1,)), pltpu.SemaphoreType.DMA((1,))],
    )(buf)


def _tile_add(ins_specs, arrays, n_steps, out_spec, out_shape, scalars, *, name):
    grid_spec = pltpu.PrefetchScalarGridSpec(
        num_scalar_prefetch=1, grid=(n_steps,), in_specs=ins_specs, out_specs=out_spec, scratch_shapes=[])

    def body(sc_ref, *refs):
        acc = refs[0][...].astype(F32)
        for r in refs[1:-1]:
            acc = acc + r[...].astype(F32)
        refs[-1][...] = acc.astype(refs[-1].dtype)

    return _pallas(body, name=name, grid_spec=grid_spec, out_shape=out_shape,
                   compiler_params=_params(("arbitrary",)))(scalars, *arrays)


def _reduce_scatter(g, *, name):
    n, R, C = g.shape
    H, Q = R // 2, R // 4
    tm = RS_ROW_MULT // 4
    assert Q % tm == 0, (R, tm)
    x, y, c = _coords()
    sx, sy, s = 2 * (1 - x) + y, 2 * x + 1 - y, 2 * x + y
    sc = jnp.stack([c, sx, sy, s]).astype(jnp.int32)
    hb, qb = H // tm, Q // tm
    blk = lambda f: pl.BlockSpec((None, tm, C), f)
    flat = lambda f: pl.BlockSpec((tm, C), f)

    land = _rs_pair(g, name=name + "_pair")
    p = _tile_add([blk(lambda i, sc: (i // hb, sc[0] * hb + i % hb, 0)), blk(lambda i, sc: (i // hb, i % hb, 0))],
                  [g, land], n * hb, blk(lambda i, sc: (i // hb, i % hb, 0)), _sds((n, H, C), g.dtype), sc, name=name + "_add0")
    la, lb = _rs_quarters(p, name=name + "_quarters")
    ax = _tile_add([blk(lambda i, sc: (sc[1], i, 0)), flat(lambda i, sc: (i, 0))], [p, la], qb,
                   flat(lambda i, sc: (i, 0)), _sds((Q, C), g.dtype), sc, name=name + "_add1x")
    ay = _tile_add([blk(lambda i, sc: (sc[2], qb + i, 0)), flat(lambda i, sc: (i, 0))], [p, lb], qb,
                   flat(lambda i, sc: (i, 0)), _sds((Q, C), g.dtype), sc, name=name + "_add1y")
    fa, fb = _rs_halves(p, ax, ay, name=name + "_halves")
    buf = _tile_add([blk(lambda i, sc: (sc[3], i, 0)), flat(lambda i, sc: (i, 0)), flat(lambda i, sc: (i, 0))],
                    [p, fa, fb], hb, flat(lambda i, sc: (sc[0] * hb + i, 0)), _sds((R, C), F32), sc, name=name + "_add2")
    return _rs_join(buf, name=name + "_join")


def _all_reduce_small(v, *, name):
    M, N = v.shape

    def body(x_ref, out_ref, send_sems, recv_sems, local_sem):
        x, y, c = _coords()
        me, sibling = (x, y, c), (x, y, 1 - c)
        chips = [(1 - x, y), (x, 1 - y), (1 - x, 1 - y)]

        def rows(px, py, pc):
            return out_ref.at[pl.ds((4 * px + 2 * py + pc) * M, M), :]

        def copy(k, block, to, src=None):
            return pltpu.make_async_remote_copy(
                src_ref=rows(*block) if src is None else src, dst_ref=rows(*block),
                send_sem=send_sems.at[k], recv_sem=recv_sems.at[k], device_id=to, device_id_type=MESH)

        mine = pltpu.make_async_copy(x_ref, rows(*me), local_sem)
        mine.start()
        first = [copy(0, me, sibling, src=x_ref)]
        first += [copy(1 + j, me, (*chip, c), src=x_ref) for j, chip in enumerate(chips)]
        for cp in first:
            cp.start()
        passed = [copy(4 + j, (*chip, c), sibling) for j, chip in enumerate(chips)]
        for j, chip in enumerate(chips):
            copy(1 + j, (*chip, c), me).wait_recv()
            passed[j].start()
        copy(0, sibling, me).wait_recv()
        for j, chip in enumerate(chips):
            copy(4 + j, (*chip, 1 - c), me).wait_recv()
        for cp in first + passed:
            cp.wait_send()
        mine.wait()

    gathered = _pallas(
        body, name=name + "_gather",
        out_shape=_sds((N_DEV * M, N), F32),
        in_specs=[pl.BlockSpec(memory_space=pltpu.VMEM)],
        out_specs=pl.BlockSpec(memory_space=pltpu.VMEM),
        scratch_shapes=[pltpu.SemaphoreType.DMA((7,)), pltpu.SemaphoreType.DMA((7,)), pltpu.SemaphoreType.DMA],
    )(v)

    def sum_body(g_ref, o_ref):
        acc = g_ref[0:M, :]
        for d in range(1, N_DEV):
            acc = acc + g_ref[d * M:(d + 1) * M, :]
        o_ref[...] = acc

    return _pallas(sum_body, name=name + "_sum", out_shape=_sds((M, N), F32))(gathered)


MATS = [("ffn_w1", 2), ("ffn_w2", 1), ("ple_w_gate", 1), ("conv_w_pw1", 2), ("conv_w_pw2", 1), ("attn_w_q", 1), ("attn_w_o", 1),
        ("ple_w_proj", 2), ("w_kvf", 1)]
VECS = [("conv_b_pw1", 1), ("conv_w_dw", 2), ("conv_b_dw", 1), ("conv_ln_g", 1), ("conv_ln_b", 1), ("conv_b_pw2", 1)]
REPL = ["mix_norm", "ffn_norm", "ple_norm", "kv_norm", "final_norm", "b_f"]
WEIGHTS = ["mix_norm", "conv_w_pw1", "conv_b_pw1", "conv_w_dw", "conv_b_dw", "conv_ln_g", "conv_ln_b", "conv_w_pw2",
           "conv_b_pw2", "kv_norm", "w_kvf", "b_f", "attn_w_q", "attn_w_o", "ffn_norm", "ffn_w1", "ffn_w2", "ple_norm",
           "ple_w_gate", "ple_w_proj", "final_norm"]


def _round_up(n, m):
    return -(-n // m) * m


def _to_rows(t, C, mult):
    flat = t.reshape(-1)
    rows = _round_up(_round_up(flat.shape[0], C) // C, mult)
    flat = jnp.pad(flat, (0, rows * C - flat.shape[0]))
    return flat.reshape(rows, C)


def _pack(tensors, C, mult, total_mult):
    parts = [_to_rows(t, C, mult) for t in tensors]
    rows = sum(p.shape[0] for p in parts)
    pad = _round_up(rows, total_mult) - rows
    if pad:
        parts.append(jnp.zeros((pad, C), parts[0].dtype))
    return jnp.concatenate(parts, axis=0)


def _row_counts(shapes, C, mult):
    return [_round_up(_round_up(int(np.prod(s)), C) // C, mult) for s in shapes]


def _unpack(packed, shapes, C, mult):
    outs, r0 = [], 0
    lead = packed.shape[:-2]
    for shp, nr in zip(shapes, _row_counts(shapes, C, mult)):
        n = int(np.prod(shp))
        seg = packed[..., r0:r0 + nr, :].reshape(lead + (nr * C,))[..., :n]
        outs.append(seg.reshape(lead + tuple(shp)))
        r0 += nr
    return outs


def _unshard(t, axis):
    return jnp.concatenate([t[s] for s in range(N_CHIPS)], axis=axis)


def _shards(t, axis):
    return jnp.split(t, N_CHIPS, axis=axis)


class _GradPack:
    DIRECT = {"conv_w_pw2": "rows", "attn_w_q": "rows", "attn_w_o": "rows", "ffn_w2": "rows", "ple_w_gate": "rows",
              "ffn_w1": "cols"}

    def __init__(self, names, shard_shapes, C):
        self.C, self.names, self.shapes, self.off = C, names, dict(zip(names, shard_shapes)), {}
        r = 0
        for n, cnt in zip(names, _row_counts(shard_shapes, C, 16)):
            self.off[n] = r
            r += cnt
        self.R = _round_up(r, RS_ROW_MULT)
        self.buf = jnp.zeros((N_CHIPS, self.R, C), BF16)

    def matmul(self, wname, layer, a, b, **kw):
        rows_s, cols_s = self.shapes[wname][-2:]
        assert cols_s == self.C
        base = self.off[wname] + layer * rows_s
        if self.DIRECT[wname] == "rows" and N_CHIPS * rows_s <= 1024:
            assert base % rows_s == 0
            tm, block = N_CHIPS * rows_s, (N_CHIPS, rows_s, self.C)
            out_map = lambda i, j, k: (0, base // rows_s, 0)
        else:
            tm = next(t for t in (1024, 512, 256, 128, 64, 32, 16) if rows_s % t == 0 and base % t == 0)
            per, first, block = rows_s // tm, base // tm, (None, tm, self.C)
            if self.DIRECT[wname] == "rows":
                out_map = lambda i, j, k: (i // per, first + i % per, 0)
            else:
                assert a.shape[1] == rows_s
                out_map = lambda i, j, k: (j, first + i, 0)
        self.buf = _mm(a, b, mode="tn", tk=2048, tm=tm, tn=self.C, out_dtype=BF16, into=self.buf, out_block=block,
                       out_map=out_map, **kw)

    def insert(self, grads, axes):
        run = []
        for n in self.names + [None]:
            if n is not None and n not in self.DIRECT:
                run.append(n)
                continue
            if run:
                for s in range(N_CHIPS):
                    rows = jnp.concatenate([_to_rows(_shards(grads[m].astype(BF16), axes[m])[s], self.C, 16) for m in run], axis=0)
                    self.buf = lax.dynamic_update_slice(self.buf, rows[None], (s, self.off[run[0]], 0))
                run = []


def _local_step(x, p, tgt, w, gp=None):
    S, D = x.shape
    L = p.shape[0]
    NA = w["conv_w_pw1"].shape[0]
    H = w["b_f"].shape[0]
    HP = D // LANES
    row = lambda v: v.reshape(1, -1)
    act = lambda dt, n=D: _sds((S, n), dt)
    g = {}

    def dw(wname, layer, a, b, **kw):
        if gp is not None and wname in gp.DIRECT:
            gp.matmul(wname, layer, a, b, **kw)
            return None
        return _mm(a, b, mode="tn", tk=2048, out_dtype=BF16, **kw)

    def rms(hh, gain, name):
        return _rows(_rms_fwd, [hh], [row(gain)], [act(BF16)], name=name)[0]

    saved = []
    h = x
    kv = ck = fl = nkv = h_kv = None
    bfp = jnp.pad(w["b_f"], (0, LANES - H)).reshape(1, LANES)
    wk = w["w_kvf"][:, :D]
    wv = w["w_kvf"][:, D:2 * D]
    wkv = w["w_kvf"][:, :2 * D]
    wf = jnp.pad(w["w_kvf"][:, 2 * D:], ((0, 0), (0, LANES - H)))
    res_rms = lambda acc, r, gn: _with_rms(acc + r, gn)
    hn = rms(h, w["mix_norm"][0], "rms_mix")
    for i in range(L):
        sv = {"h0": h, "hn": hn}
        g_ffn, g_ple = row(w["ffn_norm"][i]), row(w["ple_norm"][i])
        if i < NA:
            u = _mm(hn, w["conv_w_pw1"][i], extras=[row(w["conv_b_pw1"][i])], epi=lambda acc, b: acc + b, name="mm_pw1")
            glu = _rows(_glu_fwd, [u], [], [act(F32)], name="glu_fwd")[0]
            wdw = jnp.pad(w["conv_w_dw"][i], ((0, CONV_PAD - CONV_WIDTH), (0, 0)))
            cv = _dwconv_fwd(glu, wdw, row(w["conv_b_dw"][i]), name="dwconv_fwd")
            sw = _rows(_ln_silu_fwd, [cv], [row(w["conv_ln_g"][i]), row(w["conv_ln_b"][i])], [act(BF16)], name="ln_silu_fwd")[0]
            h1, hn2 = _mm(sw, w["conv_w_pw2"][i], extras=[row(w["conv_b_pw2"][i]), h, g_ffn],
                          epi=lambda acc, b, r, gn: _with_rms(acc + b + r, gn), out_dtype=[F32, BF16], tm=512, tn=D, name="mm_pw2")
            sv.update(u=u, glu=glu, cv=cv, sw=sw, wdw=wdw)
        else:
            if i == NA:
                h_kv = h
                nkv = rms(h, w["kv_norm"], "rms_kv")
                kv = _mm(nkv, wkv, out_dtype=BF16, name="mm_kv")
                fl = _mm(nkv, wf, name="mm_f")
                c = _gate_cumsum(fl, bfp, name="gate_cumsum")
                ck = jnp.pad(c[:, :H].T.reshape(HP, 2, S), ((0, 0), (0, 6), (0, 0)))
            j = i - NA
            q = _mm(hn, w["attn_w_q"][j], epi=lambda acc: acc * (HEAD_DIM ** -0.5), out_dtype=BF16, name="mm_q")
            o, lse = _attn_fwd(q, kv, ck, name="attn_fwd")
            h1, hn2 = _mm(o, w["attn_w_o"][j], extras=[h, g_ffn], epi=res_rms, out_dtype=[F32, BF16], tm=512, tn=D, name="mm_o")
            sv.update(q=q, o=o, lse=lse)
        zb, f = _mm(hn2, w["ffn_w1"][i], epi=lambda acc: (acc, jnp.square(jnp.maximum(acc, 0.0))), out_dtype=[BF16, BF16],
                    name="mm_ffn1")
        h2, n3 = _mm(f, w["ffn_w2"][i], extras=[h1, g_ple], epi=res_rms, out_dtype=[F32, BF16], tm=512, tn=D, tk=4 * D, name="mm_ffn2")
        zg = _mm(n3, w["ple_w_gate"][i], name="mm_gate")
        ple = lambda acc, r, zz: r + _sigmoid(zz) * acc
        if i + 1 < L:
            h, hn = _mm(p[i], w["ple_w_proj"][i], extras=[h2, zg, row(w["mix_norm"][i + 1])],
                        epi=lambda acc, r, zz, gn: _with_rms(ple(acc, r, zz), gn), out_dtype=[F32, BF16], tm=512, tn=D, name="mm_proj")
        else:
            h = _mm(p[i], w["ple_w_proj"][i], extras=[h2, zg], epi=ple, tm=512, tn=D, name="mm_proj_last")
        sv.update(h1=h1, hn2=hn2, zb=zb, f=f, h2=h2, n3=n3, zg=zg)
        saved.append(sv)

    dh, err2, g_final = _rows(_final_fn, [h, tgt], [row(w["final_norm"])], [act(F32)], [_sds((1, D), F32), _sds((1, D), F32)],
                              name="final")
    loss = 0.5 * jnp.sum(err2) / D
    g["final_norm"] = g_final.reshape(-1)

    red = _sds((1, D), F32)
    stack = {k: [None] * n for k, n in (("mix_norm", L), ("ffn_norm", L), ("ple_norm", L), ("ffn_w1", L), ("ffn_w2", L),
                                        ("ple_w_gate", L), ("ple_w_proj", L), ("conv_w_pw1", NA), ("conv_b_pw1", NA),
                                        ("conv_w_dw", NA), ("conv_b_dw", NA), ("conv_ln_g", NA), ("conv_ln_b", NA),
                                        ("conv_w_pw2", NA), ("conv_b_pw2", NA), ("attn_w_q", L - NA), ("attn_w_o", L - NA))}
    dk_sum = dv_sum = None
    dcks = []
    for i in reversed(range(L)):
        sv = saved[i]
        dzg, dpp = _mm(p[i], w["ple_w_proj"][i], extras=[dh, sv["zg"]], epi=lambda acc, d, zz: _ple_bwd(d, zz, acc),
                       out_dtype=[BF16, BF16], name="mm_ple_bwd")
        stack["ple_w_proj"][i] = _mm(p[i], dpp, mode="tn", tk=2048, out_dtype=BF16, name="mm_dproj")
        stack["ple_w_gate"][i] = dw("ple_w_gate", i, sv["n3"], dzg, name="mm_dgate")
        dh, dhb, dgain = _mm(dzg, w["ple_w_gate"][i], mode="nt", extras=[sv["h2"], dh, row(w["ple_norm"][i])], epi=_dup(_rms_bwd),
                             out_dtype=[F32, BF16], reds=1, tm=512, tn=D, name="mm_dn3")
        stack["ple_norm"][i] = dgain.reshape(-1)
        dz = _mm(dhb, w["ffn_w2"][i], mode="nt", extras=[sv["zb"]], epi=lambda acc, zz: acc * (2.0 * jnp.maximum(zz, 0.0).astype(F32)),
                 out_dtype=BF16, name="mm_dz")
        stack["ffn_w2"][i] = dw("ffn_w2", i, sv["f"], dhb, name="mm_dffn2")
        stack["ffn_w1"][i] = dw("ffn_w1", i, sv["hn2"], dz, name="mm_dffn1")
        if i < NA:
            dh, dhb, dgain, dbias = _mm(dz, w["ffn_w1"][i], mode="nt", extras=[sv["h1"], dh, row(w["ffn_norm"][i])],
                                        epi=_dup(_rms_bwd_bias), out_dtype=[F32, BF16], reds=2, tm=512, tn=D, tk=4 * D, name="mm_dhn2_bias")
            stack["conv_b_pw2"][i] = dbias.reshape(-1)
        else:
            dh, dhb, dgain = _mm(dz, w["ffn_w1"][i], mode="nt", extras=[sv["h1"], dh, row(w["ffn_norm"][i])], epi=_dup(_rms_bwd),
                                 out_dtype=[F32, BF16], reds=1, tm=512, tn=D, tk=4 * D, name="mm_dhn2")
        stack["ffn_norm"][i] = dgain.reshape(-1)
        if i < NA:
            dsw = _mm(dhb, w["conv_w_pw2"][i], mode="nt", name="mm_dsw")
            stack["conv_w_pw2"][i] = dw("conv_w_pw2", i, sv["sw"], dhb, name="mm_dpw2")
            dcv, dlg, dlb = _rows(_ln_silu_bwd, [dsw, sv["cv"]], [row(w["conv_ln_g"][i]), row(w["conv_ln_b"][i])], [act(F32)],
                                  [red, red], name="ln_silu_bwd")
            stack["conv_ln_g"][i], stack["conv_ln_b"][i] = dlg.reshape(-1), dlb.reshape(-1)
            dglu, dwdw, dbdw = _dwconv_bwd(dcv, sv["glu"], sv["wdw"], name="dwconv_bwd")
            stack["conv_w_dw"][i], stack["conv_b_dw"][i] = dwdw[:CONV_WIDTH], dbdw.reshape(-1)
            du, dbu = _rows(_glu_bwd, [dglu, sv["u"]], [], [act(BF16, 2 * D)], [_sds((1, 2 * D), F32)], name="glu_bwd")
            stack["conv_b_pw1"][i] = dbu.reshape(-1)
            stack["conv_w_pw1"][i] = _mm(sv["hn"], du, mode="tn", tk=2048, out_dtype=BF16, name="mm_dpw1")
            dh, dgain = _mm(du, w["conv_w_pw1"][i], mode="nt", extras=[sv["h0"], dh, row(w["mix_norm"][i])], epi=_rms_bwd,
                            reds=1, tm=512, tn=D, name="mm_dhn_a")
        else:
            j = i - NA
            do = _mm(dhb, w["attn_w_o"][j], mode="nt", out_dtype=BF16, name="mm_do")
            stack["attn_w_o"][j] = dw("attn_w_o", j, sv["o"], dhb, name="mm_dwo")
            dq, drs, dk_sum, dv_sum, dcs = _attn_bwd(sv["q"], kv, sv["o"], do, sv["lse"], ck,
                                                     None if dk_sum is None else (dk_sum, dv_sum), name="attn_bwd")
            scale = lambda acc: acc * (HEAD_DIM ** -0.5)
            stack["attn_w_q"][j] = dw("attn_w_q", j, sv["hn"], dq, epi=scale, name="mm_dwq")
            dh_in = dh
            dh, dgain = _mm(dq, w["attn_w_q"][j], mode="nt", extras=[sv["h0"], dh_in, row(w["mix_norm"][i])],
                            epi=lambda acc, xx, dr, gn: _rms_bwd(scale(acc), xx, dr, gn), reds=1, tm=512, tn=D, name="mm_dhn_b")
            pick = lambda t: jnp.pad(t.reshape(S, HP, 2, HEAD_DIM)[:, :, ::-1, 0].reshape(S, H), ((0, 0), (0, LANES - H)))
            dcks += [pick(drs), pick(dcs)]
        stack["mix_norm"][i] = dgain.reshape(-1)
        if i == NA:
            dfl, dbf = _gate_cumsum_bwd(dcks, fl, bfp, name="gate_cumsum_bwd")
            g["b_f"] = dbf[0, :H]
            gk = _mm(nkv, dk_sum, mode="tn", tk=2048, out_dtype=BF16, name="mm_dwk")
            gv = _mm(nkv, dv_sum, mode="tn", tk=2048, out_dtype=BF16, name="mm_dwk")
            gf = _mm(nkv, dfl, mode="tn", tk=2048, out_dtype=BF16, name="mm_dwf")
            g["w_kvf"] = jnp.concatenate([gk, gv, gf[:, :H]], axis=1)
            dn = _mm(dk_sum, wk, mode="nt", name="mm_dnk")
            dn = _mm(dv_sum, wv, mode="nt", extras=[dn], epi=lambda acc, r: acc + r, name="mm_dnv")
            dh, dgain = _mm(dfl, wf, mode="nt", extras=[dn, h_kv, dh, row(w["kv_norm"])],
                            epi=lambda acc, r, xx, dr, gn: _rms_bwd(acc + r, xx, dr, gn), reds=1, tm=512, tn=D, name="mm_dnf")
            g["kv_norm"] = dgain.reshape(-1)
    for k, v in stack.items():
        if v[0] is not None:
            g[k] = jnp.stack(v, axis=0)
    return loss, dh, g


def kernel(x, p, mix_norm, conv_w_pw1, conv_b_pw1, conv_w_dw, conv_b_dw, conv_ln_g, conv_ln_b, conv_w_pw2, conv_b_pw2, kv_norm, w_kvf, b_f, attn_w_q, attn_w_o, ffn_norm, ffn_w1, ffn_w2, ple_norm, ple_w_gate, ple_w_proj, final_norm, loss_target, m_mix_norm, m_conv_w_pw1, m_conv_b_pw1, m_conv_w_dw, m_conv_b_dw, m_conv_ln_g, m_conv_ln_b, m_conv_w_pw2, m_conv_b_pw2, m_kv_norm, m_w_kvf, m_b_f, m_attn_w_q, m_attn_w_o, m_ffn_norm, m_ffn_w1, m_ffn_w2, m_ple_norm, m_ple_w_gate, m_ple_w_proj, m_final_norm, v_mix_norm, v_conv_w_pw1, v_conv_b_pw1, v_conv_w_dw, v_conv_b_dw, v_conv_ln_g, v_conv_ln_b, v_conv_w_pw2, v_conv_b_pw2, v_kv_norm, v_w_kvf, v_b_f, v_attn_w_q, v_attn_w_o, v_ffn_norm, v_ffn_w1, v_ffn_w2, v_ple_norm, v_ple_w_gate, v_ple_w_proj, v_final_norm):
    args = dict(locals())
    wl = {n: args[n] for n in WEIGHTS}
    ml = {n: args["m_" + n] for n in WEIGHTS}
    vl = {n: args["v_" + n] for n in WEIGHTS}
    S, D = x.shape[1], x.shape[2]
    C = D

    mat_shapes = [wl[n].shape for n, _ in MATS]
    vec_shapes = [wl[n].shape for n, _ in VECS]
    mats = _all_gather_shards(_pack([wl[n].astype(BF16) for n, _ in MATS], C, 16, 64), name="ag_mats")
    vecs = _all_gather_shards(_pack([wl[n] for n, _ in VECS], C, 1, 32), name="ag_vecs")
    full = {n: wl[n] for n in REPL}
    for (n, ax), t in zip(MATS, _unpack(mats, mat_shapes, C, 16)):
        full[n] = _unshard(t, ax)
    for (n, ax), t in zip(VECS, _unpack(vecs, vec_shapes, C, 1)):
        full[n] = _unshard(t, ax)

    names = [n for n, _ in MATS] + [n for n, _ in VECS]
    axes = dict(MATS + VECS)
    shard_shapes = [wl[n].shape for n in names]
    gp = _GradPack(names, shard_shapes, C)
    loss, dx, g = _local_step(x[0], p[:, 0], loss_target[0], full, gp)
    loss = lax.psum(loss, ("x", "y", "c"))

    gp.insert(g, axes)
    gred = _reduce_scatter(gp.buf, name="rs")
    gl = dict(zip(names, _unpack(gred, shard_shapes, C, 16)))

    rep_shapes = [wl[n].shape for n in REPL]
    rpack = _pack([g[n] for n in REPL], C, 1, 8)
    for n, t in zip(REPL, _unpack(_all_reduce_small(rpack, name="ar"), rep_shapes, C, 1)):
        gl[n] = t

    grads, deltas, new_m, new_v = {}, {}, {}, {}
    rep_w, rep_m, rep_v = (_pack([d[n] for n in REPL], C, 1, 8) for d in (wl, ml, vl))
    rep_out = _adamw(rep_w, rpack_like(gl, rep_shapes, C), rep_m, rep_v, "adamw_rep")
    for dst, packed in zip((grads, deltas, new_m, new_v), rep_out):
        for n, t in zip(REPL, _unpack(packed, rep_shapes, C, 1)):
            dst[n] = t
    for n in names:
        rows_n = int(np.prod(wl[n].shape[:-1]))
        step_rows = min(512, rows_n)
        if wl[n].shape[-1] == C and gp.off[n] % step_rows == 0 and rows_n % step_rows == 0:
            res = _adamw(wl[n], gred, ml[n], vl[n], "adamw_" + n, g_row=gp.off[n])
        else:
            res = _adamw(wl[n], gl[n], ml[n], vl[n], "adamw_" + n)
        for dst, t in zip((grads, deltas, new_m, new_v), res):
            dst[n] = t
    out = [loss, dx[None]]
    for d in (grads, deltas, new_m, new_v):
        out += [d[n] for n in WEIGHTS]
    return tuple(out)


def rpack_like(gl, rep_shapes, C):
    return _pack([gl[n] for n in REPL], C, 1, 8)
```
